```python
import math
import jax, jax.numpy as jnp
from jax import lax
import numpy as np

D_MODEL = 2048
BATCH = 8
SEQ = 2048
DEPTH = 2

A_HEADS = 6
A_QK_DIM = 64
A_V_DIM = 2 * A_QK_DIM
B_HEADS = 12
B_KV_HEADS = 3
B_HEAD_DIM = 64
B_WINDOW = 128
C_HEADS = 4
C_QK_DIM = 64
C_V_DIM = 128
C_CHUNK = 128
Q_BLOCK = 128

A_WIDTH = A_HEADS * A_V_DIM
B_WIDTH = B_HEADS * B_HEAD_DIM
C_WIDTH = C_HEADS * C_V_DIM
MIX_WIDTH = A_WIDTH + B_WIDTH + C_WIDTH

PROJ_SIZES = [
    A_HEADS * 2 * A_QK_DIM,
    A_HEADS * 2 * A_QK_DIM,
    A_HEADS * A_V_DIM,
    B_HEADS * B_HEAD_DIM,
    B_KV_HEADS * B_HEAD_DIM,
    B_KV_HEADS * B_HEAD_DIM,
    C_HEADS * C_QK_DIM,
    C_HEADS * C_QK_DIM,
    C_HEADS * C_V_DIM,
    C_HEADS * C_V_DIM,
]
PROJ_WIDTH = int(sum(PROJ_SIZES))
SPLIT_POINTS = [int(v) for v in np.cumsum(PROJ_SIZES)[:-1]]

REL_BUCKETS = 32
REL_MAX_DIST = 128
N_BIAS_HEADS = A_HEADS + B_HEADS

D_FF = 5632
N_EXPERTS = 8
TOP_K = 2
EXPERT_FF = 5632
N_DENSE = (DEPTH + 1) // 2
N_MOE = DEPTH // 2

ALPHA = (2.0 * DEPTH) ** 0.25
BETA = (8.0 * DEPTH) ** -0.25
LN_EPS = 1e-5
NEG = -1e30

kernel_name = "hymba_style_diff_swa_retention_moe_deepnorm"

F32 = jnp.float32


def layer_norm(x, g, b):
    xf = x.astype(F32)
    mu = jnp.mean(xf, axis=-1, keepdims=True)
    xc = xf - mu
    var = jnp.mean(xc * xc, axis=-1, keepdims=True)
    return (xc * lax.rsqrt(var + LN_EPS) * g.astype(F32) + b.astype(F32)).astype(x.dtype)


def rel_bucket(dist):
    max_exact = REL_BUCKETS // 2
    d = jnp.maximum(dist, 0)
    ratio = jnp.maximum(d, 1).astype(F32) / max_exact
    large = max_exact + (jnp.log(ratio) / math.log(REL_MAX_DIST / max_exact)
                         * (REL_BUCKETS - max_exact)).astype(jnp.int32)
    large = jnp.minimum(large, REL_BUCKETS - 1)
    return jnp.where(d < max_exact, d, large)


def lambda_init(layer_idx):
    return 0.8 - 0.6 * math.exp(-0.3 * layer_idx)


def diff_attention(q, k, v, lam_params, lam_init, subln_g, bias_tab):
    Bsz, S = q.shape[0], q.shape[1]
    nblk = S // Q_BLOCK
    qf = q.astype(F32) * (A_QK_DIM ** -0.5)
    kf = k.astype(F32)
    vf = v.astype(F32)
    lp = lam_params.astype(F32)
    lam = jnp.exp(jnp.sum(lp[0] * lp[1])) - jnp.exp(jnp.sum(lp[2] * lp[3])) + lam_init
    qb = qf.reshape(Bsz, nblk, Q_BLOCK, A_HEADS, 2, A_QK_DIM).transpose(1, 0, 2, 3, 4, 5)
    k_pos = jnp.arange(S)
    tab = bias_tab.astype(F32)

    def block(args):
        qi, i = args
        q_pos = i * Q_BLOCK + jnp.arange(Q_BLOCK)
        dist = q_pos[:, None] - k_pos[None, :]
        bias = tab[rel_bucket(dist)].transpose(2, 0, 1)
        s = jnp.einsum('bqhmd,bkhmd->bhmqk', qi, kf) + bias[None, :, None]
        s = jnp.where(dist >= 0, s, NEG)
        p = jax.nn.softmax(s, axis=-1)
        a = p[:, :, 0] - lam * p[:, :, 1]
        return jnp.einsum('bhqk,bkhe->bqhe', a, vf)

    o = lax.map(block, (qb, jnp.arange(nblk)))
    o = o.transpose(1, 0, 2, 3, 4).reshape(Bsz, S, A_HEADS, A_V_DIM)
    o = o * lax.rsqrt(jnp.mean(o * o, axis=-1, keepdims=True) + LN_EPS) * subln_g.astype(F32)
    o = o * (1.0 - lam_init)
    return o.reshape(Bsz, S, A_WIDTH).astype(q.dtype)


def swa_sink_attention(q, k, v, sinks, bias_tab):
    Bsz, S = q.shape[0], q.shape[1]
    W = B_WINDOW
    G = B_HEADS // B_KV_HEADS
    nb = S // W
    qb = q.astype(F32).reshape(Bsz, nb, W, B_KV_HEADS, G, B_HEAD_DIM) * (B_HEAD_DIM ** -0.5)
    kb = k.astype(F32).reshape(Bsz, nb, W, B_KV_HEADS, B_HEAD_DIM)
    vb = v.astype(F32).reshape(Bsz, nb, W, B_KV_HEADS, B_HEAD_DIM)
    pad = ((0, 0), (1, 0), (0, 0), (0, 0), (0, 0))
    kk = jnp.concatenate([jnp.pad(kb, pad)[:, :-1], kb], axis=2)
    vv = jnp.concatenate([jnp.pad(vb, pad)[:, :-1], vb], axis=2)
    q_loc = jnp.arange(W)
    k_loc = jnp.arange(2 * W)
    dist = q_loc[:, None] + W - k_loc[None, :]
    band = (dist >= 0) & (dist < W)
    blk_ok = (jnp.arange(nb)[:, None] > 0) | (k_loc[None, :] >= W)
    mask = band[None] & blk_ok[:, None, :]
    bias = bias_tab.astype(F32)[rel_bucket(dist)]
    bias = bias.transpose(2, 0, 1).reshape(B_KV_HEADS, G, W, 2 * W)
    s = jnp.einsum('bnqhgd,bnkhd->bnhgqk', qb, kk) + bias[None, None]
    s = jnp.where(mask[None, :, None, None], s, NEG)
    sink = sinks.astype(F32).reshape(B_KV_HEADS, G)[None, None, :, :, None, None]
    m = jnp.maximum(jnp.max(s, axis=-1, keepdims=True), sink)
    e = jnp.exp(s - m)
    p = e / (jnp.sum(e, axis=-1, keepdims=True) + jnp.exp(sink - m))
    o = jnp.einsum('bnhgqk,bnkhd->bnqhgd', p, vv)
    return o.reshape(Bsz, S, B_WIDTH).astype(q.dtype)


def rotate_every_two(x):
    x1 = x[..., 0::2]
    x2 = x[..., 1::2]
    return jnp.stack([-x2, x1], axis=-1).reshape(x.shape)


def retention(q, k, v, g, sin, cos):
    Bsz, S = q.shape[0], q.shape[1]
    C = C_CHUNK
    nc = S // C
    sn = sin[None, :, None, :]
    cs = cos[None, :, None, :]
    qf = q.astype(F32)
    kf = k.astype(F32) * (C_QK_DIM ** -0.5)
    qf = qf * cs + rotate_every_two(qf) * sn
    kf = kf * cs + rotate_every_two(kf) * sn
    vf = v.astype(F32)
    log_g = jnp.log(1.0 - jnp.exp2(-5.0 - jnp.arange(C_HEADS, dtype=F32)))
    qc = qf.reshape(Bsz, nc, C, C_HEADS, C_QK_DIM)
    kc = kf.reshape(Bsz, nc, C, C_HEADS, C_QK_DIM)
    vc = vf.reshape(Bsz, nc, C, C_HEADS, C_V_DIM)
    pos = jnp.arange(C)
    rel = (pos[:, None] - pos[None, :]).astype(F32)
    decay = jnp.where((rel >= 0)[None], jnp.exp(jnp.maximum(rel, 0.0)[None] * log_g[:, None, None]), 0.0)
    inner = jnp.einsum('bnqhd,bnkhd->bnhqk', qc, kc) * decay[None, None]
    inner_o = jnp.einsum('bnhqk,bnkhe->bnqhe', inner, vc)
    zeta = jnp.exp((C - 1 - pos).astype(F32)[:, None] * log_g[None, :])
    xi = jnp.exp((pos + 1).astype(F32)[:, None] * log_g[None, :])
    kv = jnp.einsum('bnkhd,bnkhe,kh->nbhde', kc, vc, zeta)
    g_chunk = jnp.exp(C * log_g)[None, :, None, None]

    def step(state, kv_i):
        return state * g_chunk + kv_i, state

    init = jnp.zeros((Bsz, C_HEADS, C_QK_DIM, C_V_DIM), F32)
    _, prev = lax.scan(step, init, kv)
    cross = jnp.einsum('bnqhd,nbhde,qh->bnqhe', qc, prev, xi)
    o = (inner_o + cross).reshape(Bsz, S, C_HEADS, C_V_DIM)
    mu = jnp.mean(o, axis=-1, keepdims=True)
    oc = o - mu
    o = oc * lax.rsqrt(jnp.mean(oc * oc, axis=-1, keepdims=True) + LN_EPS)
    y = jax.nn.silu(g.astype(F32)) * o.reshape(Bsz, S, C_WIDTH)
    return y.astype(q.dtype)


def swiglu(x, w_gate, w_up, w_down):
    h = jax.nn.silu(x @ w_gate) * (x @ w_up)
    return h @ w_down


def moe_swiglu(x, w_router, w_gate, w_up, w_down):
    Bsz, S, D = x.shape
    xt = x.reshape(Bsz * S, D)
    logits = (xt @ w_router).astype(F32)
    top_v, top_i = lax.top_k(logits, TOP_K)
    top_w = jax.nn.softmax(top_v, axis=-1)
    gates = jnp.sum(jax.nn.one_hot(top_i, N_EXPERTS, dtype=F32) * top_w[..., None], axis=1)
    y = jnp.zeros((Bsz * S, D), F32)
    for e in range(N_EXPERTS):
        y = y + gates[:, e:e + 1] * swiglu(xt, w_gate[e], w_up[e], w_down[e]).astype(F32)
    return y.reshape(Bsz, S, D).astype(x.dtype)


def setup_inputs(seed: int = 0) -> dict:
    key = jax.random.key(seed)
    ks = jax.random.split(key, 20)
    nrm = jax.random.normal
    D = D_MODEL
    return {
        "x": nrm(ks[0], (BATCH, SEQ, D), F32),
        "w_in": nrm(ks[1], (DEPTH, D, PROJ_WIDTH), F32) * D ** -0.5,
        "rel_bias": nrm(ks[2], (REL_BUCKETS, N_BIAS_HEADS), F32) * 0.3,
        "a_lambda": nrm(ks[3], (DEPTH, 4, A_QK_DIM), F32) * 0.1,
        "a_subln_g": 1.0 + 0.02 * nrm(ks[4], (DEPTH, A_V_DIM), F32),
        "b_sinks": nrm(ks[5], (DEPTH, B_HEADS), F32) * 0.5,
        "w_out": nrm(ks[6], (DEPTH, MIX_WIDTH, D), F32) * (MIX_WIDTH ** -0.5) * BETA,
        "ln_mix_g": 1.0 + 0.02 * nrm(ks[7], (DEPTH, D), F32),
        "ln_mix_b": 0.02 * nrm(ks[8], (DEPTH, D), F32),
        "ln_ffn_g": 1.0 + 0.02 * nrm(ks[9], (DEPTH, D), F32),
        "ln_ffn_b": 0.02 * nrm(ks[10], (DEPTH, D), F32),
        "dense_w_gate": nrm(ks[11], (N_DENSE, D, D_FF), F32) * D ** -0.5,
        "dense_w_up": nrm(ks[12], (N_DENSE, D, D_FF), F32) * D ** -0.5,
        "dense_w_down": nrm(ks[13], (N_DENSE, D_FF, D), F32) * (D_FF ** -0.5) * BETA,
        "moe_router": nrm(ks[14], (N_MOE, D, N_EXPERTS), F32) * D ** -0.5,
        "moe_w_gate": nrm(ks[15], (N_MOE, N_EXPERTS, D, EXPERT_FF), F32) * D ** -0.5,
        "moe_w_up": nrm(ks[16], (N_MOE, N_EXPERTS, D, EXPERT_FF), F32) * D ** -0.5,
        "moe_w_down": nrm(ks[17], (N_MOE, N_EXPERTS, EXPERT_FF, D), F32) * (EXPERT_FF ** -0.5) * BETA,
    }


def reference(x, w_in, rel_bias, a_lambda, a_subln_g, b_sinks, w_out, ln_mix_g, ln_mix_b,
              ln_ffn_g, ln_ffn_b, dense_w_gate, dense_w_up, dense_w_down, moe_router,
              moe_w_gate, moe_w_up, moe_w_down):
    Bsz, S, _ = x.shape
    ang = jnp.repeat(1.0 / (10000.0 ** jnp.linspace(0.0, 1.0, C_QK_DIM // 2, dtype=F32)), 2)
    ang = jnp.arange(S, dtype=F32)[:, None] * ang[None, :]
    sin, cos = jnp.sin(ang), jnp.cos(ang)
    bias_a = rel_bias[:, :A_HEADS]
    bias_b = rel_bias[:, A_HEADS:]
    for l in range(DEPTH):
        proj = jnp.einsum('bsd,dp->bsp', x, w_in[l])
        aq, ak, av, bq, bk, bv, cq, ck, cv, cg = jnp.split(proj, SPLIT_POINTS, axis=-1)
        ya = diff_attention(aq.reshape(Bsz, S, A_HEADS, 2, A_QK_DIM),
                            ak.reshape(Bsz, S, A_HEADS, 2, A_QK_DIM),
                            av.reshape(Bsz, S, A_HEADS, A_V_DIM),
                            a_lambda[l], lambda_init(l), a_subln_g[l], bias_a)
        yb = swa_sink_attention(bq.reshape(Bsz, S, B_HEADS, B_HEAD_DIM),
                                bk.reshape(Bsz, S, B_KV_HEADS, B_HEAD_DIM),
                                bv.reshape(Bsz, S, B_KV_HEADS, B_HEAD_DIM),
                                b_sinks[l], bias_b)
        yc = retention(cq.reshape(Bsz, S, C_HEADS, C_QK_DIM),
                       ck.reshape(Bsz, S, C_HEADS, C_QK_DIM),
                       cv.reshape(Bsz, S, C_HEADS, C_V_DIM), cg, sin, cos)
        mix = jnp.einsum('bsm,md->bsd', jnp.concatenate([ya, yb, yc], axis=-1), w_out[l])
        x = layer_norm(ALPHA * x + mix, ln_mix_g[l], ln_mix_b[l])
        if l % 2 == 0:
            j = l // 2
            f = swiglu(x, dense_w_gate[j], dense_w_up[j], dense_w_down[j])
        else:
            j = l // 2
            f = moe_swiglu(x, moe_router[j], moe_w_gate[j], moe_w_up[j], moe_w_down[j])
        x = layer_norm(ALPHA * x + f, ln_ffn_g[l], ln_ffn_b[l])
    return x
```

```python
import functools
import math

import jax
import jax.numpy as jnp
import numpy as np
from jax import lax
from jax.experimental import pallas as pl
from jax.experimental.pallas import tpu as pltpu

F32 = jnp.float32
BF16 = jnp.bfloat16

D_MODEL = 2048
DEPTH = 2
A_HEADS = 6
A_QK_DIM = 64
A_V_DIM = 128
B_HEADS = 12
B_KV_HEADS = 3
B_HEAD_DIM = 64
B_WINDOW = 128
C_HEADS = 4
C_QK_DIM = 64
C_V_DIM = 128
C_CHUNK = 128
A_WIDTH = A_HEADS * A_V_DIM
B_WIDTH = B_HEADS * B_HEAD_DIM
C_WIDTH = C_HEADS * C_V_DIM
REL_BUCKETS = 32
REL_MAX_DIST = 128
D_FF = 5632
N_EXPERTS = 8
ALPHA = (2.0 * DEPTH) ** 0.25
LN_EPS = 1e-5
NEG = -1e30

LANES = 128
VMEM_LIMIT = 56 * 1024 * 1024

_REF_SIZES = [768, 768, 768, 768, 192, 192, 256, 256, 512, 512]
_REF_OFF = [int(v) for v in np.concatenate([[0], np.cumsum(_REF_SIZES)[:-1]])]
PROJ_WIDTH = int(sum(_REF_SIZES))
OFF_AQ, OFF_AK, OFF_AV, OFF_BQ, OFF_CV, OFF_CG, OFF_CQ, OFF_CK, OFF_BKV = (
    0, 768, 1536, 2304, 3072, 3584, 4096, 4352, 4608)


def _proj_perm():
    aq, ak, av, bq, bk, bv, cq, ck, cv, cg = [np.arange(o, o + s) for o, s in zip(_REF_OFF, _REF_SIZES)]
    bkv = np.concatenate([np.concatenate([bk[g * 64:(g + 1) * 64], bv[g * 64:(g + 1) * 64]])
                          for g in range(B_KV_HEADS)])
    perm = np.concatenate([aq, ak, av, bq, cv, cg, cq, ck, bkv])
    assert perm.shape[0] == PROJ_WIDTH
    return perm


_PERM = _proj_perm()


def _cparams(sem):
    return pltpu.CompilerParams(dimension_semantics=sem, vmem_limit_bytes=VMEM_LIMIT)


def _layer_norm(z, g, b):
    mu = jnp.mean(z, axis=-1, keepdims=True)
    zc = z - mu
    var = jnp.mean(zc * zc, axis=-1, keepdims=True)
    return zc * lax.rsqrt(var + LN_EPS) * g + b


def _silu(x):
    return x / (1.0 + jnp.exp(-x))


def _dot_nt(a, b):
    return lax.dot_general(a, b, (((1,), (1,)), ((), ())), preferred_element_type=F32)


def _dot_tn(a, b):
    return lax.dot_general(a, b, (((0,), (0,)), ((), ())), preferred_element_type=F32)


def _bias_kernel(tab_ref, bkt_ref, o_ref):
    h = pl.program_id(0)
    bkt = bkt_ref[...]
    acc = jnp.full(bkt.shape, NEG, F32)
    for b in range(REL_BUCKETS):
        acc = jnp.where(bkt == b, tab_ref[h, b], acc)
    o_ref[0] = acc


def _bias_lookup(tab_t, bkt):
    nh = tab_t.shape[0]
    r, c = bkt.shape
    return pl.pallas_call(
        _bias_kernel,
        grid=(nh,),
        in_specs=[pl.BlockSpec(memory_space=pltpu.SMEM),
                  pl.BlockSpec((r, c), lambda h: (0, 0))],
        out_specs=pl.BlockSpec((1, r, c), lambda h: (h, 0, 0)),
        out_shape=jax.ShapeDtypeStruct((nh, r, c), F32),
        compiler_params=_cparams(("arbitrary",)),
        name="bias_lookup",
    )(tab_t, bkt)


def _rel_bucket(dist):
    max_exact = REL_BUCKETS // 2
    d = jnp.maximum(dist, 0)
    ratio = jnp.maximum(d, 1).astype(F32) / max_exact
    large = max_exact + (jnp.log(ratio) / math.log(REL_MAX_DIST / max_exact)
                         * (REL_BUCKETS - max_exact)).astype(jnp.int32)
    large = jnp.minimum(large, REL_BUCKETS - 1)
    return jnp.where(d < max_exact, d, large)


def _in_proj_kernel(x_ref, w_ref, o_ref, xb_ref):
    @pl.when(pl.program_id(1) == 0)
    def _():
        xb_ref[...] = x_ref[...].astype(BF16)

    o_ref[...] = jnp.dot(xb_ref[...], w_ref[...], preferred_element_type=F32).astype(BF16)


def _in_proj(x, w):
    n, d = x.shape
    p = w.shape[1]
    tm = min(1024, n)
    tn = p // 3
    return pl.pallas_call(
        _in_proj_kernel,
        grid=(n // tm, p // tn),
        in_specs=[pl.BlockSpec((tm, d), lambda i, j: (i, 0)),
                  pl.BlockSpec((d, tn), lambda i, j: (0, j))],
        out_specs=pl.BlockSpec((tm, tn), lambda i, j: (i, j)),
        out_shape=jax.ShapeDtypeStruct((n, p), BF16),
        scratch_shapes=[pltpu.VMEM((tm, d), BF16)],
        compiler_params=_cparams(("parallel", "arbitrary")),
        name="in_proj",
    )(x, w)


A_BLOCK = 256


def _attn_a_kernel(q_ref, k_ref, v_ref, bias_ref, lam_ref, g_ref, o_ref, *, lam_init):
    t = A_BLOCK
    i = pl.program_id(2)
    q = q_ref[0].astype(F32) * (A_QK_DIM ** -0.5)
    lane = lax.broadcasted_iota(jnp.int32, (t, LANES), 1)
    qq = jnp.concatenate([jnp.where(lane < A_QK_DIM, q, 0.0).astype(BF16),
                          jnp.where(lane >= A_QK_DIM, q, 0.0).astype(BF16)], axis=0)

    def step(j, carry):
        m_old, l_old, acc = carry
        start = pl.multiple_of(j * t, t)
        kj = k_ref[0, pl.ds(start, t), :]
        vj = v_ref[0, pl.ds(start, t), :]
        s = _dot_nt(qq, kj) + bias_ref[0, jnp.minimum(i - j, 2)]
        m_new = jnp.maximum(m_old, jnp.max(s, axis=-1, keepdims=True))
        p = jnp.exp(s - m_new)
        alpha = jnp.exp(m_old - m_new)
        l_new = alpha * l_old + jnp.sum(p, axis=-1, keepdims=True)
        acc = alpha * acc + jnp.dot(p.astype(BF16), vj, preferred_element_type=F32)
        return m_new, l_new, acc

    init = (jnp.full((2 * t, 1), NEG, F32), jnp.zeros((2 * t, 1), F32), jnp.zeros((2 * t, LANES), F32))
    _, l_fin, acc = lax.fori_loop(0, i + 1, step, init)
    o_all = acc / l_fin
    lp = lam_ref[...]
    lam = (jnp.exp(jnp.sum(lp[0:1] * lp[1:2], axis=-1, keepdims=True))
           - jnp.exp(jnp.sum(lp[2:3] * lp[3:4], axis=-1, keepdims=True)) + lam_init)
    o = o_all[:t] - lam * o_all[t:]
    o = o * lax.rsqrt(jnp.mean(o * o, axis=-1, keepdims=True) + LN_EPS) * g_ref[...]
    o_ref[0] = (o * (1.0 - lam_init)).astype(BF16)


def _attn_a(proj, bias_a, lam_params, subln_g, lam_init):
    bsz, s, _ = proj.shape
    t = A_BLOCK
    kb, vb = OFF_AK // LANES, OFF_AV // LANES
    return pl.pallas_call(
        functools.partial(_attn_a_kernel, lam_init=lam_init),
        grid=(bsz, A_HEADS, s // t),
        in_specs=[pl.BlockSpec((1, t, LANES), lambda b, h, i: (b, i, h)),
                  pl.BlockSpec((1, s, LANES), lambda b, h, i: (b, 0, kb + h)),
                  pl.BlockSpec((1, s, LANES), lambda b, h, i: (b, 0, vb + h)),
                  pl.BlockSpec((1, 3, 2 * t, t), lambda b, h, i: (h, 0, 0, 0)),
                  pl.BlockSpec((4, A_QK_DIM), lambda b, h, i: (0, 0)),
                  pl.BlockSpec((1, A_V_DIM), lambda b, h, i: (0, 0))],
        out_specs=pl.BlockSpec((1, t, LANES), lambda b, h, i: (b, i, h)),
        out_shape=jax.ShapeDtypeStruct((bsz, s, A_WIDTH), BF16),
        compiler_params=_cparams(("parallel", "parallel", "arbitrary")),
        name="attn_a",
    )(proj, proj, proj, bias_a, lam_params, subln_g)


def _swa_kernel(sink_ref, q_ref, kvp_ref, kvc_ref, bias_ref, o_ref):
    w = B_WINDOW
    n = pl.program_id(1)
    lane = lax.broadcasted_iota(jnp.int32, (w, LANES), 1)
    col = lax.broadcasted_iota(jnp.int32, (w, 2 * w), 1)
    valid = jnp.logical_or(col >= w, n > 0)
    group = B_HEADS // B_KV_HEADS
    for g in range(B_KV_HEADS):
        kv = jnp.concatenate([kvp_ref[0, :, g * LANES:(g + 1) * LANES],
                              kvc_ref[0, :, g * LANES:(g + 1) * LANES]], axis=0)
        for pair in range(group // 2):
            blk = g * (group // 2) + pair
            qblk = q_ref[0, :, blk * LANES:(blk + 1) * LANES].astype(F32) * (B_HEAD_DIM ** -0.5)
            outs = []
            for half in range(2):
                hq = 2 * blk + half
                qh = qblk if half == 0 else pltpu.roll(qblk, B_HEAD_DIM, 1)
                qh = jnp.where(lane < B_HEAD_DIM, qh, 0.0).astype(BF16)
                s = _dot_nt(qh, kv) + bias_ref[hq]
                s = jnp.where(valid, s, NEG)
                sink = sink_ref[hq]
                m = jnp.maximum(jnp.max(s, axis=-1, keepdims=True), sink)
                e = jnp.exp(s - m)
                den = jnp.sum(e, axis=-1, keepdims=True) + jnp.exp(sink - m)
                o = jnp.dot(e.astype(BF16), kv, preferred_element_type=F32)
                outs.append(o / den)
            ob = jnp.where(lane < B_HEAD_DIM, pltpu.roll(outs[0], B_HEAD_DIM, 1), outs[1])
            o_ref[0, :, blk * LANES:(blk + 1) * LANES] = ob.astype(BF16)


def _swa(proj, bias_b, sinks):
    bsz, s, _ = proj.shape
    w = B_WINDOW
    kvw = 2 * B_KV_HEADS * B_HEAD_DIM
    qb, kvb = OFF_BQ // B_WIDTH, OFF_BKV // kvw
    return pl.pallas_call(
        _swa_kernel,
        grid=(bsz, s // w),
        in_specs=[pl.BlockSpec(memory_space=pltpu.SMEM),
                  pl.BlockSpec((1, w, B_WIDTH), lambda b, n: (b, n, qb)),
                  pl.BlockSpec((1, w, kvw), lambda b, n: (b, jnp.maximum(n - 1, 0), kvb)),
                  pl.BlockSpec((1, w, kvw), lambda b, n: (b, n, kvb)),
                  pl.BlockSpec((B_HEADS, w, 2 * w), lambda b, n: (0, 0, 0))],
        out_specs=pl.BlockSpec((1, w, B_WIDTH), lambda b, n: (b, n, 0)),
        out_shape=jax.ShapeDtypeStruct((bsz, s, B_WIDTH), BF16),
        compiler_params=_cparams(("parallel", "arbitrary")),
        name="swa",
    )(sinks, proj, proj, proj, bias_b)


def _ret_kernel(q_ref, k_ref, v_ref, g_ref, sin_ref, cos_ref, decay_ref, zeta_ref, xi_ref, gch_ref, o_ref,
                state_ref):
    c = C_CHUNK

    @pl.when(pl.program_id(1) == 0)
    def _():
        state_ref[...] = jnp.zeros_like(state_ref)

    lane = lax.broadcasted_iota(jnp.int32, (c, LANES), 1)
    even = (lane & 1) == 0
    sn = sin_ref[...]
    cs = cos_ref[...]

    def rope(x):
        swapped = jnp.where(even, pltpu.roll(x, LANES - 1, 1), pltpu.roll(x, 1, 1))
        return x * cs + swapped * sn

    for pair in range(C_HEADS // 2):
        q = rope(q_ref[0, :, pair * LANES:(pair + 1) * LANES].astype(F32))
        k = rope(k_ref[0, :, pair * LANES:(pair + 1) * LANES].astype(F32) * (C_QK_DIM ** -0.5))
        kb = k.astype(BF16)
        kz = (k * zeta_ref[pair]).astype(BF16)
        qx = q * xi_ref[pair]
        for half in range(2):
            h = 2 * pair + half
            in_head = (lane < C_QK_DIM) if half == 0 else (lane >= C_QK_DIM)
            qm = jnp.where(in_head, q, 0.0).astype(BF16)
            qxm = jnp.where(in_head, qx, 0.0).astype(BF16)
            vh = v_ref[0, :, h * LANES:(h + 1) * LANES]
            inner = _dot_nt(qm, kb) * decay_ref[h]
            st = state_ref[h]
            o = (jnp.dot(inner.astype(BF16), vh, preferred_element_type=F32)
                 + jnp.dot(qxm, st.astype(BF16), preferred_element_type=F32))
            state_ref[h] = st * gch_ref[h] + _dot_tn(kz, vh)
            mu = jnp.mean(o, axis=-1, keepdims=True)
            oc = o - mu
            o = oc * lax.rsqrt(jnp.mean(oc * oc, axis=-1, keepdims=True) + LN_EPS)
            gate = g_ref[0, :, h * LANES:(h + 1) * LANES].astype(F32)
            o_ref[0, :, h * LANES:(h + 1) * LANES] = (_silu(gate) * o).astype(BF16)


def _retention(proj, sin_t, cos_t, decay, zeta_t, xi_t, gch):
    bsz, s, _ = proj.shape
    c = C_CHUNK
    qw = C_HEADS * C_QK_DIM
    return pl.pallas_call(
        _ret_kernel,
        grid=(bsz, s // c),
        in_specs=[pl.BlockSpec((1, c, qw), lambda b, n: (b, n, OFF_CQ // qw)),
                  pl.BlockSpec((1, c, qw), lambda b, n: (b, n, OFF_CK // qw)),
                  pl.BlockSpec((1, c, C_WIDTH), lambda b, n: (b, n, OFF_CV // C_WIDTH)),
                  pl.BlockSpec((1, c, C_WIDTH), lambda b, n: (b, n, OFF_CG // C_WIDTH)),
                  pl.BlockSpec((c, LANES), lambda b, n: (n, 0)),
                  pl.BlockSpec((c, LANES), lambda b, n: (n, 0)),
                  pl.BlockSpec((C_HEADS, c, c), lambda b, n: (0, 0, 0)),
                  pl.BlockSpec((C_HEADS // 2, c, LANES), lambda b, n: (0, 0, 0)),
                  pl.BlockSpec((C_HEADS // 2, c, LANES), lambda b, n: (0, 0, 0)),
                  pl.BlockSpec((C_HEADS, 1, LANES), lambda b, n: (0, 0, 0))],
        out_specs=pl.BlockSpec((1, c, C_WIDTH), lambda b, n: (b, n, 0)),
        out_shape=jax.ShapeDtypeStruct((bsz, s, C_WIDTH), BF16),
        scratch_shapes=[pltpu.VMEM((C_HEADS, LANES, C_V_DIM), F32)],
        compiler_params=_cparams(("parallel", "arbitrary")),
        name="retention",
    )(proj, proj, proj, proj, sin_t, cos_t, decay, zeta_t, xi_t, gch)


def _out_proj_kernel(ya_ref, yb_ref, yc_ref, wa_ref, wb_ref, wc_ref, x_ref, g_ref, b_ref, o_ref):
    mix = (jnp.dot(ya_ref[...], wa_ref[...], preferred_element_type=F32)
           + jnp.dot(yb_ref[...], wb_ref[...], preferred_element_type=F32)
           + jnp.dot(yc_ref[...], wc_ref[...], preferred_element_type=F32))
    o_ref[...] = _layer_norm(ALPHA * x_ref[...] + mix, g_ref[...], b_ref[...])


def _out_proj_ln(ya, yb, yc, w, x, g, b):
    n, d = x.shape
    tm = min(256, n)
    row = lambda i: (i, 0)
    fixed = lambda i: (0, 0)
    return pl.pallas_call(
        _out_proj_kernel,
        grid=(n // tm,),
        in_specs=[pl.BlockSpec((tm, A_WIDTH), row),
                  pl.BlockSpec((tm, B_WIDTH), row),
                  pl.BlockSpec((tm, C_WIDTH), row),
                  pl.BlockSpec((A_WIDTH, d), lambda i: (0, 0)),
                  pl.BlockSpec((B_WIDTH, d), lambda i: (1, 0)),
                  pl.BlockSpec((C_WIDTH, d), lambda i: ((A_WIDTH + B_WIDTH) // C_WIDTH, 0)),
                  pl.BlockSpec((tm, d), row),
                  pl.BlockSpec((1, d), fixed),
                  pl.BlockSpec((1, d), fixed)],
        out_specs=pl.BlockSpec((tm, d), row),
        out_shape=jax.ShapeDtypeStruct((n, d), F32),
        compiler_params=_cparams(("parallel",)),
        name="out_proj_ln",
    )(ya, yb, yc, w, w, w, x, g, b)


FFN_TM = 512
FFN_TF = 512


def _ffn_kernel(eid_ref, nact_ref, x_ref, wg_ref, wu_ref, wd_ref, g_ref, b_ref, o_ref, xb_ref, acc_ref, *,
                fuse_ln):
    i = pl.program_id(0)
    f = pl.program_id(1)
    active = i < nact_ref[0]

    @pl.when(jnp.logical_and(active, f == 0))
    def _():
        xb_ref[...] = x_ref[...].astype(BF16)

    @pl.when(active)
    def _():
        xb = xb_ref[...]
        hg = jnp.dot(xb, wg_ref[0], preferred_element_type=F32)
        hu = jnp.dot(xb, wu_ref[0], preferred_element_type=F32)
        hidden = (_silu(hg) * hu).astype(BF16)
        contrib = jnp.dot(hidden, wd_ref[0], preferred_element_type=F32)

        @pl.when(f == 0)
        def _():
            acc_ref[...] = contrib

        @pl.when(f > 0)
        def _():
            acc_ref[...] += contrib

    last = f == pl.num_programs(1) - 1

    @pl.when(jnp.logical_and(active, last))
    def _():
        if fuse_ln:
            o_ref[...] = _layer_norm(ALPHA * x_ref[...] + acc_ref[...], g_ref[...], b_ref[...])
        else:
            o_ref[...] = acc_ref[...]

    @pl.when(jnp.logical_and(jnp.logical_not(active), last))
    def _():
        o_ref[...] = jnp.zeros_like(o_ref)


def _ffn(x, eid, nact, w_gate, w_up, w_down, g, b, fuse_ln):
    n, d = x.shape
    ff = w_gate.shape[2]
    tm = min(FFN_TM, n)
    tf = FFN_TF
    nf = ff // tf

    def fidx(i, f, eid_ref, nact_ref):
        return jnp.where(i < nact_ref[0], f, nf - 1)

    grid_spec = pltpu.PrefetchScalarGridSpec(
        num_scalar_prefetch=2,
        grid=(n // tm, nf),
        in_specs=[pl.BlockSpec((tm, d), lambda i, f, e, a: (i, 0)),
                  pl.BlockSpec((1, d, tf), lambda i, f, e, a: (e[i], 0, fidx(i, f, e, a))),
                  pl.BlockSpec((1, d, tf), lambda i, f, e, a: (e[i], 0, fidx(i, f, e, a))),
                  pl.BlockSpec((1, tf, d), lambda i, f, e, a: (e[i], fidx(i, f, e, a), 0)),
                  pl.BlockSpec((1, d), lambda i, f, e, a: (0, 0)),
                  pl.BlockSpec((1, d), lambda i, f, e, a: (0, 0))],
        out_specs=pl.BlockSpec((tm, d), lambda i, f, e, a: (i, 0)),
        scratch_shapes=[pltpu.VMEM((tm, d), BF16), pltpu.VMEM((tm, d), F32)],
    )
    return pl.pallas_call(
        functools.partial(_ffn_kernel, fuse_ln=fuse_ln),
        grid_spec=grid_spec,
        out_shape=jax.ShapeDtypeStruct((n, d), F32),
        compiler_params=_cparams(("parallel", "arbitrary")),
        name="ffn_ln" if fuse_ln else "ffn_grouped",
    )(eid, nact, x, w_gate, w_up, w_down, g, b)


ROUTER_TM = 512


def _router_kernel(x_ref, wr_ref, tri_ref, meta_ref, cnt_ref, carry_ref):
    @pl.when(pl.program_id(0) == 0)
    def _():
        carry_ref[...] = jnp.zeros_like(carry_ref)

    logits = jnp.dot(x_ref[...], wr_ref[...], preferred_element_type=F32, precision=lax.Precision.HIGHEST)
    lane = lax.broadcasted_iota(jnp.int32, logits.shape, 1)
    logits = jnp.where(lane < N_EXPERTS, logits, -jnp.inf)
    m1 = jnp.max(logits, axis=-1, keepdims=True)
    i1 = jnp.min(jnp.where(logits == m1, lane, LANES), axis=-1, keepdims=True)
    rest = jnp.where(lane == i1, -jnp.inf, logits)
    m2 = jnp.max(rest, axis=-1, keepdims=True)
    i2 = jnp.min(jnp.where(rest == m2, lane, LANES), axis=-1, keepdims=True)
    e2 = jnp.exp(m2 - m1)
    w1 = 1.0 / (1.0 + e2)
    w2 = e2 * w1
    hit1 = lane == i1
    hit2 = lane == i2
    onehot = jnp.where(jnp.logical_or(hit1, hit2), 1.0, 0.0)
    before = jnp.dot(tri_ref[...], onehot.astype(BF16), preferred_element_type=F32) + carry_ref[...]
    r1 = jnp.sum(jnp.where(hit1, before, 0.0), axis=-1, keepdims=True)
    r2 = jnp.sum(jnp.where(hit2, before, 0.0), axis=-1, keepdims=True)
    carry_ref[...] = carry_ref[...] + jnp.sum(onehot, axis=0, keepdims=True)
    cnt_ref[...] = carry_ref[...]
    meta = jnp.where(lane == 0, i1.astype(F32), 0.0)
    meta = jnp.where(lane == 1, i2.astype(F32), meta)
    meta = jnp.where(lane == 2, r1, meta)
    meta = jnp.where(lane == 3, r2, meta)
    meta = jnp.where(lane == 4, w1, meta)
    meta = jnp.where(lane == 5, w2, meta)
    meta_ref[...] = meta


def _router(x, w_router):
    n, d = x.shape
    tm = min(ROUTER_TM, n)
    wr = jnp.zeros((d, LANES), F32).at[:, :N_EXPERTS].set(w_router.astype(F32))
    tri = jnp.asarray(np.tril(np.ones((tm, tm), np.float32), -1), BF16)
    return pl.pallas_call(
        _router_kernel,
        grid=(n // tm,),
        in_specs=[pl.BlockSpec((tm, d), lambda i: (i, 0)),
                  pl.BlockSpec((d, LANES), lambda i: (0, 0)),
                  pl.BlockSpec((tm, tm), lambda i: (0, 0))],
        out_specs=[pl.BlockSpec((tm, LANES), lambda i: (i, 0)),
                   pl.BlockSpec((1, LANES), lambda i: (0, 0))],
        out_shape=[jax.ShapeDtypeStruct((n, LANES), F32), jax.ShapeDtypeStruct((1, LANES), F32)],
        scratch_shapes=[pltpu.VMEM((1, LANES), F32)],
        compiler_params=_cparams(("arbitrary",)),
        name="router",
    )(x, wr, tri)


MOVE_TM = 256


def _row_copy(src, dst, sem, s, t):
    return pltpu.make_async_copy(src.at[pl.ds(s, 1)], dst.at[pl.ds(t, 1)], sem)


def _dispatch_kernel(d1_ref, d2_ref, x_hbm, xs_in_hbm, xs_hbm, sem, *, tm):
    del xs_in_hbm
    base = pl.program_id(0) * tm

    def issue(t, carry):
        row = base + t
        _row_copy(x_hbm, xs_hbm, sem, row, d1_ref[row]).start()
        _row_copy(x_hbm, xs_hbm, sem, row, d2_ref[row]).start()
        return carry

    lax.fori_loop(0, tm, issue, 0)

    def drain(t, carry):
        _row_copy(x_hbm, xs_hbm, sem, 0, 0).wait()
        _row_copy(x_hbm, xs_hbm, sem, 0, 0).wait()
        return carry

    lax.fori_loop(0, tm, drain, 0)


def _dispatch(x, d1, d2, rows):
    n, d = x.shape
    tm = min(MOVE_TM, n)
    xs0 = jnp.zeros((rows, d), F32)
    grid_spec = pltpu.PrefetchScalarGridSpec(
        num_scalar_prefetch=2,
        grid=(n // tm,),
        in_specs=[pl.BlockSpec(memory_space=pl.ANY), pl.BlockSpec(memory_space=pl.ANY)],
        out_specs=pl.BlockSpec(memory_space=pl.ANY),
        scratch_shapes=[pltpu.SemaphoreType.DMA(())],
    )
    return pl.pallas_call(
        functools.partial(_dispatch_kernel, tm=tm),
        grid_spec=grid_spec,
        out_shape=jax.ShapeDtypeStruct((rows, d), F32),
        input_output_aliases={3: 0},
        compiler_params=_cparams(("arbitrary",)),
        name="moe_dispatch",
    )(d1, d2, x, xs0)


def _combine_kernel(d1_ref, d2_ref, ys_hbm, x_ref, meta_ref, g_ref, b_ref, o_ref, buf_ref, sem, *, tm):
    base = pl.program_id(0) * tm

    def issue(t, carry):
        row = base + t
        _row_copy(ys_hbm, buf_ref.at[0], sem, d1_ref[row], t).start()
        _row_copy(ys_hbm, buf_ref.at[1], sem, d2_ref[row], t).start()
        return carry

    lax.fori_loop(0, tm, issue, 0)

    def drain(t, carry):
        _row_copy(ys_hbm, buf_ref.at[0], sem, 0, 0).wait()
        _row_copy(ys_hbm, buf_ref.at[1], sem, 0, 0).wait()
        return carry

    lax.fori_loop(0, tm, drain, 0)
    meta = meta_ref[...]
    lane = lax.broadcasted_iota(jnp.int32, meta.shape, 1)
    w1 = jnp.sum(jnp.where(lane == 4, meta, 0.0), axis=-1, keepdims=True)
    w2 = jnp.sum(jnp.where(lane == 5, meta, 0.0), axis=-1, keepdims=True)
    f = w1 * buf_ref[0] + w2 * buf_ref[1]
    o_ref[...] = _layer_norm(ALPHA * x_ref[...] + f, g_ref[...], b_ref[...])


def _combine(ys, x, meta, d1, d2, g, b):
    n, d = x.shape
    tm = min(MOVE_TM, n)
    grid_spec = pltpu.PrefetchScalarGridSpec(
        num_scalar_prefetch=2,
        grid=(n // tm,),
        in_specs=[pl.BlockSpec(memory_space=pl.ANY),
                  pl.BlockSpec((tm, d), lambda i, a, c: (i, 0)),
                  pl.BlockSpec((tm, LANES), lambda i, a, c: (i, 0)),
                  pl.BlockSpec((1, d), lambda i, a, c: (0, 0)),
                  pl.BlockSpec((1, d), lambda i, a, c: (0, 0))],
        out_specs=pl.BlockSpec((tm, d), lambda i, a, c: (i, 0)),
        scratch_shapes=[pltpu.VMEM((2, tm, d), F32), pltpu.SemaphoreType.DMA(())],
    )
    return pl.pallas_call(
        functools.partial(_combine_kernel, tm=tm),
        grid_spec=grid_spec,
        out_shape=jax.ShapeDtypeStruct((n, d), F32),
        compiler_params=_cparams(("arbitrary",)),
        name="moe_combine",
    )(d1, d2, ys, x, meta, g, b)


def _moe(x, w_router, w_gate, w_up, w_down, g, b):
    n, d = x.shape
    tm = min(FFN_TM, n)
    meta, cnt = _router(x, w_router)
    i1 = meta[:, 0].astype(jnp.int32)
    i2 = meta[:, 1].astype(jnp.int32)
    counts = cnt[0, :N_EXPERTS].astype(jnp.int32)
    tiles = (counts + tm - 1) // tm
    tile_end = jnp.cumsum(tiles)
    group_start = (tile_end - tiles) * tm
    d1 = group_start[i1] + meta[:, 2].astype(jnp.int32)
    d2 = group_start[i2] + meta[:, 3].astype(jnp.int32)
    max_tiles = (2 * n) // tm + N_EXPERTS
    eid = jnp.minimum(jnp.searchsorted(tile_end, jnp.arange(max_tiles, dtype=jnp.int32), side="right"),
                      N_EXPERTS - 1).astype(jnp.int32)
    nact = tile_end[-1:].astype(jnp.int32)
    xs = _dispatch(x, d1, d2, max_tiles * tm)
    ys = _ffn(xs, eid, nact, w_gate, w_up, w_down, g, b, fuse_ln=False)
    return _combine(ys, x, meta, d1, d2, g, b)


def _lambda_init(layer_idx):
    return 0.8 - 0.6 * math.exp(-0.3 * layer_idx)


def _static_tables(s):
    c = C_CHUNK
    ang = jnp.repeat(1.0 / (10000.0 ** jnp.linspace(0.0, 1.0, C_QK_DIM // 2, dtype=F32)), 2)
    ang = jnp.arange(s, dtype=F32)[:, None] * ang[None, :]
    sign = jnp.where(jnp.arange(C_QK_DIM) % 2 == 0, -1.0, 1.0).astype(F32)
    sin_t = jnp.tile(jnp.sin(ang) * sign[None, :], (1, 2))
    cos_t = jnp.tile(jnp.cos(ang), (1, 2))
    log_g = jnp.log(1.0 - jnp.exp2(-5.0 - jnp.arange(C_HEADS, dtype=F32)))
    pos = jnp.arange(c)
    rel = (pos[:, None] - pos[None, :]).astype(F32)
    decay = jnp.where((rel >= 0)[None], jnp.exp(jnp.maximum(rel, 0.0)[None] * log_g[:, None, None]), 0.0)
    zeta = jnp.exp((c - 1 - pos).astype(F32)[:, None] * log_g[None, :])
    xi = jnp.exp((pos + 1).astype(F32)[:, None] * log_g[None, :])
    per_pair = lambda t: jnp.repeat(t.T.reshape(C_HEADS // 2, 2, c), C_QK_DIM, axis=1).transpose(0, 2, 1)
    gch = jnp.broadcast_to(jnp.exp(c * log_g)[:, None, None], (C_HEADS, 1, LANES))
    t = A_BLOCK
    qa = np.arange(t)[:, None]
    ka = np.arange(t)[None, :]
    diag = jnp.where(qa - ka >= 0, _rel_bucket(jnp.asarray(qa - ka)), REL_BUCKETS)
    prev = _rel_bucket(jnp.asarray(qa - ka + t))
    far = jnp.full((t, t), REL_BUCKETS - 1, jnp.int32)
    bkt_a = jnp.concatenate([diag, diag, prev, prev, far, far], axis=0).astype(jnp.int32)
    w = B_WINDOW
    dist = np.arange(w)[:, None] + w - np.arange(2 * w)[None, :]
    band = (dist >= 0) & (dist < w)
    bkt_b = jnp.where(band, _rel_bucket(jnp.asarray(dist)), REL_BUCKETS).astype(jnp.int32)
    return sin_t, cos_t, decay, per_pair(zeta), per_pair(xi), gch, bkt_a, bkt_b


def kernel(x, w_in, rel_bias, a_lambda, a_subln_g, b_sinks, w_out, ln_mix_g, ln_mix_b, ln_ffn_g, ln_ffn_b,
           dense_w_gate, dense_w_up, dense_w_down, moe_router, moe_w_gate, moe_w_up, moe_w_down):
    bsz, s, d = x.shape
    n = bsz * s
    sin_t, cos_t, decay, zeta_t, xi_t, gch, bkt_a, bkt_b = _static_tables(s)
    tab_t = rel_bias.astype(F32).T
    bias_a = _bias_lookup(tab_t[:A_HEADS], bkt_a).reshape(A_HEADS, 3, 2 * A_BLOCK, A_BLOCK)
    bias_b = _bias_lookup(tab_t[A_HEADS:], bkt_b)
    xf = x.reshape(n, d).astype(F32)
    dense_tiles = n // min(FFN_TM, n)
    for l in range(DEPTH):
        w_in_l = w_in[l][:, _PERM].astype(BF16)
        proj = _in_proj(xf, w_in_l).reshape(bsz, s, PROJ_WIDTH)
        ya = _attn_a(proj, bias_a, a_lambda[l].astype(F32), a_subln_g[l].astype(F32).reshape(1, A_V_DIM),
                     _lambda_init(l))
        yb = _swa(proj, bias_b, b_sinks[l].astype(F32))
        yc = _retention(proj, sin_t, cos_t, decay, zeta_t, xi_t, gch)
        g_mix = ln_mix_g[l].astype(F32).reshape(1, d)
        b_mix = ln_mix_b[l].astype(F32).reshape(1, d)
        xf = _out_proj_ln(ya.reshape(n, A_WIDTH), yb.reshape(n, B_WIDTH), yc.reshape(n, C_WIDTH),
                          w_out[l].astype(BF16), xf, g_mix, b_mix)
        g_ffn = ln_ffn_g[l].astype(F32).reshape(1, d)
        b_ffn = ln_ffn_b[l].astype(F32).reshape(1, d)
        j = l // 2
        if l % 2 == 0:
            xf = _ffn(xf, jnp.zeros((dense_tiles,), jnp.int32), jnp.full((1,), dense_tiles, jnp.int32),
                      dense_w_gate[j][None].astype(BF16), dense_w_up[j][None].astype(BF16),
                      dense_w_down[j][None].astype(BF16), g_ffn, b_ffn, fuse_ln=True)
        else:
            xf = _moe(xf, moe_router[j], moe_w_gate[j].astype(BF16), moe_w_up[j].astype(BF16),
                      moe_w_down[j].astype(BF16), g_ffn, b_ffn)
    return xf.reshape(bsz, s, d).astype(x.dtype)
```

```python
import functools
import math

import jax
import jax.numpy as jnp
import numpy as np
from jax import lax
from jax.experimental import pallas as pl
from jax.experimental.pallas import tpu as pltpu

F32 = jnp.float32
BF16 = jnp.bfloat16

D_MODEL = 2048
DEPTH = 2
A_HEADS = 6
A_QK_DIM = 64
A_V_DIM = 128
B_HEADS = 12
B_KV_HEADS = 3
B_HEAD_DIM = 64
B_WINDOW = 128
C_HEADS = 4
C_QK_DIM = 64
C_V_DIM = 128
C_CHUNK = 128
A_WIDTH = A_HEADS * A_V_DIM
B_WIDTH = B_HEADS * B_HEAD_DIM
C_WIDTH = C_HEADS * C_V_DIM
REL_BUCKETS = 32
REL_MAX_DIST = 128
D_FF = 5632
N_EXPERTS = 8
ALPHA = (2.0 * DEPTH) ** 0.25
LN_EPS = 1e-5
NEG = -1e30

LANES = 128
VMEM_LIMIT = 56 * 1024 * 1024

_REF_SIZES = [768, 768, 768, 768, 192, 192, 256, 256, 512, 512]
_REF_OFF = [int(v) for v in np.concatenate([[0], np.cumsum(_REF_SIZES)[:-1]])]
PROJ_WIDTH = int(sum(_REF_SIZES))
OFF_AQ, OFF_AK, OFF_AV, OFF_BQ, OFF_CV, OFF_CG, OFF_CQ, OFF_CK, OFF_BKV = (
    0, 768, 1536, 2304, 3072, 3584, 4096, 4352, 4608)


def _proj_perm():
    aq, ak, av, bq, bk, bv, cq, ck, cv, cg = [np.arange(o, o + s) for o, s in zip(_REF_OFF, _REF_SIZES)]
    bkv = np.concatenate([np.concatenate([bk[g * 64:(g + 1) * 64], bv[g * 64:(g + 1) * 64]])
                          for g in range(B_KV_HEADS)])
    perm = np.concatenate([aq, ak, av, bq, cv, cg, cq, ck, bkv])
    assert perm.shape[0] == PROJ_WIDTH
    return perm


_PERM = _proj_perm()


def _cparams(sem):
    return pltpu.CompilerParams(dimension_semantics=sem, vmem_limit_bytes=VMEM_LIMIT)


def _layer_norm(z, g, b):
    mu = jnp.mean(z, axis=-1, keepdims=True)
    zc = z - mu
    var = jnp.mean(zc * zc, axis=-1, keepdims=True)
    return zc * lax.rsqrt(var + LN_EPS) * g + b


def _silu(x):
    return x / (1.0 + jnp.exp(-x))


def _dot_nt(a, b):
    return lax.dot_general(a, b, (((1,), (1,)), ((), ())), preferred_element_type=F32)


def _dot_tn(a, b):
    return lax.dot_general(a, b, (((0,), (0,)), ((), ())), preferred_element_type=F32)


def _bias_kernel(tab_ref, bkt_ref, o_ref):
    h = pl.program_id(0)
    bkt = bkt_ref[...]
    acc = jnp.full(bkt.shape, NEG, F32)
    for b in range(REL_BUCKETS):
        acc = jnp.where(bkt == b, tab_ref[h, b], acc)
    o_ref[0] = acc


def _bias_lookup(tab_t, bkt):
    nh = tab_t.shape[0]
    r, c = bkt.shape
    return pl.pallas_call(
        _bias_kernel,
        grid=(nh,),
        in_specs=[pl.BlockSpec(memory_space=pltpu.SMEM),
                  pl.BlockSpec((r, c), lambda h: (0, 0))],
        out_specs=pl.BlockSpec((1, r, c), lambda h: (h, 0, 0)),
        out_shape=jax.ShapeDtypeStruct((nh, r, c), F32),
        compiler_params=_cparams(("arbitrary",)),
        name="bias_lookup",
    )(tab_t, bkt)


def _rel_bucket(dist):
    max_exact = REL_BUCKETS // 2
    d = jnp.maximum(dist, 0)
    ratio = jnp.maximum(d, 1).astype(F32) / max_exact
    large = max_exact + (jnp.log(ratio) / math.log(REL_MAX_DIST / max_exact)
                         * (REL_BUCKETS - max_exact)).astype(jnp.int32)
    large = jnp.minimum(large, REL_BUCKETS - 1)
    return jnp.where(d < max_exact, d, large)


def _in_proj_kernel(x_ref, w_ref, o_ref, xb_ref):
    @pl.when(pl.program_id(1) == 0)
    def _():
        xb_ref[...] = x_ref[...].astype(BF16)

    o_ref[...] = jnp.dot(xb_ref[...], w_ref[...], preferred_element_type=F32).astype(BF16)


def _in_proj(x, w):
    n, d = x.shape
    p = w.shape[1]
    tm = min(1024, n)
    tn = p // 3
    return pl.pallas_call(
        _in_proj_kernel,
        grid=(n // tm, p // tn),
        in_specs=[pl.BlockSpec((tm, d), lambda i, j: (i, 0)),
                  pl.BlockSpec((d, tn), lambda i, j: (0, j))],
        out_specs=pl.BlockSpec((tm, tn), lambda i, j: (i, j)),
        out_shape=jax.ShapeDtypeStruct((n, p), BF16),
        scratch_shapes=[pltpu.VMEM((tm, d), BF16)],
        compiler_params=_cparams(("parallel", "arbitrary")),
        name="in_proj",
    )(x, w)


A_BLOCK = 256
A_SUB = 128
A_QROWS = 512
A_BIAS_TYPES = 5
LOG2E = math.log2(math.e)


def _attn_a_kernel(q_ref, k_ref, v_ref, bias_ref, lam_ref, g_ref, o_ref, *, lam_init):
    t = A_BLOCK
    nch = A_QROWS // A_SUB
    per_key_block = t // A_SUB
    i = pl.program_id(2)
    lane = lax.broadcasted_iota(jnp.int32, (A_SUB, LANES), 1)
    q_t = []
    for c in range(nch):
        q = q_ref[0, c * A_SUB:(c + 1) * A_SUB, :].astype(F32) * (A_QK_DIM ** -0.5 * LOG2E)
        qq = jnp.concatenate([jnp.where(lane < A_QK_DIM, q, 0.0), jnp.where(lane >= A_QK_DIM, q, 0.0)], axis=0)
        q_t.append(qq.T.astype(BF16))
    first_block = i * (nch // per_key_block)

    def step(j, carry):
        start = pl.multiple_of(j * t, t)
        kj = k_ref[0, pl.ds(start, t), :]
        vj = v_ref[0, pl.ds(start, t), :]
        types = []
        for c in range(nch):
            back = first_block + c // per_key_block - j
            near = (0, 2) if c % per_key_block == 0 else (1, 3)
            types.append(jnp.where(back == 0, near[0], jnp.where(back == 1, near[1], jnp.where(back < 0, 4, 3))))
        scores = [jnp.dot(kj, q_t[c], preferred_element_type=F32) + bias_ref[0, types[c]]
                  for c in range(nch)]
        soft = []
        for s, (m_old, l_old, _) in zip(scores, carry):
            m_new = jnp.maximum(m_old, jnp.max(s, axis=0, keepdims=True))
            p = jnp.exp2(s - m_new)
            alpha = jnp.exp2(m_old - m_new)
            soft.append((m_new, alpha * l_old + jnp.sum(p, axis=0, keepdims=True), alpha, p.astype(BF16)))
        return tuple((m_new, l_new, alpha * acc + _dot_tn(vj, p))
                     for (m_new, l_new, alpha, p), (_, _, acc) in zip(soft, carry))

    init = tuple((jnp.full((1, 2 * A_SUB), NEG, F32), jnp.zeros((1, 2 * A_SUB), F32),
                  jnp.zeros((A_V_DIM, 2 * A_SUB), F32)) for _ in range(nch))
    fin = lax.fori_loop(0, first_block + nch // per_key_block, step, init)
    lp = lam_ref[...]
    lam = (jnp.exp(jnp.sum(lp[0:1] * lp[1:2], axis=-1, keepdims=True))
           - jnp.exp(jnp.sum(lp[2:3] * lp[3:4], axis=-1, keepdims=True)) + lam_init)
    for c, (_, l_fin, acc) in enumerate(fin):
        o_all = acc / l_fin
        o = (o_all[:, :A_SUB] - lam * o_all[:, A_SUB:]).T
        o = o * lax.rsqrt(jnp.mean(o * o, axis=-1, keepdims=True) + LN_EPS) * g_ref[...]
        o_ref[0, c * A_SUB:(c + 1) * A_SUB, :] = (o * (1.0 - lam_init)).astype(BF16)


def _attn_a(proj, bias_a, lam_params, subln_g, lam_init):
    bsz, s, _ = proj.shape
    t = A_QROWS
    kb, vb = OFF_AK // LANES, OFF_AV // LANES
    return pl.pallas_call(
        functools.partial(_attn_a_kernel, lam_init=lam_init),
        grid=(bsz, A_HEADS, s // t),
        in_specs=[pl.BlockSpec((1, t, LANES), lambda b, h, i: (b, i, h)),
                  pl.BlockSpec((1, s, LANES), lambda b, h, i: (b, 0, kb + h)),
                  pl.BlockSpec((1, s, LANES), lambda b, h, i: (b, 0, vb + h)),
                  pl.BlockSpec((1, A_BIAS_TYPES, A_BLOCK, 2 * A_SUB), lambda b, h, i: (h, 0, 0, 0)),
                  pl.BlockSpec((4, A_QK_DIM), lambda b, h, i: (0, 0)),
                  pl.BlockSpec((1, A_V_DIM), lambda b, h, i: (0, 0))],
        out_specs=pl.BlockSpec((1, t, LANES), lambda b, h, i: (b, i, h)),
        out_shape=jax.ShapeDtypeStruct((bsz, s, A_WIDTH), BF16),
        compiler_params=_cparams(("parallel", "parallel", "arbitrary")),
        name="attn_a",
    )(proj, proj, proj, bias_a, lam_params, subln_g)


def _swa_kernel(sink_ref, q_ref, kvp_ref, kvc_ref, bias_ref, o_ref):
    w = B_WINDOW
    n = pl.program_id(1)
    lane = lax.broadcasted_iota(jnp.int32, (w, LANES), 1)
    col = lax.broadcasted_iota(jnp.int32, (w, 2 * w), 1)
    valid = jnp.logical_or(col >= w, n > 0)
    group = B_HEADS // B_KV_HEADS
    for g in range(B_KV_HEADS):
        kv = jnp.concatenate([kvp_ref[0, :, g * LANES:(g + 1) * LANES],
                              kvc_ref[0, :, g * LANES:(g + 1) * LANES]], axis=0)
        for pair in range(group // 2):
            blk = g * (group // 2) + pair
            qblk = q_ref[0, :, blk * LANES:(blk + 1) * LANES].astype(F32) * (B_HEAD_DIM ** -0.5)
            outs = []
            for half in range(2):
                hq = 2 * blk + half
                qh = qblk if half == 0 else pltpu.roll(qblk, B_HEAD_DIM, 1)
                qh = jnp.where(lane < B_HEAD_DIM, qh, 0.0).astype(BF16)
                s = _dot_nt(qh, kv) + bias_ref[hq]
                s = jnp.where(valid, s, NEG)
                sink = sink_ref[hq]
                m = jnp.maximum(jnp.max(s, axis=-1, keepdims=True), sink)
                e = jnp.exp(s - m)
                den = jnp.sum(e, axis=-1, keepdims=True) + jnp.exp(sink - m)
                o = jnp.dot(e.astype(BF16), kv, preferred_element_type=F32)
                outs.append(o / den)
            ob = jnp.where(lane < B_HEAD_DIM, pltpu.roll(outs[0], B_HEAD_DIM, 1), outs[1])
            o_ref[0, :, blk * LANES:(blk + 1) * LANES] = ob.astype(BF16)


def _swa(proj, bias_b, sinks):
    bsz, s, _ = proj.shape
    w = B_WINDOW
    kvw = 2 * B_KV_HEADS * B_HEAD_DIM
    qb, kvb = OFF_BQ // B_WIDTH, OFF_BKV // kvw
    return pl.pallas_call(
        _swa_kernel,
        grid=(bsz, s // w),
        in_specs=[pl.BlockSpec(memory_space=pltpu.SMEM),
                  pl.BlockSpec((1, w, B_WIDTH), lambda b, n: (b, n, qb)),
                  pl.BlockSpec((1, w, kvw), lambda b, n: (b, jnp.maximum(n - 1, 0), kvb)),
                  pl.BlockSpec((1, w, kvw), lambda b, n: (b, n, kvb)),
                  pl.BlockSpec((B_HEADS, w, 2 * w), lambda b, n: (0, 0, 0))],
        out_specs=pl.BlockSpec((1, w, B_WIDTH), lambda b, n: (b, n, 0)),
        out_shape=jax.ShapeDtypeStruct((bsz, s, B_WIDTH), BF16),
        compiler_params=_cparams(("parallel", "arbitrary")),
        name="swa",
    )(sinks, proj, proj, proj, bias_b)


def _ret_kernel(q_ref, k_ref, v_ref, g_ref, sin_ref, cos_ref, decay_ref, zeta_ref, xi_ref, gch_ref, o_ref,
                state_ref):
    c = C_CHUNK

    @pl.when(pl.program_id(1) == 0)
    def _():
        state_ref[...] = jnp.zeros_like(state_ref)

    lane = lax.broadcasted_iota(jnp.int32, (c, LANES), 1)
    even = (lane & 1) == 0
    sn = sin_ref[...]
    cs = cos_ref[...]

    def rope(x):
        swapped = jnp.where(even, pltpu.roll(x, LANES - 1, 1), pltpu.roll(x, 1, 1))
        return x * cs + swapped * sn

    for pair in range(C_HEADS // 2):
        q = rope(q_ref[0, :, pair * LANES:(pair + 1) * LANES].astype(F32))
        k = rope(k_ref[0, :, pair * LANES:(pair + 1) * LANES].astype(F32) * (C_QK_DIM ** -0.5))
        kb = k.astype(BF16)
        kz = (k * zeta_ref[pair]).astype(BF16)
        qx = q * xi_ref[pair]
        for half in range(2):
            h = 2 * pair + half
            in_head = (lane < C_QK_DIM) if half == 0 else (lane >= C_QK_DIM)
            qm = jnp.where(in_head, q, 0.0).astype(BF16)
            qxm = jnp.where(in_head, qx, 0.0).astype(BF16)
            vh = v_ref[0, :, h * LANES:(h + 1) * LANES]
            inner = _dot_nt(qm, kb) * decay_ref[h]
            st = state_ref[h]
            o = (jnp.dot(inner.astype(BF16), vh, preferred_element_type=F32)
                 + jnp.dot(qxm, st.astype(BF16), preferred_element_type=F32))
            state_ref[h] = st * gch_ref[h] + _dot_tn(kz, vh)
            mu = jnp.mean(o, axis=-1, keepdims=True)
            oc = o - mu
            o = oc * lax.rsqrt(jnp.mean(oc * oc, axis=-1, keepdims=True) + LN_EPS)
            gate = g_ref[0, :, h * LANES:(h + 1) * LANES].astype(F32)
            o_ref[0, :, h * LANES:(h + 1) * LANES] = (_silu(gate) * o).astype(BF16)


def _retention(proj, sin_t, cos_t, decay, zeta_t, xi_t, gch):
    bsz, s, _ = proj.shape
    c = C_CHUNK
    qw = C_HEADS * C_QK_DIM
    return pl.pallas_call(
        _ret_kernel,
        grid=(bsz, s // c),
        in_specs=[pl.BlockSpec((1, c, qw), lambda b, n: (b, n, OFF_CQ // qw)),
                  pl.BlockSpec((1, c, qw), lambda b, n: (b, n, OFF_CK // qw)),
                  pl.BlockSpec((1, c, C_WIDTH), lambda b, n: (b, n, OFF_CV // C_WIDTH)),
                  pl.BlockSpec((1, c, C_WIDTH), lambda b, n: (b, n, OFF_CG // C_WIDTH)),
                  pl.BlockSpec((c, LANES), lambda b, n: (n, 0)),
                  pl.BlockSpec((c, LANES), lambda b, n: (n, 0)),
                  pl.BlockSpec((C_HEADS, c, c), lambda b, n: (0, 0, 0)),
                  pl.BlockSpec((C_HEADS // 2, c, LANES), lambda b, n: (0, 0, 0)),
                  pl.BlockSpec((C_HEADS // 2, c, LANES), lambda b, n: (0, 0, 0)),
                  pl.BlockSpec((C_HEADS, 1, LANES), lambda b, n: (0, 0, 0))],
        out_specs=pl.BlockSpec((1, c, C_WIDTH), lambda b, n: (b, n, 0)),
        out_shape=jax.ShapeDtypeStruct((bsz, s, C_WIDTH), BF16),
        scratch_shapes=[pltpu.VMEM((C_HEADS, LANES, C_V_DIM), F32)],
        compiler_params=_cparams(("parallel", "arbitrary")),
        name="retention",
    )(proj, proj, proj, proj, sin_t, cos_t, decay, zeta_t, xi_t, gch)


def _out_proj_kernel(ya_ref, yb_ref, yc_ref, wa_ref, wb_ref, wc_ref, x_ref, g_ref, b_ref, o_ref):
    mix = (jnp.dot(ya_ref[...], wa_ref[...], preferred_element_type=F32)
           + jnp.dot(yb_ref[...], wb_ref[...], preferred_element_type=F32)
           + jnp.dot(yc_ref[...], wc_ref[...], preferred_element_type=F32))
    o_ref[...] = _layer_norm(ALPHA * x_ref[...] + mix, g_ref[...], b_ref[...])


def _out_proj_ln(ya, yb, yc, w, x, g, b):
    n, d = x.shape
    tm = min(256, n)
    row = lambda i: (i, 0)
    fixed = lambda i: (0, 0)
    return pl.pallas_call(
        _out_proj_kernel,
        grid=(n // tm,),
        in_specs=[pl.BlockSpec((tm, A_WIDTH), row),
                  pl.BlockSpec((tm, B_WIDTH), row),
                  pl.BlockSpec((tm, C_WIDTH), row),
                  pl.BlockSpec((A_WIDTH, d), lambda i: (0, 0)),
                  pl.BlockSpec((B_WIDTH, d), lambda i: (1, 0)),
                  pl.BlockSpec((C_WIDTH, d), lambda i: ((A_WIDTH + B_WIDTH) // C_WIDTH, 0)),
                  pl.BlockSpec((tm, d), row),
                  pl.BlockSpec((1, d), fixed),
                  pl.BlockSpec((1, d), fixed)],
        out_specs=pl.BlockSpec((tm, d), row),
        out_shape=jax.ShapeDtypeStruct((n, d), F32),
        compiler_params=_cparams(("parallel",)),
        name="out_proj_ln",
    )(ya, yb, yc, w, w, w, x, g, b)


FFN_TM = 512
FFN_TF = 512


def _ffn_kernel(eid_ref, nact_ref, x_ref, wg_ref, wu_ref, wd_ref, g_ref, b_ref, o_ref, acc_ref, *cast_ref,
                fuse_ln):
    i = pl.program_id(0)
    f = pl.program_id(1)
    active = i < nact_ref[0]
    xb_ref = cast_ref[0] if cast_ref else x_ref

    if cast_ref:
        @pl.when(jnp.logical_and(active, f == 0))
        def _():
            xb_ref[...] = x_ref[...].astype(BF16)

    @pl.when(active)
    def _():
        xb = xb_ref[...]
        hg = jnp.dot(xb, wg_ref[0], preferred_element_type=F32)
        hu = jnp.dot(xb, wu_ref[0], preferred_element_type=F32)
        hidden = (_silu(hg) * hu).astype(BF16)
        contrib = jnp.dot(hidden, wd_ref[0], preferred_element_type=F32)

        @pl.when(f == 0)
        def _():
            acc_ref[...] = contrib

        @pl.when(f > 0)
        def _():
            acc_ref[...] += contrib

    last = f == pl.num_programs(1) - 1

    @pl.when(jnp.logical_and(active, last))
    def _():
        if fuse_ln:
            o_ref[...] = _layer_norm(ALPHA * x_ref[...] + acc_ref[...], g_ref[...], b_ref[...])
        else:
            o_ref[...] = acc_ref[...]

    @pl.when(jnp.logical_and(jnp.logical_not(active), last))
    def _():
        o_ref[...] = jnp.zeros_like(o_ref)


def _ffn(x, eid, nact, w_gate, w_up, w_down, g, b, fuse_ln):
    n, d = x.shape
    ff = w_gate.shape[2]
    tm = min(FFN_TM, n)
    tf = FFN_TF
    nf = ff // tf
    assert fuse_ln == (x.dtype == F32)
    scratch = [pltpu.VMEM((tm, d), F32)] + ([pltpu.VMEM((tm, d), BF16)] if x.dtype == F32 else [])

    def fidx(i, f, eid_ref, nact_ref):
        return jnp.where(i < nact_ref[0], f, nf - 1)

    grid_spec = pltpu.PrefetchScalarGridSpec(
        num_scalar_prefetch=2,
        grid=(n // tm, nf),
        in_specs=[pl.BlockSpec((tm, d), lambda i, f, e, a: (i, 0)),
                  pl.BlockSpec((1, d, tf), lambda i, f, e, a: (e[i], 0, fidx(i, f, e, a))),
                  pl.BlockSpec((1, d, tf), lambda i, f, e, a: (e[i], 0, fidx(i, f, e, a))),
                  pl.BlockSpec((1, tf, d), lambda i, f, e, a: (e[i], fidx(i, f, e, a), 0)),
                  pl.BlockSpec((1, d), lambda i, f, e, a: (0, 0)),
                  pl.BlockSpec((1, d), lambda i, f, e, a: (0, 0))],
        out_specs=pl.BlockSpec((tm, d), lambda i, f, e, a: (i, 0)),
        scratch_shapes=scratch,
    )
    return pl.pallas_call(
        functools.partial(_ffn_kernel, fuse_ln=fuse_ln),
        grid_spec=grid_spec,
        out_shape=jax.ShapeDtypeStruct((n, d), F32),
        compiler_params=_cparams(("parallel", "arbitrary")),
        name="ffn_ln" if fuse_ln else "ffn_grouped",
    )(eid, nact, x, w_gate, w_up, w_down, g, b)


ROUTER_TM = 512


def _router_kernel(x_ref, wr_ref, tri_ref, meta_ref, cnt_ref, carry_ref):
    @pl.when(pl.program_id(0) == 0)
    def _():
        carry_ref[...] = jnp.zeros_like(carry_ref)

    logits = jnp.dot(x_ref[...], wr_ref[...], preferred_element_type=F32, precision=lax.Precision.HIGHEST)
    lane = lax.broadcasted_iota(jnp.int32, logits.shape, 1)
    logits = jnp.where(lane < N_EXPERTS, logits, -jnp.inf)
    m1 = jnp.max(logits, axis=-1, keepdims=True)
    i1 = jnp.min(jnp.where(logits == m1, lane, LANES), axis=-1, keepdims=True)
    rest = jnp.where(lane == i1, -jnp.inf, logits)
    m2 = jnp.max(rest, axis=-1, keepdims=True)
    i2 = jnp.min(jnp.where(rest == m2, lane, LANES), axis=-1, keepdims=True)
    e2 = jnp.exp(m2 - m1)
    w1 = 1.0 / (1.0 + e2)
    w2 = e2 * w1
    hit1 = lane == i1
    hit2 = lane == i2
    onehot = jnp.where(jnp.logical_or(hit1, hit2), 1.0, 0.0)
    before = jnp.dot(tri_ref[...], onehot.astype(BF16), preferred_element_type=F32) + carry_ref[...]
    r1 = jnp.sum(jnp.where(hit1, before, 0.0), axis=-1, keepdims=True)
    r2 = jnp.sum(jnp.where(hit2, before, 0.0), axis=-1, keepdims=True)
    carry_ref[...] = carry_ref[...] + jnp.sum(onehot, axis=0, keepdims=True)
    cnt_ref[...] = carry_ref[...]
    meta = jnp.where(lane == 0, i1.astype(F32), 0.0)
    meta = jnp.where(lane == 1, i2.astype(F32), meta)
    meta = jnp.where(lane == 2, r1, meta)
    meta = jnp.where(lane == 3, r2, meta)
    meta = jnp.where(lane == 4, w1, meta)
    meta = jnp.where(lane == 5, w2, meta)
    meta_ref[...] = meta


def _router(x, w_router):
    n, d = x.shape
    tm = min(ROUTER_TM, n)
    wr = jnp.zeros((d, LANES), F32).at[:, :N_EXPERTS].set(w_router.astype(F32))
    tri = jnp.asarray(np.tril(np.ones((tm, tm), np.float32), -1), BF16)
    return pl.pallas_call(
        _router_kernel,
        grid=(n // tm,),
        in_specs=[pl.BlockSpec((tm, d), lambda i: (i, 0)),
                  pl.BlockSpec((d, LANES), lambda i: (0, 0)),
                  pl.BlockSpec((tm, tm), lambda i: (0, 0))],
        out_specs=[pl.BlockSpec((tm, LANES), lambda i: (i, 0)),
                   pl.BlockSpec((1, LANES), lambda i: (0, 0))],
        out_shape=[jax.ShapeDtypeStruct((n, LANES), F32), jax.ShapeDtypeStruct((1, LANES), F32)],
        scratch_shapes=[pltpu.VMEM((1, LANES), F32)],
        compiler_params=_cparams(("arbitrary",)),
        name="router",
    )(x, wr, tri)


MOVE_TM = 256
ISSUE_UNROLL = 8


def _row_copy(src, dst, sem, s, t):
    return pltpu.make_async_copy(src.at[pl.ds(s, 1)], dst.at[pl.ds(t, 1)], sem)


def _dispatch_kernel(src_ref, x_hbm, o_ref, buf_ref, sem, *, tm):
    base = pl.program_id(0) * tm

    def issue(t, carry):
        _row_copy(x_hbm, buf_ref, sem, src_ref[base + t], t).start()
        return carry

    lax.fori_loop(0, tm, issue, 0, unroll=ISSUE_UNROLL)

    def drain(t, carry):
        _row_copy(x_hbm, buf_ref, sem, 0, 0).wait()
        return carry

    lax.fori_loop(0, tm, drain, 0, unroll=ISSUE_UNROLL)
    o_ref[...] = buf_ref[...].astype(BF16)


def _dispatch(x, src):
    n, d = x.shape
    rows = src.shape[0]
    tm = min(MOVE_TM, n)
    grid_spec = pltpu.PrefetchScalarGridSpec(
        num_scalar_prefetch=1,
        grid=(rows // tm,),
        in_specs=[pl.BlockSpec(memory_space=pl.ANY)],
        out_specs=pl.BlockSpec((tm, d), lambda i, s: (i, 0)),
        scratch_shapes=[pltpu.VMEM((tm, d), F32), pltpu.SemaphoreType.DMA(())],
    )
    return pl.pallas_call(
        functools.partial(_dispatch_kernel, tm=tm),
        grid_spec=grid_spec,
        out_shape=jax.ShapeDtypeStruct((rows, d), BF16),
        compiler_params=_cparams(("arbitrary",)),
        name="moe_dispatch",
    )(src, x)


def _combine_kernel(d1_ref, d2_ref, ys_hbm, x_ref, meta_ref, g_ref, b_ref, o_ref, buf_ref, sem, *, tm):
    base = pl.program_id(0) * tm

    def issue(t, carry):
        row = base + t
        _row_copy(ys_hbm, buf_ref.at[0], sem, d1_ref[row], t).start()
        _row_copy(ys_hbm, buf_ref.at[1], sem, d2_ref[row], t).start()
        return carry

    lax.fori_loop(0, tm, issue, 0, unroll=ISSUE_UNROLL)

    def drain(t, carry):
        _row_copy(ys_hbm, buf_ref.at[0], sem, 0, 0).wait()
        _row_copy(ys_hbm, buf_ref.at[1], sem, 0, 0).wait()
        return carry

    lax.fori_loop(0, tm, drain, 0, unroll=ISSUE_UNROLL)
    meta = meta_ref[...]
    lane = lax.broadcasted_iota(jnp.int32, meta.shape, 1)
    w1 = jnp.sum(jnp.where(lane == 4, meta, 0.0), axis=-1, keepdims=True)
    w2 = jnp.sum(jnp.where(lane == 5, meta, 0.0), axis=-1, keepdims=True)
    f = w1 * buf_ref[0] + w2 * buf_ref[1]
    o_ref[...] = _layer_norm(ALPHA * x_ref[...] + f, g_ref[...], b_ref[...])


def _combine(ys, x, meta, d1, d2, g, b):
    n, d = x.shape
    tm = min(MOVE_TM, n)
    grid_spec = pltpu.PrefetchScalarGridSpec(
        num_scalar_prefetch=2,
        grid=(n // tm,),
        in_specs=[pl.BlockSpec(memory_space=pl.ANY),
                  pl.BlockSpec((tm, d), lambda i, a, c: (i, 0)),
                  pl.BlockSpec((tm, LANES), lambda i, a, c: (i, 0)),
                  pl.BlockSpec((1, d), lambda i, a, c: (0, 0)),
                  pl.BlockSpec((1, d), lambda i, a, c: (0, 0))],
        out_specs=pl.BlockSpec((tm, d), lambda i, a, c: (i, 0)),
        scratch_shapes=[pltpu.VMEM((2, tm, d), F32), pltpu.SemaphoreType.DMA(())],
    )
    return pl.pallas_call(
        functools.partial(_combine_kernel, tm=tm),
        grid_spec=grid_spec,
        out_shape=jax.ShapeDtypeStruct((n, d), F32),
        compiler_params=_cparams(("arbitrary",)),
        name="moe_combine",
    )(d1, d2, ys, x, meta, g, b)


def _moe(x, w_router, w_gate, w_up, w_down, g, b):
    n, d = x.shape
    tm = min(FFN_TM, n)
    meta, cnt = _router(x, w_router)
    i1 = meta[:, 0].astype(jnp.int32)
    i2 = meta[:, 1].astype(jnp.int32)
    counts = cnt[0, :N_EXPERTS].astype(jnp.int32)
    tiles = (counts + tm - 1) // tm
    tile_end = jnp.cumsum(tiles)
    group_start = (tile_end - tiles) * tm
    experts = jnp.arange(N_EXPERTS, dtype=jnp.int32)[None, :]
    start_of = lambda idx: jnp.sum(jnp.where(idx[:, None] == experts, group_start[None, :], 0), axis=1)
    d1 = start_of(i1) + meta[:, 2].astype(jnp.int32)
    d2 = start_of(i2) + meta[:, 3].astype(jnp.int32)
    max_tiles = (2 * n) // tm + N_EXPERTS
    tile_ids = jnp.arange(max_tiles, dtype=jnp.int32)[:, None]
    eid = jnp.minimum(jnp.sum((tile_ids >= tile_end[None, :]).astype(jnp.int32), axis=1), N_EXPERTS - 1)
    nact = tile_end[-1:].astype(jnp.int32)
    token = jnp.arange(n, dtype=jnp.int32)
    src = jnp.zeros((max_tiles * tm,), jnp.int32).at[jnp.concatenate([d1, d2])].set(
        jnp.concatenate([token, token]), unique_indices=True)
    xs = _dispatch(x, src)
    ys = _ffn(xs, eid, nact, w_gate, w_up, w_down, g, b, fuse_ln=False)
    return _combine(ys, x, meta, d1, d2, g, b)


def _lambda_init(layer_idx):
    return 0.8 - 0.6 * math.exp(-0.3 * layer_idx)


def _static_tables(s):
    c = C_CHUNK
    ang = jnp.repeat(1.0 / (10000.0 ** jnp.linspace(0.0, 1.0, C_QK_DIM // 2, dtype=F32)), 2)
    ang = jnp.arange(s, dtype=F32)[:, None] * ang[None, :]
    sign = jnp.where(jnp.arange(C_QK_DIM) % 2 == 0, -1.0, 1.0).astype(F32)
    sin_t = jnp.tile(jnp.sin(ang) * sign[None, :], (1, 2))
    cos_t = jnp.tile(jnp.cos(ang), (1, 2))
    log_g = jnp.log(1.0 - jnp.exp2(-5.0 - jnp.arange(C_HEADS, dtype=F32)))
    pos = jnp.arange(c)
    rel = (pos[:, None] - pos[None, :]).astype(F32)
    decay = jnp.where((rel >= 0)[None], jnp.exp(jnp.maximum(rel, 0.0)[None] * log_g[:, None, None]), 0.0)
    zeta = jnp.exp((c - 1 - pos).astype(F32)[:, None] * log_g[None, :])
    xi = jnp.exp((pos + 1).astype(F32)[:, None] * log_g[None, :])
    per_pair = lambda t: jnp.repeat(t.T.reshape(C_HEADS // 2, 2, c), C_QK_DIM, axis=1).transpose(0, 2, 1)
    gch = jnp.broadcast_to(jnp.exp(c * log_g)[:, None, None], (C_HEADS, 1, LANES))
    dist_a = np.arange(A_SUB)[None, :] - np.arange(A_BLOCK)[:, None]
    types = [jnp.where(dist_a + off >= 0, _rel_bucket(jnp.asarray(dist_a + off)), REL_BUCKETS)
             for off in (0, A_SUB, A_BLOCK)]
    types.append(jnp.full(dist_a.shape, REL_BUCKETS - 1, jnp.int32))
    types.append(jnp.full(dist_a.shape, REL_BUCKETS, jnp.int32))
    bkt_a = jnp.concatenate([jnp.tile(b, (1, 2)) for b in types], axis=0).astype(jnp.int32)
    w = B_WINDOW
    dist = np.arange(w)[:, None] + w - np.arange(2 * w)[None, :]
    band = (dist >= 0) & (dist < w)
    bkt_b = jnp.where(band, _rel_bucket(jnp.asarray(dist)), REL_BUCKETS).astype(jnp.int32)
    return sin_t, cos_t, decay, per_pair(zeta), per_pair(xi), gch, bkt_a, bkt_b


def kernel(x, w_in, rel_bias, a_lambda, a_subln_g, b_sinks, w_out, ln_mix_g, ln_mix_b, ln_ffn_g, ln_ffn_b,
           dense_w_gate, dense_w_up, dense_w_down, moe_router, moe_w_gate, moe_w_up, moe_w_down):
    bsz, s, d = x.shape
    n = bsz * s
    sin_t, cos_t, decay, zeta_t, xi_t, gch, bkt_a, bkt_b = _static_tables(s)
    tab_t = rel_bias.astype(F32).T
    bias_a = _bias_lookup(tab_t[:A_HEADS] * LOG2E, bkt_a).reshape(A_HEADS, A_BIAS_TYPES, A_BLOCK, 2 * A_SUB)
    bias_b = _bias_lookup(tab_t[A_HEADS:], bkt_b)
    xf = x.reshape(n, d).astype(F32)
    dense_tiles = n // min(FFN_TM, n)
    for l in range(DEPTH):
        w_in_l = w_in[l][:, _PERM].astype(BF16)
        proj = _in_proj(xf, w_in_l).reshape(bsz, s, PROJ_WIDTH)
        ya = _attn_a(proj, bias_a, a_lambda[l].astype(F32), a_subln_g[l].astype(F32).reshape(1, A_V_DIM),
                     _lambda_init(l))
        yb = _swa(proj, bias_b, b_sinks[l].astype(F32))
        yc = _retention(proj, sin_t, cos_t, decay, zeta_t, xi_t, gch)
        g_mix = ln_mix_g[l].astype(F32).reshape(1, d)
        b_mix = ln_mix_b[l].astype(F32).reshape(1, d)
        xf = _out_proj_ln(ya.reshape(n, A_WIDTH), yb.reshape(n, B_WIDTH), yc.reshape(n, C_WIDTH),
                          w_out[l].astype(BF16), xf, g_mix, b_mix)
        g_ffn = ln_ffn_g[l].astype(F32).reshape(1, d)
        b_ffn = ln_ffn_b[l].astype(F32).reshape(1, d)
        j = l // 2
        if l % 2 == 0:
            xf = _ffn(xf, jnp.zeros((dense_tiles,), jnp.int32), jnp.full((1,), dense_tiles, jnp.int32),
                      dense_w_gate[j][None].astype(BF16), dense_w_up[j][None].astype(BF16),
                      dense_w_down[j][None].astype(BF16), g_ffn, b_ffn, fuse_ln=True)
        else:
            xf = _moe(xf, moe_router[j], moe_w_gate[j].astype(BF16), moe_w_up[j].astype(BF16),
                      moe_w_down[j].astype(BF16), g_ffn, b_ffn)
    return xf.reshape(bsz, s, d).astype(x.dtype)
```

```python
import functools
import math

import jax
import jax.numpy as jnp
import numpy as np
from jax import lax
from jax.experimental import pallas as pl
from jax.experimental.pallas import tpu as pltpu

F32 = jnp.float32
BF16 = jnp.bfloat16

D_MODEL = 2048
DEPTH = 2
A_HEADS = 6
A_QK_DIM = 64
A_V_DIM = 128
B_HEADS = 12
B_KV_HEADS = 3
B_HEAD_DIM = 64
B_WINDOW = 128
C_HEADS = 4
C_QK_DIM = 64
C_V_DIM = 128
C_CHUNK = 128
A_WIDTH = A_HEADS * A_V_DIM
B_WIDTH = B_HEADS * B_HEAD_DIM
C_WIDTH = C_HEADS * C_V_DIM
REL_BUCKETS = 32
REL_MAX_DIST = 128
D_FF = 5632
N_EXPERTS = 8
ALPHA = (2.0 * DEPTH) ** 0.25
LN_EPS = 1e-5
NEG = -1e30

LANES = 128
VMEM_LIMIT = 56 * 1024 * 1024

_REF_SIZES = [768, 768, 768, 768, 192, 192, 256, 256, 512, 512]
_REF_OFF = [int(v) for v in np.concatenate([[0], np.cumsum(_REF_SIZES)[:-1]])]
PROJ_WIDTH = int(sum(_REF_SIZES))
OFF_AQ, OFF_AK, OFF_AV, OFF_BQ, OFF_CV, OFF_CG, OFF_CQ, OFF_CK, OFF_BKV = (
    0, 768, 1536, 2304, 3072, 3584, 4096, 4352, 4608)


def _proj_perm():
    aq, ak, av, bq, bk, bv, cq, ck, cv, cg = [np.arange(o, o + s) for o, s in zip(_REF_OFF, _REF_SIZES)]
    bkv = np.concatenate([np.concatenate([bk[g * 64:(g + 1) * 64], bv[g * 64:(g + 1) * 64]])
                          for g in range(B_KV_HEADS)])
    perm = np.concatenate([aq, ak, av, bq, cv, cg, cq, ck, bkv])
    assert perm.shape[0] == PROJ_WIDTH
    return perm


def _perm_runs():
    perm = _proj_perm()
    cuts = np.flatnonzero(np.diff(perm) != 1) + 1
    return [(int(r[0]), int(r[-1]) + 1) for r in np.split(perm, cuts)]


_PERM_RUNS = _perm_runs()


def _cparams(sem):
    return pltpu.CompilerParams(dimension_semantics=sem, vmem_limit_bytes=VMEM_LIMIT)


def _layer_norm(z, g, b):
    mu = jnp.mean(z, axis=-1, keepdims=True)
    zc = z - mu
    var = jnp.mean(zc * zc, axis=-1, keepdims=True)
    return zc * lax.rsqrt(var + LN_EPS) * g + b


def _silu(x):
    return x / (1.0 + jnp.exp(-x))


def _dot_nt(a, b):
    return lax.dot_general(a, b, (((1,), (1,)), ((), ())), preferred_element_type=F32)


def _dot_tn(a, b):
    return lax.dot_general(a, b, (((0,), (0,)), ((), ())), preferred_element_type=F32)


def _bias_kernel(tab_ref, bkt_ref, o_ref):
    h = pl.program_id(0)
    bkt = bkt_ref[...]
    acc = jnp.full(bkt.shape, NEG, F32)
    for b in range(REL_BUCKETS):
        acc = jnp.where(bkt == b, tab_ref[h, b], acc)
    o_ref[0] = acc


def _bias_lookup(tab_t, bkt):
    nh = tab_t.shape[0]
    r, c = bkt.shape
    return pl.pallas_call(
        _bias_kernel,
        grid=(nh,),
        in_specs=[pl.BlockSpec(memory_space=pltpu.SMEM),
                  pl.BlockSpec((r, c), lambda h: (0, 0))],
        out_specs=pl.BlockSpec((1, r, c), lambda h: (h, 0, 0)),
        out_shape=jax.ShapeDtypeStruct((nh, r, c), F32),
        compiler_params=_cparams(("arbitrary",)),
        name="bias_lookup",
    )(tab_t, bkt)


def _rel_bucket(dist):
    max_exact = REL_BUCKETS // 2
    d = jnp.maximum(dist, 0)
    ratio = jnp.maximum(d, 1).astype(F32) / max_exact
    large = max_exact + (jnp.log(ratio) / math.log(REL_MAX_DIST / max_exact)
                         * (REL_BUCKETS - max_exact)).astype(jnp.int32)
    large = jnp.minimum(large, REL_BUCKETS - 1)
    return jnp.where(d < max_exact, d, large)


def _in_proj_kernel(x_ref, w_ref, o_ref, xb_ref):
    @pl.when(pl.program_id(1) == 0)
    def _():
        xb_ref[...] = x_ref[...].astype(BF16)

    o_ref[...] = jnp.dot(xb_ref[...], w_ref[...], preferred_element_type=F32).astype(BF16)


def _in_proj(x, w):
    n, d = x.shape
    p = w.shape[1]
    tm = min(1024, n)
    tn = p // 3
    return pl.pallas_call(
        _in_proj_kernel,
        grid=(n // tm, p // tn),
        in_specs=[pl.BlockSpec((tm, d), lambda i, j: (i, 0)),
                  pl.BlockSpec((d, tn), lambda i, j: (0, j))],
        out_specs=pl.BlockSpec((tm, tn), lambda i, j: (i, j)),
        out_shape=jax.ShapeDtypeStruct((n, p), BF16),
        scratch_shapes=[pltpu.VMEM((tm, d), BF16)],
        compiler_params=_cparams(("parallel", "arbitrary")),
        name="in_proj",
    )(x, w)


A_BLOCK = 256
A_SUB = 128
A_QROWS = 512
A_BIAS_TYPES = 5
LOG2E = math.log2(math.e)


def _attn_a_kernel(q_ref, k_ref, v_ref, bias_ref, lam_ref, g_ref, o_ref, qt_ref, s_ref, p_ref, acc_ref, *,
                   lam_init):
    t = A_BLOCK
    nch = A_QROWS // A_SUB
    per_key_block = t // A_SUB
    i = pl.program_id(2)
    lane = lax.broadcasted_iota(jnp.int32, (A_SUB, LANES), 1)
    for c in range(nch):
        q = q_ref[0, c * A_SUB:(c + 1) * A_SUB, :].astype(F32) * (A_QK_DIM ** -0.5 * LOG2E)
        qq = jnp.concatenate([jnp.where(lane < A_QK_DIM, q, 0.0), jnp.where(lane >= A_QK_DIM, q, 0.0)], axis=0)
        qt_ref[c] = qq.T.astype(BF16)
    first_block = i * (nch // per_key_block)
    n_blocks = first_block + nch // per_key_block

    def scores_into(slot, j):
        kj = k_ref[0, pl.ds(pl.multiple_of(j * t, t), t), :]
        for c in range(nch):
            back = first_block + c // per_key_block - j
            near = (0, 2) if c % per_key_block == 0 else (1, 3)
            kind = jnp.where(back == 0, near[0], jnp.where(back == 1, near[1], jnp.where(back < 0, 4, 3)))
            s_ref[slot, c] = jnp.dot(kj, qt_ref[c], preferred_element_type=F32) + bias_ref[0, kind]

    def values_from(slot, j, alphas):
        vj = v_ref[0, pl.ds(pl.multiple_of(j * t, t), t), :]
        for c in range(nch):
            acc_ref[c] = alphas[c] * acc_ref[c] + _dot_tn(vj, p_ref[slot, c])

    def sub_step(j, cur, carry):
        scores_into(1 - cur, jnp.minimum(j + 1, n_blocks - 1))
        new = []
        for c, (m_old, l_old, _) in enumerate(carry):
            s = s_ref[cur, c]
            m_new = jnp.maximum(m_old, jnp.max(s, axis=0, keepdims=True))
            p = jnp.exp2(s - m_new)
            new.append((m_new, jnp.exp2(m_old - m_new) * l_old + jnp.sum(p, axis=0, keepdims=True),
                        jnp.exp2(m_old - m_new)))
            p_ref[cur, c] = p.astype(BF16)
        values_from(1 - cur, jnp.maximum(j - 1, 0), [a for _, _, a in carry])
        return tuple(new)

    def pair(jj, carry):
        carry = sub_step(2 * jj, 0, carry)
        return sub_step(2 * jj + 1, 1, carry)

    scores_into(0, 0)
    p_ref[1] = jnp.zeros(p_ref.shape[1:], BF16)
    acc_ref[...] = jnp.zeros_like(acc_ref)
    init = tuple((jnp.full((1, 2 * A_SUB), NEG, F32), jnp.zeros((1, 2 * A_SUB), F32),
                  jnp.ones((1, 2 * A_SUB), F32)) for _ in range(nch))
    fin = lax.fori_loop(0, n_blocks // 2, pair, init)
    values_from(1, n_blocks - 1, [a for _, _, a in fin])
    lp = lam_ref[...]
    lam = (jnp.exp(jnp.sum(lp[0:1] * lp[1:2], axis=-1, keepdims=True))
           - jnp.exp(jnp.sum(lp[2:3] * lp[3:4], axis=-1, keepdims=True)) + lam_init)
    for c, (_, l_fin, _) in enumerate(fin):
        o_all = acc_ref[c] / l_fin
        o = (o_all[:, :A_SUB] - lam * o_all[:, A_SUB:]).T
        o = o * lax.rsqrt(jnp.mean(o * o, axis=-1, keepdims=True) + LN_EPS) * g_ref[...]
        o_ref[0, c * A_SUB:(c + 1) * A_SUB, :] = (o * (1.0 - lam_init)).astype(BF16)


def _attn_a(proj, bias_a, lam_params, subln_g, lam_init):
    bsz, s, _ = proj.shape
    t = A_QROWS
    nch = A_QROWS // A_SUB
    kb, vb = OFF_AK // LANES, OFF_AV // LANES
    return pl.pallas_call(
        functools.partial(_attn_a_kernel, lam_init=lam_init),
        grid=(bsz, A_HEADS, s // t),
        in_specs=[pl.BlockSpec((1, t, LANES), lambda b, h, i: (b, i, h)),
                  pl.BlockSpec((1, s, LANES), lambda b, h, i: (b, 0, kb + h)),
                  pl.BlockSpec((1, s, LANES), lambda b, h, i: (b, 0, vb + h)),
                  pl.BlockSpec((1, A_BIAS_TYPES, A_BLOCK, 2 * A_SUB), lambda b, h, i: (h, 0, 0, 0)),
                  pl.BlockSpec((4, A_QK_DIM), lambda b, h, i: (0, 0)),
                  pl.BlockSpec((1, A_V_DIM), lambda b, h, i: (0, 0))],
        out_specs=pl.BlockSpec((1, t, LANES), lambda b, h, i: (b, i, h)),
        out_shape=jax.ShapeDtypeStruct((bsz, s, A_WIDTH), BF16),
        scratch_shapes=[pltpu.VMEM((nch, LANES, 2 * A_SUB), BF16),
                        pltpu.VMEM((2, nch, A_BLOCK, 2 * A_SUB), F32),
                        pltpu.VMEM((2, nch, A_BLOCK, 2 * A_SUB), BF16),
                        pltpu.VMEM((nch, A_V_DIM, 2 * A_SUB), F32)],
        compiler_params=_cparams(("parallel", "parallel", "arbitrary")),
        name="attn_a",
    )(proj, proj, proj, bias_a, lam_params, subln_g)


def _swa_kernel(sink_ref, q_ref, kvp_ref, kvc_ref, bias_ref, o_ref):
    w = B_WINDOW
    n = pl.program_id(1)
    lane = lax.broadcasted_iota(jnp.int32, (w, LANES), 1)
    col = lax.broadcasted_iota(jnp.int32, (w, 2 * w), 1)
    valid = jnp.logical_or(col >= w, n > 0)
    group = B_HEADS // B_KV_HEADS
    for g in range(B_KV_HEADS):
        kv = jnp.concatenate([kvp_ref[0, :, g * LANES:(g + 1) * LANES],
                              kvc_ref[0, :, g * LANES:(g + 1) * LANES]], axis=0)
        for pair in range(group // 2):
            blk = g * (group // 2) + pair
            qblk = q_ref[0, :, blk * LANES:(blk + 1) * LANES].astype(F32) * (B_HEAD_DIM ** -0.5)
            outs = []
            for half in range(2):
                hq = 2 * blk + half
                qh = qblk if half == 0 else pltpu.roll(qblk, B_HEAD_DIM, 1)
                qh = jnp.where(lane < B_HEAD_DIM, qh, 0.0).astype(BF16)
                s = _dot_nt(qh, kv) + bias_ref[hq]
                s = jnp.where(valid, s, NEG)
                sink = sink_ref[hq]
                m = jnp.maximum(jnp.max(s, axis=-1, keepdims=True), sink)
                e = jnp.exp(s - m)
                den = jnp.sum(e, axis=-1, keepdims=True) + jnp.exp(sink - m)
                o = jnp.dot(e.astype(BF16), kv, preferred_element_type=F32)
                outs.append(o / den)
            ob = jnp.where(lane < B_HEAD_DIM, pltpu.roll(outs[0], B_HEAD_DIM, 1), outs[1])
            o_ref[0, :, blk * LANES:(blk + 1) * LANES] = ob.astype(BF16)


def _swa(proj, bias_b, sinks):
    bsz, s, _ = proj.shape
    w = B_WINDOW
    kvw = 2 * B_KV_HEADS * B_HEAD_DIM
    qb, kvb = OFF_BQ // B_WIDTH, OFF_BKV // kvw
    return pl.pallas_call(
        _swa_kernel,
        grid=(bsz, s // w),
        in_specs=[pl.BlockSpec(memory_space=pltpu.SMEM),
                  pl.BlockSpec((1, w, B_WIDTH), lambda b, n: (b, n, qb)),
                  pl.BlockSpec((1, w, kvw), lambda b, n: (b, jnp.maximum(n - 1, 0), kvb)),
                  pl.BlockSpec((1, w, kvw), lambda b, n: (b, n, kvb)),
                  pl.BlockSpec((B_HEADS, w, 2 * w), lambda b, n: (0, 0, 0))],
        out_specs=pl.BlockSpec((1, w, B_WIDTH), lambda b, n: (b, n, 0)),
        out_shape=jax.ShapeDtypeStruct((bsz, s, B_WIDTH), BF16),
        compiler_params=_cparams(("parallel", "arbitrary")),
        name="swa",
    )(sinks, proj, proj, proj, bias_b)


def _ret_kernel(q_ref, k_ref, v_ref, g_ref, sin_ref, cos_ref, decay_ref, zeta_ref, xi_ref, gch_ref, o_ref,
                state_ref):
    c = C_CHUNK

    @pl.when(pl.program_id(1) == 0)
    def _():
        state_ref[...] = jnp.zeros_like(state_ref)

    lane = lax.broadcasted_iota(jnp.int32, (c, LANES), 1)
    even = (lane & 1) == 0
    sn = sin_ref[...]
    cs = cos_ref[...]

    def rope(x):
        swapped = jnp.where(even, pltpu.roll(x, LANES - 1, 1), pltpu.roll(x, 1, 1))
        return x * cs + swapped * sn

    for pair in range(C_HEADS // 2):
        q = rope(q_ref[0, :, pair * LANES:(pair + 1) * LANES].astype(F32))
        k = rope(k_ref[0, :, pair * LANES:(pair + 1) * LANES].astype(F32) * (C_QK_DIM ** -0.5))
        kb = k.astype(BF16)
        kz = (k * zeta_ref[pair]).astype(BF16)
        qx = q * xi_ref[pair]
        for half in range(2):
            h = 2 * pair + half
            in_head = (lane < C_QK_DIM) if half == 0 else (lane >= C_QK_DIM)
            qm = jnp.where(in_head, q, 0.0).astype(BF16)
            qxm = jnp.where(in_head, qx, 0.0).astype(BF16)
            vh = v_ref[0, :, h * LANES:(h + 1) * LANES]
            inner = _dot_nt(qm, kb) * decay_ref[h]
            st = state_ref[h]
            o = (jnp.dot(inner.astype(BF16), vh, preferred_element_type=F32)
                 + jnp.dot(qxm, st.astype(BF16), preferred_element_type=F32))
            state_ref[h] = st * gch_ref[h] + _dot_tn(kz, vh)
            mu = jnp.mean(o, axis=-1, keepdims=True)
            oc = o - mu
            o = oc * lax.rsqrt(jnp.mean(oc * oc, axis=-1, keepdims=True) + LN_EPS)
            gate = g_ref[0, :, h * LANES:(h + 1) * LANES].astype(F32)
            o_ref[0, :, h * LANES:(h + 1) * LANES] = (_silu(gate) * o).astype(BF16)


def _retention(proj, sin_t, cos_t, decay, zeta_t, xi_t, gch):
    bsz, s, _ = proj.shape
    c = C_CHUNK
    qw = C_HEADS * C_QK_DIM
    return pl.pallas_call(
        _ret_kernel,
        grid=(bsz, s // c),
        in_specs=[pl.BlockSpec((1, c, qw), lambda b, n: (b, n, OFF_CQ // qw)),
                  pl.BlockSpec((1, c, qw), lambda b, n: (b, n, OFF_CK // qw)),
                  pl.BlockSpec((1, c, C_WIDTH), lambda b, n: (b, n, OFF_CV // C_WIDTH)),
                  pl.BlockSpec((1, c, C_WIDTH), lambda b, n: (b, n, OFF_CG // C_WIDTH)),
                  pl.BlockSpec((c, LANES), lambda b, n: (n, 0)),
                  pl.BlockSpec((c, LANES), lambda b, n: (n, 0)),
                  pl.BlockSpec((C_HEADS, c, c), lambda b, n: (0, 0, 0)),
                  pl.BlockSpec((C_HEADS // 2, c, LANES), lambda b, n: (0, 0, 0)),
                  pl.BlockSpec((C_HEADS // 2, c, LANES), lambda b, n: (0, 0, 0)),
                  pl.BlockSpec((C_HEADS, 1, LANES), lambda b, n: (0, 0, 0))],
        out_specs=pl.BlockSpec((1, c, C_WIDTH), lambda b, n: (b, n, 0)),
        out_shape=jax.ShapeDtypeStruct((bsz, s, C_WIDTH), BF16),
        scratch_shapes=[pltpu.VMEM((C_HEADS, LANES, C_V_DIM), F32)],
        compiler_params=_cparams(("parallel", "arbitrary")),
        name="retention",
    )(proj, proj, proj, proj, sin_t, cos_t, decay, zeta_t, xi_t, gch)


def _out_proj_kernel(ya_ref, yb_ref, yc_ref, wa_ref, wb_ref, wc_ref, x_ref, g_ref, b_ref, o_ref):
    mix = (jnp.dot(ya_ref[...], wa_ref[...], preferred_element_type=F32)
           + jnp.dot(yb_ref[...], wb_ref[...], preferred_element_type=F32)
           + jnp.dot(yc_ref[...], wc_ref[...], preferred_element_type=F32))
    o_ref[...] = _layer_norm(ALPHA * x_ref[...] + mix, g_ref[...], b_ref[...])


def _out_proj_ln(ya, yb, yc, w, x, g, b):
    n, d = x.shape
    tm = min(512, n)
    row = lambda i: (i, 0)
    fixed = lambda i: (0, 0)
    return pl.pallas_call(
        _out_proj_kernel,
        grid=(n // tm,),
        in_specs=[pl.BlockSpec((tm, A_WIDTH), row),
                  pl.BlockSpec((tm, B_WIDTH), row),
                  pl.BlockSpec((tm, C_WIDTH), row),
                  pl.BlockSpec((A_WIDTH, d), lambda i: (0, 0)),
                  pl.BlockSpec((B_WIDTH, d), lambda i: (1, 0)),
                  pl.BlockSpec((C_WIDTH, d), lambda i: ((A_WIDTH + B_WIDTH) // C_WIDTH, 0)),
                  pl.BlockSpec((tm, d), row),
                  pl.BlockSpec((1, d), fixed),
                  pl.BlockSpec((1, d), fixed)],
        out_specs=pl.BlockSpec((tm, d), row),
        out_shape=jax.ShapeDtypeStruct((n, d), F32),
        compiler_params=_cparams(("parallel",)),
        name="out_proj_ln",
    )(ya, yb, yc, w, w, w, x, g, b)


FFN_TM = 512
FFN_TF = 512


def _ffn_kernel(eid_ref, nact_ref, x_ref, wg_ref, wu_ref, wd_ref, g_ref, b_ref, o_ref, *cast_ref, fuse_ln):
    i = pl.program_id(0)
    f = pl.program_id(1)
    active = i < nact_ref[0]
    xb_ref = cast_ref[0] if cast_ref else x_ref

    @pl.when(f == 0)
    def _():
        o_ref[...] = jnp.zeros_like(o_ref)
        if cast_ref:
            xb_ref[...] = x_ref[...].astype(BF16)

    @pl.when(active)
    def _():
        xb = xb_ref[...]
        hg = jnp.dot(xb, wg_ref[0], preferred_element_type=F32)
        hu = jnp.dot(xb, wu_ref[0], preferred_element_type=F32)
        hidden = (_silu(hg) * hu).astype(BF16)
        o_ref[...] += jnp.dot(hidden, wd_ref[0], preferred_element_type=F32)

    if fuse_ln:
        @pl.when(jnp.logical_and(active, f == pl.num_programs(1) - 1))
        def _():
            o_ref[...] = _layer_norm(ALPHA * x_ref[...] + o_ref[...], g_ref[...], b_ref[...])


def _ffn(x, eid, nact, w_gate, w_up, w_down, g, b, fuse_ln):
    n, d = x.shape
    ff = w_gate.shape[2]
    tm = min(FFN_TM, n)
    tf = FFN_TF
    nf = ff // tf
    assert fuse_ln == (x.dtype == F32)
    scratch = [pltpu.VMEM((tm, d), BF16)] if x.dtype == F32 else []

    def fidx(i, f, eid_ref, nact_ref):
        return jnp.where(i < nact_ref[0], f, nf - 1)

    grid_spec = pltpu.PrefetchScalarGridSpec(
        num_scalar_prefetch=2,
        grid=(n // tm, nf),
        in_specs=[pl.BlockSpec((tm, d), lambda i, f, e, a: (i, 0)),
                  pl.BlockSpec((1, d, tf), lambda i, f, e, a: (e[i], 0, fidx(i, f, e, a))),
                  pl.BlockSpec((1, d, tf), lambda i, f, e, a: (e[i], 0, fidx(i, f, e, a))),
                  pl.BlockSpec((1, tf, d), lambda i, f, e, a: (e[i], fidx(i, f, e, a), 0)),
                  pl.BlockSpec((1, d), lambda i, f, e, a: (0, 0)),
                  pl.BlockSpec((1, d), lambda i, f, e, a: (0, 0))],
        out_specs=pl.BlockSpec((tm, d), lambda i, f, e, a: (i, 0)),
        scratch_shapes=scratch,
    )
    return pl.pallas_call(
        functools.partial(_ffn_kernel, fuse_ln=fuse_ln),
        grid_spec=grid_spec,
        out_shape=jax.ShapeDtypeStruct((n, d), F32),
        compiler_params=_cparams(("parallel", "arbitrary")),
        name="ffn_ln" if fuse_ln else "ffn_grouped",
    )(eid, nact, x, w_gate, w_up, w_down, g, b)


ROUTER_TM = 512


def _router_kernel(x_ref, wr_ref, tri_ref, meta_ref, cnt_ref, carry_ref):
    @pl.when(pl.program_id(0) == 0)
    def _():
        carry_ref[...] = jnp.zeros_like(carry_ref)

    logits = jnp.dot(x_ref[...], wr_ref[...], preferred_element_type=F32, precision=lax.Precision.HIGHEST)
    lane = lax.broadcasted_iota(jnp.int32, logits.shape, 1)
    logits = jnp.where(lane < N_EXPERTS, logits, -jnp.inf)
    m1 = jnp.max(logits, axis=-1, keepdims=True)
    i1 = jnp.min(jnp.where(logits == m1, lane, LANES), axis=-1, keepdims=True)
    rest = jnp.where(lane == i1, -jnp.inf, logits)
    m2 = jnp.max(rest, axis=-1, keepdims=True)
    i2 = jnp.min(jnp.where(rest == m2, lane, LANES), axis=-1, keepdims=True)
    e2 = jnp.exp(m2 - m1)
    w1 = 1.0 / (1.0 + e2)
    w2 = e2 * w1
    hit1 = lane == i1
    hit2 = lane == i2
    onehot = jnp.where(jnp.logical_or(hit1, hit2), 1.0, 0.0)
    before = jnp.dot(tri_ref[...], onehot.astype(BF16), preferred_element_type=F32) + carry_ref[...]
    r1 = jnp.sum(jnp.where(hit1, before, 0.0), axis=-1, keepdims=True)
    r2 = jnp.sum(jnp.where(hit2, before, 0.0), axis=-1, keepdims=True)
    carry_ref[...] = carry_ref[...] + jnp.sum(onehot, axis=0, keepdims=True)
    cnt_ref[...] = carry_ref[...]
    meta = jnp.where(lane == 0, i1.astype(F32), 0.0)
    meta = jnp.where(lane == 1, i2.astype(F32), meta)
    meta = jnp.where(lane == 2, r1, meta)
    meta = jnp.where(lane == 3, r2, meta)
    meta = jnp.where(lane == 4, w1, meta)
    meta = jnp.where(lane == 5, w2, meta)
    meta_ref[...] = meta


def _router(x, w_router):
    n, d = x.shape
    tm = min(ROUTER_TM, n)
    wr = jnp.zeros((d, LANES), F32).at[:, :N_EXPERTS].set(w_router.astype(F32))
    tri = jnp.asarray(np.tril(np.ones((tm, tm), np.float32), -1), BF16)
    return pl.pallas_call(
        _router_kernel,
        grid=(n // tm,),
        in_specs=[pl.BlockSpec((tm, d), lambda i: (i, 0)),
                  pl.BlockSpec((d, LANES), lambda i: (0, 0)),
                  pl.BlockSpec((tm, tm), lambda i: (0, 0))],
        out_specs=[pl.BlockSpec((tm, LANES), lambda i: (i, 0)),
                   pl.BlockSpec((1, LANES), lambda i: (0, 0))],
        out_shape=[jax.ShapeDtypeStruct((n, LANES), F32), jax.ShapeDtypeStruct((1, LANES), F32)],
        scratch_shapes=[pltpu.VMEM((1, LANES), F32)],
        compiler_params=_cparams(("arbitrary",)),
        name="router",
    )(x, wr, tri)


MOVE_TM = 256
ISSUE_UNROLL = 8


def _row_copy(src, dst, sem, s, t):
    return pltpu.make_async_copy(src.at[pl.ds(s, 1)], dst.at[pl.ds(t, 1)], sem)


def _dispatch_kernel(src_ref, x_hbm, o_ref, buf_ref, sem, *, tm):
    base = pl.program_id(0) * tm

    def issue(t, carry):
        _row_copy(x_hbm, buf_ref, sem, src_ref[base + t], t).start()
        return carry

    lax.fori_loop(0, tm, issue, 0, unroll=ISSUE_UNROLL)

    def drain(t, carry):
        _row_copy(x_hbm, buf_ref, sem, 0, 0).wait()
        return carry

    lax.fori_loop(0, tm, drain, 0, unroll=ISSUE_UNROLL)
    o_ref[...] = buf_ref[...].astype(BF16)


def _dispatch(x, src):
    n, d = x.shape
    rows = src.shape[0]
    tm = min(MOVE_TM, n)
    grid_spec = pltpu.PrefetchScalarGridSpec(
        num_scalar_prefetch=1,
        grid=(rows // tm,),
        in_specs=[pl.BlockSpec(memory_space=pl.ANY)],
        out_specs=pl.BlockSpec((tm, d), lambda i, s: (i, 0)),
        scratch_shapes=[pltpu.VMEM((tm, d), F32), pltpu.SemaphoreType.DMA(())],
    )
    return pl.pallas_call(
        functools.partial(_dispatch_kernel, tm=tm),
        grid_spec=grid_spec,
        out_shape=jax.ShapeDtypeStruct((rows, d), BF16),
        compiler_params=_cparams(("arbitrary",)),
        name="moe_dispatch",
    )(src, x)


def _combine_kernel(d1_ref, d2_ref, ys_hbm, x_ref, meta_ref, g_ref, b_ref, o_ref, buf_ref, sem, *, tm):
    base = pl.program_id(0) * tm

    def issue(t, carry):
        row = base + t
        _row_copy(ys_hbm, buf_ref.at[0], sem, d1_ref[row], t).start()
        _row_copy(ys_hbm, buf_ref.at[1], sem, d2_ref[row], t).start()
        return carry

    lax.fori_loop(0, tm, issue, 0, unroll=ISSUE_UNROLL)

    def drain(t, carry):
        _row_copy(ys_hbm, buf_ref.at[0], sem, 0, 0).wait()
        _row_copy(ys_hbm, buf_ref.at[1], sem, 0, 0).wait()
        return carry

    lax.fori_loop(0, tm, drain, 0, unroll=ISSUE_UNROLL)
    meta = meta_ref[...]
    lane = lax.broadcasted_iota(jnp.int32, meta.shape, 1)
    w1 = jnp.sum(jnp.where(lane == 4, meta, 0.0), axis=-1, keepdims=True)
    w2 = jnp.sum(jnp.where(lane == 5, meta, 0.0), axis=-1, keepdims=True)
    f = w1 * buf_ref[0] + w2 * buf_ref[1]
    o_ref[...] = _layer_norm(ALPHA * x_ref[...] + f, g_ref[...], b_ref[...])


def _combine(ys, x, meta, d1, d2, g, b):
    n, d = x.shape
    tm = min(MOVE_TM, n)
    grid_spec = pltpu.PrefetchScalarGridSpec(
        num_scalar_prefetch=2,
        grid=(n // tm,),
        in_specs=[pl.BlockSpec(memory_space=pl.ANY),
                  pl.BlockSpec((tm, d), lambda i, a, c: (i, 0)),
                  pl.BlockSpec((tm, LANES), lambda i, a, c: (i, 0)),
                  pl.BlockSpec((1, d), lambda i, a, c: (0, 0)),
                  pl.BlockSpec((1, d), lambda i, a, c: (0, 0))],
        out_specs=pl.BlockSpec((tm, d), lambda i, a, c: (i, 0)),
        scratch_shapes=[pltpu.VMEM((2, tm, d), F32), pltpu.SemaphoreType.DMA(())],
    )
    return pl.pallas_call(
        functools.partial(_combine_kernel, tm=tm),
        grid_spec=grid_spec,
        out_shape=jax.ShapeDtypeStruct((n, d), F32),
        compiler_params=_cparams(("arbitrary",)),
        name="moe_combine",
    )(d1, d2, ys, x, meta, g, b)


def _moe(x, w_router, w_gate, w_up, w_down, g, b):
    n, d = x.shape
    tm = min(FFN_TM, n)
    meta, cnt = _router(x, w_router)
    i1 = meta[:, 0].astype(jnp.int32)
    i2 = meta[:, 1].astype(jnp.int32)
    counts = cnt[0, :N_EXPERTS].astype(jnp.int32)
    tiles = (counts + tm - 1) // tm
    tile_end = jnp.cumsum(tiles)
    group_start = (tile_end - tiles) * tm
    experts = jnp.arange(N_EXPERTS, dtype=jnp.int32)[None, :]
    start_of = lambda idx: jnp.sum(jnp.where(idx[:, None] == experts, group_start[None, :], 0), axis=1)
    d1 = start_of(i1) + meta[:, 2].astype(jnp.int32)
    d2 = start_of(i2) + meta[:, 3].astype(jnp.int32)
    max_tiles = (2 * n) // tm + N_EXPERTS
    tile_ids = jnp.arange(max_tiles, dtype=jnp.int32)[:, None]
    eid = jnp.minimum(jnp.sum((tile_ids >= tile_end[None, :]).astype(jnp.int32), axis=1), N_EXPERTS - 1)
    nact = tile_end[-1:].astype(jnp.int32)
    token = jnp.arange(n, dtype=jnp.int32)
    src = jnp.zeros((max_tiles * tm,), jnp.int32).at[jnp.concatenate([d1, d2])].set(
        jnp.concatenate([token, token]), unique_indices=True)
    xs = _dispatch(x, src)
    ys = _ffn(xs, eid, nact, w_gate, w_up, w_down, g, b, fuse_ln=False)
    return _combine(ys, x, meta, d1, d2, g, b)


def _lambda_init(layer_idx):
    return 0.8 - 0.6 * math.exp(-0.3 * layer_idx)


def _static_tables(s):
    c = C_CHUNK
    ang = jnp.repeat(1.0 / (10000.0 ** jnp.linspace(0.0, 1.0, C_QK_DIM // 2, dtype=F32)), 2)
    ang = jnp.arange(s, dtype=F32)[:, None] * ang[None, :]
    sign = jnp.where(jnp.arange(C_QK_DIM) % 2 == 0, -1.0, 1.0).astype(F32)
    sin_t = jnp.tile(jnp.sin(ang) * sign[None, :], (1, 2))
    cos_t = jnp.tile(jnp.cos(ang), (1, 2))
    log_g = jnp.log(1.0 - jnp.exp2(-5.0 - jnp.arange(C_HEADS, dtype=F32)))
    pos = jnp.arange(c)
    rel = (pos[:, None] - pos[None, :]).astype(F32)
    decay = jnp.where((rel >= 0)[None], jnp.exp(jnp.maximum(rel, 0.0)[None] * log_g[:, None, None]), 0.0)
    zeta = jnp.exp((c - 1 - pos).astype(F32)[:, None] * log_g[None, :])
    xi = jnp.exp((pos + 1).astype(F32)[:, None] * log_g[None, :])
    per_pair = lambda t: jnp.repeat(t.T.reshape(C_HEADS // 2, 2, c), C_QK_DIM, axis=1).transpose(0, 2, 1)
    gch = jnp.broadcast_to(jnp.exp(c * log_g)[:, None, None], (C_HEADS, 1, LANES))
    dist_a = np.arange(A_SUB)[None, :] - np.arange(A_BLOCK)[:, None]
    types = [jnp.where(dist_a + off >= 0, _rel_bucket(jnp.asarray(dist_a + off)), REL_BUCKETS)
             for off in (0, A_SUB, A_BLOCK)]
    types.append(jnp.full(dist_a.shape, REL_BUCKETS - 1, jnp.int32))
    types.append(jnp.full(dist_a.shape, REL_BUCKETS, jnp.int32))
    bkt_a = jnp.concatenate([jnp.tile(b, (1, 2)) for b in types], axis=0).astype(jnp.int32)
    w = B_WINDOW
    dist = np.arange(w)[:, None] + w - np.arange(2 * w)[None, :]
    band = (dist >= 0) & (dist < w)
    bkt_b = jnp.where(band, _rel_bucket(jnp.asarray(dist)), REL_BUCKETS).astype(jnp.int32)
    return sin_t, cos_t, decay, per_pair(zeta), per_pair(xi), gch, bkt_a, bkt_b


def kernel(x, w_in, rel_bias, a_lambda, a_subln_g, b_sinks, w_out, ln_mix_g, ln_mix_b, ln_ffn_g, ln_ffn_b,
           dense_w_gate, dense_w_up, dense_w_down, moe_router, moe_w_gate, moe_w_up, moe_w_down):
    bsz, s, d = x.shape
    n = bsz * s
    sin_t, cos_t, decay, zeta_t, xi_t, gch, bkt_a, bkt_b = _static_tables(s)
    tab_t = rel_bias.astype(F32).T
    bias_a = _bias_lookup(tab_t[:A_HEADS] * LOG2E, bkt_a).reshape(A_HEADS, A_BIAS_TYPES, A_BLOCK, 2 * A_SUB)
    bias_b = _bias_lookup(tab_t[A_HEADS:], bkt_b)
    xf = x.reshape(n, d).astype(F32)
    dense_tiles = n // min(FFN_TM, n)
    for l in range(DEPTH):
        w_in_l = jnp.concatenate([w_in[l][:, a:b].astype(BF16) for a, b in _PERM_RUNS], axis=1)
        proj = _in_proj(xf, w_in_l).reshape(bsz, s, PROJ_WIDTH)
        ya = _attn_a(proj, bias_a, a_lambda[l].astype(F32), a_subln_g[l].astype(F32).reshape(1, A_V_DIM),
                     _lambda_init(l))
        yb = _swa(proj, bias_b, b_sinks[l].astype(F32))
        yc = _retention(proj, sin_t, cos_t, decay, zeta_t, xi_t, gch)
        g_mix = ln_mix_g[l].astype(F32).reshape(1, d)
        b_mix = ln_mix_b[l].astype(F32).reshape(1, d)
        xf = _out_proj_ln(ya.reshape(n, A_WIDTH), yb.reshape(n, B_WIDTH), yc.reshape(n, C_WIDTH),
                          w_out[l].astype(BF16), xf, g_mix, b_mix)
        g_ffn = ln_ffn_g[l].astype(F32).reshape(1, d)
        b_ffn = ln_ffn_b[l].astype(F32).reshape(1, d)
        j = l // 2
        if l % 2 == 0:
            xf = _ffn(xf, jnp.zeros((dense_tiles,), jnp.int32), jnp.full((1,), dense_tiles, jnp.int32),
                      dense_w_gate[j][None].astype(BF16), dense_w_up[j][None].astype(BF16),
                      dense_w_down[j][None].astype(BF16), g_ffn, b_ffn, fuse_ln=True)
        else:
            xf = _moe(xf, moe_router[j], moe_w_gate[j].astype(BF16), moe_w_up[j].astype(BF16),
                      moe_w_down[j].astype(BF16), g_ffn, b_ffn)
    return xf.reshape(bsz, s, d).astype(x.dtype)
```

```python
import functools
import math

import jax
import jax.numpy as jnp
import numpy as np
from jax import lax
from jax.experimental import pallas as pl
from jax.experimental.pallas import tpu as pltpu

F32 = jnp.float32
BF16 = jnp.bfloat16

D_MODEL = 2048
DEPTH = 2
A_HEADS = 6
A_QK_DIM = 64
A_V_DIM = 128
B_HEADS = 12
B_KV_HEADS = 3
B_HEAD_DIM = 64
B_WINDOW = 128
C_HEADS = 4
C_QK_DIM = 64
C_V_DIM = 128
C_CHUNK = 128
A_WIDTH = A_HEADS * A_V_DIM
B_WIDTH = B_HEADS * B_HEAD_DIM
C_WIDTH = C_HEADS * C_V_DIM
REL_BUCKETS = 32
REL_MAX_DIST = 128
D_FF = 5632
N_EXPERTS = 8
ALPHA = (2.0 * DEPTH) ** 0.25
LN_EPS = 1e-5
NEG = -1e30

LANES = 128
VMEM_LIMIT = 56 * 1024 * 1024

_REF_SIZES = [768, 768, 768, 768, 192, 192, 256, 256, 512, 512]
_REF_OFF = [int(v) for v in np.concatenate([[0], np.cumsum(_REF_SIZES)[:-1]])]
PROJ_WIDTH = int(sum(_REF_SIZES))
OFF_AQ, OFF_AK, OFF_AV, OFF_BQ, OFF_CV, OFF_CG, OFF_CQ, OFF_CK, OFF_BKV = (
    0, 768, 1536, 2304, 3072, 3584, 4096, 4352, 4608)


def _proj_perm():
    aq, ak, av, bq, bk, bv, cq, ck, cv, cg = [np.arange(o, o + s) for o, s in zip(_REF_OFF, _REF_SIZES)]
    bkv = np.concatenate([np.concatenate([bk[g * 64:(g + 1) * 64], bv[g * 64:(g + 1) * 64]])
                          for g in range(B_KV_HEADS)])
    perm = np.concatenate([aq, ak, av, bq, cv, cg, cq, ck, bkv])
    assert perm.shape[0] == PROJ_WIDTH
    return perm


def _perm_runs():
    perm = _proj_perm()
    cuts = np.flatnonzero(np.diff(perm) != 1) + 1
    return [(int(r[0]), int(r[-1]) + 1) for r in np.split(perm, cuts)]


_PERM_RUNS = _perm_runs()


def _cparams(sem):
    return pltpu.CompilerParams(dimension_semantics=sem, vmem_limit_bytes=VMEM_LIMIT)


def _layer_norm(z, g, b):
    mu = jnp.mean(z, axis=-1, keepdims=True)
    zc = z - mu
    var = jnp.mean(zc * zc, axis=-1, keepdims=True)
    return zc * lax.rsqrt(var + LN_EPS) * g + b


def _silu(x):
    return x / (1.0 + jnp.exp(-x))


def _dot_nt(a, b):
    return lax.dot_general(a, b, (((1,), (1,)), ((), ())), preferred_element_type=F32)


def _dot_tn(a, b):
    return lax.dot_general(a, b, (((0,), (0,)), ((), ())), preferred_element_type=F32)


def _bias_kernel(tab_ref, bkt_ref, o_ref):
    h = pl.program_id(0)
    bkt = bkt_ref[...]
    acc = jnp.full(bkt.shape, NEG, F32)
    for b in range(REL_BUCKETS):
        acc = jnp.where(bkt == b, tab_ref[h, b], acc)
    o_ref[0] = acc


def _bias_lookup(tab_t, bkt):
    nh = tab_t.shape[0]
    r, c = bkt.shape
    return pl.pallas_call(
        _bias_kernel,
        grid=(nh,),
        in_specs=[pl.BlockSpec(memory_space=pltpu.SMEM),
                  pl.BlockSpec((r, c), lambda h: (0, 0))],
        out_specs=pl.BlockSpec((1, r, c), lambda h: (h, 0, 0)),
        out_shape=jax.ShapeDtypeStruct((nh, r, c), F32),
        compiler_params=_cparams(("arbitrary",)),
        name="bias_lookup",
    )(tab_t, bkt)


def _rel_bucket(dist):
    max_exact = REL_BUCKETS // 2
    d = jnp.maximum(dist, 0)
    ratio = jnp.maximum(d, 1).astype(F32) / max_exact
    large = max_exact + (jnp.log(ratio) / math.log(REL_MAX_DIST / max_exact)
                         * (REL_BUCKETS - max_exact)).astype(jnp.int32)
    large = jnp.minimum(large, REL_BUCKETS - 1)
    return jnp.where(d < max_exact, d, large)


def _in_proj_kernel(x_ref, w_ref, o_ref, xb_ref):
    @pl.when(pl.program_id(1) == 0)
    def _():
        xb_ref[...] = x_ref[...].astype(BF16)

    o_ref[...] = jnp.dot(xb_ref[...], w_ref[...], preferred_element_type=F32).astype(BF16)


def _in_proj(x, w):
    n, d = x.shape
    p = w.shape[1]
    tm = min(1024, n)
    tn = p // 3
    return pl.pallas_call(
        _in_proj_kernel,
        grid=(n // tm, p // tn),
        in_specs=[pl.BlockSpec((tm, d), lambda i, j: (i, 0)),
                  pl.BlockSpec((d, tn), lambda i, j: (0, j))],
        out_specs=pl.BlockSpec((tm, tn), lambda i, j: (i, j)),
        out_shape=jax.ShapeDtypeStruct((n, p), BF16),
        scratch_shapes=[pltpu.VMEM((tm, d), BF16)],
        compiler_params=_cparams(("parallel", "arbitrary")),
        name="in_proj",
    )(x, w)


A_BLOCK = 256
A_SUB = 128
A_QROWS = 512
A_BIAS_TYPES = 5
LOG2E = math.log2(math.e)


def _attn_a_kernel(q_ref, k_ref, v_ref, bias_ref, lam_ref, g_ref, o_ref, qt_ref, s_ref, p_ref, acc_ref, *,
                   lam_init):
    t = A_BLOCK
    nch = A_QROWS // A_SUB
    per_key_block = t // A_SUB
    i = pl.program_id(2)
    lane = lax.broadcasted_iota(jnp.int32, (A_SUB, LANES), 1)
    for c in range(nch):
        q = q_ref[0, c * A_SUB:(c + 1) * A_SUB, :].astype(F32) * (A_QK_DIM ** -0.5 * LOG2E)
        qq = jnp.concatenate([jnp.where(lane < A_QK_DIM, q, 0.0), jnp.where(lane >= A_QK_DIM, q, 0.0)], axis=0)
        qt_ref[c] = qq.T.astype(BF16)
    first_block = i * (nch // per_key_block)
    n_blocks = first_block + nch // per_key_block

    def scores_into(slot, j):
        kj = k_ref[0, pl.ds(pl.multiple_of(j * t, t), t), :]
        for c in range(nch):
            back = first_block + c // per_key_block - j
            near = (0, 2) if c % per_key_block == 0 else (1, 3)
            kind = jnp.where(back == 0, near[0], jnp.where(back == 1, near[1], jnp.where(back < 0, 4, 3)))
            s_ref[slot, c] = jnp.dot(kj, qt_ref[c], preferred_element_type=F32) + bias_ref[0, kind]

    def values_from(slot, j, alphas):
        vj = v_ref[0, pl.ds(pl.multiple_of(j * t, t), t), :]
        for c in range(nch):
            acc_ref[c] = alphas[c] * acc_ref[c] + _dot_tn(vj, p_ref[slot, c])

    def sub_step(j, cur, carry):
        scores_into(1 - cur, jnp.minimum(j + 1, n_blocks - 1))
        new = []
        for c, (m_old, l_old, _) in enumerate(carry):
            s = s_ref[cur, c]
            m_new = jnp.maximum(m_old, jnp.max(s, axis=0, keepdims=True))
            p = jnp.exp2(s - m_new)
            new.append((m_new, jnp.exp2(m_old - m_new) * l_old + jnp.sum(p, axis=0, keepdims=True),
                        jnp.exp2(m_old - m_new)))
            p_ref[cur, c] = p.astype(BF16)
        values_from(1 - cur, jnp.maximum(j - 1, 0), [a for _, _, a in carry])
        return tuple(new)

    def pair(jj, carry):
        carry = sub_step(2 * jj, 0, carry)
        return sub_step(2 * jj + 1, 1, carry)

    scores_into(0, 0)
    p_ref[1] = jnp.zeros(p_ref.shape[1:], BF16)
    acc_ref[...] = jnp.zeros_like(acc_ref)
    init = tuple((jnp.full((1, 2 * A_SUB), NEG, F32), jnp.zeros((1, 2 * A_SUB), F32),
                  jnp.ones((1, 2 * A_SUB), F32)) for _ in range(nch))
    fin = lax.fori_loop(0, n_blocks // 2, pair, init)
    values_from(1, n_blocks - 1, [a for _, _, a in fin])
    lp = lam_ref[...]
    lam = (jnp.exp(jnp.sum(lp[0:1] * lp[1:2], axis=-1, keepdims=True))
           - jnp.exp(jnp.sum(lp[2:3] * lp[3:4], axis=-1, keepdims=True)) + lam_init)
    for c, (_, l_fin, _) in enumerate(fin):
        o_all = acc_ref[c] / l_fin
        o = (o_all[:, :A_SUB] - lam * o_all[:, A_SUB:]).T
        o = o * lax.rsqrt(jnp.mean(o * o, axis=-1, keepdims=True) + LN_EPS) * g_ref[...]
        o_ref[0, c * A_SUB:(c + 1) * A_SUB, :] = (o * (1.0 - lam_init)).astype(BF16)


def _attn_a(proj, bias_a, lam_params, subln_g, lam_init):
    bsz, s, _ = proj.shape
    t = A_QROWS
    nch = A_QROWS // A_SUB
    kb, vb = OFF_AK // LANES, OFF_AV // LANES
    return pl.pallas_call(
        functools.partial(_attn_a_kernel, lam_init=lam_init),
        grid=(bsz, A_HEADS, s // t),
        in_specs=[pl.BlockSpec((1, t, LANES), lambda b, h, i: (b, i, h)),
                  pl.BlockSpec((1, s, LANES), lambda b, h, i: (b, 0, kb + h)),
                  pl.BlockSpec((1, s, LANES), lambda b, h, i: (b, 0, vb + h)),
                  pl.BlockSpec((1, A_BIAS_TYPES, A_BLOCK, 2 * A_SUB), lambda b, h, i: (h, 0, 0, 0)),
                  pl.BlockSpec((4, A_QK_DIM), lambda b, h, i: (0, 0)),
                  pl.BlockSpec((1, A_V_DIM), lambda b, h, i: (0, 0))],
        out_specs=pl.BlockSpec((1, t, LANES), lambda b, h, i: (b, i, h)),
        out_shape=jax.ShapeDtypeStruct((bsz, s, A_WIDTH), BF16),
        scratch_shapes=[pltpu.VMEM((nch, LANES, 2 * A_SUB), BF16),
                        pltpu.VMEM((2, nch, A_BLOCK, 2 * A_SUB), F32),
                        pltpu.VMEM((2, nch, A_BLOCK, 2 * A_SUB), BF16),
                        pltpu.VMEM((nch, A_V_DIM, 2 * A_SUB), F32)],
        compiler_params=_cparams(("parallel", "parallel", "arbitrary")),
        name="attn_a",
    )(proj, proj, proj, bias_a, lam_params, subln_g)


def _swa_kernel(sink_ref, q_ref, kvp_ref, kvc_ref, bias_ref, o_ref):
    w = B_WINDOW
    n = pl.program_id(1)
    lane = lax.broadcasted_iota(jnp.int32, (w, LANES), 1)
    col = lax.broadcasted_iota(jnp.int32, (w, 2 * w), 1)
    valid = jnp.logical_or(col >= w, n > 0)
    group = B_HEADS // B_KV_HEADS
    for g in range(B_KV_HEADS):
        kv = jnp.concatenate([kvp_ref[0, :, g * LANES:(g + 1) * LANES],
                              kvc_ref[0, :, g * LANES:(g + 1) * LANES]], axis=0)
        for pair in range(group // 2):
            blk = g * (group // 2) + pair
            qblk = q_ref[0, :, blk * LANES:(blk + 1) * LANES].astype(F32) * (B_HEAD_DIM ** -0.5)
            outs = []
            for half in range(2):
                hq = 2 * blk + half
                qh = qblk if half == 0 else pltpu.roll(qblk, B_HEAD_DIM, 1)
                qh = jnp.where(lane < B_HEAD_DIM, qh, 0.0).astype(BF16)
                s = _dot_nt(qh, kv) + bias_ref[hq]
                s = jnp.where(valid, s, NEG)
                sink = sink_ref[hq]
                m = jnp.maximum(jnp.max(s, axis=-1, keepdims=True), sink)
                e = jnp.exp(s - m)
                den = jnp.sum(e, axis=-1, keepdims=True) + jnp.exp(sink - m)
                o = jnp.dot(e.astype(BF16), kv, preferred_element_type=F32)
                outs.append(o / den)
            ob = jnp.where(lane < B_HEAD_DIM, pltpu.roll(outs[0], B_HEAD_DIM, 1), outs[1])
            o_ref[0, :, blk * LANES:(blk + 1) * LANES] = ob.astype(BF16)


def _swa(proj, bias_b, sinks):
    bsz, s, _ = proj.shape
    w = B_WINDOW
    kvw = 2 * B_KV_HEADS * B_HEAD_DIM
    qb, kvb = OFF_BQ // B_WIDTH, OFF_BKV // kvw
    return pl.pallas_call(
        _swa_kernel,
        grid=(bsz, s // w),
        in_specs=[pl.BlockSpec(memory_space=pltpu.SMEM),
                  pl.BlockSpec((1, w, B_WIDTH), lambda b, n: (b, n, qb)),
                  pl.BlockSpec((1, w, kvw), lambda b, n: (b, jnp.maximum(n - 1, 0), kvb)),
                  pl.BlockSpec((1, w, kvw), lambda b, n: (b, n, kvb)),
                  pl.BlockSpec((B_HEADS, w, 2 * w), lambda b, n: (0, 0, 0))],
        out_specs=pl.BlockSpec((1, w, B_WIDTH), lambda b, n: (b, n, 0)),
        out_shape=jax.ShapeDtypeStruct((bsz, s, B_WIDTH), BF16),
        compiler_params=_cparams(("parallel", "arbitrary")),
        name="swa",
    )(sinks, proj, proj, proj, bias_b)


def _ret_kernel(q_ref, k_ref, v_ref, g_ref, sin_ref, cos_ref, decay_ref, zeta_ref, xi_ref, gch_ref, o_ref,
                state_ref):
    c = C_CHUNK

    @pl.when(pl.program_id(1) == 0)
    def _():
        state_ref[...] = jnp.zeros_like(state_ref)

    lane = lax.broadcasted_iota(jnp.int32, (c, LANES), 1)
    even = (lane & 1) == 0
    sn = sin_ref[...]
    cs = cos_ref[...]

    def rope(x):
        swapped = jnp.where(even, pltpu.roll(x, LANES - 1, 1), pltpu.roll(x, 1, 1))
        return x * cs + swapped * sn

    for pair in range(C_HEADS // 2):
        q = rope(q_ref[0, :, pair * LANES:(pair + 1) * LANES].astype(F32))
        k = rope(k_ref[0, :, pair * LANES:(pair + 1) * LANES].astype(F32) * (C_QK_DIM ** -0.5))
        kb = k.astype(BF16)
        kz = (k * zeta_ref[pair]).astype(BF16)
        qx = q * xi_ref[pair]
        for half in range(2):
            h = 2 * pair + half
            in_head = (lane < C_QK_DIM) if half == 0 else (lane >= C_QK_DIM)
            qm = jnp.where(in_head, q, 0.0).astype(BF16)
            qxm = jnp.where(in_head, qx, 0.0).astype(BF16)
            vh = v_ref[0, :, h * LANES:(h + 1) * LANES]
            inner = _dot_nt(qm, kb) * decay_ref[h]
            st = state_ref[h]
            o = (jnp.dot(inner.astype(BF16), vh, preferred_element_type=F32)
                 + jnp.dot(qxm, st.astype(BF16), preferred_element_type=F32))
            state_ref[h] = st * gch_ref[h] + _dot_tn(kz, vh)
            mu = jnp.mean(o, axis=-1, keepdims=True)
            oc = o - mu
            o = oc * lax.rsqrt(jnp.mean(oc * oc, axis=-1, keepdims=True) + LN_EPS)
            gate = g_ref[0, :, h * LANES:(h + 1) * LANES].astype(F32)
            o_ref[0, :, h * LANES:(h + 1) * LANES] = (_silu(gate) * o).astype(BF16)


def _retention(proj, sin_t, cos_t, decay, zeta_t, xi_t, gch):
    bsz, s, _ = proj.shape
    c = C_CHUNK
    qw = C_HEADS * C_QK_DIM
    return pl.pallas_call(
        _ret_kernel,
        grid=(bsz, s // c),
        in_specs=[pl.BlockSpec((1, c, qw), lambda b, n: (b, n, OFF_CQ // qw)),
                  pl.BlockSpec((1, c, qw), lambda b, n: (b, n, OFF_CK // qw)),
                  pl.BlockSpec((1, c, C_WIDTH), lambda b, n: (b, n, OFF_CV // C_WIDTH)),
                  pl.BlockSpec((1, c, C_WIDTH), lambda b, n: (b, n, OFF_CG // C_WIDTH)),
                  pl.BlockSpec((c, LANES), lambda b, n: (n, 0)),
                  pl.BlockSpec((c, LANES), lambda b, n: (n, 0)),
                  pl.BlockSpec((C_HEADS, c, c), lambda b, n: (0, 0, 0)),
                  pl.BlockSpec((C_HEADS // 2, c, LANES), lambda b, n: (0, 0, 0)),
                  pl.BlockSpec((C_HEADS // 2, c, LANES), lambda b, n: (0, 0, 0)),
                  pl.BlockSpec((C_HEADS, 1, LANES), lambda b, n: (0, 0, 0))],
        out_specs=pl.BlockSpec((1, c, C_WIDTH), lambda b, n: (b, n, 0)),
        out_shape=jax.ShapeDtypeStruct((bsz, s, C_WIDTH), BF16),
        scratch_shapes=[pltpu.VMEM((C_HEADS, LANES, C_V_DIM), F32)],
        compiler_params=_cparams(("parallel", "arbitrary")),
        name="retention",
    )(proj, proj, proj, proj, sin_t, cos_t, decay, zeta_t, xi_t, gch)


def _out_proj_kernel(ya_ref, yb_ref, yc_ref, wa_ref, wb_ref, wc_ref, x_ref, g_ref, b_ref, o_ref):
    mix = (jnp.dot(ya_ref[...], wa_ref[...], preferred_element_type=F32)
           + jnp.dot(yb_ref[...], wb_ref[...], preferred_element_type=F32)
           + jnp.dot(yc_ref[...], wc_ref[...], preferred_element_type=F32))
    o_ref[...] = _layer_norm(ALPHA * x_ref[...] + mix, g_ref[...], b_ref[...])


def _out_proj_ln(ya, yb, yc, w, x, g, b):
    n, d = x.shape
    tm = min(512, n)
    row = lambda i: (i, 0)
    fixed = lambda i: (0, 0)
    return pl.pallas_call(
        _out_proj_kernel,
        grid=(n // tm,),
        in_specs=[pl.BlockSpec((tm, A_WIDTH), row),
                  pl.BlockSpec((tm, B_WIDTH), row),
                  pl.BlockSpec((tm, C_WIDTH), row),
                  pl.BlockSpec((A_WIDTH, d), lambda i: (0, 0)),
                  pl.BlockSpec((B_WIDTH, d), lambda i: (1, 0)),
                  pl.BlockSpec((C_WIDTH, d), lambda i: ((A_WIDTH + B_WIDTH) // C_WIDTH, 0)),
                  pl.BlockSpec((tm, d), row),
                  pl.BlockSpec((1, d), fixed),
                  pl.BlockSpec((1, d), fixed)],
        out_specs=pl.BlockSpec((tm, d), row),
        out_shape=jax.ShapeDtypeStruct((n, d), F32),
        compiler_params=_cparams(("parallel",)),
        name="out_proj_ln",
    )(ya, yb, yc, w, w, w, x, g, b)


FFN_TM = 512
FFN_TF = 512


def _swiglu_step(xb, wg, wu, wd):
    hg = jnp.dot(xb, wg, preferred_element_type=F32)
    hu = jnp.dot(xb, wu, preferred_element_type=F32)
    return jnp.dot((_silu(hg) * hu).astype(BF16), wd, preferred_element_type=F32)


def _ffn_kernel(x_ref, wg_ref, wu_ref, wd_ref, g_ref, b_ref, o_ref, xb_ref):
    f = pl.program_id(1)

    @pl.when(f == 0)
    def _():
        o_ref[...] = jnp.zeros_like(o_ref)
        xb_ref[...] = x_ref[...].astype(BF16)

    o_ref[...] += _swiglu_step(xb_ref[...], wg_ref[...], wu_ref[...], wd_ref[...])

    @pl.when(f == pl.num_programs(1) - 1)
    def _():
        o_ref[...] = _layer_norm(ALPHA * x_ref[...] + o_ref[...], g_ref[...], b_ref[...])


def _ffn(x, w_gate, w_up, w_down, g, b):
    n, d = x.shape
    ff = w_gate.shape[1]
    tm = min(FFN_TM, n)
    tf = FFN_TF
    return pl.pallas_call(
        _ffn_kernel,
        grid=(n // tm, ff // tf),
        in_specs=[pl.BlockSpec((tm, d), lambda i, f: (i, 0)),
                  pl.BlockSpec((d, tf), lambda i, f: (0, f)),
                  pl.BlockSpec((d, tf), lambda i, f: (0, f)),
                  pl.BlockSpec((tf, d), lambda i, f: (f, 0)),
                  pl.BlockSpec((1, d), lambda i, f: (0, 0)),
                  pl.BlockSpec((1, d), lambda i, f: (0, 0))],
        out_specs=pl.BlockSpec((tm, d), lambda i, f: (i, 0)),
        out_shape=jax.ShapeDtypeStruct((n, d), F32),
        scratch_shapes=[pltpu.VMEM((tm, d), BF16)],
        compiler_params=_cparams(("parallel", "arbitrary")),
        name="ffn_ln",
    )(x, w_gate, w_up, w_down, g, b)


def _row_copy(src, dst, sem, s, t):
    return pltpu.make_async_copy(src.at[pl.ds(s, 1)], dst.at[pl.ds(t, 1)], sem)


def _ffn_grouped_kernel(eid_ref, nact_ref, src_ref, x_hbm, wg_ref, wu_ref, wd_ref, o_ref, rows_ref, xb_ref, sem,
                        *, tm, per_step):
    i = pl.program_id(0)
    f = pl.program_id(1)
    nact = nact_ref[0]
    active = i < nact
    slot = lax.rem(i, 2)
    buf_rows = rows_ref.shape[1]
    last_slot_row = src_ref.shape[0] - 1

    def fetch(tile, into, first_row, count):
        for r in range(count):
            row = first_row + r
            token = src_ref[jnp.minimum(tile * tm + row, last_slot_row)]
            _row_copy(x_hbm, rows_ref.at[into], sem.at[into], token, row).start()

    def wait_buffer(which):
        pltpu.make_async_copy(x_hbm.at[pl.ds(0, buf_rows)], rows_ref.at[which], sem.at[which]).wait()

    @pl.when(jnp.logical_and(i == 0, f == 0))
    def _():
        lax.fori_loop(0, buf_rows // per_step, lambda s, c: (fetch(0, 0, s * per_step, per_step), c)[1], 0)

    @pl.when(jnp.logical_and(f == 0, i <= nact))
    def _():
        wait_buffer(slot)

    @pl.when(f == 0)
    def _():
        o_ref[...] = jnp.zeros_like(o_ref)

    @pl.when(jnp.logical_and(f == 0, active))
    def _():
        xb_ref[...] = rows_ref[slot, :tm, :].astype(BF16)

    @pl.when(active)
    def _():
        fetch(i + 1, 1 - slot, f * per_step, per_step)
        o_ref[...] += _swiglu_step(xb_ref[...], wg_ref[0], wu_ref[0], wd_ref[0])


def _ffn_grouped(x, src, eid, nact, w_gate, w_up, w_down):
    n, d = x.shape
    slots = src.shape[0]
    ff = w_gate.shape[2]
    tm = min(FFN_TM, n)
    tf = FFN_TF
    nf = ff // tf
    per_step = -(-tm // nf)
    per_step += (-per_step) % 8
    assert per_step * nf <= n

    def fidx(i, f, nact_ref):
        return jnp.where(i < nact_ref[0], f, nf - 1)

    grid_spec = pltpu.PrefetchScalarGridSpec(
        num_scalar_prefetch=3,
        grid=(slots // tm, nf),
        in_specs=[pl.BlockSpec(memory_space=pl.ANY),
                  pl.BlockSpec((1, d, tf), lambda i, f, e, a, s: (e[i], 0, fidx(i, f, a))),
                  pl.BlockSpec((1, d, tf), lambda i, f, e, a, s: (e[i], 0, fidx(i, f, a))),
                  pl.BlockSpec((1, tf, d), lambda i, f, e, a, s: (e[i], fidx(i, f, a), 0))],
        out_specs=pl.BlockSpec((tm, d), lambda i, f, e, a, s: (i, 0)),
        scratch_shapes=[pltpu.VMEM((2, per_step * nf, d), F32), pltpu.VMEM((tm, d), BF16),
                        pltpu.SemaphoreType.DMA((2,))],
    )
    return pl.pallas_call(
        functools.partial(_ffn_grouped_kernel, tm=tm, per_step=per_step),
        grid_spec=grid_spec,
        out_shape=jax.ShapeDtypeStruct((slots, d), F32),
        compiler_params=_cparams(("arbitrary", "arbitrary")),
        name="ffn_grouped",
    )(eid, nact, src, x, w_gate, w_up, w_down)


ROUTER_TM = 512


def _router_kernel(x_ref, wr_ref, tri_ref, meta_ref, cnt_ref, carry_ref):
    @pl.when(pl.program_id(0) == 0)
    def _():
        carry_ref[...] = jnp.zeros_like(carry_ref)

    logits = jnp.dot(x_ref[...], wr_ref[...], preferred_element_type=F32, precision=lax.Precision.HIGHEST)
    lane = lax.broadcasted_iota(jnp.int32, logits.shape, 1)
    logits = jnp.where(lane < N_EXPERTS, logits, -jnp.inf)
    m1 = jnp.max(logits, axis=-1, keepdims=True)
    i1 = jnp.min(jnp.where(logits == m1, lane, LANES), axis=-1, keepdims=True)
    rest = jnp.where(lane == i1, -jnp.inf, logits)
    m2 = jnp.max(rest, axis=-1, keepdims=True)
    i2 = jnp.min(jnp.where(rest == m2, lane, LANES), axis=-1, keepdims=True)
    e2 = jnp.exp(m2 - m1)
    w1 = 1.0 / (1.0 + e2)
    w2 = e2 * w1
    hit1 = lane == i1
    hit2 = lane == i2
    onehot = jnp.where(jnp.logical_or(hit1, hit2), 1.0, 0.0)
    before = jnp.dot(tri_ref[...], onehot.astype(BF16), preferred_element_type=F32) + carry_ref[...]
    r1 = jnp.sum(jnp.where(hit1, before, 0.0), axis=-1, keepdims=True)
    r2 = jnp.sum(jnp.where(hit2, before, 0.0), axis=-1, keepdims=True)
    carry_ref[...] = carry_ref[...] + jnp.sum(onehot, axis=0, keepdims=True)
    cnt_ref[...] = carry_ref[...]
    meta = jnp.where(lane == 0, i1.astype(F32), 0.0)
    meta = jnp.where(lane == 1, i2.astype(F32), meta)
    meta = jnp.where(lane == 2, r1, meta)
    meta = jnp.where(lane == 3, r2, meta)
    meta = jnp.where(lane == 4, w1, meta)
    meta = jnp.where(lane == 5, w2, meta)
    meta_ref[...] = meta


def _router(x, w_router):
    n, d = x.shape
    tm = min(ROUTER_TM, n)
    wr = jnp.zeros((d, LANES), F32).at[:, :N_EXPERTS].set(w_router.astype(F32))
    tri = jnp.asarray(np.tril(np.ones((tm, tm), np.float32), -1), BF16)
    return pl.pallas_call(
        _router_kernel,
        grid=(n // tm,),
        in_specs=[pl.BlockSpec((tm, d), lambda i: (i, 0)),
                  pl.BlockSpec((d, LANES), lambda i: (0, 0)),
                  pl.BlockSpec((tm, tm), lambda i: (0, 0))],
        out_specs=[pl.BlockSpec((tm, LANES), lambda i: (i, 0)),
                   pl.BlockSpec((1, LANES), lambda i: (0, 0))],
        out_shape=[jax.ShapeDtypeStruct((n, LANES), F32), jax.ShapeDtypeStruct((1, LANES), F32)],
        scratch_shapes=[pltpu.VMEM((1, LANES), F32)],
        compiler_params=_cparams(("arbitrary",)),
        name="router",
    )(x, wr, tri)


MOVE_TM = 256
ISSUE_UNROLL = 8


def _combine_kernel(d1_ref, d2_ref, ys_hbm, x_ref, meta_ref, g_ref, b_ref, o_ref, buf_ref, sem, *, tm):
    base = pl.program_id(0) * tm

    def issue(t, carry):
        row = base + t
        _row_copy(ys_hbm, buf_ref.at[0], sem, d1_ref[row], t).start()
        _row_copy(ys_hbm, buf_ref.at[1], sem, d2_ref[row], t).start()
        return carry

    lax.fori_loop(0, tm, issue, 0, unroll=ISSUE_UNROLL)

    def drain(t, carry):
        _row_copy(ys_hbm, buf_ref.at[0], sem, 0, 0).wait()
        _row_copy(ys_hbm, buf_ref.at[1], sem, 0, 0).wait()
        return carry

    lax.fori_loop(0, tm, drain, 0, unroll=ISSUE_UNROLL)
    meta = meta_ref[...]
    lane = lax.broadcasted_iota(jnp.int32, meta.shape, 1)
    w1 = jnp.sum(jnp.where(lane == 4, meta, 0.0), axis=-1, keepdims=True)
    w2 = jnp.sum(jnp.where(lane == 5, meta, 0.0), axis=-1, keepdims=True)
    f = w1 * buf_ref[0] + w2 * buf_ref[1]
    o_ref[...] = _layer_norm(ALPHA * x_ref[...] + f, g_ref[...], b_ref[...])


def _combine(ys, x, meta, d1, d2, g, b):
    n, d = x.shape
    tm = min(MOVE_TM, n)
    grid_spec = pltpu.PrefetchScalarGridSpec(
        num_scalar_prefetch=2,
        grid=(n // tm,),
        in_specs=[pl.BlockSpec(memory_space=pl.ANY),
                  pl.BlockSpec((tm, d), lambda i, a, c: (i, 0)),
                  pl.BlockSpec((tm, LANES), lambda i, a, c: (i, 0)),
                  pl.BlockSpec((1, d), lambda i, a, c: (0, 0)),
                  pl.BlockSpec((1, d), lambda i, a, c: (0, 0))],
        out_specs=pl.BlockSpec((tm, d), lambda i, a, c: (i, 0)),
        scratch_shapes=[pltpu.VMEM((2, tm, d), F32), pltpu.SemaphoreType.DMA(())],
    )
    return pl.pallas_call(
        functools.partial(_combine_kernel, tm=tm),
        grid_spec=grid_spec,
        out_shape=jax.ShapeDtypeStruct((n, d), F32),
        compiler_params=_cparams(("arbitrary",)),
        name="moe_combine",
    )(d1, d2, ys, x, meta, g, b)


def _moe(x, w_router, w_gate, w_up, w_down, g, b):
    n, d = x.shape
    tm = min(FFN_TM, n)
    meta, cnt = _router(x, w_router)
    i1 = meta[:, 0].astype(jnp.int32)
    i2 = meta[:, 1].astype(jnp.int32)
    counts = cnt[0, :N_EXPERTS].astype(jnp.int32)
    tiles = (counts + tm - 1) // tm
    tile_end = jnp.cumsum(tiles)
    group_start = (tile_end - tiles) * tm
    experts = jnp.arange(N_EXPERTS, dtype=jnp.int32)[None, :]
    start_of = lambda idx: jnp.sum(jnp.where(idx[:, None] == experts, group_start[None, :], 0), axis=1)
    d1 = start_of(i1) + meta[:, 2].astype(jnp.int32)
    d2 = start_of(i2) + meta[:, 3].astype(jnp.int32)
    max_tiles = (2 * n) // tm + N_EXPERTS
    tile_ids = jnp.arange(max_tiles, dtype=jnp.int32)[:, None]
    eid = jnp.minimum(jnp.sum((tile_ids >= tile_end[None, :]).astype(jnp.int32), axis=1), N_EXPERTS - 1)
    nact = tile_end[-1:].astype(jnp.int32)
    token = jnp.arange(n, dtype=jnp.int32)
    src = jnp.zeros((max_tiles * tm,), jnp.int32).at[jnp.concatenate([d1, d2])].set(
        jnp.concatenate([token, token]), unique_indices=True)
    ys = _ffn_grouped(x, src, eid.astype(jnp.int32), nact, w_gate, w_up, w_down)
    return _combine(ys, x, meta, d1, d2, g, b)


def _lambda_init(layer_idx):
    return 0.8 - 0.6 * math.exp(-0.3 * layer_idx)


def _static_tables(s):
    c = C_CHUNK
    ang = jnp.repeat(1.0 / (10000.0 ** jnp.linspace(0.0, 1.0, C_QK_DIM // 2, dtype=F32)), 2)
    ang = jnp.arange(s, dtype=F32)[:, None] * ang[None, :]
    sign = jnp.where(jnp.arange(C_QK_DIM) % 2 == 0, -1.0, 1.0).astype(F32)
    sin_t = jnp.tile(jnp.sin(ang) * sign[None, :], (1, 2))
    cos_t = jnp.tile(jnp.cos(ang), (1, 2))
    log_g = jnp.log(1.0 - jnp.exp2(-5.0 - jnp.arange(C_HEADS, dtype=F32)))
    pos = jnp.arange(c)
    rel = (pos[:, None] - pos[None, :]).astype(F32)
    decay = jnp.where((rel >= 0)[None], jnp.exp(jnp.maximum(rel, 0.0)[None] * log_g[:, None, None]), 0.0)
    zeta = jnp.exp((c - 1 - pos).astype(F32)[:, None] * log_g[None, :])
    xi = jnp.exp((pos + 1).astype(F32)[:, None] * log_g[None, :])
    per_pair = lambda t: jnp.repeat(t.T.reshape(C_HEADS // 2, 2, c), C_QK_DIM, axis=1).transpose(0, 2, 1)
    gch = jnp.broadcast_to(jnp.exp(c * log_g)[:, None, None], (C_HEADS, 1, LANES))
    dist_a = np.arange(A_SUB)[None, :] - np.arange(A_BLOCK)[:, None]
    types = [jnp.where(dist_a + off >= 0, _rel_bucket(jnp.asarray(dist_a + off)), REL_BUCKETS)
             for off in (0, A_SUB, A_BLOCK)]
    types.append(jnp.full(dist_a.shape, REL_BUCKETS - 1, jnp.int32))
    types.append(jnp.full(dist_a.shape, REL_BUCKETS, jnp.int32))
    bkt_a = jnp.concatenate([jnp.tile(b, (1, 2)) for b in types], axis=0).astype(jnp.int32)
    w = B_WINDOW
    dist = np.arange(w)[:, None] + w - np.arange(2 * w)[None, :]
    band = (dist >= 0) & (dist < w)
    bkt_b = jnp.where(band, _rel_bucket(jnp.asarray(dist)), REL_BUCKETS).astype(jnp.int32)
    return sin_t, cos_t, decay, per_pair(zeta), per_pair(xi), gch, bkt_a, bkt_b


def kernel(x, w_in, rel_bias, a_lambda, a_subln_g, b_sinks, w_out, ln_mix_g, ln_mix_b, ln_ffn_g, ln_ffn_b,
           dense_w_gate, dense_w_up, dense_w_down, moe_router, moe_w_gate, moe_w_up, moe_w_down):
    bsz, s, d = x.shape
    n = bsz * s
    sin_t, cos_t, decay, zeta_t, xi_t, gch, bkt_a, bkt_b = _static_tables(s)
    tab_t = rel_bias.astype(F32).T
    bias_a = _bias_lookup(tab_t[:A_HEADS] * LOG2E, bkt_a).reshape(A_HEADS, A_BIAS_TYPES, A_BLOCK, 2 * A_SUB)
    bias_b = _bias_lookup(tab_t[A_HEADS:], bkt_b)
    xf = x.reshape(n, d).astype(F32)
    for l in range(DEPTH):
        w_in_l = jnp.concatenate([w_in[l][:, a:b].astype(BF16) for a, b in _PERM_RUNS], axis=1)
        proj = _in_proj(xf, w_in_l).reshape(bsz, s, PROJ_WIDTH)
        ya = _attn_a(proj, bias_a, a_lambda[l].astype(F32), a_subln_g[l].astype(F32).reshape(1, A_V_DIM),
                     _lambda_init(l))
        yb = _swa(proj, bias_b, b_sinks[l].astype(F32))
        yc = _retention(proj, sin_t, cos_t, decay, zeta_t, xi_t, gch)
        g_mix = ln_mix_g[l].astype(F32).reshape(1, d)
        b_mix = ln_mix_b[l].astype(F32).reshape(1, d)
        xf = _out_proj_ln(ya.reshape(n, A_WIDTH), yb.reshape(n, B_WIDTH), yc.reshape(n, C_WIDTH),
                          w_out[l].astype(BF16), xf, g_mix, b_mix)
        g_ffn = ln_ffn_g[l].astype(F32).reshape(1, d)
        b_ffn = ln_ffn_b[l].astype(F32).reshape(1, d)
        j = l // 2
        if l % 2 == 0:
            xf = _ffn(xf, dense_w_gate[j].astype(BF16), dense_w_up[j].astype(BF16), dense_w_down[j].astype(BF16),
                      g_ffn, b_ffn)
        else:
            xf = _moe(xf, moe_router[j], moe_w_gate[j].astype(BF16), moe_w_up[j].astype(BF16),
                      moe_w_down[j].astype(BF16), g_ffn, b_ffn)
    return xf.reshape(bsz, s, d).astype(x.dtype)
```

```python
import functools
import math

import jax
import jax.numpy as jnp
import numpy as np
from jax import lax
from jax.experimental import pallas as pl
from jax.experimental.pallas import tpu as pltpu

F32 = jnp.float32
BF16 = jnp.bfloat16

D_MODEL = 2048
DEPTH = 2
A_HEADS = 6
A_QK_DIM = 64
A_V_DIM = 128
B_HEADS = 12
B_KV_HEADS = 3
B_HEAD_DIM = 64
B_WINDOW = 128
C_HEADS = 4
C_QK_DIM = 64
C_V_DIM = 128
C_CHUNK = 128
A_WIDTH = A_HEADS * A_V_DIM
B_WIDTH = B_HEADS * B_HEAD_DIM
C_WIDTH = C_HEADS * C_V_DIM
REL_BUCKETS = 32
REL_MAX_DIST = 128
D_FF = 5632
N_EXPERTS = 8
ALPHA = (2.0 * DEPTH) ** 0.25
LN_EPS = 1e-5
NEG = -1e30

LANES = 128
VMEM_LIMIT = 56 * 1024 * 1024

_REF_SIZES = [768, 768, 768, 768, 192, 192, 256, 256, 512, 512]
_REF_OFF = [int(v) for v in np.concatenate([[0], np.cumsum(_REF_SIZES)[:-1]])]
PROJ_WIDTH = int(sum(_REF_SIZES))
OFF_AQ, OFF_AK, OFF_AV, OFF_BQ, OFF_CV, OFF_CG, OFF_CQ, OFF_CK, OFF_BKV = (
    0, 768, 1536, 2304, 3072, 3584, 4096, 4352, 4608)


def _proj_perm():
    aq, ak, av, bq, bk, bv, cq, ck, cv, cg = [np.arange(o, o + s) for o, s in zip(_REF_OFF, _REF_SIZES)]
    bkv = np.concatenate([np.concatenate([bk[g * 64:(g + 1) * 64], bv[g * 64:(g + 1) * 64]])
                          for g in range(B_KV_HEADS)])
    perm = np.concatenate([aq, ak, av, bq, cv, cg, cq, ck, bkv])
    assert perm.shape[0] == PROJ_WIDTH
    return perm


def _perm_runs():
    perm = _proj_perm()
    cuts = np.flatnonzero(np.diff(perm) != 1) + 1
    return [(int(r[0]), int(r[-1]) + 1) for r in np.split(perm, cuts)]


_PERM_RUNS = _perm_runs()


def _cparams(sem):
    return pltpu.CompilerParams(dimension_semantics=sem, vmem_limit_bytes=VMEM_LIMIT)


def _layer_norm(z, g, b):
    mu = jnp.mean(z, axis=-1, keepdims=True)
    zc = z - mu
    var = jnp.mean(zc * zc, axis=-1, keepdims=True)
    return zc * lax.rsqrt(var + LN_EPS) * g + b


def _silu(x):
    return x / (1.0 + jnp.exp(-x))


def _dot_nt(a, b):
    return lax.dot_general(a, b, (((1,), (1,)), ((), ())), preferred_element_type=F32)


def _dot_tn(a, b):
    return lax.dot_general(a, b, (((0,), (0,)), ((), ())), preferred_element_type=F32)


def _bias_kernel(tab_ref, bkt_ref, o_ref):
    h = pl.program_id(0)
    bkt = bkt_ref[...]
    acc = jnp.full(bkt.shape, NEG, F32)
    for b in range(REL_BUCKETS):
        acc = jnp.where(bkt == b, tab_ref[h, b], acc)
    o_ref[0] = acc


def _bias_lookup(tab_t, bkt):
    nh = tab_t.shape[0]
    r, c = bkt.shape
    return pl.pallas_call(
        _bias_kernel,
        grid=(nh,),
        in_specs=[pl.BlockSpec(memory_space=pltpu.SMEM),
                  pl.BlockSpec((r, c), lambda h: (0, 0))],
        out_specs=pl.BlockSpec((1, r, c), lambda h: (h, 0, 0)),
        out_shape=jax.ShapeDtypeStruct((nh, r, c), F32),
        compiler_params=_cparams(("arbitrary",)),
        name="bias_lookup",
    )(tab_t, bkt)


def _rel_bucket(dist):
    max_exact = REL_BUCKETS // 2
    d = jnp.maximum(dist, 0)
    ratio = jnp.maximum(d, 1).astype(F32) / max_exact
    large = max_exact + (jnp.log(ratio) / math.log(REL_MAX_DIST / max_exact)
                         * (REL_BUCKETS - max_exact)).astype(jnp.int32)
    large = jnp.minimum(large, REL_BUCKETS - 1)
    return jnp.where(d < max_exact, d, large)


def _in_proj_kernel(x_ref, w_ref, o_ref, xb_ref):
    @pl.when(pl.program_id(1) == 0)
    def _():
        xb_ref[...] = x_ref[...].astype(BF16)

    o_ref[...] = jnp.dot(xb_ref[...], w_ref[...], preferred_element_type=F32).astype(BF16)


def _in_proj(x, w):
    n, d = x.shape
    p = w.shape[1]
    tm = min(1024, n)
    tn = p // 3
    return pl.pallas_call(
        _in_proj_kernel,
        grid=(n // tm, p // tn),
        in_specs=[pl.BlockSpec((tm, d), lambda i, j: (i, 0)),
                  pl.BlockSpec((d, tn), lambda i, j: (0, j))],
        out_specs=pl.BlockSpec((tm, tn), lambda i, j: (i, j)),
        out_shape=jax.ShapeDtypeStruct((n, p), BF16),
        scratch_shapes=[pltpu.VMEM((tm, d), BF16)],
        compiler_params=_cparams(("parallel", "arbitrary")),
        name="in_proj",
    )(x, w)


A_BLOCK = 256
A_SUB = 128
A_QROWS = 512
A_BIAS_TYPES = 5
LOG2E = math.log2(math.e)


def _attn_a_kernel(q_ref, k_ref, v_ref, bias_ref, lam_ref, g_ref, o_ref, qt_ref, s_ref, p_ref, acc_ref, *,
                   lam_init):
    t = A_BLOCK
    nch = A_QROWS // A_SUB
    per_key_block = t // A_SUB
    i = pl.program_id(2)
    lane = lax.broadcasted_iota(jnp.int32, (A_SUB, LANES), 1)
    for c in range(nch):
        q = q_ref[0, c * A_SUB:(c + 1) * A_SUB, :].astype(F32) * (A_QK_DIM ** -0.5 * LOG2E)
        qq = jnp.concatenate([jnp.where(lane < A_QK_DIM, q, 0.0), jnp.where(lane >= A_QK_DIM, q, 0.0)], axis=0)
        qt_ref[c] = qq.T.astype(BF16)
    first_block = i * (nch // per_key_block)
    n_blocks = first_block + nch // per_key_block

    def scores_into(slot, j):
        kj = k_ref[0, pl.ds(pl.multiple_of(j * t, t), t), :]
        for c in range(nch):
            back = first_block + c // per_key_block - j
            near = (0, 2) if c % per_key_block == 0 else (1, 3)
            kind = jnp.where(back == 0, near[0], jnp.where(back == 1, near[1], jnp.where(back < 0, 4, 3)))
            s_ref[slot, c] = jnp.dot(kj, qt_ref[c], preferred_element_type=F32) + bias_ref[0, kind]

    def values_from(slot, j, alphas):
        vj = v_ref[0, pl.ds(pl.multiple_of(j * t, t), t), :]
        for c in range(nch):
            acc_ref[c] = alphas[c] * acc_ref[c] + _dot_tn(vj, p_ref[slot, c])

    def sub_step(j, cur, carry):
        scores_into(1 - cur, jnp.minimum(j + 1, n_blocks - 1))
        new = []
        for c, (m_old, l_old, _) in enumerate(carry):
            s = s_ref[cur, c]
            m_new = jnp.maximum(m_old, jnp.max(s, axis=0, keepdims=True))
            p = jnp.exp2(s - m_new)
            new.append((m_new, jnp.exp2(m_old - m_new) * l_old + jnp.sum(p, axis=0, keepdims=True),
                        jnp.exp2(m_old - m_new)))
            p_ref[cur, c] = p.astype(BF16)
        values_from(1 - cur, jnp.maximum(j - 1, 0), [a for _, _, a in carry])
        return tuple(new)

    def pair(jj, carry):
        carry = sub_step(2 * jj, 0, carry)
        return sub_step(2 * jj + 1, 1, carry)

    scores_into(0, 0)
    p_ref[1] = jnp.zeros(p_ref.shape[1:], BF16)
    acc_ref[...] = jnp.zeros_like(acc_ref)
    init = tuple((jnp.full((1, 2 * A_SUB), NEG, F32), jnp.zeros((1, 2 * A_SUB), F32),
                  jnp.ones((1, 2 * A_SUB), F32)) for _ in range(nch))
    fin = lax.fori_loop(0, n_blocks // 2, pair, init)
    values_from(1, n_blocks - 1, [a for _, _, a in fin])
    lp = lam_ref[...]
    lam = (jnp.exp(jnp.sum(lp[0:1] * lp[1:2], axis=-1, keepdims=True))
           - jnp.exp(jnp.sum(lp[2:3] * lp[3:4], axis=-1, keepdims=True)) + lam_init)
    for c, (_, l_fin, _) in enumerate(fin):
        o_all = acc_ref[c] / l_fin
        o = (o_all[:, :A_SUB] - lam * o_all[:, A_SUB:]).T
        o = o * lax.rsqrt(jnp.mean(o * o, axis=-1, keepdims=True) + LN_EPS) * g_ref[...]
        o_ref[0, c * A_SUB:(c + 1) * A_SUB, :] = (o * (1.0 - lam_init)).astype(BF16)


def _attn_a(proj, bias_a, lam_params, subln_g, lam_init):
    bsz, s, _ = proj.shape
    t = A_QROWS
    nch = A_QROWS // A_SUB
    kb, vb = OFF_AK // LANES, OFF_AV // LANES
    return pl.pallas_call(
        functools.partial(_attn_a_kernel, lam_init=lam_init),
        grid=(bsz, A_HEADS, s // t),
        in_specs=[pl.BlockSpec((1, t, LANES), lambda b, h, i: (b, i, h)),
                  pl.BlockSpec((1, s, LANES), lambda b, h, i: (b, 0, kb + h)),
                  pl.BlockSpec((1, s, LANES), lambda b, h, i: (b, 0, vb + h)),
                  pl.BlockSpec((1, A_BIAS_TYPES, A_BLOCK, 2 * A_SUB), lambda b, h, i: (h, 0, 0, 0)),
                  pl.BlockSpec((4, A_QK_DIM), lambda b, h, i: (0, 0)),
                  pl.BlockSpec((1, A_V_DIM), lambda b, h, i: (0, 0))],
        out_specs=pl.BlockSpec((1, t, LANES), lambda b, h, i: (b, i, h)),
        out_shape=jax.ShapeDtypeStruct((bsz, s, A_WIDTH), BF16),
        scratch_shapes=[pltpu.VMEM((nch, LANES, 2 * A_SUB), BF16),
                        pltpu.VMEM((2, nch, A_BLOCK, 2 * A_SUB), F32),
                        pltpu.VMEM((2, nch, A_BLOCK, 2 * A_SUB), BF16),
                        pltpu.VMEM((nch, A_V_DIM, 2 * A_SUB), F32)],
        compiler_params=_cparams(("parallel", "parallel", "arbitrary")),
        name="attn_a",
    )(proj, proj, proj, bias_a, lam_params, subln_g)


def _swa_kernel(q_ref, kvp_ref, kvc_ref, bias_ref, sink_ref, o_ref):
    w = B_WINDOW
    d = B_HEAD_DIM
    group = B_HEADS // B_KV_HEADS
    table = jnp.minimum(pl.program_id(1), 1)
    row = lax.broadcasted_iota(jnp.int32, (LANES, w), 0)
    zeros = jnp.zeros((LANES - d, w), F32)
    kvs, q_ts = [], []
    for g in range(B_KV_HEADS):
        kvs.append(jnp.concatenate([kvp_ref[0, :, g * LANES:(g + 1) * LANES],
                                    kvc_ref[0, :, g * LANES:(g + 1) * LANES]], axis=0))
        cols = []
        for pair in range(group // 2):
            blk = g * (group // 2) + pair
            t = (q_ref[0, :, blk * LANES:(blk + 1) * LANES].astype(F32) * (d ** -0.5 * LOG2E)).T
            cols += [jnp.where(row < d, t, 0.0), jnp.concatenate([t[d:], zeros], axis=0)]
        q_ts.append(jnp.concatenate(cols, axis=1).astype(BF16))
    scores = [jnp.dot(kv, q_t, preferred_element_type=F32) + bias_ref[g, table]
              for g, (kv, q_t) in enumerate(zip(kvs, q_ts))]
    soft = []
    for g, s in enumerate(scores):
        sink = sink_ref[g] * LOG2E
        m = jnp.maximum(jnp.max(s, axis=0, keepdims=True), sink)
        e = jnp.exp2(s - m)
        soft.append((e.astype(BF16), 1.0 / (jnp.sum(e, axis=0, keepdims=True) + jnp.exp2(sink - m))))
    outs = [_dot_tn(kv, e) * inv for kv, (e, inv) in zip(kvs, soft)]
    for g, o_t in enumerate(outs):
        for pair in range(group // 2):
            blk = g * (group // 2) + pair
            both = jnp.concatenate([o_t[d:, (2 * pair) * w:(2 * pair + 1) * w],
                                    o_t[d:, (2 * pair + 1) * w:(2 * pair + 2) * w]], axis=0)
            o_ref[0, :, blk * LANES:(blk + 1) * LANES] = both.T.astype(BF16)


def _swa(proj, bias_b, sink_rows):
    bsz, s, _ = proj.shape
    w = B_WINDOW
    kvw = 2 * B_KV_HEADS * B_HEAD_DIM
    qb, kvb = OFF_BQ // B_WIDTH, OFF_BKV // kvw
    cols = (B_HEADS // B_KV_HEADS) * w
    return pl.pallas_call(
        _swa_kernel,
        grid=(bsz, s // w),
        in_specs=[pl.BlockSpec((1, w, B_WIDTH), lambda b, n: (b, n, qb)),
                  pl.BlockSpec((1, w, kvw), lambda b, n: (b, jnp.maximum(n - 1, 0), kvb)),
                  pl.BlockSpec((1, w, kvw), lambda b, n: (b, n, kvb)),
                  pl.BlockSpec((B_KV_HEADS, 2, 2 * w, cols), lambda b, n: (0, 0, 0, 0)),
                  pl.BlockSpec((B_KV_HEADS, 1, cols), lambda b, n: (0, 0, 0))],
        out_specs=pl.BlockSpec((1, w, B_WIDTH), lambda b, n: (b, n, 0)),
        out_shape=jax.ShapeDtypeStruct((bsz, s, B_WIDTH), BF16),
        compiler_params=_cparams(("parallel", "arbitrary")),
        name="swa",
    )(proj, proj, proj, bias_b, sink_rows)


def _ret_kernel(q_ref, k_ref, v_ref, g_ref, sin_ref, cos_ref, decay_ref, zeta_ref, xi_ref, gch_ref, o_ref,
                state_ref):
    c = C_CHUNK

    @pl.when(pl.program_id(1) == 0)
    def _():
        state_ref[...] = jnp.zeros_like(state_ref)

    lane = lax.broadcasted_iota(jnp.int32, (c, LANES), 1)
    even = (lane & 1) == 0
    sn = sin_ref[...]
    cs = cos_ref[...]

    def rope(x):
        swapped = jnp.where(even, pltpu.roll(x, LANES - 1, 1), pltpu.roll(x, 1, 1))
        return x * cs + swapped * sn

    qk = []
    for pair in range(C_HEADS // 2):
        q = rope(q_ref[0, :, pair * LANES:(pair + 1) * LANES].astype(F32))
        k = rope(k_ref[0, :, pair * LANES:(pair + 1) * LANES].astype(F32) * (C_QK_DIM ** -0.5))
        qk.append((q, q * xi_ref[pair], k.astype(BF16), (k * zeta_ref[pair]).astype(BF16)))
    masked = []
    for h in range(C_HEADS):
        q, qx, _, _ = qk[h // 2]
        in_head = (lane < C_QK_DIM) if h % 2 == 0 else (lane >= C_QK_DIM)
        masked.append((jnp.where(in_head, q, 0.0).astype(BF16), jnp.where(in_head, qx, 0.0).astype(BF16)))
    inner = [(_dot_nt(masked[h][0], qk[h // 2][2]) * decay_ref[h]).astype(BF16) for h in range(C_HEADS)]
    outs = []
    for h in range(C_HEADS):
        vh = v_ref[0, :, h * LANES:(h + 1) * LANES]
        st = state_ref[h]
        outs.append(jnp.dot(inner[h], vh, preferred_element_type=F32)
                    + jnp.dot(masked[h][1], st.astype(BF16), preferred_element_type=F32))
        state_ref[h] = st * gch_ref[h] + _dot_tn(qk[h // 2][3], vh)
    for h, o in enumerate(outs):
        mu = jnp.mean(o, axis=-1, keepdims=True)
        oc = o - mu
        o = oc * lax.rsqrt(jnp.mean(oc * oc, axis=-1, keepdims=True) + LN_EPS)
        gate = g_ref[0, :, h * LANES:(h + 1) * LANES].astype(F32)
        o_ref[0, :, h * LANES:(h + 1) * LANES] = (_silu(gate) * o).astype(BF16)


def _retention(proj, sin_t, cos_t, decay, zeta_t, xi_t, gch):
    bsz, s, _ = proj.shape
    c = C_CHUNK
    qw = C_HEADS * C_QK_DIM
    return pl.pallas_call(
        _ret_kernel,
        grid=(bsz, s // c),
        in_specs=[pl.BlockSpec((1, c, qw), lambda b, n: (b, n, OFF_CQ // qw)),
                  pl.BlockSpec((1, c, qw), lambda b, n: (b, n, OFF_CK // qw)),
                  pl.BlockSpec((1, c, C_WIDTH), lambda b, n: (b, n, OFF_CV // C_WIDTH)),
                  pl.BlockSpec((1, c, C_WIDTH), lambda b, n: (b, n, OFF_CG // C_WIDTH)),
                  pl.BlockSpec((c, LANES), lambda b, n: (n, 0)),
                  pl.BlockSpec((c, LANES), lambda b, n: (n, 0)),
                  pl.BlockSpec((C_HEADS, c, c), lambda b, n: (0, 0, 0)),
                  pl.BlockSpec((C_HEADS // 2, c, LANES), lambda b, n: (0, 0, 0)),
                  pl.BlockSpec((C_HEADS // 2, c, LANES), lambda b, n: (0, 0, 0)),
                  pl.BlockSpec((C_HEADS, 1, LANES), lambda b, n: (0, 0, 0))],
        out_specs=pl.BlockSpec((1, c, C_WIDTH), lambda b, n: (b, n, 0)),
        out_shape=jax.ShapeDtypeStruct((bsz, s, C_WIDTH), BF16),
        scratch_shapes=[pltpu.VMEM((C_HEADS, LANES, C_V_DIM), F32)],
        compiler_params=_cparams(("parallel", "arbitrary")),
        name="retention",
    )(proj, proj, proj, proj, sin_t, cos_t, decay, zeta_t, xi_t, gch)


def _out_proj_kernel(ya_ref, yb_ref, yc_ref, wa_ref, wb_ref, wc_ref, x_ref, g_ref, b_ref, o_ref):
    mix = (jnp.dot(ya_ref[...], wa_ref[...], preferred_element_type=F32)
           + jnp.dot(yb_ref[...], wb_ref[...], preferred_element_type=F32)
           + jnp.dot(yc_ref[...], wc_ref[...], preferred_element_type=F32))
    o_ref[...] = _layer_norm(ALPHA * x_ref[...] + mix, g_ref[...], b_ref[...])


def _out_proj_ln(ya, yb, yc, w, x, g, b):
    n, d = x.shape
    tm = min(512, n)
    row = lambda i: (i, 0)
    fixed = lambda i: (0, 0)
    return pl.pallas_call(
        _out_proj_kernel,
        grid=(n // tm,),
        in_specs=[pl.BlockSpec((tm, A_WIDTH), row),
                  pl.BlockSpec((tm, B_WIDTH), row),
                  pl.BlockSpec((tm, C_WIDTH), row),
                  pl.BlockSpec((A_WIDTH, d), lambda i: (0, 0)),
                  pl.BlockSpec((B_WIDTH, d), lambda i: (1, 0)),
                  pl.BlockSpec((C_WIDTH, d), lambda i: ((A_WIDTH + B_WIDTH) // C_WIDTH, 0)),
                  pl.BlockSpec((tm, d), row),
                  pl.BlockSpec((1, d), fixed),
                  pl.BlockSpec((1, d), fixed)],
        out_specs=pl.BlockSpec((tm, d), row),
        out_shape=jax.ShapeDtypeStruct((n, d), F32),
        compiler_params=_cparams(("parallel",)),
        name="out_proj_ln",
    )(ya, yb, yc, w, w, w, x, g, b)


FFN_TM = 512
FFN_TF = 512


def _swiglu_step(xb, wg, wu, wd):
    hg = jnp.dot(xb, wg, preferred_element_type=F32)
    hu = jnp.dot(xb, wu, preferred_element_type=F32)
    return jnp.dot((_silu(hg) * hu).astype(BF16), wd, preferred_element_type=F32)


def _ffn_kernel(x_ref, wg_ref, wu_ref, wd_ref, g_ref, b_ref, o_ref, xb_ref):
    f = pl.program_id(1)

    @pl.when(f == 0)
    def _():
        o_ref[...] = jnp.zeros_like(o_ref)
        xb_ref[...] = x_ref[...].astype(BF16)

    o_ref[...] += _swiglu_step(xb_ref[...], wg_ref[...], wu_ref[...], wd_ref[...])

    @pl.when(f == pl.num_programs(1) - 1)
    def _():
        o_ref[...] = _layer_norm(ALPHA * x_ref[...] + o_ref[...], g_ref[...], b_ref[...])


def _ffn(x, w_gate, w_up, w_down, g, b):
    n, d = x.shape
    ff = w_gate.shape[1]
    tm = min(FFN_TM, n)
    tf = FFN_TF
    return pl.pallas_call(
        _ffn_kernel,
        grid=(n // tm, ff // tf),
        in_specs=[pl.BlockSpec((tm, d), lambda i, f: (i, 0)),
                  pl.BlockSpec((d, tf), lambda i, f: (0, f)),
                  pl.BlockSpec((d, tf), lambda i, f: (0, f)),
                  pl.BlockSpec((tf, d), lambda i, f: (f, 0)),
                  pl.BlockSpec((1, d), lambda i, f: (0, 0)),
                  pl.BlockSpec((1, d), lambda i, f: (0, 0))],
        out_specs=pl.BlockSpec((tm, d), lambda i, f: (i, 0)),
        out_shape=jax.ShapeDtypeStruct((n, d), F32),
        scratch_shapes=[pltpu.VMEM((tm, d), BF16)],
        compiler_params=_cparams(("parallel", "arbitrary")),
        name="ffn_ln",
    )(x, w_gate, w_up, w_down, g, b)


def _row_copy(src, dst, sem, s, t):
    return pltpu.make_async_copy(src.at[pl.ds(s, 1)], dst.at[pl.ds(t, 1)], sem)


def _ffn_grouped_kernel(eid_ref, nact_ref, src_ref, x_hbm, wg_ref, wu_ref, wd_ref, o_ref, rows_ref, xb_ref, sem,
                        *, tm, per_step):
    i = pl.program_id(0)
    f = pl.program_id(1)
    nact = nact_ref[0]
    active = i < nact
    slot = lax.rem(i, 2)
    buf_rows = rows_ref.shape[1]
    last_slot_row = src_ref.shape[0] - 1

    def fetch(tile, into, first_row, count):
        for r in range(count):
            row = first_row + r
            token = src_ref[jnp.minimum(tile * tm + row, last_slot_row)]
            _row_copy(x_hbm, rows_ref.at[into], sem.at[into], token, row).start()

    def wait_buffer(which):
        pltpu.make_async_copy(x_hbm.at[pl.ds(0, buf_rows)], rows_ref.at[which], sem.at[which]).wait()

    @pl.when(jnp.logical_and(i == 0, f == 0))
    def _():
        lax.fori_loop(0, buf_rows // per_step, lambda s, c: (fetch(0, 0, s * per_step, per_step), c)[1], 0)

    @pl.when(jnp.logical_and(f == 0, i <= nact))
    def _():
        wait_buffer(slot)

    @pl.when(f == 0)
    def _():
        o_ref[...] = jnp.zeros_like(o_ref)

    @pl.when(jnp.logical_and(f == 0, active))
    def _():
        xb_ref[...] = rows_ref[slot, :tm, :].astype(BF16)

    @pl.when(active)
    def _():
        fetch(i + 1, 1 - slot, f * per_step, per_step)
        o_ref[...] += _swiglu_step(xb_ref[...], wg_ref[0], wu_ref[0], wd_ref[0])


def _ffn_grouped(x, src, eid, nact, w_gate, w_up, w_down):
    n, d = x.shape
    slots = src.shape[0]
    ff = w_gate.shape[2]
    tm = min(FFN_TM, n)
    tf = FFN_TF
    nf = ff // tf
    per_step = -(-tm // nf)
    per_step += (-per_step) % 8
    assert per_step * nf <= n

    def fidx(i, f, nact_ref):
        return jnp.where(i < nact_ref[0], f, nf - 1)

    grid_spec = pltpu.PrefetchScalarGridSpec(
        num_scalar_prefetch=3,
        grid=(slots // tm, nf),
        in_specs=[pl.BlockSpec(memory_space=pl.ANY),
                  pl.BlockSpec((1, d, tf), lambda i, f, e, a, s: (e[i], 0, fidx(i, f, a))),
                  pl.BlockSpec((1, d, tf), lambda i, f, e, a, s: (e[i], 0, fidx(i, f, a))),
                  pl.BlockSpec((1, tf, d), lambda i, f, e, a, s: (e[i], fidx(i, f, a), 0))],
        out_specs=pl.BlockSpec((tm, d), lambda i, f, e, a, s: (i, 0)),
        scratch_shapes=[pltpu.VMEM((2, per_step * nf, d), F32), pltpu.VMEM((tm, d), BF16),
                        pltpu.SemaphoreType.DMA((2,))],
    )
    return pl.pallas_call(
        functools.partial(_ffn_grouped_kernel, tm=tm, per_step=per_step),
        grid_spec=grid_spec,
        out_shape=jax.ShapeDtypeStruct((slots, d), F32),
        compiler_params=_cparams(("arbitrary", "arbitrary")),
        name="ffn_grouped",
    )(eid, nact, src, x, w_gate, w_up, w_down)


ROUTER_TM = 512


def _router_kernel(x_ref, wr_ref, tri_ref, meta_ref, cnt_ref, carry_ref):
    @pl.when(pl.program_id(0) == 0)
    def _():
        carry_ref[...] = jnp.zeros_like(carry_ref)

    logits = jnp.dot(x_ref[...], wr_ref[...], preferred_element_type=F32, precision=lax.Precision.HIGHEST)
    lane = lax.broadcasted_iota(jnp.int32, logits.shape, 1)
    logits = jnp.where(lane < N_EXPERTS, logits, -jnp.inf)
    m1 = jnp.max(logits, axis=-1, keepdims=True)
    i1 = jnp.min(jnp.where(logits == m1, lane, LANES), axis=-1, keepdims=True)
    rest = jnp.where(lane == i1, -jnp.inf, logits)
    m2 = jnp.max(rest, axis=-1, keepdims=True)
    i2 = jnp.min(jnp.where(rest == m2, lane, LANES), axis=-1, keepdims=True)
    e2 = jnp.exp(m2 - m1)
    w1 = 1.0 / (1.0 + e2)
    w2 = e2 * w1
    hit1 = lane == i1
    hit2 = lane == i2
    onehot = jnp.where(jnp.logical_or(hit1, hit2), 1.0, 0.0)
    before = jnp.dot(tri_ref[...], onehot.astype(BF16), preferred_element_type=F32) + carry_ref[...]
    r1 = jnp.sum(jnp.where(hit1, before, 0.0), axis=-1, keepdims=True)
    r2 = jnp.sum(jnp.where(hit2, before, 0.0), axis=-1, keepdims=True)
    carry_ref[...] = carry_ref[...] + jnp.sum(onehot, axis=0, keepdims=True)
    cnt_ref[...] = carry_ref[...]
    meta = jnp.where(lane == 0, i1.astype(F32), 0.0)
    meta = jnp.where(lane == 1, i2.astype(F32), meta)
    meta = jnp.where(lane == 2, r1, meta)
    meta = jnp.where(lane == 3, r2, meta)
    meta = jnp.where(lane == 4, w1, meta)
    meta = jnp.where(lane == 5, w2, meta)
    meta_ref[...] = meta


def _router(x, w_router):
    n, d = x.shape
    tm = min(ROUTER_TM, n)
    wr = jnp.zeros((d, LANES), F32).at[:, :N_EXPERTS].set(w_router.astype(F32))
    tri = jnp.asarray(np.tril(np.ones((tm, tm), np.float32), -1), BF16)
    return pl.pallas_call(
        _router_kernel,
        grid=(n // tm,),
        in_specs=[pl.BlockSpec((tm, d), lambda i: (i, 0)),
                  pl.BlockSpec((d, LANES), lambda i: (0, 0)),
                  pl.BlockSpec((tm, tm), lambda i: (0, 0))],
        out_specs=[pl.BlockSpec((tm, LANES), lambda i: (i, 0)),
                   pl.BlockSpec((1, LANES), lambda i: (0, 0))],
        out_shape=[jax.ShapeDtypeStruct((n, LANES), F32), jax.ShapeDtypeStruct((1, LANES), F32)],
        scratch_shapes=[pltpu.VMEM((1, LANES), F32)],
        compiler_params=_cparams(("arbitrary",)),
        name="router",
    )(x, wr, tri)


MOVE_TM = 256
ISSUE_UNROLL = 8


def _combine_kernel(d1_ref, d2_ref, ys_hbm, x_ref, meta_ref, g_ref, b_ref, o_ref, buf_ref, sem, *, tm):
    base = pl.program_id(0) * tm

    def issue(t, carry):
        row = base + t
        _row_copy(ys_hbm, buf_ref.at[0], sem, d1_ref[row], t).start()
        _row_copy(ys_hbm, buf_ref.at[1], sem, d2_ref[row], t).start()
        return carry

    lax.fori_loop(0, tm, issue, 0, unroll=ISSUE_UNROLL)

    def drain(t, carry):
        _row_copy(ys_hbm, buf_ref.at[0], sem, 0, 0).wait()
        _row_copy(ys_hbm, buf_ref.at[1], sem, 0, 0).wait()
        return carry

    lax.fori_loop(0, tm, drain, 0, unroll=ISSUE_UNROLL)
    meta = meta_ref[...]
    lane = lax.broadcasted_iota(jnp.int32, meta.shape, 1)
    w1 = jnp.sum(jnp.where(lane == 4, meta, 0.0), axis=-1, keepdims=True)
    w2 = jnp.sum(jnp.where(lane == 5, meta, 0.0), axis=-1, keepdims=True)
    f = w1 * buf_ref[0] + w2 * buf_ref[1]
    o_ref[...] = _layer_norm(ALPHA * x_ref[...] + f, g_ref[...], b_ref[...])


def _combine(ys, x, meta, d1, d2, g, b):
    n, d = x.shape
    tm = min(MOVE_TM, n)
    grid_spec = pltpu.PrefetchScalarGridSpec(
        num_scalar_prefetch=2,
        grid=(n // tm,),
        in_specs=[pl.BlockSpec(memory_space=pl.ANY),
                  pl.BlockSpec((tm, d), lambda i, a, c: (i, 0)),
                  pl.BlockSpec((tm, LANES), lambda i, a, c: (i, 0)),
                  pl.BlockSpec((1, d), lambda i, a, c: (0, 0)),
                  pl.BlockSpec((1, d), lambda i, a, c: (0, 0))],
        out_specs=pl.BlockSpec((tm, d), lambda i, a, c: (i, 0)),
        scratch_shapes=[pltpu.VMEM((2, tm, d), F32), pltpu.SemaphoreType.DMA(())],
    )
    return pl.pallas_call(
        functools.partial(_combine_kernel, tm=tm),
        grid_spec=grid_spec,
        out_shape=jax.ShapeDtypeStruct((n, d), F32),
        compiler_params=_cparams(("arbitrary",)),
        name="moe_combine",
    )(d1, d2, ys, x, meta, g, b)


def _moe(x, w_router, w_gate, w_up, w_down, g, b):
    n, d = x.shape
    tm = min(FFN_TM, n)
    meta, cnt = _router(x, w_router)
    i1 = meta[:, 0].astype(jnp.int32)
    i2 = meta[:, 1].astype(jnp.int32)
    counts = cnt[0, :N_EXPERTS].astype(jnp.int32)
    tiles = (counts + tm - 1) // tm
    tile_end = jnp.cumsum(tiles)
    group_start = (tile_end - tiles) * tm
    experts = jnp.arange(N_EXPERTS, dtype=jnp.int32)[None, :]
    start_of = lambda idx: jnp.sum(jnp.where(idx[:, None] == experts, group_start[None, :], 0), axis=1)
    d1 = start_of(i1) + meta[:, 2].astype(jnp.int32)
    d2 = start_of(i2) + meta[:, 3].astype(jnp.int32)
    max_tiles = (2 * n) // tm + N_EXPERTS
    tile_ids = jnp.arange(max_tiles, dtype=jnp.int32)[:, None]
    eid = jnp.minimum(jnp.sum((tile_ids >= tile_end[None, :]).astype(jnp.int32), axis=1), N_EXPERTS - 1)
    nact = tile_end[-1:].astype(jnp.int32)
    token = jnp.arange(n, dtype=jnp.int32)
    src = jnp.zeros((max_tiles * tm,), jnp.int32).at[jnp.concatenate([d1, d2])].set(
        jnp.concatenate([token, token]), unique_indices=True)
    ys = _ffn_grouped(x, src, eid.astype(jnp.int32), nact, w_gate, w_up, w_down)
    return _combine(ys, x, meta, d1, d2, g, b)


def _lambda_init(layer_idx):
    return 0.8 - 0.6 * math.exp(-0.3 * layer_idx)


def _static_tables(s):
    c = C_CHUNK
    ang = jnp.repeat(1.0 / (10000.0 ** jnp.linspace(0.0, 1.0, C_QK_DIM // 2, dtype=F32)), 2)
    ang = jnp.arange(s, dtype=F32)[:, None] * ang[None, :]
    sign = jnp.where(jnp.arange(C_QK_DIM) % 2 == 0, -1.0, 1.0).astype(F32)
    sin_t = jnp.tile(jnp.sin(ang) * sign[None, :], (1, 2))
    cos_t = jnp.tile(jnp.cos(ang), (1, 2))
    log_g = jnp.log(1.0 - jnp.exp2(-5.0 - jnp.arange(C_HEADS, dtype=F32)))
    pos = jnp.arange(c)
    rel = (pos[:, None] - pos[None, :]).astype(F32)
    decay = jnp.where((rel >= 0)[None], jnp.exp(jnp.maximum(rel, 0.0)[None] * log_g[:, None, None]), 0.0)
    zeta = jnp.exp((c - 1 - pos).astype(F32)[:, None] * log_g[None, :])
    xi = jnp.exp((pos + 1).astype(F32)[:, None] * log_g[None, :])
    per_pair = lambda t: jnp.repeat(t.T.reshape(C_HEADS // 2, 2, c), C_QK_DIM, axis=1).transpose(0, 2, 1)
    gch = jnp.broadcast_to(jnp.exp(c * log_g)[:, None, None], (C_HEADS, 1, LANES))
    dist_a = np.arange(A_SUB)[None, :] - np.arange(A_BLOCK)[:, None]
    types = [jnp.where(dist_a + off >= 0, _rel_bucket(jnp.asarray(dist_a + off)), REL_BUCKETS)
             for off in (0, A_SUB, A_BLOCK)]
    types.append(jnp.full(dist_a.shape, REL_BUCKETS - 1, jnp.int32))
    types.append(jnp.full(dist_a.shape, REL_BUCKETS, jnp.int32))
    bkt_a = jnp.concatenate([jnp.tile(b, (1, 2)) for b in types], axis=0).astype(jnp.int32)
    w = B_WINDOW
    dist = np.arange(w)[:, None] + w - np.arange(2 * w)[None, :]
    band = (dist >= 0) & (dist < w)
    has_prev = np.stack([np.broadcast_to(np.arange(2 * w)[None, :] >= w, band.shape), np.ones_like(band)])
    bkt_b = jnp.where(band[None] & has_prev, _rel_bucket(jnp.asarray(dist))[None], REL_BUCKETS)
    bkt_b = bkt_b.transpose(0, 2, 1).reshape(2 * 2 * w, w).astype(jnp.int32)
    return sin_t, cos_t, decay, per_pair(zeta), per_pair(xi), gch, bkt_a, bkt_b


def kernel(x, w_in, rel_bias, a_lambda, a_subln_g, b_sinks, w_out, ln_mix_g, ln_mix_b, ln_ffn_g, ln_ffn_b,
           dense_w_gate, dense_w_up, dense_w_down, moe_router, moe_w_gate, moe_w_up, moe_w_down):
    bsz, s, d = x.shape
    n = bsz * s
    sin_t, cos_t, decay, zeta_t, xi_t, gch, bkt_a, bkt_b = _static_tables(s)
    tab_t = rel_bias.astype(F32).T
    bias_a = _bias_lookup(tab_t[:A_HEADS] * LOG2E, bkt_a).reshape(A_HEADS, A_BIAS_TYPES, A_BLOCK, 2 * A_SUB)
    group = B_HEADS // B_KV_HEADS
    bias_b = _bias_lookup(tab_t[A_HEADS:] * LOG2E, bkt_b)
    bias_b = bias_b.reshape(B_KV_HEADS, group, 2, 2 * B_WINDOW, B_WINDOW).transpose(0, 2, 3, 1, 4)
    bias_b = bias_b.reshape(B_KV_HEADS, 2, 2 * B_WINDOW, group * B_WINDOW)
    xf = x.reshape(n, d).astype(F32)
    for l in range(DEPTH):
        w_in_l = jnp.concatenate([w_in[l][:, a:b].astype(BF16) for a, b in _PERM_RUNS], axis=1)
        proj = _in_proj(xf, w_in_l).reshape(bsz, s, PROJ_WIDTH)
        ya = _attn_a(proj, bias_a, a_lambda[l].astype(F32), a_subln_g[l].astype(F32).reshape(1, A_V_DIM),
                     _lambda_init(l))
        sink_rows = jnp.repeat(b_sinks[l].astype(F32).reshape(B_KV_HEADS, 1, group), B_WINDOW, axis=2)
        yb = _swa(proj, bias_b, sink_rows)
        yc = _retention(proj, sin_t, cos_t, decay, zeta_t, xi_t, gch)
        g_mix = ln_mix_g[l].astype(F32).reshape(1, d)
        b_mix = ln_mix_b[l].astype(F32).reshape(1, d)
        xf = _out_proj_ln(ya.reshape(n, A_WIDTH), yb.reshape(n, B_WIDTH), yc.reshape(n, C_WIDTH),
                          w_out[l].astype(BF16), xf, g_mix, b_mix)
        g_ffn = ln_ffn_g[l].astype(F32).reshape(1, d)
        b_ffn = ln_ffn_b[l].astype(F32).reshape(1, d)
        j = l // 2
        if l % 2 == 0:
            xf = _ffn(xf, dense_w_gate[j].astype(BF16), dense_w_up[j].astype(BF16), dense_w_down[j].astype(BF16),
                      g_ffn, b_ffn)
        else:
            xf = _moe(xf, moe_router[j], moe_w_gate[j].astype(BF16), moe_w_up[j].astype(BF16),
                      moe_w_down[j].astype(BF16), g_ffn, b_ffn)
    return xf.reshape(bsz, s, d).astype(x.dtype)
```

```python
import functools
import math

import jax
import jax.numpy as jnp
import numpy as np
from jax import lax
from jax.experimental import pallas as pl
from jax.experimental.pallas import tpu as pltpu

F32 = jnp.float32
BF16 = jnp.bfloat16

D_MODEL = 2048
DEPTH = 2
A_HEADS = 6
A_QK_DIM = 64
A_V_DIM = 128
B_HEADS = 12
B_KV_HEADS = 3
B_HEAD_DIM = 64
B_WINDOW = 128
C_HEADS = 4
C_QK_DIM = 64
C_V_DIM = 128
C_CHUNK = 128
A_WIDTH = A_HEADS * A_V_DIM
B_WIDTH = B_HEADS * B_HEAD_DIM
C_WIDTH = C_HEADS * C_V_DIM
REL_BUCKETS = 32
REL_MAX_DIST = 128
D_FF = 5632
N_EXPERTS = 8
ALPHA = (2.0 * DEPTH) ** 0.25
LN_EPS = 1e-5
NEG = -1e30

LANES = 128
VMEM_LIMIT = 56 * 1024 * 1024

_REF_SIZES = [768, 768, 768, 768, 192, 192, 256, 256, 512, 512]
_REF_OFF = [int(v) for v in np.concatenate([[0], np.cumsum(_REF_SIZES)[:-1]])]
PROJ_WIDTH = int(sum(_REF_SIZES))
OFF_AQ, OFF_AK, OFF_AV, OFF_BQ, OFF_CV, OFF_CG, OFF_CQ, OFF_CK, OFF_BKV = (
    0, 768, 1536, 2304, 3072, 3584, 4096, 4352, 4608)


def _proj_perm():
    aq, ak, av, bq, bk, bv, cq, ck, cv, cg = [np.arange(o, o + s) for o, s in zip(_REF_OFF, _REF_SIZES)]
    bkv = np.concatenate([np.concatenate([bk[g * 64:(g + 1) * 64], bv[g * 64:(g + 1) * 64]])
                          for g in range(B_KV_HEADS)])
    perm = np.concatenate([aq, ak, av, bq, cv, cg, cq, ck, bkv])
    assert perm.shape[0] == PROJ_WIDTH
    return perm


def _perm_runs():
    perm = _proj_perm()
    cuts = np.flatnonzero(np.diff(perm) != 1) + 1
    return [(int(r[0]), int(r[-1]) + 1) for r in np.split(perm, cuts)]


_PERM_RUNS = _perm_runs()


def _cparams(sem):
    return pltpu.CompilerParams(dimension_semantics=sem, vmem_limit_bytes=VMEM_LIMIT)


def _layer_norm(z, g, b):
    mu = jnp.mean(z, axis=-1, keepdims=True)
    zc = z - mu
    var = jnp.mean(zc * zc, axis=-1, keepdims=True)
    return zc * lax.rsqrt(var + LN_EPS) * g + b


def _silu(x):
    return x / (1.0 + jnp.exp(-x))


def _dot_nt(a, b):
    return lax.dot_general(a, b, (((1,), (1,)), ((), ())), preferred_element_type=F32)


def _dot_tn(a, b):
    return lax.dot_general(a, b, (((0,), (0,)), ((), ())), preferred_element_type=F32)


def _bias_kernel(tab_ref, bkt_ref, o_ref):
    h = pl.program_id(0)
    bkt = bkt_ref[...]
    acc = jnp.full(bkt.shape, NEG, F32)
    for b in range(REL_BUCKETS):
        acc = jnp.where(bkt == b, tab_ref[h, b], acc)
    o_ref[0] = acc


def _bias_lookup(tab_t, bkt):
    nh = tab_t.shape[0]
    r, c = bkt.shape
    return pl.pallas_call(
        _bias_kernel,
        grid=(nh,),
        in_specs=[pl.BlockSpec(memory_space=pltpu.SMEM),
                  pl.BlockSpec((r, c), lambda h: (0, 0))],
        out_specs=pl.BlockSpec((1, r, c), lambda h: (h, 0, 0)),
        out_shape=jax.ShapeDtypeStruct((nh, r, c), F32),
        compiler_params=_cparams(("arbitrary",)),
        name="bias_lookup",
    )(tab_t, bkt)


def _rel_bucket(dist):
    max_exact = REL_BUCKETS // 2
    d = jnp.maximum(dist, 0)
    ratio = jnp.maximum(d, 1).astype(F32) / max_exact
    large = max_exact + (jnp.log(ratio) / math.log(REL_MAX_DIST / max_exact)
                         * (REL_BUCKETS - max_exact)).astype(jnp.int32)
    large = jnp.minimum(large, REL_BUCKETS - 1)
    return jnp.where(d < max_exact, d, large)


def _in_proj_kernel(x_ref, w_ref, o_ref, xb_ref):
    @pl.when(pl.program_id(1) == 0)
    def _():
        xb_ref[...] = x_ref[...].astype(BF16)

    o_ref[...] = jnp.dot(xb_ref[...], w_ref[...], preferred_element_type=F32).astype(BF16)


def _in_proj(x, w):
    n, d = x.shape
    p = w.shape[1]
    tm = min(1024, n)
    tn = p // 3
    return pl.pallas_call(
        _in_proj_kernel,
        grid=(n // tm, p // tn),
        in_specs=[pl.BlockSpec((tm, d), lambda i, j: (i, 0)),
                  pl.BlockSpec((d, tn), lambda i, j: (0, j))],
        out_specs=pl.BlockSpec((tm, tn), lambda i, j: (i, j)),
        out_shape=jax.ShapeDtypeStruct((n, p), BF16),
        scratch_shapes=[pltpu.VMEM((tm, d), BF16)],
        compiler_params=_cparams(("parallel", "arbitrary")),
        name="in_proj",
    )(x, w)


A_BLOCK = 256
A_SUB = 128
A_QROWS = 512
A_BIAS_TYPES = 5
LOG2E = math.log2(math.e)


def _attn_a_kernel(q_ref, k_ref, v_ref, bias_ref, lam_ref, g_ref, o_ref, qt_ref, s_ref, p_ref, acc_ref, *,
                   lam_init):
    t = A_BLOCK
    nch = A_QROWS // A_SUB
    per_key_block = t // A_SUB
    i = pl.program_id(2)
    lane = lax.broadcasted_iota(jnp.int32, (A_SUB, LANES), 1)
    for c in range(nch):
        q = q_ref[0, c * A_SUB:(c + 1) * A_SUB, :].astype(F32) * (A_QK_DIM ** -0.5 * LOG2E)
        qq = jnp.concatenate([jnp.where(lane < A_QK_DIM, q, 0.0), jnp.where(lane >= A_QK_DIM, q, 0.0)], axis=0)
        qt_ref[c] = qq.T.astype(BF16)
    first_block = i * (nch // per_key_block)
    n_blocks = first_block + nch // per_key_block

    def scores_into(slot, j):
        kj = k_ref[0, pl.ds(pl.multiple_of(j * t, t), t), :]
        for c in range(nch):
            back = first_block + c // per_key_block - j
            near = (0, 2) if c % per_key_block == 0 else (1, 3)
            kind = jnp.where(back == 0, near[0], jnp.where(back == 1, near[1], jnp.where(back < 0, 4, 3)))
            s_ref[slot, c] = jnp.dot(kj, qt_ref[c], preferred_element_type=F32) + bias_ref[0, kind]

    def values_from(slot, j, alphas):
        vj = v_ref[0, pl.ds(pl.multiple_of(j * t, t), t), :]
        for c in range(nch):
            acc_ref[c] = alphas[c] * acc_ref[c] + _dot_tn(vj, p_ref[slot, c])

    def sub_step(j, cur, carry):
        scores_into(1 - cur, jnp.minimum(j + 1, n_blocks - 1))
        new = []
        for c, (m_old, l_old, _) in enumerate(carry):
            s = s_ref[cur, c]
            m_new = jnp.maximum(m_old, jnp.max(s, axis=0, keepdims=True))
            p = jnp.exp2(s - m_new)
            new.append((m_new, jnp.exp2(m_old - m_new) * l_old + jnp.sum(p, axis=0, keepdims=True),
                        jnp.exp2(m_old - m_new)))
            p_ref[cur, c] = p.astype(BF16)
        values_from(1 - cur, jnp.maximum(j - 1, 0), [a for _, _, a in carry])
        return tuple(new)

    def pair(jj, carry):
        carry = sub_step(2 * jj, 0, carry)
        return sub_step(2 * jj + 1, 1, carry)

    scores_into(0, 0)
    p_ref[1] = jnp.zeros(p_ref.shape[1:], BF16)
    acc_ref[...] = jnp.zeros_like(acc_ref)
    init = tuple((jnp.full((1, 2 * A_SUB), NEG, F32), jnp.zeros((1, 2 * A_SUB), F32),
                  jnp.ones((1, 2 * A_SUB), F32)) for _ in range(nch))
    fin = lax.fori_loop(0, n_blocks // 2, pair, init)
    values_from(1, n_blocks - 1, [a for _, _, a in fin])
    lp = lam_ref[...]
    lam = (jnp.exp(jnp.sum(lp[0:1] * lp[1:2], axis=-1, keepdims=True))
           - jnp.exp(jnp.sum(lp[2:3] * lp[3:4], axis=-1, keepdims=True)) + lam_init)
    for c, (_, l_fin, _) in enumerate(fin):
        o_all = acc_ref[c] / l_fin
        o = (o_all[:, :A_SUB] - lam * o_all[:, A_SUB:]).T
        o = o * lax.rsqrt(jnp.mean(o * o, axis=-1, keepdims=True) + LN_EPS) * g_ref[...]
        o_ref[0, c * A_SUB:(c + 1) * A_SUB, :] = (o * (1.0 - lam_init)).astype(BF16)


def _attn_a(proj, bias_a, lam_params, subln_g, lam_init):
    bsz, s, _ = proj.shape
    t = A_QROWS
    nch = A_QROWS // A_SUB
    kb, vb = OFF_AK // LANES, OFF_AV // LANES
    return pl.pallas_call(
        functools.partial(_attn_a_kernel, lam_init=lam_init),
        grid=(bsz, A_HEADS, s // t),
        in_specs=[pl.BlockSpec((1, t, LANES), lambda b, h, i: (b, i, h)),
                  pl.BlockSpec((1, s, LANES), lambda b, h, i: (b, 0, kb + h)),
                  pl.BlockSpec((1, s, LANES), lambda b, h, i: (b, 0, vb + h)),
                  pl.BlockSpec((1, A_BIAS_TYPES, A_BLOCK, 2 * A_SUB), lambda b, h, i: (h, 0, 0, 0)),
                  pl.BlockSpec((4, A_QK_DIM), lambda b, h, i: (0, 0)),
                  pl.BlockSpec((1, A_V_DIM), lambda b, h, i: (0, 0))],
        out_specs=pl.BlockSpec((1, t, LANES), lambda b, h, i: (b, i, h)),
        out_shape=jax.ShapeDtypeStruct((bsz, s, A_WIDTH), BF16),
        scratch_shapes=[pltpu.VMEM((nch, LANES, 2 * A_SUB), BF16),
                        pltpu.VMEM((2, nch, A_BLOCK, 2 * A_SUB), F32),
                        pltpu.VMEM((2, nch, A_BLOCK, 2 * A_SUB), BF16),
                        pltpu.VMEM((nch, A_V_DIM, 2 * A_SUB), F32)],
        compiler_params=_cparams(("parallel", "parallel", "arbitrary")),
        name="attn_a",
    )(proj, proj, proj, bias_a, lam_params, subln_g)


def _swa_kernel(q_ref, kvp_ref, kvc_ref, bias_ref, sink_ref, o_ref):
    w = B_WINDOW
    d = B_HEAD_DIM
    group = B_HEADS // B_KV_HEADS
    table = jnp.minimum(pl.program_id(1), 1)
    row = lax.broadcasted_iota(jnp.int32, (LANES, w), 0)
    zeros = jnp.zeros((LANES - d, w), F32)
    kvs, q_ts = [], []
    for g in range(B_KV_HEADS):
        kvs.append(jnp.concatenate([kvp_ref[0, :, g * LANES:(g + 1) * LANES],
                                    kvc_ref[0, :, g * LANES:(g + 1) * LANES]], axis=0))
        cols = []
        for pair in range(group // 2):
            blk = g * (group // 2) + pair
            t = (q_ref[0, :, blk * LANES:(blk + 1) * LANES].astype(F32) * (d ** -0.5 * LOG2E)).T
            cols += [jnp.where(row < d, t, 0.0), jnp.concatenate([t[d:], zeros], axis=0)]
        q_ts.append(jnp.concatenate(cols, axis=1).astype(BF16))
    scores = [jnp.dot(kv, q_t, preferred_element_type=F32) + bias_ref[g, table]
              for g, (kv, q_t) in enumerate(zip(kvs, q_ts))]
    soft = []
    for g, s in enumerate(scores):
        sink = sink_ref[g] * LOG2E
        m = jnp.maximum(jnp.max(s, axis=0, keepdims=True), sink)
        e = jnp.exp2(s - m)
        soft.append((e.astype(BF16), 1.0 / (jnp.sum(e, axis=0, keepdims=True) + jnp.exp2(sink - m))))
    outs = [_dot_tn(kv, e) * inv for kv, (e, inv) in zip(kvs, soft)]
    for g, o_t in enumerate(outs):
        for pair in range(group // 2):
            blk = g * (group // 2) + pair
            both = jnp.concatenate([o_t[d:, (2 * pair) * w:(2 * pair + 1) * w],
                                    o_t[d:, (2 * pair + 1) * w:(2 * pair + 2) * w]], axis=0)
            o_ref[0, :, blk * LANES:(blk + 1) * LANES] = both.T.astype(BF16)


def _swa(proj, bias_b, sink_rows):
    bsz, s, _ = proj.shape
    w = B_WINDOW
    kvw = 2 * B_KV_HEADS * B_HEAD_DIM
    qb, kvb = OFF_BQ // B_WIDTH, OFF_BKV // kvw
    cols = (B_HEADS // B_KV_HEADS) * w
    return pl.pallas_call(
        _swa_kernel,
        grid=(bsz, s // w),
        in_specs=[pl.BlockSpec((1, w, B_WIDTH), lambda b, n: (b, n, qb)),
                  pl.BlockSpec((1, w, kvw), lambda b, n: (b, jnp.maximum(n - 1, 0), kvb)),
                  pl.BlockSpec((1, w, kvw), lambda b, n: (b, n, kvb)),
                  pl.BlockSpec((B_KV_HEADS, 2, 2 * w, cols), lambda b, n: (0, 0, 0, 0)),
                  pl.BlockSpec((B_KV_HEADS, 1, cols), lambda b, n: (0, 0, 0))],
        out_specs=pl.BlockSpec((1, w, B_WIDTH), lambda b, n: (b, n, 0)),
        out_shape=jax.ShapeDtypeStruct((bsz, s, B_WIDTH), BF16),
        compiler_params=_cparams(("parallel", "arbitrary")),
        name="swa",
    )(proj, proj, proj, bias_b, sink_rows)


def _ret_kernel(q_ref, k_ref, v_ref, g_ref, sin_ref, cos_ref, decay_ref, zeta_ref, xi_ref, gch_ref, o_ref,
                state_ref):
    c = C_CHUNK

    @pl.when(pl.program_id(1) == 0)
    def _():
        state_ref[...] = jnp.zeros_like(state_ref)

    lane = lax.broadcasted_iota(jnp.int32, (c, LANES), 1)
    even = (lane & 1) == 0
    sn = sin_ref[...]
    cs = cos_ref[...]

    def rope(x):
        swapped = jnp.where(even, pltpu.roll(x, LANES - 1, 1), pltpu.roll(x, 1, 1))
        return x * cs + swapped * sn

    qk = []
    for pair in range(C_HEADS // 2):
        q = rope(q_ref[0, :, pair * LANES:(pair + 1) * LANES].astype(F32))
        k = rope(k_ref[0, :, pair * LANES:(pair + 1) * LANES].astype(F32) * (C_QK_DIM ** -0.5))
        qk.append((q, q * xi_ref[pair], k.astype(BF16), (k * zeta_ref[pair]).astype(BF16)))
    masked = []
    for h in range(C_HEADS):
        q, qx, _, _ = qk[h // 2]
        in_head = (lane < C_QK_DIM) if h % 2 == 0 else (lane >= C_QK_DIM)
        masked.append((jnp.where(in_head, q, 0.0).astype(BF16), jnp.where(in_head, qx, 0.0).astype(BF16)))
    inner = [(_dot_nt(masked[h][0], qk[h // 2][2]) * decay_ref[h]).astype(BF16) for h in range(C_HEADS)]
    outs = []
    for h in range(C_HEADS):
        vh = v_ref[0, :, h * LANES:(h + 1) * LANES]
        st = state_ref[h]
        outs.append(jnp.dot(inner[h], vh, preferred_element_type=F32)
                    + jnp.dot(masked[h][1], st.astype(BF16), preferred_element_type=F32))
        state_ref[h] = st * gch_ref[h] + _dot_tn(qk[h // 2][3], vh)
    for h, o in enumerate(outs):
        mu = jnp.mean(o, axis=-1, keepdims=True)
        oc = o - mu
        o = oc * lax.rsqrt(jnp.mean(oc * oc, axis=-1, keepdims=True) + LN_EPS)
        gate = g_ref[0, :, h * LANES:(h + 1) * LANES].astype(F32)
        o_ref[0, :, h * LANES:(h + 1) * LANES] = (_silu(gate) * o).astype(BF16)


def _retention(proj, sin_t, cos_t, decay, zeta_t, xi_t, gch):
    bsz, s, _ = proj.shape
    c = C_CHUNK
    qw = C_HEADS * C_QK_DIM
    return pl.pallas_call(
        _ret_kernel,
        grid=(bsz, s // c),
        in_specs=[pl.BlockSpec((1, c, qw), lambda b, n: (b, n, OFF_CQ // qw)),
                  pl.BlockSpec((1, c, qw), lambda b, n: (b, n, OFF_CK // qw)),
                  pl.BlockSpec((1, c, C_WIDTH), lambda b, n: (b, n, OFF_CV // C_WIDTH)),
                  pl.BlockSpec((1, c, C_WIDTH), lambda b, n: (b, n, OFF_CG // C_WIDTH)),
                  pl.BlockSpec((c, LANES), lambda b, n: (n, 0)),
                  pl.BlockSpec((c, LANES), lambda b, n: (n, 0)),
                  pl.BlockSpec((C_HEADS, c, c), lambda b, n: (0, 0, 0)),
                  pl.BlockSpec((C_HEADS // 2, c, LANES), lambda b, n: (0, 0, 0)),
                  pl.BlockSpec((C_HEADS // 2, c, LANES), lambda b, n: (0, 0, 0)),
                  pl.BlockSpec((C_HEADS, 1, LANES), lambda b, n: (0, 0, 0))],
        out_specs=pl.BlockSpec((1, c, C_WIDTH), lambda b, n: (b, n, 0)),
        out_shape=jax.ShapeDtypeStruct((bsz, s, C_WIDTH), BF16),
        scratch_shapes=[pltpu.VMEM((C_HEADS, LANES, C_V_DIM), F32)],
        compiler_params=_cparams(("parallel", "arbitrary")),
        name="retention",
    )(proj, proj, proj, proj, sin_t, cos_t, decay, zeta_t, xi_t, gch)


def _cast_riders(riders_in, riders_out):
    for src, dst in zip(riders_in, riders_out):
        dst[...] = src[...].astype(BF16)


def _row_split_riders(riders, steps):
    views = [r.reshape(r.size // r.shape[-1], r.shape[-1]) for r in riders]
    for v in views:
        assert v.shape[0] % (steps * 16) == 0, v.shape
    return views, [(v.shape[0] // steps, v.shape[1]) for v in views]


def _out_proj_kernel(ya_ref, yb_ref, yc_ref, wa_ref, wb_ref, wc_ref, x_ref, g_ref, b_ref, *rest):
    n_riders = (len(rest) - 1) // 2
    o_ref = rest[n_riders]
    mix = (jnp.dot(ya_ref[...], wa_ref[...], preferred_element_type=F32)
           + jnp.dot(yb_ref[...], wb_ref[...], preferred_element_type=F32)
           + jnp.dot(yc_ref[...], wc_ref[...], preferred_element_type=F32))
    o_ref[...] = _layer_norm(ALPHA * x_ref[...] + mix, g_ref[...], b_ref[...])
    _cast_riders(rest[:n_riders], rest[n_riders + 1:])


def _out_proj_ln(ya, yb, yc, w, x, g, b, riders=()):
    n, d = x.shape
    tm = min(512, n)
    row = lambda i: (i, 0)
    fixed = lambda i: (0, 0)
    once = pl.Buffered(1)
    views, blocks = _row_split_riders(riders, n // tm)
    rider_specs = [pl.BlockSpec(blk, row) for blk in blocks]
    outs = pl.pallas_call(
        _out_proj_kernel,
        grid=(n // tm,),
        in_specs=[pl.BlockSpec((tm, A_WIDTH), row),
                  pl.BlockSpec((tm, B_WIDTH), row),
                  pl.BlockSpec((tm, C_WIDTH), row),
                  pl.BlockSpec((A_WIDTH, d), lambda i: (0, 0), pipeline_mode=once),
                  pl.BlockSpec((B_WIDTH, d), lambda i: (1, 0), pipeline_mode=once),
                  pl.BlockSpec((C_WIDTH, d), lambda i: ((A_WIDTH + B_WIDTH) // C_WIDTH, 0), pipeline_mode=once),
                  pl.BlockSpec((tm, d), row),
                  pl.BlockSpec((1, d), fixed),
                  pl.BlockSpec((1, d), fixed)] + rider_specs,
        out_specs=[pl.BlockSpec((tm, d), row)] + rider_specs,
        out_shape=[jax.ShapeDtypeStruct((n, d), F32)] + [jax.ShapeDtypeStruct(v.shape, BF16) for v in views],
        compiler_params=_cparams(("arbitrary",)),
        name="out_proj_ln",
    )(ya, yb, yc, w, w, w, x, g, b, *views)
    return outs[0], [o.reshape(r.shape) for o, r in zip(outs[1:], riders)]


FFN_TM = 512
FFN_TF = 512


def _swiglu_step(xb, wg, wu, wd):
    hg = jnp.dot(xb, wg, preferred_element_type=F32)
    hu = jnp.dot(xb, wu, preferred_element_type=F32)
    return jnp.dot((_silu(hg) * hu).astype(BF16), wd, preferred_element_type=F32)


def _ffn_kernel(x_ref, wg_ref, wu_ref, wd_ref, g_ref, b_ref, *rest):
    n_riders = (len(rest) - 2) // 2
    riders_in, o_ref = rest[:n_riders], rest[n_riders]
    riders_out, xb_ref = rest[n_riders + 1:-1], rest[-1]
    f = pl.program_id(1)

    @pl.when(f == 0)
    def _():
        o_ref[...] = jnp.zeros_like(o_ref)
        xb_ref[...] = x_ref[...].astype(BF16)

    o_ref[...] += _swiglu_step(xb_ref[...], wg_ref[...], wu_ref[...], wd_ref[...])
    _cast_riders(riders_in, riders_out)

    @pl.when(f == pl.num_programs(1) - 1)
    def _():
        o_ref[...] = _layer_norm(ALPHA * x_ref[...] + o_ref[...], g_ref[...], b_ref[...])


def _ffn(x, w_gate, w_up, w_down, g, b, riders=()):
    n, d = x.shape
    ff = w_gate.shape[1]
    tm = min(FFN_TM, n)
    tf = FFN_TF
    ni, nf = n // tm, ff // tf
    flat, rider_specs = [], []
    for r in riders:
        cols = r.shape[-1]
        rows = r.size // cols
        flat.append(r.reshape(rows, cols))
        if rows % (ni * nf * 16) == 0:
            rider_specs.append(pl.BlockSpec((rows // (ni * nf), cols), lambda i, f: (i * nf + f, 0)))
        else:
            assert rows % (ni * 16) == 0 and cols % (nf * LANES) == 0
            rider_specs.append(pl.BlockSpec((rows // ni, cols // nf), lambda i, f: (i, f)))
    outs = pl.pallas_call(
        _ffn_kernel,
        grid=(n // tm, ff // tf),
        in_specs=[pl.BlockSpec((tm, d), lambda i, f: (i, 0)),
                  pl.BlockSpec((d, tf), lambda i, f: (0, f)),
                  pl.BlockSpec((d, tf), lambda i, f: (0, f)),
                  pl.BlockSpec((tf, d), lambda i, f: (f, 0)),
                  pl.BlockSpec((1, d), lambda i, f: (0, 0)),
                  pl.BlockSpec((1, d), lambda i, f: (0, 0))] + rider_specs,
        out_specs=[pl.BlockSpec((tm, d), lambda i, f: (i, 0))] + rider_specs,
        out_shape=[jax.ShapeDtypeStruct((n, d), F32)] + [jax.ShapeDtypeStruct(r.shape, BF16) for r in flat],
        scratch_shapes=[pltpu.VMEM((tm, d), BF16)],
        compiler_params=_cparams(("arbitrary", "arbitrary")),
        name="ffn_ln",
    )(x, w_gate, w_up, w_down, g, b, *flat)
    return outs[0], [o.reshape(r.shape) for o, r in zip(outs[1:], riders)]


def _row_copy(src, dst, sem, s, t):
    return pltpu.make_async_copy(src.at[pl.ds(s, 1)], dst.at[pl.ds(t, 1)], sem)


def _ffn_grouped_kernel(eid_ref, nact_ref, src_ref, x_hbm, wg_ref, wu_ref, wd_ref, o_ref, rows_ref, xb_ref, sem,
                        *, tm, per_step):
    i = pl.program_id(0)
    f = pl.program_id(1)
    nact = nact_ref[0]
    active = i < nact
    slot = lax.rem(i, 2)
    buf_rows = rows_ref.shape[1]
    last_slot_row = src_ref.shape[0] - 1

    def fetch(tile, into, first_row, count):
        for r in range(count):
            row = first_row + r
            token = src_ref[jnp.minimum(tile * tm + row, last_slot_row)]
            _row_copy(x_hbm, rows_ref.at[into], sem.at[into], token, row).start()

    def wait_buffer(which):
        pltpu.make_async_copy(x_hbm.at[pl.ds(0, buf_rows)], rows_ref.at[which], sem.at[which]).wait()

    @pl.when(jnp.logical_and(i == 0, f == 0))
    def _():
        lax.fori_loop(0, buf_rows // per_step, lambda s, c: (fetch(0, 0, s * per_step, per_step), c)[1], 0)

    @pl.when(jnp.logical_and(f == 0, i <= nact))
    def _():
        wait_buffer(slot)

    @pl.when(f == 0)
    def _():
        o_ref[...] = jnp.zeros_like(o_ref)

    @pl.when(jnp.logical_and(f == 0, active))
    def _():
        xb_ref[...] = rows_ref[slot, :tm, :].astype(BF16)

    @pl.when(active)
    def _():
        fetch(i + 1, 1 - slot, f * per_step, per_step)
        o_ref[...] += _swiglu_step(xb_ref[...], wg_ref[0], wu_ref[0], wd_ref[0])


def _ffn_grouped(x, src, eid, nact, w_gate, w_up, w_down):
    n, d = x.shape
    slots = src.shape[0]
    ff = w_gate.shape[2]
    tm = min(FFN_TM, n)
    tf = FFN_TF
    nf = ff // tf
    per_step = -(-tm // nf)
    per_step += (-per_step) % 8
    assert per_step * nf <= n

    def fidx(i, f, nact_ref):
        return jnp.where(i < nact_ref[0], f, nf - 1)

    grid_spec = pltpu.PrefetchScalarGridSpec(
        num_scalar_prefetch=3,
        grid=(slots // tm, nf),
        in_specs=[pl.BlockSpec(memory_space=pl.ANY),
                  pl.BlockSpec((1, d, tf), lambda i, f, e, a, s: (e[i], 0, fidx(i, f, a))),
                  pl.BlockSpec((1, d, tf), lambda i, f, e, a, s: (e[i], 0, fidx(i, f, a))),
                  pl.BlockSpec((1, tf, d), lambda i, f, e, a, s: (e[i], fidx(i, f, a), 0))],
        out_specs=pl.BlockSpec((tm, d), lambda i, f, e, a, s: (i, 0)),
        scratch_shapes=[pltpu.VMEM((2, per_step * nf, d), F32), pltpu.VMEM((tm, d), BF16),
                        pltpu.SemaphoreType.DMA((2,))],
    )
    return pl.pallas_call(
        functools.partial(_ffn_grouped_kernel, tm=tm, per_step=per_step),
        grid_spec=grid_spec,
        out_shape=jax.ShapeDtypeStruct((slots, d), F32),
        compiler_params=_cparams(("arbitrary", "arbitrary")),
        name="ffn_grouped",
    )(eid, nact, src, x, w_gate, w_up, w_down)


ROUTER_TM = 512


def _router_kernel(x_ref, wr_ref, tri_ref, meta_ref, cnt_ref, carry_ref):
    @pl.when(pl.program_id(0) == 0)
    def _():
        carry_ref[...] = jnp.zeros_like(carry_ref)

    logits = jnp.dot(x_ref[...], wr_ref[...], preferred_element_type=F32, precision=lax.Precision.HIGHEST)
    lane = lax.broadcasted_iota(jnp.int32, logits.shape, 1)
    logits = jnp.where(lane < N_EXPERTS, logits, -jnp.inf)
    m1 = jnp.max(logits, axis=-1, keepdims=True)
    i1 = jnp.min(jnp.where(logits == m1, lane, LANES), axis=-1, keepdims=True)
    rest = jnp.where(lane == i1, -jnp.inf, logits)
    m2 = jnp.max(rest, axis=-1, keepdims=True)
    i2 = jnp.min(jnp.where(rest == m2, lane, LANES), axis=-1, keepdims=True)
    e2 = jnp.exp(m2 - m1)
    w1 = 1.0 / (1.0 + e2)
    w2 = e2 * w1
    hit1 = lane == i1
    hit2 = lane == i2
    onehot = jnp.where(jnp.logical_or(hit1, hit2), 1.0, 0.0)
    before = jnp.dot(tri_ref[...], onehot.astype(BF16), preferred_element_type=F32) + carry_ref[...]
    r1 = jnp.sum(jnp.where(hit1, before, 0.0), axis=-1, keepdims=True)
    r2 = jnp.sum(jnp.where(hit2, before, 0.0), axis=-1, keepdims=True)
    carry_ref[...] = carry_ref[...] + jnp.sum(onehot, axis=0, keepdims=True)
    cnt_ref[...] = carry_ref[...]
    meta = jnp.where(lane == 0, i1.astype(F32), 0.0)
    meta = jnp.where(lane == 1, i2.astype(F32), meta)
    meta = jnp.where(lane == 2, r1, meta)
    meta = jnp.where(lane == 3, r2, meta)
    meta = jnp.where(lane == 4, w1, meta)
    meta = jnp.where(lane == 5, w2, meta)
    meta_ref[...] = meta


def _router(x, w_router):
    n, d = x.shape
    tm = min(ROUTER_TM, n)
    wr = jnp.zeros((d, LANES), F32).at[:, :N_EXPERTS].set(w_router.astype(F32))
    tri = jnp.asarray(np.tril(np.ones((tm, tm), np.float32), -1), BF16)
    return pl.pallas_call(
        _router_kernel,
        grid=(n // tm,),
        in_specs=[pl.BlockSpec((tm, d), lambda i: (i, 0)),
                  pl.BlockSpec((d, LANES), lambda i: (0, 0)),
                  pl.BlockSpec((tm, tm), lambda i: (0, 0))],
        out_specs=[pl.BlockSpec((tm, LANES), lambda i: (i, 0)),
                   pl.BlockSpec((1, LANES), lambda i: (0, 0))],
        out_shape=[jax.ShapeDtypeStruct((n, LANES), F32), jax.ShapeDtypeStruct((1, LANES), F32)],
        scratch_shapes=[pltpu.VMEM((1, LANES), F32)],
        compiler_params=_cparams(("arbitrary",)),
        name="router",
    )(x, wr, tri)


MOVE_TM = 256
ISSUE_UNROLL = 8


def _combine_kernel(d1_ref, d2_ref, ys_hbm, x_ref, meta_ref, g_ref, b_ref, o_ref, buf_ref, sem, *, tm):
    base = pl.program_id(0) * tm

    def issue(t, carry):
        row = base + t
        _row_copy(ys_hbm, buf_ref.at[0], sem, d1_ref[row], t).start()
        _row_copy(ys_hbm, buf_ref.at[1], sem, d2_ref[row], t).start()
        return carry

    lax.fori_loop(0, tm, issue, 0, unroll=ISSUE_UNROLL)

    def drain(t, carry):
        _row_copy(ys_hbm, buf_ref.at[0], sem, 0, 0).wait()
        _row_copy(ys_hbm, buf_ref.at[1], sem, 0, 0).wait()
        return carry

    lax.fori_loop(0, tm, drain, 0, unroll=ISSUE_UNROLL)
    meta = meta_ref[...]
    lane = lax.broadcasted_iota(jnp.int32, meta.shape, 1)
    w1 = jnp.sum(jnp.where(lane == 4, meta, 0.0), axis=-1, keepdims=True)
    w2 = jnp.sum(jnp.where(lane == 5, meta, 0.0), axis=-1, keepdims=True)
    f = w1 * buf_ref[0] + w2 * buf_ref[1]
    o_ref[...] = _layer_norm(ALPHA * x_ref[...] + f, g_ref[...], b_ref[...])


def _combine(ys, x, meta, d1, d2, g, b):
    n, d = x.shape
    tm = min(MOVE_TM, n)
    grid_spec = pltpu.PrefetchScalarGridSpec(
        num_scalar_prefetch=2,
        grid=(n // tm,),
        in_specs=[pl.BlockSpec(memory_space=pl.ANY),
                  pl.BlockSpec((tm, d), lambda i, a, c: (i, 0)),
                  pl.BlockSpec((tm, LANES), lambda i, a, c: (i, 0)),
                  pl.BlockSpec((1, d), lambda i, a, c: (0, 0)),
                  pl.BlockSpec((1, d), lambda i, a, c: (0, 0))],
        out_specs=pl.BlockSpec((tm, d), lambda i, a, c: (i, 0)),
        scratch_shapes=[pltpu.VMEM((2, tm, d), F32), pltpu.SemaphoreType.DMA(())],
    )
    return pl.pallas_call(
        functools.partial(_combine_kernel, tm=tm),
        grid_spec=grid_spec,
        out_shape=jax.ShapeDtypeStruct((n, d), F32),
        compiler_params=_cparams(("arbitrary",)),
        name="moe_combine",
    )(d1, d2, ys, x, meta, g, b)


def _moe(x, w_router, w_gate, w_up, w_down, g, b):
    n, d = x.shape
    tm = min(FFN_TM, n)
    meta, cnt = _router(x, w_router)
    i1 = meta[:, 0].astype(jnp.int32)
    i2 = meta[:, 1].astype(jnp.int32)
    counts = cnt[0, :N_EXPERTS].astype(jnp.int32)
    tiles = (counts + tm - 1) // tm
    tile_end = jnp.cumsum(tiles)
    group_start = (tile_end - tiles) * tm
    experts = jnp.arange(N_EXPERTS, dtype=jnp.int32)[None, :]
    start_of = lambda idx: jnp.sum(jnp.where(idx[:, None] == experts, group_start[None, :], 0), axis=1)
    d1 = start_of(i1) + meta[:, 2].astype(jnp.int32)
    d2 = start_of(i2) + meta[:, 3].astype(jnp.int32)
    max_tiles = (2 * n) // tm + N_EXPERTS
    tile_ids = jnp.arange(max_tiles, dtype=jnp.int32)[:, None]
    eid = jnp.minimum(jnp.sum((tile_ids >= tile_end[None, :]).astype(jnp.int32), axis=1), N_EXPERTS - 1)
    nact = tile_end[-1:].astype(jnp.int32)
    token = jnp.arange(n, dtype=jnp.int32)
    src = jnp.zeros((max_tiles * tm,), jnp.int32).at[jnp.concatenate([d1, d2])].set(
        jnp.concatenate([token, token]), unique_indices=True)
    ys = _ffn_grouped(x, src, eid.astype(jnp.int32), nact, w_gate, w_up, w_down)
    return _combine(ys, x, meta, d1, d2, g, b)


def _lambda_init(layer_idx):
    return 0.8 - 0.6 * math.exp(-0.3 * layer_idx)


def _static_tables(s):
    c = C_CHUNK
    ang = jnp.repeat(1.0 / (10000.0 ** jnp.linspace(0.0, 1.0, C_QK_DIM // 2, dtype=F32)), 2)
    ang = jnp.arange(s, dtype=F32)[:, None] * ang[None, :]
    sign = jnp.where(jnp.arange(C_QK_DIM) % 2 == 0, -1.0, 1.0).astype(F32)
    sin_t = jnp.tile(jnp.sin(ang) * sign[None, :], (1, 2))
    cos_t = jnp.tile(jnp.cos(ang), (1, 2))
    log_g = jnp.log(1.0 - jnp.exp2(-5.0 - jnp.arange(C_HEADS, dtype=F32)))
    pos = jnp.arange(c)
    rel = (pos[:, None] - pos[None, :]).astype(F32)
    decay = jnp.where((rel >= 0)[None], jnp.exp(jnp.maximum(rel, 0.0)[None] * log_g[:, None, None]), 0.0)
    zeta = jnp.exp((c - 1 - pos).astype(F32)[:, None] * log_g[None, :])
    xi = jnp.exp((pos + 1).astype(F32)[:, None] * log_g[None, :])
    per_pair = lambda t: jnp.repeat(t.T.reshape(C_HEADS // 2, 2, c), C_QK_DIM, axis=1).transpose(0, 2, 1)
    gch = jnp.broadcast_to(jnp.exp(c * log_g)[:, None, None], (C_HEADS, 1, LANES))
    dist_a = np.arange(A_SUB)[None, :] - np.arange(A_BLOCK)[:, None]
    types = [jnp.where(dist_a + off >= 0, _rel_bucket(jnp.asarray(dist_a + off)), REL_BUCKETS)
             for off in (0, A_SUB, A_BLOCK)]
    types.append(jnp.full(dist_a.shape, REL_BUCKETS - 1, jnp.int32))
    types.append(jnp.full(dist_a.shape, REL_BUCKETS, jnp.int32))
    bkt_a = jnp.concatenate([jnp.tile(b, (1, 2)) for b in types], axis=0).astype(jnp.int32)
    w = B_WINDOW
    dist = np.arange(w)[:, None] + w - np.arange(2 * w)[None, :]
    band = (dist >= 0) & (dist < w)
    has_prev = np.stack([np.broadcast_to(np.arange(2 * w)[None, :] >= w, band.shape), np.ones_like(band)])
    bkt_b = jnp.where(band[None] & has_prev, _rel_bucket(jnp.asarray(dist))[None], REL_BUCKETS)
    bkt_b = bkt_b.transpose(0, 2, 1).reshape(2 * 2 * w, w).astype(jnp.int32)
    return sin_t, cos_t, decay, per_pair(zeta), per_pair(xi), gch, bkt_a, bkt_b


def kernel(x, w_in, rel_bias, a_lambda, a_subln_g, b_sinks, w_out, ln_mix_g, ln_mix_b, ln_ffn_g, ln_ffn_b,
           dense_w_gate, dense_w_up, dense_w_down, moe_router, moe_w_gate, moe_w_up, moe_w_down):
    bsz, s, d = x.shape
    n = bsz * s
    sin_t, cos_t, decay, zeta_t, xi_t, gch, bkt_a, bkt_b = _static_tables(s)
    tab_t = rel_bias.astype(F32).T
    bias_a = _bias_lookup(tab_t[:A_HEADS] * LOG2E, bkt_a).reshape(A_HEADS, A_BIAS_TYPES, A_BLOCK, 2 * A_SUB)
    group = B_HEADS // B_KV_HEADS
    bias_b = _bias_lookup(tab_t[A_HEADS:] * LOG2E, bkt_b)
    bias_b = bias_b.reshape(B_KV_HEADS, group, 2, 2 * B_WINDOW, B_WINDOW).transpose(0, 2, 3, 1, 4)
    bias_b = bias_b.reshape(B_KV_HEADS, 2, 2 * B_WINDOW, group * B_WINDOW)
    xf = x.reshape(n, d).astype(F32)
    for l in range(DEPTH):
        w_in_l = jnp.concatenate([w_in[l][:, a:b].astype(BF16) for a, b in _PERM_RUNS], axis=1)
        proj = _in_proj(xf, w_in_l).reshape(bsz, s, PROJ_WIDTH)
        ya = _attn_a(proj, bias_a, a_lambda[l].astype(F32), a_subln_g[l].astype(F32).reshape(1, A_V_DIM),
                     _lambda_init(l))
        sink_rows = jnp.repeat(b_sinks[l].astype(F32).reshape(B_KV_HEADS, 1, group), B_WINDOW, axis=2)
        yb = _swa(proj, bias_b, sink_rows)
        yc = _retention(proj, sin_t, cos_t, decay, zeta_t, xi_t, gch)
        g_mix = ln_mix_g[l].astype(F32).reshape(1, d)
        b_mix = ln_mix_b[l].astype(F32).reshape(1, d)
        j = l // 2
        dense = (dense_w_gate[j], dense_w_up[j], dense_w_down[j]) if l % 2 == 0 else ()
        xf, dense_bf16 = _out_proj_ln(ya.reshape(n, A_WIDTH), yb.reshape(n, B_WIDTH), yc.reshape(n, C_WIDTH),
                                      w_out[l].astype(BF16), xf, g_mix, b_mix, dense)
        g_ffn = ln_ffn_g[l].astype(F32).reshape(1, d)
        b_ffn = ln_ffn_b[l].astype(F32).reshape(1, d)
        if l % 2 == 0:
            experts = (moe_w_gate[j], moe_w_up[j], moe_w_down[j]) if l + 1 < DEPTH else ()
            xf, moe_bf16 = _ffn(xf, *dense_bf16, g_ffn, b_ffn, experts)
        else:
            assert l > 0, "expert weights are cast by the preceding dense layer"
            xf = _moe(xf, moe_router[j], *moe_bf16, g_ffn, b_ffn)
    return xf.reshape(bsz, s, d).astype(x.dtype)
```

```python
import functools
import math

import jax
import jax.numpy as jnp
import numpy as np
from jax import lax
from jax.experimental import pallas as pl
from jax.experimental.pallas import tpu as pltpu

F32 = jnp.float32
BF16 = jnp.bfloat16

D_MODEL = 2048
DEPTH = 2
A_HEADS = 6
A_QK_DIM = 64
A_V_DIM = 128
B_HEADS = 12
B_KV_HEADS = 3
B_HEAD_DIM = 64
B_WINDOW = 128
C_HEADS = 4
C_QK_DIM = 64
C_V_DIM = 128
C_CHUNK = 128
A_WIDTH = A_HEADS * A_V_DIM
B_WIDTH = B_HEADS * B_HEAD_DIM
C_WIDTH = C_HEADS * C_V_DIM
REL_BUCKETS = 32
REL_MAX_DIST = 128
D_FF = 5632
N_EXPERTS = 8
ALPHA = (2.0 * DEPTH) ** 0.25
LN_EPS = 1e-5
NEG = -1e30

LANES = 128
VMEM_LIMIT = 56 * 1024 * 1024

_REF_SIZES = [768, 768, 768, 768, 192, 192, 256, 256, 512, 512]
_REF_OFF = [int(v) for v in np.concatenate([[0], np.cumsum(_REF_SIZES)[:-1]])]
PROJ_WIDTH = int(sum(_REF_SIZES))
OFF_AQ, OFF_AK, OFF_AV, OFF_BQ, OFF_CV, OFF_CG, OFF_CQ, OFF_CK, OFF_BKV = (
    0, 768, 1536, 2304, 3072, 3584, 4096, 4352, 4608)


def _proj_perm():
    aq, ak, av, bq, bk, bv, cq, ck, cv, cg = [np.arange(o, o + s) for o, s in zip(_REF_OFF, _REF_SIZES)]
    bkv = np.concatenate([np.concatenate([bk[g * 64:(g + 1) * 64], bv[g * 64:(g + 1) * 64]])
                          for g in range(B_KV_HEADS)])
    perm = np.concatenate([aq, ak, av, bq, cv, cg, cq, ck, bkv])
    assert perm.shape[0] == PROJ_WIDTH
    return perm


def _perm_runs():
    perm = _proj_perm()
    cuts = np.flatnonzero(np.diff(perm) != 1) + 1
    return [(int(r[0]), int(r[-1]) + 1) for r in np.split(perm, cuts)]


_PERM_RUNS = _perm_runs()


def _cparams(sem):
    return pltpu.CompilerParams(dimension_semantics=sem, vmem_limit_bytes=VMEM_LIMIT)


def _layer_norm(z, g, b):
    mu = jnp.mean(z, axis=-1, keepdims=True)
    zc = z - mu
    var = jnp.mean(zc * zc, axis=-1, keepdims=True)
    return zc * lax.rsqrt(var + LN_EPS) * g + b


def _silu(x):
    return x / (1.0 + jnp.exp(-x))


def _dot_nt(a, b):
    return lax.dot_general(a, b, (((1,), (1,)), ((), ())), preferred_element_type=F32)


def _dot_tn(a, b):
    return lax.dot_general(a, b, (((0,), (0,)), ((), ())), preferred_element_type=F32)


def _bias_kernel(tab_ref, bkt_ref, o_ref):
    h = pl.program_id(0)
    bkt = bkt_ref[...]
    acc = jnp.full(bkt.shape, NEG, F32)
    for b in range(REL_BUCKETS):
        acc = jnp.where(bkt == b, tab_ref[h, b], acc)
    o_ref[0] = acc


def _bias_lookup(tab_t, bkt):
    nh = tab_t.shape[0]
    r, c = bkt.shape
    return pl.pallas_call(
        _bias_kernel,
        grid=(nh,),
        in_specs=[pl.BlockSpec(memory_space=pltpu.SMEM),
                  pl.BlockSpec((r, c), lambda h: (0, 0))],
        out_specs=pl.BlockSpec((1, r, c), lambda h: (h, 0, 0)),
        out_shape=jax.ShapeDtypeStruct((nh, r, c), F32),
        compiler_params=_cparams(("arbitrary",)),
        name="bias_lookup",
    )(tab_t, bkt)


def _rel_bucket(dist):
    max_exact = REL_BUCKETS // 2
    d = jnp.maximum(dist, 0)
    ratio = jnp.maximum(d, 1).astype(F32) / max_exact
    large = max_exact + (jnp.log(ratio) / math.log(REL_MAX_DIST / max_exact)
                         * (REL_BUCKETS - max_exact)).astype(jnp.int32)
    large = jnp.minimum(large, REL_BUCKETS - 1)
    return jnp.where(d < max_exact, d, large)


def _in_proj_kernel(x_ref, w_ref, o_ref, xb_ref):
    @pl.when(pl.program_id(1) == 0)
    def _():
        xb_ref[...] = x_ref[...].astype(BF16)

    o_ref[...] = jnp.dot(xb_ref[...], w_ref[...], preferred_element_type=F32).astype(BF16)


def _in_proj(x, w):
    n, d = x.shape
    p = w.shape[1]
    tm = min(1024, n)
    tn = p // 3
    return pl.pallas_call(
        _in_proj_kernel,
        grid=(n // tm, p // tn),
        in_specs=[pl.BlockSpec((tm, d), lambda i, j: (i, 0)),
                  pl.BlockSpec((d, tn), lambda i, j: (0, j))],
        out_specs=pl.BlockSpec((tm, tn), lambda i, j: (i, j)),
        out_shape=jax.ShapeDtypeStruct((n, p), BF16),
        scratch_shapes=[pltpu.VMEM((tm, d), BF16)],
        compiler_params=_cparams(("parallel", "arbitrary")),
        name="in_proj",
    )(x, w)


A_BLOCK = 256
A_SUB = 128
A_QROWS = 512
A_BIAS_TYPES = 4
LOG2E = math.log2(math.e)


def _attn_a_kernel(q_ref, k_ref, v_ref, bias_ref, lam_ref, g_ref, o_ref, qt_ref, s_ref, p_ref, acc_ref, *,
                   lam_init):
    t = A_BLOCK
    nch = A_QROWS // A_SUB
    per_key_block = t // A_SUB
    i = pl.program_id(2)
    lane = lax.broadcasted_iota(jnp.int32, (A_SUB, LANES), 1)
    for c in range(nch):
        q = q_ref[0, c * A_SUB:(c + 1) * A_SUB, :].astype(F32) * (A_QK_DIM ** -0.5 * LOG2E)
        qq = jnp.concatenate([jnp.where(lane < A_QK_DIM, q, 0.0), jnp.where(lane >= A_QK_DIM, q, 0.0)], axis=0)
        qt_ref[c] = qq.T.astype(BF16)
    first_block = i * (nch // per_key_block)
    n_blocks = first_block + nch // per_key_block

    all_chains = tuple(range(nch))
    late_chains = all_chains[per_key_block:]

    def scores_into(slot, j, chains):
        kj = k_ref[0, pl.ds(pl.multiple_of(j * t, t), t), :]
        for c in chains:
            back = first_block + c // per_key_block - j
            near = (0, 2) if c % per_key_block == 0 else (1, 3)
            kind = jnp.where(back == 0, near[0], jnp.where(back == 1, near[1], 3))
            s_ref[slot, c] = jnp.dot(kj, qt_ref[c], preferred_element_type=F32) + bias_ref[0, kind]

    def values_from(slot, j, alphas, chains):
        vj = v_ref[0, pl.ds(pl.multiple_of(j * t, t), t), :]
        for c in chains:
            acc_ref[c] = alphas[c] * acc_ref[c] + _dot_tn(vj, p_ref[slot, c])

    def sub_step(j, cur, carry, next_chains, chains):
        if next_chains:
            scores_into(1 - cur, j + 1, next_chains)
        new = list(carry)
        for c in chains:
            m_old, l_old, _ = carry[c]
            s = s_ref[cur, c]
            m_new = jnp.maximum(m_old, jnp.max(s, axis=0, keepdims=True))
            p = jnp.exp2(s - m_new)
            new[c] = (m_new, jnp.exp2(m_old - m_new) * l_old + jnp.sum(p, axis=0, keepdims=True),
                      jnp.exp2(m_old - m_new))
            p_ref[cur, c] = p.astype(BF16)
        values_from(1 - cur, jnp.maximum(j - 1, 0), [a for _, _, a in carry], all_chains)
        return tuple(new)

    def pair(jj, carry):
        carry = sub_step(2 * jj, 0, carry, all_chains, all_chains)
        return sub_step(2 * jj + 1, 1, carry, all_chains, all_chains)

    scores_into(0, 0, all_chains)
    p_ref[1] = jnp.zeros(p_ref.shape[1:], BF16)
    acc_ref[...] = jnp.zeros_like(acc_ref)
    init = tuple((jnp.full((1, 2 * A_SUB), NEG, F32), jnp.zeros((1, 2 * A_SUB), F32),
                  jnp.ones((1, 2 * A_SUB), F32)) for _ in range(nch))
    fin = lax.fori_loop(0, n_blocks // 2 - 1, pair, init)
    fin = sub_step(n_blocks - 2, 0, fin, late_chains, all_chains)
    fin = sub_step(n_blocks - 1, 1, fin, (), late_chains)
    values_from(1, n_blocks - 1, [a for _, _, a in fin], late_chains)
    lp = lam_ref[...]
    lam = (jnp.exp(jnp.sum(lp[0:1] * lp[1:2], axis=-1, keepdims=True))
           - jnp.exp(jnp.sum(lp[2:3] * lp[3:4], axis=-1, keepdims=True)) + lam_init)
    for c, (_, l_fin, _) in enumerate(fin):
        o_all = acc_ref[c] / l_fin
        o = (o_all[:, :A_SUB] - lam * o_all[:, A_SUB:]).T
        o = o * lax.rsqrt(jnp.mean(o * o, axis=-1, keepdims=True) + LN_EPS) * g_ref[...]
        o_ref[0, c * A_SUB:(c + 1) * A_SUB, :] = (o * (1.0 - lam_init)).astype(BF16)


def _attn_a(proj, bias_a, lam_params, subln_g, lam_init):
    bsz, s, _ = proj.shape
    t = A_QROWS
    nch = A_QROWS // A_SUB
    kb, vb = OFF_AK // LANES, OFF_AV // LANES
    return pl.pallas_call(
        functools.partial(_attn_a_kernel, lam_init=lam_init),
        grid=(bsz, A_HEADS, s // t),
        in_specs=[pl.BlockSpec((1, t, LANES), lambda b, h, i: (b, i, h)),
                  pl.BlockSpec((1, s, LANES), lambda b, h, i: (b, 0, kb + h)),
                  pl.BlockSpec((1, s, LANES), lambda b, h, i: (b, 0, vb + h)),
                  pl.BlockSpec((1, A_BIAS_TYPES, A_BLOCK, 2 * A_SUB), lambda b, h, i: (h, 0, 0, 0)),
                  pl.BlockSpec((4, A_QK_DIM), lambda b, h, i: (0, 0)),
                  pl.BlockSpec((1, A_V_DIM), lambda b, h, i: (0, 0))],
        out_specs=pl.BlockSpec((1, t, LANES), lambda b, h, i: (b, i, h)),
        out_shape=jax.ShapeDtypeStruct((bsz, s, A_WIDTH), BF16),
        scratch_shapes=[pltpu.VMEM((nch, LANES, 2 * A_SUB), BF16),
                        pltpu.VMEM((2, nch, A_BLOCK, 2 * A_SUB), F32),
                        pltpu.VMEM((2, nch, A_BLOCK, 2 * A_SUB), BF16),
                        pltpu.VMEM((nch, A_V_DIM, 2 * A_SUB), F32)],
        compiler_params=_cparams(("parallel", "parallel", "arbitrary")),
        name="attn_a",
    )(proj, proj, proj, bias_a, lam_params, subln_g)


def _swa_kernel(q_ref, kvp_ref, kvc_ref, bias_ref, sink_ref, o_ref):
    w = B_WINDOW
    d = B_HEAD_DIM
    group = B_HEADS // B_KV_HEADS
    table = jnp.minimum(pl.program_id(1), 1)
    row = lax.broadcasted_iota(jnp.int32, (LANES, w), 0)
    zeros = jnp.zeros((LANES - d, w), F32)
    kvs, q_ts = [], []
    for g in range(B_KV_HEADS):
        kvs.append(jnp.concatenate([kvp_ref[0, :, g * LANES:(g + 1) * LANES],
                                    kvc_ref[0, :, g * LANES:(g + 1) * LANES]], axis=0))
        cols = []
        for pair in range(group // 2):
            blk = g * (group // 2) + pair
            t = (q_ref[0, :, blk * LANES:(blk + 1) * LANES].astype(F32) * (d ** -0.5 * LOG2E)).T
            cols += [jnp.where(row < d, t, 0.0), jnp.concatenate([t[d:], zeros], axis=0)]
        q_ts.append(jnp.concatenate(cols, axis=1).astype(BF16))
    scores = [jnp.dot(kv, q_t, preferred_element_type=F32) + bias_ref[g, table]
              for g, (kv, q_t) in enumerate(zip(kvs, q_ts))]
    soft = []
    for g, s in enumerate(scores):
        sink = sink_ref[g] * LOG2E
        m = jnp.maximum(jnp.max(s, axis=0, keepdims=True), sink)
        e = jnp.exp2(s - m)
        soft.append((e.astype(BF16), 1.0 / (jnp.sum(e, axis=0, keepdims=True) + jnp.exp2(sink - m))))
    outs = [_dot_tn(kv, e) * inv for kv, (e, inv) in zip(kvs, soft)]
    for g, o_t in enumerate(outs):
        for pair in range(group // 2):
            blk = g * (group // 2) + pair
            both = jnp.concatenate([o_t[d:, (2 * pair) * w:(2 * pair + 1) * w],
                                    o_t[d:, (2 * pair + 1) * w:(2 * pair + 2) * w]], axis=0)
            o_ref[0, :, blk * LANES:(blk + 1) * LANES] = both.T.astype(BF16)


def _swa(proj, bias_b, sink_rows):
    bsz, s, _ = proj.shape
    w = B_WINDOW
    kvw = 2 * B_KV_HEADS * B_HEAD_DIM
    qb, kvb = OFF_BQ // B_WIDTH, OFF_BKV // kvw
    cols = (B_HEADS // B_KV_HEADS) * w
    return pl.pallas_call(
        _swa_kernel,
        grid=(bsz, s // w),
        in_specs=[pl.BlockSpec((1, w, B_WIDTH), lambda b, n: (b, n, qb)),
                  pl.BlockSpec((1, w, kvw), lambda b, n: (b, jnp.maximum(n - 1, 0), kvb)),
                  pl.BlockSpec((1, w, kvw), lambda b, n: (b, n, kvb)),
                  pl.BlockSpec((B_KV_HEADS, 2, 2 * w, cols), lambda b, n: (0, 0, 0, 0)),
                  pl.BlockSpec((B_KV_HEADS, 1, cols), lambda b, n: (0, 0, 0))],
        out_specs=pl.BlockSpec((1, w, B_WIDTH), lambda b, n: (b, n, 0)),
        out_shape=jax.ShapeDtypeStruct((bsz, s, B_WIDTH), BF16),
        compiler_params=_cparams(("parallel", "arbitrary")),
        name="swa",
    )(proj, proj, proj, bias_b, sink_rows)


def _ret_kernel(q_ref, k_ref, v_ref, g_ref, sin_ref, cos_ref, decay_ref, zeta_ref, xi_ref, gch_ref, o_ref,
                state_ref):
    c = C_CHUNK

    @pl.when(pl.program_id(1) == 0)
    def _():
        state_ref[...] = jnp.zeros_like(state_ref)

    lane = lax.broadcasted_iota(jnp.int32, (c, LANES), 1)
    even = (lane & 1) == 0
    sn = sin_ref[...]
    cs = cos_ref[...]

    def rope(x):
        swapped = jnp.where(even, pltpu.roll(x, LANES - 1, 1), pltpu.roll(x, 1, 1))
        return x * cs + swapped * sn

    qk = []
    for pair in range(C_HEADS // 2):
        q = rope(q_ref[0, :, pair * LANES:(pair + 1) * LANES].astype(F32))
        k = rope(k_ref[0, :, pair * LANES:(pair + 1) * LANES].astype(F32) * (C_QK_DIM ** -0.5))
        qk.append((q, q * xi_ref[pair], k.astype(BF16), (k * zeta_ref[pair]).astype(BF16)))
    masked = []
    for h in range(C_HEADS):
        q, qx, _, _ = qk[h // 2]
        in_head = (lane < C_QK_DIM) if h % 2 == 0 else (lane >= C_QK_DIM)
        masked.append((jnp.where(in_head, q, 0.0).astype(BF16), jnp.where(in_head, qx, 0.0).astype(BF16)))
    inner = [(_dot_nt(masked[h][0], qk[h // 2][2]) * decay_ref[h]).astype(BF16) for h in range(C_HEADS)]
    outs = []
    for h in range(C_HEADS):
        vh = v_ref[0, :, h * LANES:(h + 1) * LANES]
        st = state_ref[h]
        outs.append(jnp.dot(inner[h], vh, preferred_element_type=F32)
                    + jnp.dot(masked[h][1], st.astype(BF16), preferred_element_type=F32))
        state_ref[h] = st * gch_ref[h] + _dot_tn(qk[h // 2][3], vh)
    for h, o in enumerate(outs):
        mu = jnp.mean(o, axis=-1, keepdims=True)
        oc = o - mu
        o = oc * lax.rsqrt(jnp.mean(oc * oc, axis=-1, keepdims=True) + LN_EPS)
        gate = g_ref[0, :, h * LANES:(h + 1) * LANES].astype(F32)
        o_ref[0, :, h * LANES:(h + 1) * LANES] = (_silu(gate) * o).astype(BF16)


def _retention(proj, sin_t, cos_t, decay, zeta_t, xi_t, gch):
    bsz, s, _ = proj.shape
    c = C_CHUNK
    qw = C_HEADS * C_QK_DIM
    return pl.pallas_call(
        _ret_kernel,
        grid=(bsz, s // c),
        in_specs=[pl.BlockSpec((1, c, qw), lambda b, n: (b, n, OFF_CQ // qw)),
                  pl.BlockSpec((1, c, qw), lambda b, n: (b, n, OFF_CK // qw)),
                  pl.BlockSpec((1, c, C_WIDTH), lambda b, n: (b, n, OFF_CV // C_WIDTH)),
                  pl.BlockSpec((1, c, C_WIDTH), lambda b, n: (b, n, OFF_CG // C_WIDTH)),
                  pl.BlockSpec((c, LANES), lambda b, n: (n, 0)),
                  pl.BlockSpec((c, LANES), lambda b, n: (n, 0)),
                  pl.BlockSpec((C_HEADS, c, c), lambda b, n: (0, 0, 0)),
                  pl.BlockSpec((C_HEADS // 2, c, LANES), lambda b, n: (0, 0, 0)),
                  pl.BlockSpec((C_HEADS // 2, c, LANES), lambda b, n: (0, 0, 0)),
                  pl.BlockSpec((C_HEADS, 1, LANES), lambda b, n: (0, 0, 0))],
        out_specs=pl.BlockSpec((1, c, C_WIDTH), lambda b, n: (b, n, 0)),
        out_shape=jax.ShapeDtypeStruct((bsz, s, C_WIDTH), BF16),
        scratch_shapes=[pltpu.VMEM((C_HEADS, LANES, C_V_DIM), F32)],
        compiler_params=_cparams(("parallel", "arbitrary")),
        name="retention",
    )(proj, proj, proj, proj, sin_t, cos_t, decay, zeta_t, xi_t, gch)


def _cast_riders(riders_in, riders_out):
    for src, dst in zip(riders_in, riders_out):
        dst[...] = src[...].astype(BF16)


def _row_split_riders(riders, steps):
    views = [r.reshape(r.size // r.shape[-1], r.shape[-1]) for r in riders]
    for v in views:
        assert v.shape[0] % (steps * 16) == 0, v.shape
    return views, [(v.shape[0] // steps, v.shape[1]) for v in views]


def _out_proj_kernel(ya_ref, yb_ref, yc_ref, wa_ref, wb_ref, wc_ref, x_ref, g_ref, b_ref, *rest):
    n_riders = (len(rest) - 1) // 2
    o_ref = rest[n_riders]
    mix = (jnp.dot(ya_ref[...], wa_ref[...], preferred_element_type=F32)
           + jnp.dot(yb_ref[...], wb_ref[...], preferred_element_type=F32)
           + jnp.dot(yc_ref[...], wc_ref[...], preferred_element_type=F32))
    o_ref[...] = _layer_norm(ALPHA * x_ref[...] + mix, g_ref[...], b_ref[...])
    _cast_riders(rest[:n_riders], rest[n_riders + 1:])


def _out_proj_ln(ya, yb, yc, w, x, g, b, riders=()):
    n, d = x.shape
    tm = min(512, n)
    row = lambda i: (i, 0)
    fixed = lambda i: (0, 0)
    once = pl.Buffered(1)
    views, blocks = _row_split_riders(riders, n // tm)
    rider_specs = [pl.BlockSpec(blk, row) for blk in blocks]
    outs = pl.pallas_call(
        _out_proj_kernel,
        grid=(n // tm,),
        in_specs=[pl.BlockSpec((tm, A_WIDTH), row),
                  pl.BlockSpec((tm, B_WIDTH), row),
                  pl.BlockSpec((tm, C_WIDTH), row),
                  pl.BlockSpec((A_WIDTH, d), lambda i: (0, 0), pipeline_mode=once),
                  pl.BlockSpec((B_WIDTH, d), lambda i: (1, 0), pipeline_mode=once),
                  pl.BlockSpec((C_WIDTH, d), lambda i: ((A_WIDTH + B_WIDTH) // C_WIDTH, 0), pipeline_mode=once),
                  pl.BlockSpec((tm, d), row),
                  pl.BlockSpec((1, d), fixed),
                  pl.BlockSpec((1, d), fixed)] + rider_specs,
        out_specs=[pl.BlockSpec((tm, d), row)] + rider_specs,
        out_shape=[jax.ShapeDtypeStruct((n, d), F32)] + [jax.ShapeDtypeStruct(v.shape, BF16) for v in views],
        compiler_params=_cparams(("arbitrary",)),
        name="out_proj_ln",
    )(ya, yb, yc, w, w, w, x, g, b, *views)
    return outs[0], [o.reshape(r.shape) for o, r in zip(outs[1:], riders)]


FFN_TM = 512
FFN_TF = 512


def _swiglu_step(xb, wg, wu, wd):
    hg = jnp.dot(xb, wg, preferred_element_type=F32)
    hu = jnp.dot(xb, wu, preferred_element_type=F32)
    return jnp.dot((_silu(hg) * hu).astype(BF16), wd, preferred_element_type=F32)


def _ffn_kernel(x_ref, wg_ref, wu_ref, wd_ref, g_ref, b_ref, *rest):
    n_riders = (len(rest) - 2) // 2
    riders_in, o_ref = rest[:n_riders], rest[n_riders]
    riders_out, xb_ref = rest[n_riders + 1:-1], rest[-1]
    f = pl.program_id(1)

    @pl.when(f == 0)
    def _():
        o_ref[...] = jnp.zeros_like(o_ref)
        xb_ref[...] = x_ref[...].astype(BF16)

    o_ref[...] += _swiglu_step(xb_ref[...], wg_ref[...], wu_ref[...], wd_ref[...])
    _cast_riders(riders_in, riders_out)

    @pl.when(f == pl.num_programs(1) - 1)
    def _():
        o_ref[...] = _layer_norm(ALPHA * x_ref[...] + o_ref[...], g_ref[...], b_ref[...])


def _ffn(x, w_gate, w_up, w_down, g, b, riders=()):
    n, d = x.shape
    ff = w_gate.shape[1]
    tm = min(FFN_TM, n)
    tf = FFN_TF
    ni, nf = n // tm, ff // tf
    flat, rider_specs = [], []
    for r in riders:
        cols = r.shape[-1]
        rows = r.size // cols
        flat.append(r.reshape(rows, cols))
        if rows % (ni * nf * 16) == 0:
            rider_specs.append(pl.BlockSpec((rows // (ni * nf), cols), lambda i, f: (i * nf + f, 0)))
        else:
            assert rows % (ni * 16) == 0 and cols % (nf * LANES) == 0
            rider_specs.append(pl.BlockSpec((rows // ni, cols // nf), lambda i, f: (i, f)))
    outs = pl.pallas_call(
        _ffn_kernel,
        grid=(n // tm, ff // tf),
        in_specs=[pl.BlockSpec((tm, d), lambda i, f: (i, 0)),
                  pl.BlockSpec((d, tf), lambda i, f: (0, f)),
                  pl.BlockSpec((d, tf), lambda i, f: (0, f)),
                  pl.BlockSpec((tf, d), lambda i, f: (f, 0)),
                  pl.BlockSpec((1, d), lambda i, f: (0, 0)),
                  pl.BlockSpec((1, d), lambda i, f: (0, 0))] + rider_specs,
        out_specs=[pl.BlockSpec((tm, d), lambda i, f: (i, 0))] + rider_specs,
        out_shape=[jax.ShapeDtypeStruct((n, d), F32)] + [jax.ShapeDtypeStruct(r.shape, BF16) for r in flat],
        scratch_shapes=[pltpu.VMEM((tm, d), BF16)],
        compiler_params=_cparams(("arbitrary", "arbitrary")),
        name="ffn_ln",
    )(x, w_gate, w_up, w_down, g, b, *flat)
    return outs[0], [o.reshape(r.shape) for o, r in zip(outs[1:], riders)]


def _row_copy(src, dst, sem, s, t):
    return pltpu.make_async_copy(src.at[pl.ds(s, 1)], dst.at[pl.ds(t, 1)], sem)


def _ffn_grouped_kernel(eid_ref, nact_ref, src_ref, x_hbm, wg_ref, wu_ref, wd_ref, o_ref, rows_ref, xb_ref, sem,
                        *, tm, per_step):
    i = pl.program_id(0)
    f = pl.program_id(1)
    nact = nact_ref[0]
    active = i < nact
    slot = lax.rem(i, 2)
    buf_rows = rows_ref.shape[1]
    last_slot_row = src_ref.shape[0] - 1

    def fetch(tile, into, first_row, count):
        for r in range(count):
            row = first_row + r
            token = src_ref[jnp.minimum(tile * tm + row, last_slot_row)]
            _row_copy(x_hbm, rows_ref.at[into], sem.at[into], token, row).start()

    def wait_buffer(which):
        pltpu.make_async_copy(x_hbm.at[pl.ds(0, buf_rows)], rows_ref.at[which], sem.at[which]).wait()

    @pl.when(jnp.logical_and(i == 0, f == 0))
    def _():
        lax.fori_loop(0, buf_rows // per_step, lambda s, c: (fetch(0, 0, s * per_step, per_step), c)[1], 0)

    @pl.when(jnp.logical_and(f == 0, i <= nact))
    def _():
        wait_buffer(slot)

    @pl.when(f == 0)
    def _():
        o_ref[...] = jnp.zeros_like(o_ref)

    @pl.when(jnp.logical_and(f == 0, active))
    def _():
        xb_ref[...] = rows_ref[slot, :tm, :].astype(BF16)

    @pl.when(active)
    def _():
        fetch(i + 1, 1 - slot, f * per_step, per_step)
        o_ref[...] += _swiglu_step(xb_ref[...], wg_ref[0], wu_ref[0], wd_ref[0])


def _ffn_grouped(x, src, eid, nact, w_gate, w_up, w_down):
    n, d = x.shape
    slots = src.shape[0]
    ff = w_gate.shape[2]
    tm = min(FFN_TM, n)
    tf = FFN_TF
    nf = ff // tf
    per_step = -(-tm // nf)
    per_step += (-per_step) % 8
    assert per_step * nf <= n

    def fidx(i, f, nact_ref):
        return jnp.where(i < nact_ref[0], f, nf - 1)

    grid_spec = pltpu.PrefetchScalarGridSpec(
        num_scalar_prefetch=3,
        grid=(slots // tm, nf),
        in_specs=[pl.BlockSpec(memory_space=pl.ANY),
                  pl.BlockSpec((1, d, tf), lambda i, f, e, a, s: (e[i], 0, fidx(i, f, a))),
                  pl.BlockSpec((1, d, tf), lambda i, f, e, a, s: (e[i], 0, fidx(i, f, a))),
                  pl.BlockSpec((1, tf, d), lambda i, f, e, a, s: (e[i], fidx(i, f, a), 0))],
        out_specs=pl.BlockSpec((tm, d), lambda i, f, e, a, s: (i, 0)),
        scratch_shapes=[pltpu.VMEM((2, per_step * nf, d), F32), pltpu.VMEM((tm, d), BF16),
                        pltpu.SemaphoreType.DMA((2,))],
    )
    return pl.pallas_call(
        functools.partial(_ffn_grouped_kernel, tm=tm, per_step=per_step),
        grid_spec=grid_spec,
        out_shape=jax.ShapeDtypeStruct((slots, d), F32),
        compiler_params=_cparams(("arbitrary", "arbitrary")),
        name="ffn_grouped",
    )(eid, nact, src, x, w_gate, w_up, w_down)


ROUTER_TM = 512


def _router_kernel(x_ref, wr_ref, tri_ref, meta_ref, cnt_ref, carry_ref):
    @pl.when(pl.program_id(0) == 0)
    def _():
        carry_ref[...] = jnp.zeros_like(carry_ref)

    logits = jnp.dot(x_ref[...], wr_ref[...], preferred_element_type=F32, precision=lax.Precision.HIGHEST)
    lane = lax.broadcasted_iota(jnp.int32, logits.shape, 1)
    logits = jnp.where(lane < N_EXPERTS, logits, -jnp.inf)
    m1 = jnp.max(logits, axis=-1, keepdims=True)
    i1 = jnp.min(jnp.where(logits == m1, lane, LANES), axis=-1, keepdims=True)
    rest = jnp.where(lane == i1, -jnp.inf, logits)
    m2 = jnp.max(rest, axis=-1, keepdims=True)
    i2 = jnp.min(jnp.where(rest == m2, lane, LANES), axis=-1, keepdims=True)
    e2 = jnp.exp(m2 - m1)
    w1 = 1.0 / (1.0 + e2)
    w2 = e2 * w1
    hit1 = lane == i1
    hit2 = lane == i2
    onehot = jnp.where(jnp.logical_or(hit1, hit2), 1.0, 0.0)
    before = jnp.dot(tri_ref[...], onehot.astype(BF16), preferred_element_type=F32) + carry_ref[...]
    r1 = jnp.sum(jnp.where(hit1, before, 0.0), axis=-1, keepdims=True)
    r2 = jnp.sum(jnp.where(hit2, before, 0.0), axis=-1, keepdims=True)
    carry_ref[...] = carry_ref[...] + jnp.sum(onehot, axis=0, keepdims=True)
    cnt_ref[...] = carry_ref[...]
    meta = jnp.where(lane == 0, i1.astype(F32), 0.0)
    meta = jnp.where(lane == 1, i2.astype(F32), meta)
    meta = jnp.where(lane == 2, r1, meta)
    meta = jnp.where(lane == 3, r2, meta)
    meta = jnp.where(lane == 4, w1, meta)
    meta = jnp.where(lane == 5, w2, meta)
    meta_ref[...] = meta


def _router(x, w_router):
    n, d = x.shape
    tm = min(ROUTER_TM, n)
    wr = jnp.zeros((d, LANES), F32).at[:, :N_EXPERTS].set(w_router.astype(F32))
    tri = jnp.asarray(np.tril(np.ones((tm, tm), np.float32), -1), BF16)
    return pl.pallas_call(
        _router_kernel,
        grid=(n // tm,),
        in_specs=[pl.BlockSpec((tm, d), lambda i: (i, 0)),
                  pl.BlockSpec((d, LANES), lambda i: (0, 0)),
                  pl.BlockSpec((tm, tm), lambda i: (0, 0))],
        out_specs=[pl.BlockSpec((tm, LANES), lambda i: (i, 0)),
                   pl.BlockSpec((1, LANES), lambda i: (0, 0))],
        out_shape=[jax.ShapeDtypeStruct((n, LANES), F32), jax.ShapeDtypeStruct((1, LANES), F32)],
        scratch_shapes=[pltpu.VMEM((1, LANES), F32)],
        compiler_params=_cparams(("arbitrary",)),
        name="router",
    )(x, wr, tri)


MOVE_TM = 256
ISSUE_UNROLL = 8


def _combine_kernel(d1_ref, d2_ref, ys_hbm, x_ref, meta_ref, g_ref, b_ref, o_ref, buf_ref, sem, *, tm):
    base = pl.program_id(0) * tm

    def issue(t, carry):
        row = base + t
        _row_copy(ys_hbm, buf_ref.at[0], sem, d1_ref[row], t).start()
        _row_copy(ys_hbm, buf_ref.at[1], sem, d2_ref[row], t).start()
        return carry

    lax.fori_loop(0, tm, issue, 0, unroll=ISSUE_UNROLL)

    def drain(t, carry):
        _row_copy(ys_hbm, buf_ref.at[0], sem, 0, 0).wait()
        _row_copy(ys_hbm, buf_ref.at[1], sem, 0, 0).wait()
        return carry

    lax.fori_loop(0, tm, drain, 0, unroll=ISSUE_UNROLL)
    meta = meta_ref[...]
    lane = lax.broadcasted_iota(jnp.int32, meta.shape, 1)
    w1 = jnp.sum(jnp.where(lane == 4, meta, 0.0), axis=-1, keepdims=True)
    w2 = jnp.sum(jnp.where(lane == 5, meta, 0.0), axis=-1, keepdims=True)
    f = w1 * buf_ref[0] + w2 * buf_ref[1]
    o_ref[...] = _layer_norm(ALPHA * x_ref[...] + f, g_ref[...], b_ref[...])


def _combine(ys, x, meta, d1, d2, g, b):
    n, d = x.shape
    tm = min(MOVE_TM, n)
    grid_spec = pltpu.PrefetchScalarGridSpec(
        num_scalar_prefetch=2,
        grid=(n // tm,),
        in_specs=[pl.BlockSpec(memory_space=pl.ANY),
                  pl.BlockSpec((tm, d), lambda i, a, c: (i, 0)),
                  pl.BlockSpec((tm, LANES), lambda i, a, c: (i, 0)),
                  pl.BlockSpec((1, d), lambda i, a, c: (0, 0)),
                  pl.BlockSpec((1, d), lambda i, a, c: (0, 0))],
        out_specs=pl.BlockSpec((tm, d), lambda i, a, c: (i, 0)),
        scratch_shapes=[pltpu.VMEM((2, tm, d), F32), pltpu.SemaphoreType.DMA(())],
    )
    return pl.pallas_call(
        functools.partial(_combine_kernel, tm=tm),
        grid_spec=grid_spec,
        out_shape=jax.ShapeDtypeStruct((n, d), F32),
        compiler_params=_cparams(("arbitrary",)),
        name="moe_combine",
    )(d1, d2, ys, x, meta, g, b)


def _moe(x, w_router, w_gate, w_up, w_down, g, b):
    n, d = x.shape
    tm = min(FFN_TM, n)
    meta, cnt = _router(x, w_router)
    i1 = meta[:, 0].astype(jnp.int32)
    i2 = meta[:, 1].astype(jnp.int32)
    counts = cnt[0, :N_EXPERTS].astype(jnp.int32)
    tiles = (counts + tm - 1) // tm
    tile_end = jnp.cumsum(tiles)
    group_start = (tile_end - tiles) * tm
    experts = jnp.arange(N_EXPERTS, dtype=jnp.int32)[None, :]
    start_of = lambda idx: jnp.sum(jnp.where(idx[:, None] == experts, group_start[None, :], 0), axis=1)
    d1 = start_of(i1) + meta[:, 2].astype(jnp.int32)
    d2 = start_of(i2) + meta[:, 3].astype(jnp.int32)
    max_tiles = (2 * n) // tm + N_EXPERTS
    tile_ids = jnp.arange(max_tiles, dtype=jnp.int32)[:, None]
    eid = jnp.minimum(jnp.sum((tile_ids >= tile_end[None, :]).astype(jnp.int32), axis=1), N_EXPERTS - 1)
    nact = tile_end[-1:].astype(jnp.int32)
    token = jnp.arange(n, dtype=jnp.int32)
    src = jnp.zeros((max_tiles * tm,), jnp.int32).at[jnp.concatenate([d1, d2])].set(
        jnp.concatenate([token, token]), unique_indices=True)
    ys = _ffn_grouped(x, src, eid.astype(jnp.int32), nact, w_gate, w_up, w_down)
    return _combine(ys, x, meta, d1, d2, g, b)


def _lambda_init(layer_idx):
    return 0.8 - 0.6 * math.exp(-0.3 * layer_idx)


def _static_tables(s):
    c = C_CHUNK
    ang = jnp.repeat(1.0 / (10000.0 ** jnp.linspace(0.0, 1.0, C_QK_DIM // 2, dtype=F32)), 2)
    ang = jnp.arange(s, dtype=F32)[:, None] * ang[None, :]
    sign = jnp.where(jnp.arange(C_QK_DIM) % 2 == 0, -1.0, 1.0).astype(F32)
    sin_t = jnp.tile(jnp.sin(ang) * sign[None, :], (1, 2))
    cos_t = jnp.tile(jnp.cos(ang), (1, 2))
    log_g = jnp.log(1.0 - jnp.exp2(-5.0 - jnp.arange(C_HEADS, dtype=F32)))
    pos = jnp.arange(c)
    rel = (pos[:, None] - pos[None, :]).astype(F32)
    decay = jnp.where((rel >= 0)[None], jnp.exp(jnp.maximum(rel, 0.0)[None] * log_g[:, None, None]), 0.0)
    zeta = jnp.exp((c - 1 - pos).astype(F32)[:, None] * log_g[None, :])
    xi = jnp.exp((pos + 1).astype(F32)[:, None] * log_g[None, :])
    per_pair = lambda t: jnp.repeat(t.T.reshape(C_HEADS // 2, 2, c), C_QK_DIM, axis=1).transpose(0, 2, 1)
    gch = jnp.broadcast_to(jnp.exp(c * log_g)[:, None, None], (C_HEADS, 1, LANES))
    dist_a = np.arange(A_SUB)[None, :] - np.arange(A_BLOCK)[:, None]
    types = [jnp.where(dist_a + off >= 0, _rel_bucket(jnp.asarray(dist_a + off)), REL_BUCKETS)
             for off in (0, A_SUB, A_BLOCK)]
    types.append(jnp.full(dist_a.shape, REL_BUCKETS - 1, jnp.int32))
    bkt_a = jnp.concatenate([jnp.tile(b, (1, 2)) for b in types], axis=0).astype(jnp.int32)
    w = B_WINDOW
    dist = np.arange(w)[:, None] + w - np.arange(2 * w)[None, :]
    band = (dist >= 0) & (dist < w)
    has_prev = np.stack([np.broadcast_to(np.arange(2 * w)[None, :] >= w, band.shape), np.ones_like(band)])
    bkt_b = jnp.where(band[None] & has_prev, _rel_bucket(jnp.asarray(dist))[None], REL_BUCKETS)
    bkt_b = bkt_b.transpose(0, 2, 1).reshape(2 * 2 * w, w).astype(jnp.int32)
    return sin_t, cos_t, decay, per_pair(zeta), per_pair(xi), gch, bkt_a, bkt_b


def kernel(x, w_in, rel_bias, a_lambda, a_subln_g, b_sinks, w_out, ln_mix_g, ln_mix_b, ln_ffn_g, ln_ffn_b,
           dense_w_gate, dense_w_up, dense_w_down, moe_router, moe_w_gate, moe_w_up, moe_w_down):
    bsz, s, d = x.shape
    n = bsz * s
    sin_t, cos_t, decay, zeta_t, xi_t, gch, bkt_a, bkt_b = _static_tables(s)
    tab_t = rel_bias.astype(F32).T
    bias_a = _bias_lookup(tab_t[:A_HEADS] * LOG2E, bkt_a).reshape(A_HEADS, A_BIAS_TYPES, A_BLOCK, 2 * A_SUB)
    group = B_HEADS // B_KV_HEADS
    bias_b = _bias_lookup(tab_t[A_HEADS:] * LOG2E, bkt_b)
    bias_b = bias_b.reshape(B_KV_HEADS, group, 2, 2 * B_WINDOW, B_WINDOW).transpose(0, 2, 3, 1, 4)
    bias_b = bias_b.reshape(B_KV_HEADS, 2, 2 * B_WINDOW, group * B_WINDOW)
    xf = x.reshape(n, d).astype(F32)
    for l in range(DEPTH):
        w_in_l = jnp.concatenate([w_in[l][:, a:b].astype(BF16) for a, b in _PERM_RUNS], axis=1)
        proj = _in_proj(xf, w_in_l).reshape(bsz, s, PROJ_WIDTH)
        ya = _attn_a(proj, bias_a, a_lambda[l].astype(F32), a_subln_g[l].astype(F32).reshape(1, A_V_DIM),
                     _lambda_init(l))
        sink_rows = jnp.repeat(b_sinks[l].astype(F32).reshape(B_KV_HEADS, 1, group), B_WINDOW, axis=2)
        yb = _swa(proj, bias_b, sink_rows)
        yc = _retention(proj, sin_t, cos_t, decay, zeta_t, xi_t, gch)
        g_mix = ln_mix_g[l].astype(F32).reshape(1, d)
        b_mix = ln_mix_b[l].astype(F32).reshape(1, d)
        j = l // 2
        dense = (dense_w_gate[j], dense_w_up[j], dense_w_down[j]) if l % 2 == 0 else ()
        xf, dense_bf16 = _out_proj_ln(ya.reshape(n, A_WIDTH), yb.reshape(n, B_WIDTH), yc.reshape(n, C_WIDTH),
                                      w_out[l].astype(BF16), xf, g_mix, b_mix, dense)
        g_ffn = ln_ffn_g[l].astype(F32).reshape(1, d)
        b_ffn = ln_ffn_b[l].astype(F32).reshape(1, d)
        if l % 2 == 0:
            experts = (moe_w_gate[j], moe_w_up[j], moe_w_down[j]) if l + 1 < DEPTH else ()
            xf, moe_bf16 = _ffn(xf, *dense_bf16, g_ffn, b_ffn, experts)
        else:
            assert l > 0, "expert weights are cast by the preceding dense layer"
            xf = _moe(xf, moe_router[j], *moe_bf16, g_ffn, b_ffn)
    return xf.reshape(bsz, s, d).astype(x.dtype)
```

```python
import functools
import math

import jax
import jax.numpy as jnp
import numpy as np
from jax import lax
from jax.experimental import pallas as pl
from jax.experimental.pallas import tpu as pltpu

F32 = jnp.float32
BF16 = jnp.bfloat16

D_MODEL = 2048
DEPTH = 2
A_HEADS = 6
A_QK_DIM = 64
A_V_DIM = 128
B_HEADS = 12
B_KV_HEADS = 3
B_HEAD_DIM = 64
B_WINDOW = 128
C_HEADS = 4
C_QK_DIM = 64
C_V_DIM = 128
C_CHUNK = 128
A_WIDTH = A_HEADS * A_V_DIM
B_WIDTH = B_HEADS * B_HEAD_DIM
C_WIDTH = C_HEADS * C_V_DIM
REL_BUCKETS = 32
REL_MAX_DIST = 128
D_FF = 5632
N_EXPERTS = 8
ALPHA = (2.0 * DEPTH) ** 0.25
LN_EPS = 1e-5
NEG = -1e30

LANES = 128
VMEM_LIMIT = 56 * 1024 * 1024

_REF_SIZES = [768, 768, 768, 768, 192, 192, 256, 256, 512, 512]
_REF_OFF = [int(v) for v in np.concatenate([[0], np.cumsum(_REF_SIZES)[:-1]])]
PROJ_WIDTH = int(sum(_REF_SIZES))
OFF_AQ, OFF_AK, OFF_AV, OFF_BQ, OFF_CV, OFF_CG, OFF_CQ, OFF_CK, OFF_BKV = (
    0, 768, 1536, 2304, 3072, 3584, 4096, 4352, 4608)


def _proj_perm():
    aq, ak, av, bq, bk, bv, cq, ck, cv, cg = [np.arange(o, o + s) for o, s in zip(_REF_OFF, _REF_SIZES)]
    bkv = np.concatenate([np.concatenate([bk[g * 64:(g + 1) * 64], bv[g * 64:(g + 1) * 64]])
                          for g in range(B_KV_HEADS)])
    perm = np.concatenate([aq, ak, av, bq, cv, cg, cq, ck, bkv])
    assert perm.shape[0] == PROJ_WIDTH
    return perm


def _perm_runs():
    perm = _proj_perm()
    cuts = np.flatnonzero(np.diff(perm) != 1) + 1
    return [(int(r[0]), int(r[-1]) + 1) for r in np.split(perm, cuts)]


_PERM_RUNS = _perm_runs()


def _cparams(sem):
    return pltpu.CompilerParams(dimension_semantics=sem, vmem_limit_bytes=VMEM_LIMIT)


def _layer_norm(z, g, b):
    mu = jnp.mean(z, axis=-1, keepdims=True)
    zc = z - mu
    var = jnp.mean(zc * zc, axis=-1, keepdims=True)
    return zc * lax.rsqrt(var + LN_EPS) * g + b


def _silu(x):
    return x / (1.0 + jnp.exp(-x))


def _dot_nt(a, b):
    return lax.dot_general(a, b, (((1,), (1,)), ((), ())), preferred_element_type=F32)


def _dot_tn(a, b):
    return lax.dot_general(a, b, (((0,), (0,)), ((), ())), preferred_element_type=F32)


def _bias_kernel(tab_ref, bkt_ref, o_ref):
    h = pl.program_id(0)
    bkt = bkt_ref[...]
    acc = jnp.full(bkt.shape, NEG, F32)
    for b in range(REL_BUCKETS):
        acc = jnp.where(bkt == b, tab_ref[h, b], acc)
    o_ref[0] = acc


def _bias_lookup(tab_t, bkt):
    nh = tab_t.shape[0]
    r, c = bkt.shape
    return pl.pallas_call(
        _bias_kernel,
        grid=(nh,),
        in_specs=[pl.BlockSpec(memory_space=pltpu.SMEM),
                  pl.BlockSpec((r, c), lambda h: (0, 0))],
        out_specs=pl.BlockSpec((1, r, c), lambda h: (h, 0, 0)),
        out_shape=jax.ShapeDtypeStruct((nh, r, c), F32),
        compiler_params=_cparams(("arbitrary",)),
        name="bias_lookup",
    )(tab_t, bkt)


def _rel_bucket(dist):
    max_exact = REL_BUCKETS // 2
    d = jnp.maximum(dist, 0)
    ratio = jnp.maximum(d, 1).astype(F32) / max_exact
    large = max_exact + (jnp.log(ratio) / math.log(REL_MAX_DIST / max_exact)
                         * (REL_BUCKETS - max_exact)).astype(jnp.int32)
    large = jnp.minimum(large, REL_BUCKETS - 1)
    return jnp.where(d < max_exact, d, large)


def _in_proj_kernel(x_ref, w_ref, o_ref, xb_ref):
    @pl.when(pl.program_id(1) == 0)
    def _():
        xb_ref[...] = x_ref[...].astype(BF16)

    o_ref[...] = jnp.dot(xb_ref[...], w_ref[...], preferred_element_type=F32).astype(BF16)


def _in_proj(x, w):
    n, d = x.shape
    p = w.shape[1]
    tm = min(1024, n)
    tn = p // 3
    return pl.pallas_call(
        _in_proj_kernel,
        grid=(n // tm, p // tn),
        in_specs=[pl.BlockSpec((tm, d), lambda i, j: (i, 0)),
                  pl.BlockSpec((d, tn), lambda i, j: (0, j))],
        out_specs=pl.BlockSpec((tm, tn), lambda i, j: (i, j)),
        out_shape=jax.ShapeDtypeStruct((n, p), BF16),
        scratch_shapes=[pltpu.VMEM((tm, d), BF16)],
        compiler_params=_cparams(("parallel", "arbitrary")),
        name="in_proj",
    )(x, w)


A_BLOCK = 256
A_SUB = 128
A_QROWS = 512
A_BIAS_TYPES = 4
LOG2E = math.log2(math.e)


def _attn_a_kernel(q_ref, k_ref, v_ref, bias_ref, lam_ref, g_ref, o_ref, qt_ref, s_ref, p_ref, acc_ref, *,
                   lam_init):
    t = A_BLOCK
    nch = A_QROWS // A_SUB
    per_key_block = t // A_SUB
    i = pl.program_id(2)
    lane = lax.broadcasted_iota(jnp.int32, (A_SUB, LANES), 1)
    for c in range(nch):
        q = q_ref[0, c * A_SUB:(c + 1) * A_SUB, :].astype(F32) * (A_QK_DIM ** -0.5 * LOG2E)
        qq = jnp.concatenate([jnp.where(lane < A_QK_DIM, q, 0.0), jnp.where(lane >= A_QK_DIM, q, 0.0)], axis=0)
        qt_ref[c] = qq.T.astype(BF16)
    first_block = i * (nch // per_key_block)
    n_blocks = first_block + nch // per_key_block

    all_chains = tuple(range(nch))
    late_chains = all_chains[per_key_block:]

    def scores_into(slot, j, chains):
        kj = k_ref[0, pl.ds(pl.multiple_of(j * t, t), t), :]
        for c in chains:
            back = first_block + c // per_key_block - j
            near = (0, 2) if c % per_key_block == 0 else (1, 3)
            kind = jnp.where(back == 0, near[0], jnp.where(back == 1, near[1], 3))
            s_ref[slot, c] = jnp.dot(kj, qt_ref[c], preferred_element_type=F32) + bias_ref[0, kind]

    def values_from(slot, j, alphas, chains):
        vj = v_ref[0, pl.ds(pl.multiple_of(j * t, t), t), :]
        for c in chains:
            acc_ref[c] = alphas[c] * acc_ref[c] + _dot_tn(vj, p_ref[slot, c])

    def sub_step(j, cur, carry, next_chains, chains):
        if next_chains:
            scores_into(1 - cur, j + 1, next_chains)
        new = list(carry)
        for c in chains:
            m_old, l_old, _ = carry[c]
            s = s_ref[cur, c]
            m_new = jnp.maximum(m_old, jnp.max(s, axis=0, keepdims=True))
            p = jnp.exp2(s - m_new)
            new[c] = (m_new, jnp.exp2(m_old - m_new) * l_old + jnp.sum(p, axis=0, keepdims=True),
                      jnp.exp2(m_old - m_new))
            p_ref[cur, c] = p.astype(BF16)
        values_from(1 - cur, jnp.maximum(j - 1, 0), [a for _, _, a in carry], all_chains)
        return tuple(new)

    def pair(jj, carry):
        carry = sub_step(2 * jj, 0, carry, all_chains, all_chains)
        return sub_step(2 * jj + 1, 1, carry, all_chains, all_chains)

    scores_into(0, 0, all_chains)
    p_ref[1] = jnp.zeros(p_ref.shape[1:], BF16)
    acc_ref[...] = jnp.zeros_like(acc_ref)
    init = tuple((jnp.full((1, 2 * A_SUB), NEG, F32), jnp.zeros((1, 2 * A_SUB), F32),
                  jnp.ones((1, 2 * A_SUB), F32)) for _ in range(nch))
    fin = lax.fori_loop(0, n_blocks // 2 - 1, pair, init)
    fin = sub_step(n_blocks - 2, 0, fin, late_chains, all_chains)
    fin = sub_step(n_blocks - 1, 1, fin, (), late_chains)
    values_from(1, n_blocks - 1, [a for _, _, a in fin], late_chains)
    lp = lam_ref[...]
    lam = (jnp.exp(jnp.sum(lp[0:1] * lp[1:2], axis=-1, keepdims=True))
           - jnp.exp(jnp.sum(lp[2:3] * lp[3:4], axis=-1, keepdims=True)) + lam_init)
    for c, (_, l_fin, _) in enumerate(fin):
        o_all = acc_ref[c] / l_fin
        o = (o_all[:, :A_SUB] - lam * o_all[:, A_SUB:]).T
        o = o * lax.rsqrt(jnp.mean(o * o, axis=-1, keepdims=True) + LN_EPS) * g_ref[...]
        o_ref[0, c * A_SUB:(c + 1) * A_SUB, :] = (o * (1.0 - lam_init)).astype(BF16)


def _attn_a(proj, bias_a, lam_params, subln_g, lam_init):
    bsz, s, _ = proj.shape
    t = A_QROWS
    nch = A_QROWS // A_SUB
    kb, vb = OFF_AK // LANES, OFF_AV // LANES
    return pl.pallas_call(
        functools.partial(_attn_a_kernel, lam_init=lam_init),
        grid=(bsz, A_HEADS, s // t),
        in_specs=[pl.BlockSpec((1, t, LANES), lambda b, h, i: (b, i, h)),
                  pl.BlockSpec((1, s, LANES), lambda b, h, i: (b, 0, kb + h)),
                  pl.BlockSpec((1, s, LANES), lambda b, h, i: (b, 0, vb + h)),
                  pl.BlockSpec((1, A_BIAS_TYPES, A_BLOCK, 2 * A_SUB), lambda b, h, i: (h, 0, 0, 0)),
                  pl.BlockSpec((4, A_QK_DIM), lambda b, h, i: (0, 0)),
                  pl.BlockSpec((1, A_V_DIM), lambda b, h, i: (0, 0))],
        out_specs=pl.BlockSpec((1, t, LANES), lambda b, h, i: (b, i, h)),
        out_shape=jax.ShapeDtypeStruct((bsz, s, A_WIDTH), BF16),
        scratch_shapes=[pltpu.VMEM((nch, LANES, 2 * A_SUB), BF16),
                        pltpu.VMEM((2, nch, A_BLOCK, 2 * A_SUB), F32),
                        pltpu.VMEM((2, nch, A_BLOCK, 2 * A_SUB), BF16),
                        pltpu.VMEM((nch, A_V_DIM, 2 * A_SUB), F32)],
        compiler_params=_cparams(("parallel", "parallel", "arbitrary")),
        name="attn_a",
    )(proj, proj, proj, bias_a, lam_params, subln_g)


def _swa_kernel(q_ref, kvp_ref, kvc_ref, bias_ref, sink_ref, o_ref):
    w = B_WINDOW
    d = B_HEAD_DIM
    group = B_HEADS // B_KV_HEADS
    table = jnp.minimum(pl.program_id(1), 1)
    row = lax.broadcasted_iota(jnp.int32, (LANES, w), 0)
    zeros = jnp.zeros((LANES - d, w), F32)
    kvs, q_ts = [], []
    for g in range(B_KV_HEADS):
        kvs.append(jnp.concatenate([kvp_ref[0, :, g * LANES:(g + 1) * LANES],
                                    kvc_ref[0, :, g * LANES:(g + 1) * LANES]], axis=0))
        cols = []
        for pair in range(group // 2):
            blk = g * (group // 2) + pair
            t = (q_ref[0, :, blk * LANES:(blk + 1) * LANES].astype(F32) * (d ** -0.5 * LOG2E)).T
            cols += [jnp.where(row < d, t, 0.0), jnp.concatenate([t[d:], zeros], axis=0)]
        q_ts.append(jnp.concatenate(cols, axis=1).astype(BF16))
    scores = [jnp.dot(kv, q_t, preferred_element_type=F32) + bias_ref[g, table]
              for g, (kv, q_t) in enumerate(zip(kvs, q_ts))]
    soft = []
    for g, s in enumerate(scores):
        sink = sink_ref[g] * LOG2E
        m = jnp.maximum(jnp.max(s, axis=0, keepdims=True), sink)
        e = jnp.exp2(s - m)
        soft.append((e.astype(BF16), 1.0 / (jnp.sum(e, axis=0, keepdims=True) + jnp.exp2(sink - m))))
    outs = [_dot_tn(kv, e) * inv for kv, (e, inv) in zip(kvs, soft)]
    for g, o_t in enumerate(outs):
        for pair in range(group // 2):
            blk = g * (group // 2) + pair
            both = jnp.concatenate([o_t[d:, (2 * pair) * w:(2 * pair + 1) * w],
                                    o_t[d:, (2 * pair + 1) * w:(2 * pair + 2) * w]], axis=0)
            o_ref[0, :, blk * LANES:(blk + 1) * LANES] = both.T.astype(BF16)


def _ret_kernel(q_ref, k_ref, v_ref, g_ref, sin_ref, cos_ref, decay_ref, zeta_ref, xi_ref, gch_ref, o_ref,
                state_ref):
    c = C_CHUNK

    @pl.when(pl.program_id(1) == 0)
    def _():
        state_ref[...] = jnp.zeros_like(state_ref)

    lane = lax.broadcasted_iota(jnp.int32, (c, LANES), 1)
    even = (lane & 1) == 0
    sn = sin_ref[...]
    cs = cos_ref[...]

    def rope(x):
        swapped = jnp.where(even, pltpu.roll(x, LANES - 1, 1), pltpu.roll(x, 1, 1))
        return x * cs + swapped * sn

    qk = []
    for pair in range(C_HEADS // 2):
        q = rope(q_ref[0, :, pair * LANES:(pair + 1) * LANES].astype(F32))
        k = rope(k_ref[0, :, pair * LANES:(pair + 1) * LANES].astype(F32) * (C_QK_DIM ** -0.5))
        qk.append((q, q * xi_ref[pair], k.astype(BF16), (k * zeta_ref[pair]).astype(BF16)))
    masked = []
    for h in range(C_HEADS):
        q, qx, _, _ = qk[h // 2]
        in_head = (lane < C_QK_DIM) if h % 2 == 0 else (lane >= C_QK_DIM)
        masked.append((jnp.where(in_head, q, 0.0).astype(BF16), jnp.where(in_head, qx, 0.0).astype(BF16)))
    inner = [(_dot_nt(masked[h][0], qk[h // 2][2]) * decay_ref[h]).astype(BF16) for h in range(C_HEADS)]
    outs = []
    for h in range(C_HEADS):
        vh = v_ref[0, :, h * LANES:(h + 1) * LANES]
        st = state_ref[h]
        outs.append(jnp.dot(inner[h], vh, preferred_element_type=F32)
                    + jnp.dot(masked[h][1], st.astype(BF16), preferred_element_type=F32))
        state_ref[h] = st * gch_ref[h] + _dot_tn(qk[h // 2][3], vh)
    for h, o in enumerate(outs):
        mu = jnp.mean(o, axis=-1, keepdims=True)
        oc = o - mu
        o = oc * lax.rsqrt(jnp.mean(oc * oc, axis=-1, keepdims=True) + LN_EPS)
        gate = g_ref[0, :, h * LANES:(h + 1) * LANES].astype(F32)
        o_ref[0, :, h * LANES:(h + 1) * LANES] = (_silu(gate) * o).astype(BF16)


N_SWA_INPUTS = 5


def _swa_ret_kernel(*refs):
    n_in = N_SWA_INPUTS + 10
    yb_ref, yc_ref, state_ref = refs[n_in:]
    _swa_kernel(*refs[:N_SWA_INPUTS], yb_ref)
    _ret_kernel(*refs[N_SWA_INPUTS:n_in], yc_ref, state_ref)


def _swa_retention(proj, bias_b, sink_rows, sin_t, cos_t, decay, zeta_t, xi_t, gch):
    assert B_WINDOW == C_CHUNK
    bsz, s, _ = proj.shape
    c = C_CHUNK
    qw = C_HEADS * C_QK_DIM
    kvw = 2 * B_KV_HEADS * B_HEAD_DIM
    qb, kvb = OFF_BQ // B_WIDTH, OFF_BKV // kvw
    cols = (B_HEADS // B_KV_HEADS) * c
    return pl.pallas_call(
        _swa_ret_kernel,
        grid=(bsz, s // c),
        in_specs=[pl.BlockSpec((1, c, B_WIDTH), lambda b, n: (b, n, qb)),
                  pl.BlockSpec((1, c, kvw), lambda b, n: (b, jnp.maximum(n - 1, 0), kvb)),
                  pl.BlockSpec((1, c, kvw), lambda b, n: (b, n, kvb)),
                  pl.BlockSpec((B_KV_HEADS, 2, 2 * c, cols), lambda b, n: (0, 0, 0, 0)),
                  pl.BlockSpec((B_KV_HEADS, 1, cols), lambda b, n: (0, 0, 0)),
                  pl.BlockSpec((1, c, qw), lambda b, n: (b, n, OFF_CQ // qw)),
                  pl.BlockSpec((1, c, qw), lambda b, n: (b, n, OFF_CK // qw)),
                  pl.BlockSpec((1, c, C_WIDTH), lambda b, n: (b, n, OFF_CV // C_WIDTH)),
                  pl.BlockSpec((1, c, C_WIDTH), lambda b, n: (b, n, OFF_CG // C_WIDTH)),
                  pl.BlockSpec((c, LANES), lambda b, n: (n, 0)),
                  pl.BlockSpec((c, LANES), lambda b, n: (n, 0)),
                  pl.BlockSpec((C_HEADS, c, c), lambda b, n: (0, 0, 0)),
                  pl.BlockSpec((C_HEADS // 2, c, LANES), lambda b, n: (0, 0, 0)),
                  pl.BlockSpec((C_HEADS // 2, c, LANES), lambda b, n: (0, 0, 0)),
                  pl.BlockSpec((C_HEADS, 1, LANES), lambda b, n: (0, 0, 0))],
        out_specs=[pl.BlockSpec((1, c, B_WIDTH), lambda b, n: (b, n, 0)),
                   pl.BlockSpec((1, c, C_WIDTH), lambda b, n: (b, n, 0))],
        out_shape=[jax.ShapeDtypeStruct((bsz, s, B_WIDTH), BF16), jax.ShapeDtypeStruct((bsz, s, C_WIDTH), BF16)],
        scratch_shapes=[pltpu.VMEM((C_HEADS, LANES, C_V_DIM), F32)],
        compiler_params=_cparams(("parallel", "arbitrary")),
        name="swa_retention",
    )(proj, proj, proj, bias_b, sink_rows, proj, proj, proj, proj, sin_t, cos_t, decay, zeta_t, xi_t, gch)


def _cast_riders(riders_in, riders_out):
    for src, dst in zip(riders_in, riders_out):
        dst[...] = src[...].astype(BF16)


def _row_split_riders(riders, steps):
    views = [r.reshape(r.size // r.shape[-1], r.shape[-1]) for r in riders]
    for v in views:
        assert v.shape[0] % (steps * 16) == 0, v.shape
    return views, [(v.shape[0] // steps, v.shape[1]) for v in views]


def _out_proj_kernel(ya_ref, yb_ref, yc_ref, wa_ref, wb_ref, wc_ref, x_ref, g_ref, b_ref, *rest):
    n_riders = (len(rest) - 1) // 2
    o_ref = rest[n_riders]
    half = o_ref.shape[0] // 2
    spans = [slice(0, half), slice(half, 2 * half)]
    mixes = [jnp.dot(ya_ref[r, :], wa_ref[...], preferred_element_type=F32)
             + jnp.dot(yb_ref[r, :], wb_ref[...], preferred_element_type=F32)
             + jnp.dot(yc_ref[r, :], wc_ref[...], preferred_element_type=F32) for r in spans]
    for r, mix in zip(spans, mixes):
        o_ref[r, :] = _layer_norm(ALPHA * x_ref[r, :] + mix, g_ref[...], b_ref[...])
    _cast_riders(rest[:n_riders], rest[n_riders + 1:])


def _out_proj_ln(ya, yb, yc, w, x, g, b, riders=()):
    n, d = x.shape
    tm = min(512, n)
    row = lambda i: (i, 0)
    fixed = lambda i: (0, 0)
    once = pl.Buffered(1)
    views, blocks = _row_split_riders(riders, n // tm)
    rider_specs = [pl.BlockSpec(blk, row) for blk in blocks]
    outs = pl.pallas_call(
        _out_proj_kernel,
        grid=(n // tm,),
        in_specs=[pl.BlockSpec((tm, A_WIDTH), row),
                  pl.BlockSpec((tm, B_WIDTH), row),
                  pl.BlockSpec((tm, C_WIDTH), row),
                  pl.BlockSpec((A_WIDTH, d), lambda i: (0, 0), pipeline_mode=once),
                  pl.BlockSpec((B_WIDTH, d), lambda i: (1, 0), pipeline_mode=once),
                  pl.BlockSpec((C_WIDTH, d), lambda i: ((A_WIDTH + B_WIDTH) // C_WIDTH, 0), pipeline_mode=once),
                  pl.BlockSpec((tm, d), row),
                  pl.BlockSpec((1, d), fixed),
                  pl.BlockSpec((1, d), fixed)] + rider_specs,
        out_specs=[pl.BlockSpec((tm, d), row)] + rider_specs,
        out_shape=[jax.ShapeDtypeStruct((n, d), F32)] + [jax.ShapeDtypeStruct(v.shape, BF16) for v in views],
        compiler_params=_cparams(("arbitrary",)),
        name="out_proj_ln",
    )(ya, yb, yc, w, w, w, x, g, b, *views)
    return outs[0], [o.reshape(r.shape) for o, r in zip(outs[1:], riders)]


FFN_TM = 512
FFN_TF = 512


def _swiglu_step(xb, wg, wu, wd):
    hg = jnp.dot(xb, wg, preferred_element_type=F32)
    hu = jnp.dot(xb, wu, preferred_element_type=F32)
    return jnp.dot((_silu(hg) * hu).astype(BF16), wd, preferred_element_type=F32)


def _ffn_kernel(x_ref, wg_ref, wu_ref, wd_ref, g_ref, b_ref, *rest):
    n_riders = (len(rest) - 2) // 2
    riders_in, o_ref = rest[:n_riders], rest[n_riders]
    riders_out, xb_ref = rest[n_riders + 1:-1], rest[-1]
    f = pl.program_id(1)

    @pl.when(f == 0)
    def _():
        o_ref[...] = jnp.zeros_like(o_ref)
        xb_ref[...] = x_ref[...].astype(BF16)

    o_ref[...] += _swiglu_step(xb_ref[...], wg_ref[...], wu_ref[...], wd_ref[...])
    _cast_riders(riders_in, riders_out)

    @pl.when(f == pl.num_programs(1) - 1)
    def _():
        o_ref[...] = _layer_norm(ALPHA * x_ref[...] + o_ref[...], g_ref[...], b_ref[...])


def _ffn(x, w_gate, w_up, w_down, g, b, riders=()):
    n, d = x.shape
    ff = w_gate.shape[1]
    tm = min(FFN_TM, n)
    tf = FFN_TF
    ni, nf = n // tm, ff // tf
    flat, rider_specs = [], []
    for r in riders:
        cols = r.shape[-1]
        rows = r.size // cols
        flat.append(r.reshape(rows, cols))
        if rows % (ni * nf * 16) == 0:
            rider_specs.append(pl.BlockSpec((rows // (ni * nf), cols), lambda i, f: (i * nf + f, 0)))
        else:
            assert rows % (ni * 16) == 0 and cols % (nf * LANES) == 0
            rider_specs.append(pl.BlockSpec((rows // ni, cols // nf), lambda i, f: (i, f)))
    outs = pl.pallas_call(
        _ffn_kernel,
        grid=(n // tm, ff // tf),
        in_specs=[pl.BlockSpec((tm, d), lambda i, f: (i, 0)),
                  pl.BlockSpec((d, tf), lambda i, f: (0, f)),
                  pl.BlockSpec((d, tf), lambda i, f: (0, f)),
                  pl.BlockSpec((tf, d), lambda i, f: (f, 0)),
                  pl.BlockSpec((1, d), lambda i, f: (0, 0)),
                  pl.BlockSpec((1, d), lambda i, f: (0, 0))] + rider_specs,
        out_specs=[pl.BlockSpec((tm, d), lambda i, f: (i, 0))] + rider_specs,
        out_shape=[jax.ShapeDtypeStruct((n, d), F32)] + [jax.ShapeDtypeStruct(r.shape, BF16) for r in flat],
        scratch_shapes=[pltpu.VMEM((tm, d), BF16)],
        compiler_params=_cparams(("arbitrary", "arbitrary")),
        name="ffn_ln",
    )(x, w_gate, w_up, w_down, g, b, *flat)
    return outs[0], [o.reshape(r.shape) for o, r in zip(outs[1:], riders)]


def _row_copy(src, dst, sem, s, t):
    return pltpu.make_async_copy(src.at[pl.ds(s, 1)], dst.at[pl.ds(t, 1)], sem)


def _ffn_grouped_kernel(eid_ref, nact_ref, src_ref, x_hbm, wg_ref, wu_ref, wd_ref, o_ref, rows_ref, xb_ref, sem,
                        *, tm, per_step):
    i = pl.program_id(0)
    f = pl.program_id(1)
    nact = nact_ref[0]
    active = i < nact
    slot = lax.rem(i, 2)
    buf_rows = rows_ref.shape[1]
    last_slot_row = src_ref.shape[0] - 1

    def fetch(tile, into, first_row, count):
        for r in range(count):
            row = first_row + r
            token = src_ref[jnp.minimum(tile * tm + row, last_slot_row)]
            _row_copy(x_hbm, rows_ref.at[into], sem.at[into], token, row).start()

    def wait_buffer(which):
        pltpu.make_async_copy(x_hbm.at[pl.ds(0, buf_rows)], rows_ref.at[which], sem.at[which]).wait()

    @pl.when(jnp.logical_and(i == 0, f == 0))
    def _():
        lax.fori_loop(0, buf_rows // per_step, lambda s, c: (fetch(0, 0, s * per_step, per_step), c)[1], 0)

    @pl.when(jnp.logical_and(f == 0, i <= nact))
    def _():
        wait_buffer(slot)

    @pl.when(f == 0)
    def _():
        o_ref[...] = jnp.zeros_like(o_ref)

    @pl.when(jnp.logical_and(f == 0, active))
    def _():
        xb_ref[...] = rows_ref[slot, :tm, :].astype(BF16)

    @pl.when(active)
    def _():
        fetch(i + 1, 1 - slot, f * per_step, per_step)
        o_ref[...] += _swiglu_step(xb_ref[...], wg_ref[0], wu_ref[0], wd_ref[0])


def _ffn_grouped(x, src, eid, nact, w_gate, w_up, w_down):
    n, d = x.shape
    slots = src.shape[0]
    ff = w_gate.shape[2]
    tm = min(FFN_TM, n)
    tf = FFN_TF
    nf = ff // tf
    per_step = -(-tm // nf)
    per_step += (-per_step) % 8
    assert per_step * nf <= n

    def fidx(i, f, nact_ref):
        return jnp.where(i < nact_ref[0], f, nf - 1)

    grid_spec = pltpu.PrefetchScalarGridSpec(
        num_scalar_prefetch=3,
        grid=(slots // tm, nf),
        in_specs=[pl.BlockSpec(memory_space=pl.ANY),
                  pl.BlockSpec((1, d, tf), lambda i, f, e, a, s: (e[i], 0, fidx(i, f, a))),
                  pl.BlockSpec((1, d, tf), lambda i, f, e, a, s: (e[i], 0, fidx(i, f, a))),
                  pl.BlockSpec((1, tf, d), lambda i, f, e, a, s: (e[i], fidx(i, f, a), 0))],
        out_specs=pl.BlockSpec((tm, d), lambda i, f, e, a, s: (i, 0)),
        scratch_shapes=[pltpu.VMEM((2, per_step * nf, d), F32), pltpu.VMEM((tm, d), BF16),
                        pltpu.SemaphoreType.DMA((2,))],
    )
    return pl.pallas_call(
        functools.partial(_ffn_grouped_kernel, tm=tm, per_step=per_step),
        grid_spec=grid_spec,
        out_shape=jax.ShapeDtypeStruct((slots, d), F32),
        compiler_params=_cparams(("arbitrary", "arbitrary")),
        name="ffn_grouped",
    )(eid, nact, src, x, w_gate, w_up, w_down)


ROUTER_TM = 512


def _router_kernel(x_ref, wr_ref, tri_ref, meta_ref, cnt_ref, carry_ref):
    @pl.when(pl.program_id(0) == 0)
    def _():
        carry_ref[...] = jnp.zeros_like(carry_ref)

    x = x_ref[...]
    x_hi = x.astype(BF16)
    x_lo = (x - x_hi.astype(F32)).astype(BF16)
    parts = (jnp.dot(x_hi, wr_ref[...], preferred_element_type=F32)
             + jnp.dot(x_lo, wr_ref[...], preferred_element_type=F32))
    logits = parts + pltpu.roll(parts, LANES - N_EXPERTS, 1)
    lane = lax.broadcasted_iota(jnp.int32, logits.shape, 1)
    logits = jnp.where(lane < N_EXPERTS, logits, -jnp.inf)
    m1 = jnp.max(logits, axis=-1, keepdims=True)
    i1 = jnp.min(jnp.where(logits == m1, lane, LANES), axis=-1, keepdims=True)
    rest = jnp.where(lane == i1, -jnp.inf, logits)
    m2 = jnp.max(rest, axis=-1, keepdims=True)
    i2 = jnp.min(jnp.where(rest == m2, lane, LANES), axis=-1, keepdims=True)
    e2 = jnp.exp(m2 - m1)
    w1 = 1.0 / (1.0 + e2)
    w2 = e2 * w1
    hit1 = lane == i1
    hit2 = lane == i2
    onehot = jnp.where(jnp.logical_or(hit1, hit2), 1.0, 0.0)
    before = jnp.dot(tri_ref[...], onehot.astype(BF16), preferred_element_type=F32) + carry_ref[...]
    r1 = jnp.sum(jnp.where(hit1, before, 0.0), axis=-1, keepdims=True)
    r2 = jnp.sum(jnp.where(hit2, before, 0.0), axis=-1, keepdims=True)
    carry_ref[...] = carry_ref[...] + jnp.sum(onehot, axis=0, keepdims=True)
    cnt_ref[...] = carry_ref[...]
    meta = jnp.where(lane == 0, i1.astype(F32), 0.0)
    meta = jnp.where(lane == 1, i2.astype(F32), meta)
    meta = jnp.where(lane == 2, r1, meta)
    meta = jnp.where(lane == 3, r2, meta)
    meta = jnp.where(lane == 4, w1, meta)
    meta = jnp.where(lane == 5, w2, meta)
    meta_ref[...] = meta


def _router(x, w_router):
    n, d = x.shape
    tm = min(ROUTER_TM, n)
    w_hi = w_router.astype(BF16)
    w_lo = (w_router.astype(F32) - w_hi.astype(F32)).astype(BF16)
    wr = jnp.zeros((d, LANES), BF16).at[:, :N_EXPERTS].set(w_hi).at[:, N_EXPERTS:2 * N_EXPERTS].set(w_lo)
    tri = jnp.asarray(np.tril(np.ones((tm, tm), np.float32), -1), BF16)
    return pl.pallas_call(
        _router_kernel,
        grid=(n // tm,),
        in_specs=[pl.BlockSpec((tm, d), lambda i: (i, 0)),
                  pl.BlockSpec((d, LANES), lambda i: (0, 0)),
                  pl.BlockSpec((tm, tm), lambda i: (0, 0))],
        out_specs=[pl.BlockSpec((tm, LANES), lambda i: (i, 0)),
                   pl.BlockSpec((1, LANES), lambda i: (0, 0))],
        out_shape=[jax.ShapeDtypeStruct((n, LANES), F32), jax.ShapeDtypeStruct((1, LANES), F32)],
        scratch_shapes=[pltpu.VMEM((1, LANES), F32)],
        compiler_params=_cparams(("arbitrary",)),
        name="router",
    )(x, wr, tri)


MOVE_TM = 256
ISSUE_UNROLL = 8


def _combine_kernel(d1_ref, d2_ref, ys_hbm, x_ref, meta_ref, g_ref, b_ref, o_ref, buf_ref, sem, *, tm):
    base = pl.program_id(0) * tm

    def issue(t, carry):
        row = base + t
        _row_copy(ys_hbm, buf_ref.at[0], sem, d1_ref[row], t).start()
        _row_copy(ys_hbm, buf_ref.at[1], sem, d2_ref[row], t).start()
        return carry

    lax.fori_loop(0, tm, issue, 0, unroll=ISSUE_UNROLL)

    def drain(t, carry):
        _row_copy(ys_hbm, buf_ref.at[0], sem, 0, 0).wait()
        _row_copy(ys_hbm, buf_ref.at[1], sem, 0, 0).wait()
        return carry

    lax.fori_loop(0, tm, drain, 0, unroll=ISSUE_UNROLL)
    meta = meta_ref[...]
    lane = lax.broadcasted_iota(jnp.int32, meta.shape, 1)
    w1 = jnp.sum(jnp.where(lane == 4, meta, 0.0), axis=-1, keepdims=True)
    w2 = jnp.sum(jnp.where(lane == 5, meta, 0.0), axis=-1, keepdims=True)
    f = w1 * buf_ref[0] + w2 * buf_ref[1]
    o_ref[...] = _layer_norm(ALPHA * x_ref[...] + f, g_ref[...], b_ref[...])


def _combine(ys, x, meta, d1, d2, g, b):
    n, d = x.shape
    tm = min(MOVE_TM, n)
    grid_spec = pltpu.PrefetchScalarGridSpec(
        num_scalar_prefetch=2,
        grid=(n // tm,),
        in_specs=[pl.BlockSpec(memory_space=pl.ANY),
                  pl.BlockSpec((tm, d), lambda i, a, c: (i, 0)),
                  pl.BlockSpec((tm, LANES), lambda i, a, c: (i, 0)),
                  pl.BlockSpec((1, d), lambda i, a, c: (0, 0)),
                  pl.BlockSpec((1, d), lambda i, a, c: (0, 0))],
        out_specs=pl.BlockSpec((tm, d), lambda i, a, c: (i, 0)),
        scratch_shapes=[pltpu.VMEM((2, tm, d), F32), pltpu.SemaphoreType.DMA(())],
    )
    return pl.pallas_call(
        functools.partial(_combine_kernel, tm=tm),
        grid_spec=grid_spec,
        out_shape=jax.ShapeDtypeStruct((n, d), F32),
        compiler_params=_cparams(("arbitrary",)),
        name="moe_combine",
    )(d1, d2, ys, x, meta, g, b)


def _moe(x, w_router, w_gate, w_up, w_down, g, b):
    n, d = x.shape
    tm = min(FFN_TM, n)
    meta, cnt = _router(x, w_router)
    i1 = meta[:, 0].astype(jnp.int32)
    i2 = meta[:, 1].astype(jnp.int32)
    counts = cnt[0, :N_EXPERTS].astype(jnp.int32)
    tiles = (counts + tm - 1) // tm
    tile_end = jnp.cumsum(tiles)
    group_start = (tile_end - tiles) * tm
    experts = jnp.arange(N_EXPERTS, dtype=jnp.int32)[None, :]
    start_of = lambda idx: jnp.sum(jnp.where(idx[:, None] == experts, group_start[None, :], 0), axis=1)
    d1 = start_of(i1) + meta[:, 2].astype(jnp.int32)
    d2 = start_of(i2) + meta[:, 3].astype(jnp.int32)
    max_tiles = (2 * n) // tm + N_EXPERTS
    tile_ids = jnp.arange(max_tiles, dtype=jnp.int32)[:, None]
    eid = jnp.minimum(jnp.sum((tile_ids >= tile_end[None, :]).astype(jnp.int32), axis=1), N_EXPERTS - 1)
    nact = tile_end[-1:].astype(jnp.int32)
    token = jnp.arange(n, dtype=jnp.int32)
    src = jnp.zeros((max_tiles * tm,), jnp.int32).at[jnp.concatenate([d1, d2])].set(
        jnp.concatenate([token, token]), unique_indices=True)
    ys = _ffn_grouped(x, src, eid.astype(jnp.int32), nact, w_gate, w_up, w_down)
    return _combine(ys, x, meta, d1, d2, g, b)


def _lambda_init(layer_idx):
    return 0.8 - 0.6 * math.exp(-0.3 * layer_idx)


def _static_tables(s):
    c = C_CHUNK
    ang = jnp.repeat(1.0 / (10000.0 ** jnp.linspace(0.0, 1.0, C_QK_DIM // 2, dtype=F32)), 2)
    ang = jnp.arange(s, dtype=F32)[:, None] * ang[None, :]
    sign = jnp.where(jnp.arange(C_QK_DIM) % 2 == 0, -1.0, 1.0).astype(F32)
    sin_t = jnp.tile(jnp.sin(ang) * sign[None, :], (1, 2))
    cos_t = jnp.tile(jnp.cos(ang), (1, 2))
    log_g = jnp.log(1.0 - jnp.exp2(-5.0 - jnp.arange(C_HEADS, dtype=F32)))
    pos = jnp.arange(c)
    rel = (pos[:, None] - pos[None, :]).astype(F32)
    decay = jnp.where((rel >= 0)[None], jnp.exp(jnp.maximum(rel, 0.0)[None] * log_g[:, None, None]), 0.0)
    zeta = jnp.exp((c - 1 - pos).astype(F32)[:, None] * log_g[None, :])
    xi = jnp.exp((pos + 1).astype(F32)[:, None] * log_g[None, :])
    per_pair = lambda t: jnp.repeat(t.T.reshape(C_HEADS // 2, 2, c), C_QK_DIM, axis=1).transpose(0, 2, 1)
    gch = jnp.broadcast_to(jnp.exp(c * log_g)[:, None, None], (C_HEADS, 1, LANES))
    dist_a = np.arange(A_SUB)[None, :] - np.arange(A_BLOCK)[:, None]
    types = [jnp.where(dist_a + off >= 0, _rel_bucket(jnp.asarray(dist_a + off)), REL_BUCKETS)
             for off in (0, A_SUB, A_BLOCK)]
    types.append(jnp.full(dist_a.shape, REL_BUCKETS - 1, jnp.int32))
    bkt_a = jnp.concatenate([jnp.tile(b, (1, 2)) for b in types], axis=0).astype(jnp.int32)
    w = B_WINDOW
    dist = np.arange(w)[:, None] + w - np.arange(2 * w)[None, :]
    band = (dist >= 0) & (dist < w)
    has_prev = np.stack([np.broadcast_to(np.arange(2 * w)[None, :] >= w, band.shape), np.ones_like(band)])
    bkt_b = jnp.where(band[None] & has_prev, _rel_bucket(jnp.asarray(dist))[None], REL_BUCKETS)
    bkt_b = bkt_b.transpose(0, 2, 1).reshape(2 * 2 * w, w).astype(jnp.int32)
    return sin_t, cos_t, decay, per_pair(zeta), per_pair(xi), gch, bkt_a, bkt_b


def kernel(x, w_in, rel_bias, a_lambda, a_subln_g, b_sinks, w_out, ln_mix_g, ln_mix_b, ln_ffn_g, ln_ffn_b,
           dense_w_gate, dense_w_up, dense_w_down, moe_router, moe_w_gate, moe_w_up, moe_w_down):
    bsz, s, d = x.shape
    n = bsz * s
    sin_t, cos_t, decay, zeta_t, xi_t, gch, bkt_a, bkt_b = _static_tables(s)
    tab_t = rel_bias.astype(F32).T
    bias_a = _bias_lookup(tab_t[:A_HEADS] * LOG2E, bkt_a).reshape(A_HEADS, A_BIAS_TYPES, A_BLOCK, 2 * A_SUB)
    group = B_HEADS // B_KV_HEADS
    bias_b = _bias_lookup(tab_t[A_HEADS:] * LOG2E, bkt_b)
    bias_b = bias_b.reshape(B_KV_HEADS, group, 2, 2 * B_WINDOW, B_WINDOW).transpose(0, 2, 3, 1, 4)
    bias_b = bias_b.reshape(B_KV_HEADS, 2, 2 * B_WINDOW, group * B_WINDOW)
    xf = x.reshape(n, d).astype(F32)
    for l in range(DEPTH):
        w_in_l = jnp.concatenate([w_in[l][:, a:b].astype(BF16) for a, b in _PERM_RUNS], axis=1)
        proj = _in_proj(xf, w_in_l).reshape(bsz, s, PROJ_WIDTH)
        ya = _attn_a(proj, bias_a, a_lambda[l].astype(F32), a_subln_g[l].astype(F32).reshape(1, A_V_DIM),
                     _lambda_init(l))
        sink_rows = jnp.repeat(b_sinks[l].astype(F32).reshape(B_KV_HEADS, 1, group), B_WINDOW, axis=2)
        yb, yc = _swa_retention(proj, bias_b, sink_rows, sin_t, cos_t, decay, zeta_t, xi_t, gch)
        g_mix = ln_mix_g[l].astype(F32).reshape(1, d)
        b_mix = ln_mix_b[l].astype(F32).reshape(1, d)
        j = l // 2
        dense = (dense_w_gate[j], dense_w_up[j], dense_w_down[j]) if l % 2 == 0 else ()
        xf, dense_bf16 = _out_proj_ln(ya.reshape(n, A_WIDTH), yb.reshape(n, B_WIDTH), yc.reshape(n, C_WIDTH),
                                      w_out[l].astype(BF16), xf, g_mix, b_mix, dense)
        g_ffn = ln_ffn_g[l].astype(F32).reshape(1, d)
        b_ffn = ln_ffn_b[l].astype(F32).reshape(1, d)
        if l % 2 == 0:
            experts = (moe_w_gate[j], moe_w_up[j], moe_w_down[j]) if l + 1 < DEPTH else ()
            xf, moe_bf16 = _ffn(xf, *dense_bf16, g_ffn, b_ffn, experts)
        else:
            assert l > 0, "expert weights are cast by the preceding dense layer"
            xf = _moe(xf, moe_router[j], *moe_bf16, g_ffn, b_ffn)
    return xf.reshape(bsz, s, d).astype(x.dtype)
```

```python
import functools
import math

import jax
import jax.numpy as jnp
import numpy as np
from jax import lax
from jax.experimental import pallas as pl
from jax.experimental.pallas import tpu as pltpu

F32 = jnp.float32
BF16 = jnp.bfloat16

D_MODEL = 2048
DEPTH = 2
A_HEADS = 6
A_QK_DIM = 64
A_V_DIM = 128
B_HEADS = 12
B_KV_HEADS = 3
B_HEAD_DIM = 64
B_WINDOW = 128
C_HEADS = 4
C_QK_DIM = 64
C_V_DIM = 128
C_CHUNK = 128
A_WIDTH = A_HEADS * A_V_DIM
B_WIDTH = B_HEADS * B_HEAD_DIM
C_WIDTH = C_HEADS * C_V_DIM
REL_BUCKETS = 32
REL_MAX_DIST = 128
D_FF = 5632
N_EXPERTS = 8
ALPHA = (2.0 * DEPTH) ** 0.25
LN_EPS = 1e-5
NEG = -1e30

LANES = 128
VMEM_LIMIT = 56 * 1024 * 1024

_REF_SIZES = [768, 768, 768, 768, 192, 192, 256, 256, 512, 512]
_REF_OFF = [int(v) for v in np.concatenate([[0], np.cumsum(_REF_SIZES)[:-1]])]
PROJ_WIDTH = int(sum(_REF_SIZES))
OFF_AQ, OFF_AK, OFF_AV, OFF_BQ, OFF_CV, OFF_CG, OFF_CQ, OFF_CK, OFF_BKV = (
    0, 768, 1536, 2304, 3072, 3584, 4096, 4352, 4608)


def _proj_perm():
    aq, ak, av, bq, bk, bv, cq, ck, cv, cg = [np.arange(o, o + s) for o, s in zip(_REF_OFF, _REF_SIZES)]
    bkv = np.concatenate([np.concatenate([bk[g * 64:(g + 1) * 64], bv[g * 64:(g + 1) * 64]])
                          for g in range(B_KV_HEADS)])
    perm = np.concatenate([aq, ak, av, bq, cv, cg, cq, ck, bkv])
    assert perm.shape[0] == PROJ_WIDTH
    return perm


def _perm_runs():
    perm = _proj_perm()
    cuts = np.flatnonzero(np.diff(perm) != 1) + 1
    return [(int(r[0]), int(r[-1]) + 1) for r in np.split(perm, cuts)]


_PERM_RUNS = _perm_runs()


def _cparams(sem):
    return pltpu.CompilerParams(dimension_semantics=sem, vmem_limit_bytes=VMEM_LIMIT)


def _layer_norm(z, g, b):
    mu = jnp.mean(z, axis=-1, keepdims=True)
    zc = z - mu
    var = jnp.mean(zc * zc, axis=-1, keepdims=True)
    return zc * lax.rsqrt(var + LN_EPS) * g + b


def _silu(x):
    return x / (1.0 + jnp.exp(-x))


def _dot_nt(a, b):
    return lax.dot_general(a, b, (((1,), (1,)), ((), ())), preferred_element_type=F32)


def _dot_tn(a, b):
    return lax.dot_general(a, b, (((0,), (0,)), ((), ())), preferred_element_type=F32)


def _bias_kernel(tab_ref, bkt_ref, o_ref):
    h = pl.program_id(0)
    bkt = bkt_ref[...]
    acc = jnp.full(bkt.shape, NEG, F32)
    for b in range(REL_BUCKETS):
        acc = jnp.where(bkt == b, tab_ref[h, b], acc)
    o_ref[0] = acc


def _bias_lookup(tab_t, bkt):
    nh = tab_t.shape[0]
    r, c = bkt.shape
    return pl.pallas_call(
        _bias_kernel,
        grid=(nh,),
        in_specs=[pl.BlockSpec(memory_space=pltpu.SMEM),
                  pl.BlockSpec((r, c), lambda h: (0, 0))],
        out_specs=pl.BlockSpec((1, r, c), lambda h: (h, 0, 0)),
        out_shape=jax.ShapeDtypeStruct((nh, r, c), F32),
        compiler_params=_cparams(("arbitrary",)),
        name="bias_lookup",
    )(tab_t, bkt)


def _rel_bucket(dist):
    max_exact = REL_BUCKETS // 2
    d = jnp.maximum(dist, 0)
    ratio = jnp.maximum(d, 1).astype(F32) / max_exact
    large = max_exact + (jnp.log(ratio) / math.log(REL_MAX_DIST / max_exact)
                         * (REL_BUCKETS - max_exact)).astype(jnp.int32)
    large = jnp.minimum(large, REL_BUCKETS - 1)
    return jnp.where(d < max_exact, d, large)


MXU_COLS = 256
IN_PROJ_COL_STEPS = 4
PROJ_PAD = -(-PROJ_WIDTH // (IN_PROJ_COL_STEPS * MXU_COLS)) * (IN_PROJ_COL_STEPS * MXU_COLS)


def _in_proj_kernel(x_ref, w_ref, o_ref, xb_ref):
    @pl.when(pl.program_id(1) == 0)
    def _():
        xb_ref[...] = x_ref[...].astype(BF16)

    o_ref[...] = jnp.dot(xb_ref[...], w_ref[...], preferred_element_type=F32).astype(BF16)


def _in_proj(x, w):
    n, d = x.shape
    p = w.shape[1]
    tm = min(1024, n)
    tn = p // IN_PROJ_COL_STEPS
    return pl.pallas_call(
        _in_proj_kernel,
        grid=(n // tm, p // tn),
        in_specs=[pl.BlockSpec((tm, d), lambda i, j: (i, 0)),
                  pl.BlockSpec((d, tn), lambda i, j: (0, j))],
        out_specs=pl.BlockSpec((tm, tn), lambda i, j: (i, j)),
        out_shape=jax.ShapeDtypeStruct((n, p), BF16),
        scratch_shapes=[pltpu.VMEM((tm, d), BF16)],
        compiler_params=_cparams(("parallel", "arbitrary")),
        name="in_proj",
    )(x, w)


A_BLOCK = 256
A_SUB = 128
A_QROWS = 512
A_BIAS_TYPES = 4
LOG2E = math.log2(math.e)


def _attn_a_kernel(q_ref, k_ref, v_ref, bias_ref, lam_ref, g_ref, o_ref, qt_ref, s_ref, p_ref, acc_ref, *,
                   lam_init):
    t = A_BLOCK
    nch = A_QROWS // A_SUB
    per_key_block = t // A_SUB
    i = pl.program_id(2)
    lane = lax.broadcasted_iota(jnp.int32, (A_SUB, LANES), 1)
    for c in range(nch):
        q = q_ref[0, c * A_SUB:(c + 1) * A_SUB, :].astype(F32) * (A_QK_DIM ** -0.5 * LOG2E)
        qq = jnp.concatenate([jnp.where(lane < A_QK_DIM, q, 0.0), jnp.where(lane >= A_QK_DIM, q, 0.0)], axis=0)
        qt_ref[c] = qq.T.astype(BF16)
    first_block = i * (nch // per_key_block)
    n_blocks = first_block + nch // per_key_block

    all_chains = tuple(range(nch))
    late_chains = all_chains[per_key_block:]

    def scores_into(slot, j, chains):
        kj = k_ref[0, pl.ds(pl.multiple_of(j * t, t), t), :]
        for c in chains:
            back = first_block + c // per_key_block - j
            near = (0, 2) if c % per_key_block == 0 else (1, 3)
            kind = jnp.where(back == 0, near[0], jnp.where(back == 1, near[1], 3))
            s_ref[slot, c] = jnp.dot(kj, qt_ref[c], preferred_element_type=F32) + bias_ref[0, kind]

    def values_from(slot, j, alphas, chains):
        vj = v_ref[0, pl.ds(pl.multiple_of(j * t, t), t), :]
        for c in chains:
            acc_ref[c] = alphas[c] * acc_ref[c] + _dot_tn(vj, p_ref[slot, c])

    def sub_step(j, cur, carry, next_chains, chains):
        if next_chains:
            scores_into(1 - cur, j + 1, next_chains)
        new = list(carry)
        for c in chains:
            m_old, l_old, _ = carry[c]
            s = s_ref[cur, c]
            m_new = jnp.maximum(m_old, jnp.max(s, axis=0, keepdims=True))
            p = jnp.exp2(s - m_new)
            new[c] = (m_new, jnp.exp2(m_old - m_new) * l_old + jnp.sum(p, axis=0, keepdims=True),
                      jnp.exp2(m_old - m_new))
            p_ref[cur, c] = p.astype(BF16)
        values_from(1 - cur, jnp.maximum(j - 1, 0), [a for _, _, a in carry], all_chains)
        return tuple(new)

    def pair(jj, carry):
        carry = sub_step(2 * jj, 0, carry, all_chains, all_chains)
        return sub_step(2 * jj + 1, 1, carry, all_chains, all_chains)

    scores_into(0, 0, all_chains)
    p_ref[1] = jnp.zeros(p_ref.shape[1:], BF16)
    acc_ref[...] = jnp.zeros_like(acc_ref)
    init = tuple((jnp.full((1, 2 * A_SUB), NEG, F32), jnp.zeros((1, 2 * A_SUB), F32),
                  jnp.ones((1, 2 * A_SUB), F32)) for _ in range(nch))
    fin = lax.fori_loop(0, n_blocks // 2 - 1, pair, init)
    fin = sub_step(n_blocks - 2, 0, fin, late_chains, all_chains)
    fin = sub_step(n_blocks - 1, 1, fin, (), late_chains)
    values_from(1, n_blocks - 1, [a for _, _, a in fin], late_chains)
    lp = lam_ref[...]
    lam = (jnp.exp(jnp.sum(lp[0:1] * lp[1:2], axis=-1, keepdims=True))
           - jnp.exp(jnp.sum(lp[2:3] * lp[3:4], axis=-1, keepdims=True)) + lam_init)
    for c, (_, l_fin, _) in enumerate(fin):
        o_all = acc_ref[c] / l_fin
        o = (o_all[:, :A_SUB] - lam * o_all[:, A_SUB:]).T
        o = o * lax.rsqrt(jnp.mean(o * o, axis=-1, keepdims=True) + LN_EPS) * g_ref[...]
        o_ref[0, c * A_SUB:(c + 1) * A_SUB, :] = (o * (1.0 - lam_init)).astype(BF16)


def _attn_a(proj, bias_a, lam_params, subln_g, lam_init):
    bsz, s, _ = proj.shape
    t = A_QROWS
    nch = A_QROWS // A_SUB
    kb, vb = OFF_AK // LANES, OFF_AV // LANES
    return pl.pallas_call(
        functools.partial(_attn_a_kernel, lam_init=lam_init),
        grid=(bsz, A_HEADS, s // t),
        in_specs=[pl.BlockSpec((1, t, LANES), lambda b, h, i: (b, i, h)),
                  pl.BlockSpec((1, s, LANES), lambda b, h, i: (b, 0, kb + h)),
                  pl.BlockSpec((1, s, LANES), lambda b, h, i: (b, 0, vb + h)),
                  pl.BlockSpec((1, A_BIAS_TYPES, A_BLOCK, 2 * A_SUB), lambda b, h, i: (h, 0, 0, 0)),
                  pl.BlockSpec((4, A_QK_DIM), lambda b, h, i: (0, 0)),
                  pl.BlockSpec((1, A_V_DIM), lambda b, h, i: (0, 0))],
        out_specs=pl.BlockSpec((1, t, LANES), lambda b, h, i: (b, i, h)),
        out_shape=jax.ShapeDtypeStruct((bsz, s, A_WIDTH), BF16),
        scratch_shapes=[pltpu.VMEM((nch, LANES, 2 * A_SUB), BF16),
                        pltpu.VMEM((2, nch, A_BLOCK, 2 * A_SUB), F32),
                        pltpu.VMEM((2, nch, A_BLOCK, 2 * A_SUB), BF16),
                        pltpu.VMEM((nch, A_V_DIM, 2 * A_SUB), F32)],
        compiler_params=_cparams(("parallel", "parallel", "arbitrary")),
        name="attn_a",
    )(proj, proj, proj, bias_a, lam_params, subln_g)


def _swa_kernel(q_ref, kvp_ref, kvc_ref, bias_ref, sink_ref, o_ref):
    w = B_WINDOW
    d = B_HEAD_DIM
    group = B_HEADS // B_KV_HEADS
    table = jnp.minimum(pl.program_id(1), 1)
    row = lax.broadcasted_iota(jnp.int32, (LANES, w), 0)
    zeros = jnp.zeros((LANES - d, w), F32)
    kvs, q_ts = [], []
    for g in range(B_KV_HEADS):
        kvs.append(jnp.concatenate([kvp_ref[0, :, g * LANES:(g + 1) * LANES],
                                    kvc_ref[0, :, g * LANES:(g + 1) * LANES]], axis=0))
        cols = []
        for pair in range(group // 2):
            blk = g * (group // 2) + pair
            t = (q_ref[0, :, blk * LANES:(blk + 1) * LANES].astype(F32) * (d ** -0.5 * LOG2E)).T
            cols += [jnp.where(row < d, t, 0.0), jnp.concatenate([t[d:], zeros], axis=0)]
        q_ts.append(jnp.concatenate(cols, axis=1).astype(BF16))
    scores = [jnp.dot(kv, q_t, preferred_element_type=F32) + bias_ref[g, table]
              for g, (kv, q_t) in enumerate(zip(kvs, q_ts))]
    soft = []
    for g, s in enumerate(scores):
        sink = sink_ref[g] * LOG2E
        m = jnp.maximum(jnp.max(s, axis=0, keepdims=True), sink)
        e = jnp.exp2(s - m)
        soft.append((e.astype(BF16), 1.0 / (jnp.sum(e, axis=0, keepdims=True) + jnp.exp2(sink - m))))
    outs = [_dot_tn(kv, e) * inv for kv, (e, inv) in zip(kvs, soft)]
    for g, o_t in enumerate(outs):
        for pair in range(group // 2):
            blk = g * (group // 2) + pair
            both = jnp.concatenate([o_t[d:, (2 * pair) * w:(2 * pair + 1) * w],
                                    o_t[d:, (2 * pair + 1) * w:(2 * pair + 2) * w]], axis=0)
            o_ref[0, :, blk * LANES:(blk + 1) * LANES] = both.T.astype(BF16)


def _ret_kernel(q_ref, k_ref, v_ref, g_ref, sin_ref, cos_ref, decay_ref, zeta_ref, xi_ref, gch_ref, o_ref,
                state_ref):
    c = C_CHUNK

    @pl.when(pl.program_id(1) == 0)
    def _():
        state_ref[...] = jnp.zeros_like(state_ref)

    lane = lax.broadcasted_iota(jnp.int32, (c, LANES), 1)
    even = (lane & 1) == 0
    sn = sin_ref[...]
    cs = cos_ref[...]

    def rope(x):
        swapped = jnp.where(even, pltpu.roll(x, LANES - 1, 1), pltpu.roll(x, 1, 1))
        return x * cs + swapped * sn

    qk = []
    for pair in range(C_HEADS // 2):
        q = rope(q_ref[0, :, pair * LANES:(pair + 1) * LANES].astype(F32))
        k = rope(k_ref[0, :, pair * LANES:(pair + 1) * LANES].astype(F32) * (C_QK_DIM ** -0.5))
        qk.append((q, q * xi_ref[pair], k.astype(BF16), (k * zeta_ref[pair]).astype(BF16)))
    masked = []
    for h in range(C_HEADS):
        q, qx, _, _ = qk[h // 2]
        in_head = (lane < C_QK_DIM) if h % 2 == 0 else (lane >= C_QK_DIM)
        masked.append((jnp.where(in_head, q, 0.0).astype(BF16), jnp.where(in_head, qx, 0.0).astype(BF16)))
    inner = [(_dot_nt(masked[h][0], qk[h // 2][2]) * decay_ref[h]).astype(BF16) for h in range(C_HEADS)]
    outs = []
    for h in range(C_HEADS):
        vh = v_ref[0, :, h * LANES:(h + 1) * LANES]
        st = state_ref[h]
        outs.append(jnp.dot(inner[h], vh, preferred_element_type=F32)
                    + jnp.dot(masked[h][1], st.astype(BF16), preferred_element_type=F32))
        state_ref[h] = st * gch_ref[h] + _dot_tn(qk[h // 2][3], vh)
    for h, o in enumerate(outs):
        mu = jnp.mean(o, axis=-1, keepdims=True)
        oc = o - mu
        o = oc * lax.rsqrt(jnp.mean(oc * oc, axis=-1, keepdims=True) + LN_EPS)
        gate = g_ref[0, :, h * LANES:(h + 1) * LANES].astype(F32)
        o_ref[0, :, h * LANES:(h + 1) * LANES] = (_silu(gate) * o).astype(BF16)


N_SWA_INPUTS = 5


def _swa_ret_kernel(*refs):
    n_in = N_SWA_INPUTS + 10
    yb_ref, yc_ref, state_ref = refs[n_in:]
    _swa_kernel(*refs[:N_SWA_INPUTS], yb_ref)
    _ret_kernel(*refs[N_SWA_INPUTS:n_in], yc_ref, state_ref)


def _swa_retention(proj, bias_b, sink_rows, sin_t, cos_t, decay, zeta_t, xi_t, gch):
    assert B_WINDOW == C_CHUNK
    bsz, s, _ = proj.shape
    c = C_CHUNK
    qw = C_HEADS * C_QK_DIM
    kvw = 2 * B_KV_HEADS * B_HEAD_DIM
    qb, kvb = OFF_BQ // B_WIDTH, OFF_BKV // kvw
    cols = (B_HEADS // B_KV_HEADS) * c
    return pl.pallas_call(
        _swa_ret_kernel,
        grid=(bsz, s // c),
        in_specs=[pl.BlockSpec((1, c, B_WIDTH), lambda b, n: (b, n, qb)),
                  pl.BlockSpec((1, c, kvw), lambda b, n: (b, jnp.maximum(n - 1, 0), kvb)),
                  pl.BlockSpec((1, c, kvw), lambda b, n: (b, n, kvb)),
                  pl.BlockSpec((B_KV_HEADS, 2, 2 * c, cols), lambda b, n: (0, 0, 0, 0)),
                  pl.BlockSpec((B_KV_HEADS, 1, cols), lambda b, n: (0, 0, 0)),
                  pl.BlockSpec((1, c, qw), lambda b, n: (b, n, OFF_CQ // qw)),
                  pl.BlockSpec((1, c, qw), lambda b, n: (b, n, OFF_CK // qw)),
                  pl.BlockSpec((1, c, C_WIDTH), lambda b, n: (b, n, OFF_CV // C_WIDTH)),
                  pl.BlockSpec((1, c, C_WIDTH), lambda b, n: (b, n, OFF_CG // C_WIDTH)),
                  pl.BlockSpec((c, LANES), lambda b, n: (n, 0)),
                  pl.BlockSpec((c, LANES), lambda b, n: (n, 0)),
                  pl.BlockSpec((C_HEADS, c, c), lambda b, n: (0, 0, 0)),
                  pl.BlockSpec((C_HEADS // 2, c, LANES), lambda b, n: (0, 0, 0)),
                  pl.BlockSpec((C_HEADS // 2, c, LANES), lambda b, n: (0, 0, 0)),
                  pl.BlockSpec((C_HEADS, 1, LANES), lambda b, n: (0, 0, 0))],
        out_specs=[pl.BlockSpec((1, c, B_WIDTH), lambda b, n: (b, n, 0)),
                   pl.BlockSpec((1, c, C_WIDTH), lambda b, n: (b, n, 0))],
        out_shape=[jax.ShapeDtypeStruct((bsz, s, B_WIDTH), BF16), jax.ShapeDtypeStruct((bsz, s, C_WIDTH), BF16)],
        scratch_shapes=[pltpu.VMEM((C_HEADS, LANES, C_V_DIM), F32)],
        compiler_params=_cparams(("parallel", "arbitrary")),
        name="swa_retention",
    )(proj, proj, proj, bias_b, sink_rows, proj, proj, proj, proj, sin_t, cos_t, decay, zeta_t, xi_t, gch)


def _cast_riders(riders_in, riders_out):
    for src, dst in zip(riders_in, riders_out):
        dst[...] = src[...].astype(BF16)


def _row_split_riders(riders, steps):
    views = [r.reshape(r.size // r.shape[-1], r.shape[-1]) for r in riders]
    for v in views:
        assert v.shape[0] % (steps * 16) == 0, v.shape
    return views, [(v.shape[0] // steps, v.shape[1]) for v in views]


def _out_proj_kernel(ya_ref, yb_ref, yc_ref, wa_ref, wb_ref, wc_ref, x_ref, g_ref, b_ref, *rest):
    n_riders = (len(rest) - 1) // 2
    o_ref = rest[n_riders]
    half = o_ref.shape[0] // 2
    spans = [slice(0, half), slice(half, 2 * half)]
    mixes = [jnp.dot(ya_ref[r, :], wa_ref[...], preferred_element_type=F32)
             + jnp.dot(yb_ref[r, :], wb_ref[...], preferred_element_type=F32)
             + jnp.dot(yc_ref[r, :], wc_ref[...], preferred_element_type=F32) for r in spans]
    for r, mix in zip(spans, mixes):
        o_ref[r, :] = _layer_norm(ALPHA * x_ref[r, :] + mix, g_ref[...], b_ref[...])
    _cast_riders(rest[:n_riders], rest[n_riders + 1:])


def _out_proj_ln(ya, yb, yc, w, x, g, b, riders=()):
    n, d = x.shape
    tm = min(512, n)
    row = lambda i: (i, 0)
    fixed = lambda i: (0, 0)
    once = pl.Buffered(1)
    views, blocks = _row_split_riders(riders, n // tm)
    rider_specs = [pl.BlockSpec(blk, row) for blk in blocks]
    outs = pl.pallas_call(
        _out_proj_kernel,
        grid=(n // tm,),
        in_specs=[pl.BlockSpec((tm, A_WIDTH), row),
                  pl.BlockSpec((tm, B_WIDTH), row),
                  pl.BlockSpec((tm, C_WIDTH), row),
                  pl.BlockSpec((A_WIDTH, d), lambda i: (0, 0), pipeline_mode=once),
                  pl.BlockSpec((B_WIDTH, d), lambda i: (1, 0), pipeline_mode=once),
                  pl.BlockSpec((C_WIDTH, d), lambda i: ((A_WIDTH + B_WIDTH) // C_WIDTH, 0), pipeline_mode=once),
                  pl.BlockSpec((tm, d), row),
                  pl.BlockSpec((1, d), fixed),
                  pl.BlockSpec((1, d), fixed)] + rider_specs,
        out_specs=[pl.BlockSpec((tm, d), row)] + rider_specs,
        out_shape=[jax.ShapeDtypeStruct((n, d), F32)] + [jax.ShapeDtypeStruct(v.shape, BF16) for v in views],
        compiler_params=_cparams(("arbitrary",)),
        name="out_proj_ln",
    )(ya, yb, yc, w, w, w, x, g, b, *views)
    return outs[0], [o.reshape(r.shape) for o, r in zip(outs[1:], riders)]


FFN_TM = 512
FFN_TF = 512


def _swiglu_step(xb, wg, wu, wd):
    hg = jnp.dot(xb, wg, preferred_element_type=F32)
    hu = jnp.dot(xb, wu, preferred_element_type=F32)
    return jnp.dot((_silu(hg) * hu).astype(BF16), wd, preferred_element_type=F32)


def _ffn_kernel(x_ref, wg_ref, wu_ref, wd_ref, g_ref, b_ref, *rest):
    n_riders = (len(rest) - 2) // 2
    riders_in, o_ref = rest[:n_riders], rest[n_riders]
    riders_out, xb_ref = rest[n_riders + 1:-1], rest[-1]
    f = pl.program_id(1)

    @pl.when(f == 0)
    def _():
        o_ref[...] = jnp.zeros_like(o_ref)
        xb_ref[...] = x_ref[...].astype(BF16)

    o_ref[...] += _swiglu_step(xb_ref[...], wg_ref[...], wu_ref[...], wd_ref[...])
    _cast_riders(riders_in, riders_out)

    @pl.when(f == pl.num_programs(1) - 1)
    def _():
        o_ref[...] = _layer_norm(ALPHA * x_ref[...] + o_ref[...], g_ref[...], b_ref[...])


def _ffn(x, w_gate, w_up, w_down, g, b, riders=()):
    n, d = x.shape
    ff = w_gate.shape[1]
    tm = min(FFN_TM, n)
    tf = FFN_TF
    ni, nf = n // tm, ff // tf
    flat, rider_specs = [], []
    for r in riders:
        cols = r.shape[-1]
        rows = r.size // cols
        flat.append(r.reshape(rows, cols))
        if rows % (ni * nf * 16) == 0:
            rider_specs.append(pl.BlockSpec((rows // (ni * nf), cols), lambda i, f: (i * nf + f, 0)))
        else:
            assert rows % (ni * 16) == 0 and cols % (nf * LANES) == 0
            rider_specs.append(pl.BlockSpec((rows // ni, cols // nf), lambda i, f: (i, f)))
    outs = pl.pallas_call(
        _ffn_kernel,
        grid=(n // tm, ff // tf),
        in_specs=[pl.BlockSpec((tm, d), lambda i, f: (i, 0)),
                  pl.BlockSpec((d, tf), lambda i, f: (0, f)),
                  pl.BlockSpec((d, tf), lambda i, f: (0, f)),
                  pl.BlockSpec((tf, d), lambda i, f: (f, 0)),
                  pl.BlockSpec((1, d), lambda i, f: (0, 0)),
                  pl.BlockSpec((1, d), lambda i, f: (0, 0))] + rider_specs,
        out_specs=[pl.BlockSpec((tm, d), lambda i, f: (i, 0))] + rider_specs,
        out_shape=[jax.ShapeDtypeStruct((n, d), F32)] + [jax.ShapeDtypeStruct(r.shape, BF16) for r in flat],
        scratch_shapes=[pltpu.VMEM((tm, d), BF16)],
        compiler_params=_cparams(("arbitrary", "arbitrary")),
        name="ffn_ln",
    )(x, w_gate, w_up, w_down, g, b, *flat)
    return outs[0], [o.reshape(r.shape) for o, r in zip(outs[1:], riders)]


def _row_copy(src, dst, sem, s, t):
    return pltpu.make_async_copy(src.at[pl.ds(s, 1)], dst.at[pl.ds(t, 1)], sem)


def _ffn_grouped_kernel(eid_ref, nact_ref, src_ref, x_hbm, wg_ref, wu_ref, wd_ref, o_ref, rows_ref, xb_ref, sem,
                        *, tm, per_step):
    i = pl.program_id(0)
    f = pl.program_id(1)
    nact = nact_ref[0]
    active = i < nact
    slot = lax.rem(i, 2)
    buf_rows = rows_ref.shape[1]
    last_slot_row = src_ref.shape[0] - 1

    def fetch(tile, into, first_row, count):
        for r in range(count):
            row = first_row + r
            token = src_ref[jnp.minimum(tile * tm + row, last_slot_row)]
            _row_copy(x_hbm, rows_ref.at[into], sem.at[into], token, row).start()

    def wait_buffer(which):
        pltpu.make_async_copy(x_hbm.at[pl.ds(0, buf_rows)], rows_ref.at[which], sem.at[which]).wait()

    @pl.when(jnp.logical_and(i == 0, f == 0))
    def _():
        lax.fori_loop(0, buf_rows // per_step, lambda s, c: (fetch(0, 0, s * per_step, per_step), c)[1], 0)

    @pl.when(jnp.logical_and(f == 0, i <= nact))
    def _():
        wait_buffer(slot)

    @pl.when(f == 0)
    def _():
        o_ref[...] = jnp.zeros_like(o_ref)

    @pl.when(jnp.logical_and(f == 0, active))
    def _():
        xb_ref[...] = rows_ref[slot, :tm, :].astype(BF16)

    @pl.when(active)
    def _():
        fetch(i + 1, 1 - slot, f * per_step, per_step)
        o_ref[...] += _swiglu_step(xb_ref[...], wg_ref[0], wu_ref[0], wd_ref[0])


def _ffn_grouped(x, src, eid, nact, w_gate, w_up, w_down):
    n, d = x.shape
    slots = src.shape[0]
    ff = w_gate.shape[2]
    tm = min(FFN_TM, n)
    tf = FFN_TF
    nf = ff // tf
    per_step = -(-tm // nf)
    per_step += (-per_step) % 8
    assert per_step * nf <= n

    def fidx(i, f, nact_ref):
        return jnp.where(i < nact_ref[0], f, nf - 1)

    grid_spec = pltpu.PrefetchScalarGridSpec(
        num_scalar_prefetch=3,
        grid=(slots // tm, nf),
        in_specs=[pl.BlockSpec(memory_space=pl.ANY),
                  pl.BlockSpec((1, d, tf), lambda i, f, e, a, s: (e[i], 0, fidx(i, f, a))),
                  pl.BlockSpec((1, d, tf), lambda i, f, e, a, s: (e[i], 0, fidx(i, f, a))),
                  pl.BlockSpec((1, tf, d), lambda i, f, e, a, s: (e[i], fidx(i, f, a), 0))],
        out_specs=pl.BlockSpec((tm, d), lambda i, f, e, a, s: (i, 0)),
        scratch_shapes=[pltpu.VMEM((2, per_step * nf, d), F32), pltpu.VMEM((tm, d), BF16),
                        pltpu.SemaphoreType.DMA((2,))],
    )
    return pl.pallas_call(
        functools.partial(_ffn_grouped_kernel, tm=tm, per_step=per_step),
        grid_spec=grid_spec,
        out_shape=jax.ShapeDtypeStruct((slots, d), F32),
        compiler_params=_cparams(("arbitrary", "arbitrary")),
        name="ffn_grouped",
    )(eid, nact, src, x, w_gate, w_up, w_down)


ROUTER_TM = 512


def _router_kernel(x_ref, wr_ref, tri_ref, meta_ref, cnt_ref, carry_ref):
    @pl.when(pl.program_id(0) == 0)
    def _():
        carry_ref[...] = jnp.zeros_like(carry_ref)

    x = x_ref[...]
    x_hi = x.astype(BF16)
    x_lo = (x - x_hi.astype(F32)).astype(BF16)
    parts = (jnp.dot(x_hi, wr_ref[...], preferred_element_type=F32)
             + jnp.dot(x_lo, wr_ref[...], preferred_element_type=F32))
    logits = parts + pltpu.roll(parts, LANES - N_EXPERTS, 1)
    lane = lax.broadcasted_iota(jnp.int32, logits.shape, 1)
    logits = jnp.where(lane < N_EXPERTS, logits, -jnp.inf)
    m1 = jnp.max(logits, axis=-1, keepdims=True)
    i1 = jnp.min(jnp.where(logits == m1, lane, LANES), axis=-1, keepdims=True)
    rest = jnp.where(lane == i1, -jnp.inf, logits)
    m2 = jnp.max(rest, axis=-1, keepdims=True)
    i2 = jnp.min(jnp.where(rest == m2, lane, LANES), axis=-1, keepdims=True)
    e2 = jnp.exp(m2 - m1)
    w1 = 1.0 / (1.0 + e2)
    w2 = e2 * w1
    hit1 = lane == i1
    hit2 = lane == i2
    onehot = jnp.where(jnp.logical_or(hit1, hit2), 1.0, 0.0)
    before = jnp.dot(tri_ref[...], onehot.astype(BF16), preferred_element_type=F32) + carry_ref[...]
    r1 = jnp.sum(jnp.where(hit1, before, 0.0), axis=-1, keepdims=True)
    r2 = jnp.sum(jnp.where(hit2, before, 0.0), axis=-1, keepdims=True)
    carry_ref[...] = carry_ref[...] + jnp.sum(onehot, axis=0, keepdims=True)
    cnt_ref[...] = carry_ref[...]
    meta = jnp.where(lane == 0, i1.astype(F32), 0.0)
    meta = jnp.where(lane == 1, i2.astype(F32), meta)
    meta = jnp.where(lane == 2, r1, meta)
    meta = jnp.where(lane == 3, r2, meta)
    meta = jnp.where(lane == 4, w1, meta)
    meta = jnp.where(lane == 5, w2, meta)
    meta_ref[...] = meta


def _router(x, w_router):
    n, d = x.shape
    tm = min(ROUTER_TM, n)
    w_hi = w_router.astype(BF16)
    w_lo = (w_router.astype(F32) - w_hi.astype(F32)).astype(BF16)
    wr = jnp.zeros((d, LANES), BF16).at[:, :N_EXPERTS].set(w_hi).at[:, N_EXPERTS:2 * N_EXPERTS].set(w_lo)
    tri = jnp.asarray(np.tril(np.ones((tm, tm), np.float32), -1), BF16)
    return pl.pallas_call(
        _router_kernel,
        grid=(n // tm,),
        in_specs=[pl.BlockSpec((tm, d), lambda i: (i, 0)),
                  pl.BlockSpec((d, LANES), lambda i: (0, 0)),
                  pl.BlockSpec((tm, tm), lambda i: (0, 0))],
        out_specs=[pl.BlockSpec((tm, LANES), lambda i: (i, 0)),
                   pl.BlockSpec((1, LANES), lambda i: (0, 0))],
        out_shape=[jax.ShapeDtypeStruct((n, LANES), F32), jax.ShapeDtypeStruct((1, LANES), F32)],
        scratch_shapes=[pltpu.VMEM((1, LANES), F32)],
        compiler_params=_cparams(("arbitrary",)),
        name="router",
    )(x, wr, tri)


MOVE_TM = 256
ISSUE_UNROLL = 8


def _combine_kernel(d1_ref, d2_ref, ys_hbm, x_ref, meta_ref, g_ref, b_ref, o_ref, buf_ref, sem, *, tm):
    base = pl.program_id(0) * tm

    def issue(t, carry):
        row = base + t
        _row_copy(ys_hbm, buf_ref.at[0], sem, d1_ref[row], t).start()
        _row_copy(ys_hbm, buf_ref.at[1], sem, d2_ref[row], t).start()
        return carry

    lax.fori_loop(0, tm, issue, 0, unroll=ISSUE_UNROLL)

    def drain(t, carry):
        _row_copy(ys_hbm, buf_ref.at[0], sem, 0, 0).wait()
        _row_copy(ys_hbm, buf_ref.at[1], sem, 0, 0).wait()
        return carry

    lax.fori_loop(0, tm, drain, 0, unroll=ISSUE_UNROLL)
    meta = meta_ref[...]
    lane = lax.broadcasted_iota(jnp.int32, meta.shape, 1)
    w1 = jnp.sum(jnp.where(lane == 4, meta, 0.0), axis=-1, keepdims=True)
    w2 = jnp.sum(jnp.where(lane == 5, meta, 0.0), axis=-1, keepdims=True)
    f = w1 * buf_ref[0] + w2 * buf_ref[1]
    o_ref[...] = _layer_norm(ALPHA * x_ref[...] + f, g_ref[...], b_ref[...])


def _combine(ys, x, meta, d1, d2, g, b):
    n, d = x.shape
    tm = min(MOVE_TM, n)
    grid_spec = pltpu.PrefetchScalarGridSpec(
        num_scalar_prefetch=2,
        grid=(n // tm,),
        in_specs=[pl.BlockSpec(memory_space=pl.ANY),
                  pl.BlockSpec((tm, d), lambda i, a, c: (i, 0)),
                  pl.BlockSpec((tm, LANES), lambda i, a, c: (i, 0)),
                  pl.BlockSpec((1, d), lambda i, a, c: (0, 0)),
                  pl.BlockSpec((1, d), lambda i, a, c: (0, 0))],
        out_specs=pl.BlockSpec((tm, d), lambda i, a, c: (i, 0)),
        scratch_shapes=[pltpu.VMEM((2, tm, d), F32), pltpu.SemaphoreType.DMA(())],
    )
    return pl.pallas_call(
        functools.partial(_combine_kernel, tm=tm),
        grid_spec=grid_spec,
        out_shape=jax.ShapeDtypeStruct((n, d), F32),
        compiler_params=_cparams(("arbitrary",)),
        name="moe_combine",
    )(d1, d2, ys, x, meta, g, b)


def _moe(x, w_router, w_gate, w_up, w_down, g, b):
    n, d = x.shape
    tm = min(FFN_TM, n)
    meta, cnt = _router(x, w_router)
    i1 = meta[:, 0].astype(jnp.int32)
    i2 = meta[:, 1].astype(jnp.int32)
    counts = cnt[0, :N_EXPERTS].astype(jnp.int32)
    tiles = (counts + tm - 1) // tm
    tile_end = jnp.cumsum(tiles)
    group_start = (tile_end - tiles) * tm
    experts = jnp.arange(N_EXPERTS, dtype=jnp.int32)[None, :]
    start_of = lambda idx: jnp.sum(jnp.where(idx[:, None] == experts, group_start[None, :], 0), axis=1)
    d1 = start_of(i1) + meta[:, 2].astype(jnp.int32)
    d2 = start_of(i2) + meta[:, 3].astype(jnp.int32)
    max_tiles = (2 * n) // tm + N_EXPERTS
    tile_ids = jnp.arange(max_tiles, dtype=jnp.int32)[:, None]
    eid = jnp.minimum(jnp.sum((tile_ids >= tile_end[None, :]).astype(jnp.int32), axis=1), N_EXPERTS - 1)
    nact = tile_end[-1:].astype(jnp.int32)
    token = jnp.arange(n, dtype=jnp.int32)
    src = jnp.zeros((max_tiles * tm,), jnp.int32).at[jnp.concatenate([d1, d2])].set(
        jnp.concatenate([token, token]), unique_indices=True)
    ys = _ffn_grouped(x, src, eid.astype(jnp.int32), nact, w_gate, w_up, w_down)
    return _combine(ys, x, meta, d1, d2, g, b)


def _lambda_init(layer_idx):
    return 0.8 - 0.6 * math.exp(-0.3 * layer_idx)


def _static_tables(s):
    c = C_CHUNK
    ang = jnp.repeat(1.0 / (10000.0 ** jnp.linspace(0.0, 1.0, C_QK_DIM // 2, dtype=F32)), 2)
    ang = jnp.arange(s, dtype=F32)[:, None] * ang[None, :]
    sign = jnp.where(jnp.arange(C_QK_DIM) % 2 == 0, -1.0, 1.0).astype(F32)
    sin_t = jnp.tile(jnp.sin(ang) * sign[None, :], (1, 2))
    cos_t = jnp.tile(jnp.cos(ang), (1, 2))
    log_g = jnp.log(1.0 - jnp.exp2(-5.0 - jnp.arange(C_HEADS, dtype=F32)))
    pos = jnp.arange(c)
    rel = (pos[:, None] - pos[None, :]).astype(F32)
    decay = jnp.where((rel >= 0)[None], jnp.exp(jnp.maximum(rel, 0.0)[None] * log_g[:, None, None]), 0.0)
    zeta = jnp.exp((c - 1 - pos).astype(F32)[:, None] * log_g[None, :])
    xi = jnp.exp((pos + 1).astype(F32)[:, None] * log_g[None, :])
    per_pair = lambda t: jnp.repeat(t.T.reshape(C_HEADS // 2, 2, c), C_QK_DIM, axis=1).transpose(0, 2, 1)
    gch = jnp.broadcast_to(jnp.exp(c * log_g)[:, None, None], (C_HEADS, 1, LANES))
    dist_a = np.arange(A_SUB)[None, :] - np.arange(A_BLOCK)[:, None]
    types = [jnp.where(dist_a + off >= 0, _rel_bucket(jnp.asarray(dist_a + off)), REL_BUCKETS)
             for off in (0, A_SUB, A_BLOCK)]
    types.append(jnp.full(dist_a.shape, REL_BUCKETS - 1, jnp.int32))
    bkt_a = jnp.concatenate([jnp.tile(b, (1, 2)) for b in types], axis=0).astype(jnp.int32)
    w = B_WINDOW
    dist = np.arange(w)[:, None] + w - np.arange(2 * w)[None, :]
    band = (dist >= 0) & (dist < w)
    has_prev = np.stack([np.broadcast_to(np.arange(2 * w)[None, :] >= w, band.shape), np.ones_like(band)])
    bkt_b = jnp.where(band[None] & has_prev, _rel_bucket(jnp.asarray(dist))[None], REL_BUCKETS)
    bkt_b = bkt_b.transpose(0, 2, 1).reshape(2 * 2 * w, w).astype(jnp.int32)
    return sin_t, cos_t, decay, per_pair(zeta), per_pair(xi), gch, bkt_a, bkt_b


def kernel(x, w_in, rel_bias, a_lambda, a_subln_g, b_sinks, w_out, ln_mix_g, ln_mix_b, ln_ffn_g, ln_ffn_b,
           dense_w_gate, dense_w_up, dense_w_down, moe_router, moe_w_gate, moe_w_up, moe_w_down):
    bsz, s, d = x.shape
    n = bsz * s
    sin_t, cos_t, decay, zeta_t, xi_t, gch, bkt_a, bkt_b = _static_tables(s)
    tab_t = rel_bias.astype(F32).T
    bias_a = _bias_lookup(tab_t[:A_HEADS] * LOG2E, bkt_a).reshape(A_HEADS, A_BIAS_TYPES, A_BLOCK, 2 * A_SUB)
    group = B_HEADS // B_KV_HEADS
    bias_b = _bias_lookup(tab_t[A_HEADS:] * LOG2E, bkt_b)
    bias_b = bias_b.reshape(B_KV_HEADS, group, 2, 2 * B_WINDOW, B_WINDOW).transpose(0, 2, 3, 1, 4)
    bias_b = bias_b.reshape(B_KV_HEADS, 2, 2 * B_WINDOW, group * B_WINDOW)
    xf = x.reshape(n, d).astype(F32)
    w_in_bf16, w_out_bf16 = w_in[0].astype(BF16), w_out[0].astype(BF16)
    for l in range(DEPTH):
        w_in_l = jnp.concatenate([w_in_bf16[:, a:b] for a, b in _PERM_RUNS]
                                 + [jnp.zeros((d, PROJ_PAD - PROJ_WIDTH), BF16)], axis=1)
        proj = _in_proj(xf, w_in_l).reshape(bsz, s, PROJ_PAD)
        ya = _attn_a(proj, bias_a, a_lambda[l].astype(F32), a_subln_g[l].astype(F32).reshape(1, A_V_DIM),
                     _lambda_init(l))
        sink_rows = jnp.repeat(b_sinks[l].astype(F32).reshape(B_KV_HEADS, 1, group), B_WINDOW, axis=2)
        yb, yc = _swa_retention(proj, bias_b, sink_rows, sin_t, cos_t, decay, zeta_t, xi_t, gch)
        g_mix = ln_mix_g[l].astype(F32).reshape(1, d)
        b_mix = ln_mix_b[l].astype(F32).reshape(1, d)
        j = l // 2
        dense = (dense_w_gate[j], dense_w_up[j], dense_w_down[j]) if l % 2 == 0 else ()
        ahead = (w_in[l + 1], w_out[l + 1]) if l + 1 < DEPTH else ()
        xf, cast = _out_proj_ln(ya.reshape(n, A_WIDTH), yb.reshape(n, B_WIDTH), yc.reshape(n, C_WIDTH),
                                w_out_bf16, xf, g_mix, b_mix, dense + ahead)
        dense_bf16 = cast[:len(dense)]
        if ahead:
            w_in_bf16, w_out_bf16 = cast[len(dense):]
        g_ffn = ln_ffn_g[l].astype(F32).reshape(1, d)
        b_ffn = ln_ffn_b[l].astype(F32).reshape(1, d)
        if l % 2 == 0:
            experts = (moe_w_gate[j], moe_w_up[j], moe_w_down[j]) if l + 1 < DEPTH else ()
            xf, moe_bf16 = _ffn(xf, *dense_bf16, g_ffn, b_ffn, experts)
        else:
            assert l > 0, "expert weights are cast by the preceding dense layer"
            xf = _moe(xf, moe_router[j], *moe_bf16, g_ffn, b_ffn)
    return xf.reshape(bsz, s, d).astype(x.dtype)
```

```python
import functools
import math

import jax
import jax.numpy as jnp
import numpy as np
from jax import lax
from jax.experimental import pallas as pl
from jax.experimental.pallas import tpu as pltpu

F32 = jnp.float32
BF16 = jnp.bfloat16

D_MODEL = 2048
DEPTH = 2
A_HEADS = 6
A_QK_DIM = 64
A_V_DIM = 128
B_HEADS = 12
B_KV_HEADS = 3
B_HEAD_DIM = 64
B_WINDOW = 128
C_HEADS = 4
C_QK_DIM = 64
C_V_DIM = 128
C_CHUNK = 128
A_WIDTH = A_HEADS * A_V_DIM
B_WIDTH = B_HEADS * B_HEAD_DIM
C_WIDTH = C_HEADS * C_V_DIM
REL_BUCKETS = 32
REL_MAX_DIST = 128
D_FF = 5632
N_EXPERTS = 8
ALPHA = (2.0 * DEPTH) ** 0.25
LN_EPS = 1e-5
NEG = -1e30

LANES = 128
VMEM_LIMIT = 56 * 1024 * 1024

_REF_SIZES = [768, 768, 768, 768, 192, 192, 256, 256, 512, 512]
_REF_OFF = [int(v) for v in np.concatenate([[0], np.cumsum(_REF_SIZES)[:-1]])]
PROJ_WIDTH = int(sum(_REF_SIZES))
OFF_AQ, OFF_AK, OFF_AV, OFF_BQ, OFF_CV, OFF_CG, OFF_CQ, OFF_CK, OFF_BKV = (
    0, 768, 1536, 2304, 3072, 3584, 4096, 4352, 4608)


def _proj_perm():
    aq, ak, av, bq, bk, bv, cq, ck, cv, cg = [np.arange(o, o + s) for o, s in zip(_REF_OFF, _REF_SIZES)]
    bkv = np.concatenate([np.concatenate([bk[g * 64:(g + 1) * 64], bv[g * 64:(g + 1) * 64]])
                          for g in range(B_KV_HEADS)])
    perm = np.concatenate([aq, ak, av, bq, cv, cg, cq, ck, bkv])
    assert perm.shape[0] == PROJ_WIDTH
    return perm


def _perm_runs():
    perm = _proj_perm()
    cuts = np.flatnonzero(np.diff(perm) != 1) + 1
    return [(int(r[0]), int(r[-1]) + 1) for r in np.split(perm, cuts)]


_PERM_RUNS = _perm_runs()


def _cparams(sem):
    return pltpu.CompilerParams(dimension_semantics=sem, vmem_limit_bytes=VMEM_LIMIT)


def _layer_norm(z, g, b):
    mu = jnp.mean(z, axis=-1, keepdims=True)
    zc = z - mu
    var = jnp.mean(zc * zc, axis=-1, keepdims=True)
    return zc * lax.rsqrt(var + LN_EPS) * g + b


def _silu(x):
    return x / (1.0 + jnp.exp(-x))


def _dot_nt(a, b):
    return lax.dot_general(a, b, (((1,), (1,)), ((), ())), preferred_element_type=F32)


def _dot_tn(a, b):
    return lax.dot_general(a, b, (((0,), (0,)), ((), ())), preferred_element_type=F32)


def _bias_kernel(tab_ref, bkt_ref, o_ref):
    h = pl.program_id(0)
    bkt = bkt_ref[...]
    acc = jnp.full(bkt.shape, NEG, F32)
    for b in range(REL_BUCKETS):
        acc = jnp.where(bkt == b, tab_ref[h, b], acc)
    o_ref[0] = acc


def _bias_lookup(tab_t, bkt):
    nh = tab_t.shape[0]
    r, c = bkt.shape
    return pl.pallas_call(
        _bias_kernel,
        grid=(nh,),
        in_specs=[pl.BlockSpec(memory_space=pltpu.SMEM),
                  pl.BlockSpec((r, c), lambda h: (0, 0))],
        out_specs=pl.BlockSpec((1, r, c), lambda h: (h, 0, 0)),
        out_shape=jax.ShapeDtypeStruct((nh, r, c), F32),
        compiler_params=_cparams(("arbitrary",)),
        name="bias_lookup",
    )(tab_t, bkt)


def _rel_bucket(dist):
    max_exact = REL_BUCKETS // 2
    d = jnp.maximum(dist, 0)
    ratio = jnp.maximum(d, 1).astype(F32) / max_exact
    large = max_exact + (jnp.log(ratio) / math.log(REL_MAX_DIST / max_exact)
                         * (REL_BUCKETS - max_exact)).astype(jnp.int32)
    large = jnp.minimum(large, REL_BUCKETS - 1)
    return jnp.where(d < max_exact, d, large)


MXU_COLS = 256
IN_PROJ_COL_STEPS = 4
PROJ_PAD = -(-PROJ_WIDTH // (IN_PROJ_COL_STEPS * MXU_COLS)) * (IN_PROJ_COL_STEPS * MXU_COLS)


def _in_proj_kernel(x_ref, w_ref, o_ref, xb_ref):
    @pl.when(pl.program_id(1) == 0)
    def _():
        xb_ref[...] = x_ref[...].astype(BF16)

    o_ref[...] = jnp.dot(xb_ref[...], w_ref[...], preferred_element_type=F32).astype(BF16)


def _in_proj(x, w):
    n, d = x.shape
    p = w.shape[1]
    tm = min(1024, n)
    tn = p // IN_PROJ_COL_STEPS
    return pl.pallas_call(
        _in_proj_kernel,
        grid=(n // tm, p // tn),
        in_specs=[pl.BlockSpec((tm, d), lambda i, j: (i, 0)),
                  pl.BlockSpec((d, tn), lambda i, j: (0, j))],
        out_specs=pl.BlockSpec((tm, tn), lambda i, j: (i, j)),
        out_shape=jax.ShapeDtypeStruct((n, p), BF16),
        scratch_shapes=[pltpu.VMEM((tm, d), BF16)],
        compiler_params=_cparams(("parallel", "arbitrary")),
        name="in_proj",
    )(x, w)


A_BLOCK = 256
A_SUB = 128
A_QROWS = 512
A_BIAS_TYPES = 4
LOG2E = math.log2(math.e)


def _attn_a_kernel(q_ref, k_ref, v_ref, bias_ref, lam_ref, g_ref, o_ref, qt_ref, s_ref, p_ref, acc_ref, *,
                   lam_init):
    t = A_BLOCK
    nch = A_QROWS // A_SUB
    per_key_block = t // A_SUB
    i = pl.program_id(2)
    lane = lax.broadcasted_iota(jnp.int32, (A_SUB, LANES), 1)
    for c in range(nch):
        q = q_ref[0, c * A_SUB:(c + 1) * A_SUB, :].astype(F32) * (A_QK_DIM ** -0.5 * LOG2E)
        qq = jnp.concatenate([jnp.where(lane < A_QK_DIM, q, 0.0), jnp.where(lane >= A_QK_DIM, q, 0.0)], axis=0)
        qt_ref[c] = qq.T.astype(BF16)
    first_block = i * (nch // per_key_block)
    n_blocks = first_block + nch // per_key_block

    all_chains = tuple(range(nch))
    late_chains = all_chains[per_key_block:]

    def scores_into(slot, j, chains):
        kj = k_ref[0, pl.ds(pl.multiple_of(j * t, t), t), :]
        for c in chains:
            back = first_block + c // per_key_block - j
            near = (0, 2) if c % per_key_block == 0 else (1, 3)
            kind = jnp.where(back == 0, near[0], jnp.where(back == 1, near[1], 3))
            s_ref[slot, c] = jnp.dot(kj, qt_ref[c], preferred_element_type=F32) + bias_ref[0, kind]

    def values_from(slot, j, alphas, chains):
        vj = v_ref[0, pl.ds(pl.multiple_of(j * t, t), t), :]
        for c in chains:
            acc_ref[c] = alphas[c] * acc_ref[c] + _dot_tn(vj, p_ref[slot, c])

    def sub_step(j, cur, carry, next_chains, chains):
        if next_chains:
            scores_into(1 - cur, j + 1, next_chains)
        new = list(carry)
        for c in chains:
            m_old, l_old, _ = carry[c]
            s = s_ref[cur, c]
            m_new = jnp.maximum(m_old, jnp.max(s, axis=0, keepdims=True))
            p = jnp.exp2(s - m_new)
            new[c] = (m_new, jnp.exp2(m_old - m_new) * l_old + jnp.sum(p, axis=0, keepdims=True),
                      jnp.exp2(m_old - m_new))
            p_ref[cur, c] = p.astype(BF16)
        values_from(1 - cur, jnp.maximum(j - 1, 0), [a for _, _, a in carry], all_chains)
        return tuple(new)

    def pair(jj, carry):
        carry = sub_step(2 * jj, 0, carry, all_chains, all_chains)
        return sub_step(2 * jj + 1, 1, carry, all_chains, all_chains)

    scores_into(0, 0, all_chains)
    p_ref[1] = jnp.zeros(p_ref.shape[1:], BF16)
    acc_ref[...] = jnp.zeros_like(acc_ref)
    init = tuple((jnp.full((1, 2 * A_SUB), NEG, F32), jnp.zeros((1, 2 * A_SUB), F32),
                  jnp.ones((1, 2 * A_SUB), F32)) for _ in range(nch))
    fin = lax.fori_loop(0, n_blocks // 2 - 1, pair, init)
    fin = sub_step(n_blocks - 2, 0, fin, late_chains, all_chains)
    fin = sub_step(n_blocks - 1, 1, fin, (), late_chains)
    values_from(1, n_blocks - 1, [a for _, _, a in fin], late_chains)
    lp = lam_ref[...]
    lam = (jnp.exp(jnp.sum(lp[0:1] * lp[1:2], axis=-1, keepdims=True))
           - jnp.exp(jnp.sum(lp[2:3] * lp[3:4], axis=-1, keepdims=True)) + lam_init)
    for c, (_, l_fin, _) in enumerate(fin):
        o_all = acc_ref[c] / l_fin
        o = (o_all[:, :A_SUB] - lam * o_all[:, A_SUB:]).T
        o = o * lax.rsqrt(jnp.mean(o * o, axis=-1, keepdims=True) + LN_EPS) * g_ref[...]
        o_ref[0, c * A_SUB:(c + 1) * A_SUB, :] = (o * (1.0 - lam_init)).astype(BF16)


def _attn_a(proj, bias_a, lam_params, subln_g, lam_init):
    bsz, s, _ = proj.shape
    t = A_QROWS
    nch = A_QROWS // A_SUB
    kb, vb = OFF_AK // LANES, OFF_AV // LANES
    return pl.pallas_call(
        functools.partial(_attn_a_kernel, lam_init=lam_init),
        grid=(bsz, A_HEADS, s // t),
        in_specs=[pl.BlockSpec((1, t, LANES), lambda b, h, i: (b, i, h)),
                  pl.BlockSpec((1, s, LANES), lambda b, h, i: (b, 0, kb + h)),
                  pl.BlockSpec((1, s, LANES), lambda b, h, i: (b, 0, vb + h)),
                  pl.BlockSpec((1, A_BIAS_TYPES, A_BLOCK, 2 * A_SUB), lambda b, h, i: (h, 0, 0, 0)),
                  pl.BlockSpec((4, A_QK_DIM), lambda b, h, i: (0, 0)),
                  pl.BlockSpec((1, A_V_DIM), lambda b, h, i: (0, 0))],
        out_specs=pl.BlockSpec((1, t, LANES), lambda b, h, i: (b, i, h)),
        out_shape=jax.ShapeDtypeStruct((bsz, s, A_WIDTH), BF16),
        scratch_shapes=[pltpu.VMEM((nch, LANES, 2 * A_SUB), BF16),
                        pltpu.VMEM((2, nch, A_BLOCK, 2 * A_SUB), F32),
                        pltpu.VMEM((2, nch, A_BLOCK, 2 * A_SUB), BF16),
                        pltpu.VMEM((nch, A_V_DIM, 2 * A_SUB), F32)],
        compiler_params=_cparams(("parallel", "parallel", "arbitrary")),
        name="attn_a",
    )(proj, proj, proj, bias_a, lam_params, subln_g)


def _swa_kernel(q_ref, kvp_ref, kvc_ref, bias_ref, sink_ref, o_ref):
    w = B_WINDOW
    d = B_HEAD_DIM
    group = B_HEADS // B_KV_HEADS
    table = jnp.minimum(pl.program_id(1), 1)
    row = lax.broadcasted_iota(jnp.int32, (LANES, w), 0)
    zeros = jnp.zeros((LANES - d, w), F32)
    kvs, q_ts = [], []
    for g in range(B_KV_HEADS):
        kvs.append(jnp.concatenate([kvp_ref[0, :, g * LANES:(g + 1) * LANES],
                                    kvc_ref[0, :, g * LANES:(g + 1) * LANES]], axis=0))
        cols = []
        for pair in range(group // 2):
            blk = g * (group // 2) + pair
            t = (q_ref[0, :, blk * LANES:(blk + 1) * LANES].astype(F32) * (d ** -0.5 * LOG2E)).T
            cols += [jnp.where(row < d, t, 0.0), jnp.concatenate([t[d:], zeros], axis=0)]
        q_ts.append(jnp.concatenate(cols, axis=1).astype(BF16))
    scores = [jnp.dot(kv, q_t, preferred_element_type=F32) + bias_ref[g, table]
              for g, (kv, q_t) in enumerate(zip(kvs, q_ts))]
    soft = []
    for g, s in enumerate(scores):
        sink = sink_ref[g] * LOG2E
        m = jnp.maximum(jnp.max(s, axis=0, keepdims=True), sink)
        e = jnp.exp2(s - m)
        soft.append((e.astype(BF16), 1.0 / (jnp.sum(e, axis=0, keepdims=True) + jnp.exp2(sink - m))))
    outs = [_dot_tn(kv, e) * inv for kv, (e, inv) in zip(kvs, soft)]
    for g, o_t in enumerate(outs):
        for pair in range(group // 2):
            blk = g * (group // 2) + pair
            both = jnp.concatenate([o_t[d:, (2 * pair) * w:(2 * pair + 1) * w],
                                    o_t[d:, (2 * pair + 1) * w:(2 * pair + 2) * w]], axis=0)
            o_ref[0, :, blk * LANES:(blk + 1) * LANES] = both.T.astype(BF16)


def _ret_kernel(q_ref, k_ref, v_ref, g_ref, sin_ref, cos_ref, decay_ref, zeta_ref, xi_ref, gch_ref, o_ref,
                state_ref):
    c = C_CHUNK

    @pl.when(pl.program_id(1) == 0)
    def _():
        state_ref[...] = jnp.zeros_like(state_ref)

    lane = lax.broadcasted_iota(jnp.int32, (c, LANES), 1)
    even = (lane & 1) == 0
    sn = sin_ref[...]
    cs = cos_ref[...]

    def rope(x):
        swapped = jnp.where(even, pltpu.roll(x, LANES - 1, 1), pltpu.roll(x, 1, 1))
        return x * cs + swapped * sn

    qk = []
    for pair in range(C_HEADS // 2):
        q = rope(q_ref[0, :, pair * LANES:(pair + 1) * LANES].astype(F32))
        k = rope(k_ref[0, :, pair * LANES:(pair + 1) * LANES].astype(F32) * (C_QK_DIM ** -0.5))
        qk.append((q, q * xi_ref[pair], k.astype(BF16), (k * zeta_ref[pair]).astype(BF16)))
    masked = []
    for h in range(C_HEADS):
        q, qx, _, _ = qk[h // 2]
        in_head = (lane < C_QK_DIM) if h % 2 == 0 else (lane >= C_QK_DIM)
        masked.append((jnp.where(in_head, q, 0.0).astype(BF16), jnp.where(in_head, qx, 0.0).astype(BF16)))
    inner = [(_dot_nt(masked[h][0], qk[h // 2][2]) * decay_ref[h]).astype(BF16) for h in range(C_HEADS)]
    outs = []
    for h in range(C_HEADS):
        vh = v_ref[0, :, h * LANES:(h + 1) * LANES]
        st = state_ref[h]
        outs.append(jnp.dot(inner[h], vh, preferred_element_type=F32)
                    + jnp.dot(masked[h][1], st.astype(BF16), preferred_element_type=F32))
        state_ref[h] = st * gch_ref[h] + _dot_tn(qk[h // 2][3], vh)
    for h, o in enumerate(outs):
        mu = jnp.mean(o, axis=-1, keepdims=True)
        oc = o - mu
        o = oc * lax.rsqrt(jnp.mean(oc * oc, axis=-1, keepdims=True) + LN_EPS)
        gate = g_ref[0, :, h * LANES:(h + 1) * LANES].astype(F32)
        o_ref[0, :, h * LANES:(h + 1) * LANES] = (_silu(gate) * o).astype(BF16)


N_SWA_INPUTS = 5


def _swa_ret_kernel(*refs):
    n_in = N_SWA_INPUTS + 10
    yb_ref, yc_ref, state_ref = refs[n_in:]
    _swa_kernel(*refs[:N_SWA_INPUTS], yb_ref)
    _ret_kernel(*refs[N_SWA_INPUTS:n_in], yc_ref, state_ref)


def _swa_retention(proj, bias_b, sink_rows, sin_t, cos_t, decay, zeta_t, xi_t, gch):
    assert B_WINDOW == C_CHUNK
    bsz, s, _ = proj.shape
    c = C_CHUNK
    qw = C_HEADS * C_QK_DIM
    kvw = 2 * B_KV_HEADS * B_HEAD_DIM
    qb, kvb = OFF_BQ // B_WIDTH, OFF_BKV // kvw
    cols = (B_HEADS // B_KV_HEADS) * c
    return pl.pallas_call(
        _swa_ret_kernel,
        grid=(bsz, s // c),
        in_specs=[pl.BlockSpec((1, c, B_WIDTH), lambda b, n: (b, n, qb)),
                  pl.BlockSpec((1, c, kvw), lambda b, n: (b, jnp.maximum(n - 1, 0), kvb)),
                  pl.BlockSpec((1, c, kvw), lambda b, n: (b, n, kvb)),
                  pl.BlockSpec((B_KV_HEADS, 2, 2 * c, cols), lambda b, n: (0, 0, 0, 0)),
                  pl.BlockSpec((B_KV_HEADS, 1, cols), lambda b, n: (0, 0, 0)),
                  pl.BlockSpec((1, c, qw), lambda b, n: (b, n, OFF_CQ // qw)),
                  pl.BlockSpec((1, c, qw), lambda b, n: (b, n, OFF_CK // qw)),
                  pl.BlockSpec((1, c, C_WIDTH), lambda b, n: (b, n, OFF_CV // C_WIDTH)),
                  pl.BlockSpec((1, c, C_WIDTH), lambda b, n: (b, n, OFF_CG // C_WIDTH)),
                  pl.BlockSpec((c, LANES), lambda b, n: (n, 0)),
                  pl.BlockSpec((c, LANES), lambda b, n: (n, 0)),
                  pl.BlockSpec((C_HEADS, c, c), lambda b, n: (0, 0, 0)),
                  pl.BlockSpec((C_HEADS // 2, c, LANES), lambda b, n: (0, 0, 0)),
                  pl.BlockSpec((C_HEADS // 2, c, LANES), lambda b, n: (0, 0, 0)),
                  pl.BlockSpec((C_HEADS, 1, LANES), lambda b, n: (0, 0, 0))],
        out_specs=[pl.BlockSpec((1, c, B_WIDTH), lambda b, n: (b, n, 0)),
                   pl.BlockSpec((1, c, C_WIDTH), lambda b, n: (b, n, 0))],
        out_shape=[jax.ShapeDtypeStruct((bsz, s, B_WIDTH), BF16), jax.ShapeDtypeStruct((bsz, s, C_WIDTH), BF16)],
        scratch_shapes=[pltpu.VMEM((C_HEADS, LANES, C_V_DIM), F32)],
        compiler_params=_cparams(("parallel", "arbitrary")),
        name="swa_retention",
    )(proj, proj, proj, bias_b, sink_rows, proj, proj, proj, proj, sin_t, cos_t, decay, zeta_t, xi_t, gch)


def _cast_riders(riders_in, riders_out):
    for src, dst in zip(riders_in, riders_out):
        dst[...] = src[...].astype(BF16)


def _row_split_riders(riders, steps):
    views = [r.reshape(r.size // r.shape[-1], r.shape[-1]) for r in riders]
    for v in views:
        assert v.shape[0] % (steps * 16) == 0, v.shape
    return views, [(v.shape[0] // steps, v.shape[1]) for v in views]


def _out_proj_kernel(ya_ref, yb_ref, yc_ref, wa_ref, wb_ref, wc_ref, x_ref, g_ref, b_ref, *rest):
    n_riders = (len(rest) - 1) // 2
    o_ref = rest[n_riders]
    half = o_ref.shape[0] // 2
    spans = [slice(0, half), slice(half, 2 * half)]
    mixes = [jnp.dot(ya_ref[r, :], wa_ref[...], preferred_element_type=F32)
             + jnp.dot(yb_ref[r, :], wb_ref[...], preferred_element_type=F32)
             + jnp.dot(yc_ref[r, :], wc_ref[...], preferred_element_type=F32) for r in spans]
    for r, mix in zip(spans, mixes):
        o_ref[r, :] = _layer_norm(ALPHA * x_ref[r, :] + mix, g_ref[...], b_ref[...])
    _cast_riders(rest[:n_riders], rest[n_riders + 1:])


def _out_proj_ln(ya, yb, yc, w, x, g, b, riders=()):
    n, d = x.shape
    tm = min(512, n)
    row = lambda i: (i, 0)
    fixed = lambda i: (0, 0)
    once = pl.Buffered(1)
    views, blocks = _row_split_riders(riders, n // tm)
    rider_specs = [pl.BlockSpec(blk, row) for blk in blocks]
    outs = pl.pallas_call(
        _out_proj_kernel,
        grid=(n // tm,),
        in_specs=[pl.BlockSpec((tm, A_WIDTH), row),
                  pl.BlockSpec((tm, B_WIDTH), row),
                  pl.BlockSpec((tm, C_WIDTH), row),
                  pl.BlockSpec((A_WIDTH, d), lambda i: (0, 0), pipeline_mode=once),
                  pl.BlockSpec((B_WIDTH, d), lambda i: (1, 0), pipeline_mode=once),
                  pl.BlockSpec((C_WIDTH, d), lambda i: ((A_WIDTH + B_WIDTH) // C_WIDTH, 0), pipeline_mode=once),
                  pl.BlockSpec((tm, d), row),
                  pl.BlockSpec((1, d), fixed),
                  pl.BlockSpec((1, d), fixed)] + rider_specs,
        out_specs=[pl.BlockSpec((tm, d), row)] + rider_specs,
        out_shape=[jax.ShapeDtypeStruct((n, d), F32)] + [jax.ShapeDtypeStruct(v.shape, BF16) for v in views],
        compiler_params=_cparams(("arbitrary",)),
        name="out_proj_ln",
    )(ya, yb, yc, w, w, w, x, g, b, *views)
    return outs[0], [o.reshape(r.shape) for o, r in zip(outs[1:], riders)]


FFN_TM = 512
FFN_TF = 512


def _swiglu_step(xb, wg, wu, wd):
    hg = jnp.dot(xb, wg, preferred_element_type=F32)
    hu = jnp.dot(xb, wu, preferred_element_type=F32)
    return jnp.dot((_silu(hg) * hu).astype(BF16), wd, preferred_element_type=F32)


def _ffn_kernel(x_ref, wg_ref, wu_ref, wd_ref, g_ref, b_ref, *rest):
    n_riders = (len(rest) - 2) // 2
    riders_in, o_ref = rest[:n_riders], rest[n_riders]
    riders_out, xb_ref = rest[n_riders + 1:-1], rest[-1]
    f = pl.program_id(1)

    @pl.when(f == 0)
    def _():
        xb = x_ref[...].astype(BF16)
        xb_ref[...] = xb
        o_ref[...] = _swiglu_step(xb, wg_ref[...], wu_ref[...], wd_ref[...])
        _cast_riders(riders_in, riders_out)

    @pl.when(f > 0)
    def _():
        o_ref[...] += _swiglu_step(xb_ref[...], wg_ref[...], wu_ref[...], wd_ref[...])
        _cast_riders(riders_in, riders_out)

    @pl.when(f == pl.num_programs(1) - 1)
    def _():
        o_ref[...] = _layer_norm(ALPHA * x_ref[...] + o_ref[...], g_ref[...], b_ref[...])


def _ffn(x, w_gate, w_up, w_down, g, b, riders=()):
    n, d = x.shape
    ff = w_gate.shape[1]
    tm = min(FFN_TM, n)
    tf = FFN_TF
    ni, nf = n // tm, ff // tf
    flat, rider_specs = [], []
    for r in riders:
        cols = r.shape[-1]
        rows = r.size // cols
        flat.append(r.reshape(rows, cols))
        if rows % (ni * nf * 16) == 0:
            rider_specs.append(pl.BlockSpec((rows // (ni * nf), cols), lambda i, f: (i * nf + f, 0)))
        else:
            assert rows % (ni * 16) == 0 and cols % (nf * LANES) == 0
            rider_specs.append(pl.BlockSpec((rows // ni, cols // nf), lambda i, f: (i, f)))
    outs = pl.pallas_call(
        _ffn_kernel,
        grid=(n // tm, ff // tf),
        in_specs=[pl.BlockSpec((tm, d), lambda i, f: (i, 0)),
                  pl.BlockSpec((d, tf), lambda i, f: (0, f)),
                  pl.BlockSpec((d, tf), lambda i, f: (0, f)),
                  pl.BlockSpec((tf, d), lambda i, f: (f, 0)),
                  pl.BlockSpec((1, d), lambda i, f: (0, 0)),
                  pl.BlockSpec((1, d), lambda i, f: (0, 0))] + rider_specs,
        out_specs=[pl.BlockSpec((tm, d), lambda i, f: (i, 0))] + rider_specs,
        out_shape=[jax.ShapeDtypeStruct((n, d), F32)] + [jax.ShapeDtypeStruct(r.shape, BF16) for r in flat],
        scratch_shapes=[pltpu.VMEM((tm, d), BF16)],
        compiler_params=_cparams(("arbitrary", "arbitrary")),
        name="ffn_ln",
    )(x, w_gate, w_up, w_down, g, b, *flat)
    return outs[0], [o.reshape(r.shape) for o, r in zip(outs[1:], riders)]


def _row_copy(src, dst, sem, s, t):
    return pltpu.make_async_copy(src.at[pl.ds(s, 1)], dst.at[pl.ds(t, 1)], sem)


def _ffn_grouped_kernel(eid_ref, nact_ref, src_ref, x_hbm, wg_ref, wu_ref, wd_ref, o_ref, rows_ref, xb_ref, sem,
                        *, tm, per_step):
    i = pl.program_id(0)
    f = pl.program_id(1)
    nact = nact_ref[0]
    active = i < nact
    slot = lax.rem(i, 2)
    buf_rows = rows_ref.shape[1]
    last_slot_row = src_ref.shape[0] - 1

    def fetch(tile, into, first_row, count):
        for r in range(count):
            row = first_row + r
            token = src_ref[jnp.minimum(tile * tm + row, last_slot_row)]
            _row_copy(x_hbm, rows_ref.at[into], sem.at[into], token, row).start()

    def wait_buffer(which):
        pltpu.make_async_copy(x_hbm.at[pl.ds(0, buf_rows)], rows_ref.at[which], sem.at[which]).wait()

    @pl.when(jnp.logical_and(i == 0, f == 0))
    def _():
        lax.fori_loop(0, buf_rows // per_step, lambda s, c: (fetch(0, 0, s * per_step, per_step), c)[1], 0)

    @pl.when(jnp.logical_and(f == 0, i <= nact))
    def _():
        wait_buffer(slot)

    @pl.when(jnp.logical_and(f == 0, jnp.logical_not(active)))
    def _():
        o_ref[...] = jnp.zeros_like(o_ref)

    @pl.when(jnp.logical_and(f == 0, active))
    def _():
        fetch(i + 1, 1 - slot, 0, per_step)
        xb = rows_ref[slot, :tm, :].astype(BF16)
        xb_ref[...] = xb
        o_ref[...] = _swiglu_step(xb, wg_ref[0], wu_ref[0], wd_ref[0])

    @pl.when(jnp.logical_and(f > 0, active))
    def _():
        fetch(i + 1, 1 - slot, f * per_step, per_step)
        o_ref[...] += _swiglu_step(xb_ref[...], wg_ref[0], wu_ref[0], wd_ref[0])


def _ffn_grouped(x, src, eid, nact, w_gate, w_up, w_down):
    n, d = x.shape
    slots = src.shape[0]
    ff = w_gate.shape[2]
    tm = min(FFN_TM, n)
    tf = FFN_TF
    nf = ff // tf
    per_step = -(-tm // nf)
    per_step += (-per_step) % 8
    assert per_step * nf <= n

    def fidx(i, f, nact_ref):
        return jnp.where(i < nact_ref[0], f, nf - 1)

    grid_spec = pltpu.PrefetchScalarGridSpec(
        num_scalar_prefetch=3,
        grid=(slots // tm, nf),
        in_specs=[pl.BlockSpec(memory_space=pl.ANY),
                  pl.BlockSpec((1, d, tf), lambda i, f, e, a, s: (e[i], 0, fidx(i, f, a))),
                  pl.BlockSpec((1, d, tf), lambda i, f, e, a, s: (e[i], 0, fidx(i, f, a))),
                  pl.BlockSpec((1, tf, d), lambda i, f, e, a, s: (e[i], fidx(i, f, a), 0))],
        out_specs=pl.BlockSpec((tm, d), lambda i, f, e, a, s: (i, 0)),
        scratch_shapes=[pltpu.VMEM((2, per_step * nf, d), F32), pltpu.VMEM((tm, d), BF16),
                        pltpu.SemaphoreType.DMA((2,))],
    )
    return pl.pallas_call(
        functools.partial(_ffn_grouped_kernel, tm=tm, per_step=per_step),
        grid_spec=grid_spec,
        out_shape=jax.ShapeDtypeStruct((slots, d), F32),
        compiler_params=_cparams(("arbitrary", "arbitrary")),
        name="ffn_grouped",
    )(eid, nact, src, x, w_gate, w_up, w_down)


ROUTER_TM = 512


def _router_kernel(x_ref, wr_ref, tri_ref, meta_ref, cnt_ref, carry_ref):
    @pl.when(pl.program_id(0) == 0)
    def _():
        carry_ref[...] = jnp.zeros_like(carry_ref)

    x = x_ref[...]
    x_hi = x.astype(BF16)
    x_lo = (x - x_hi.astype(F32)).astype(BF16)
    parts = (jnp.dot(x_hi, wr_ref[...], preferred_element_type=F32)
             + jnp.dot(x_lo, wr_ref[...], preferred_element_type=F32))
    logits = parts + pltpu.roll(parts, LANES - N_EXPERTS, 1)
    lane = lax.broadcasted_iota(jnp.int32, logits.shape, 1)
    logits = jnp.where(lane < N_EXPERTS, logits, -jnp.inf)
    m1 = jnp.max(logits, axis=-1, keepdims=True)
    i1 = jnp.min(jnp.where(logits == m1, lane, LANES), axis=-1, keepdims=True)
    rest = jnp.where(lane == i1, -jnp.inf, logits)
    m2 = jnp.max(rest, axis=-1, keepdims=True)
    i2 = jnp.min(jnp.where(rest == m2, lane, LANES), axis=-1, keepdims=True)
    e2 = jnp.exp(m2 - m1)
    w1 = 1.0 / (1.0 + e2)
    w2 = e2 * w1
    hit1 = lane == i1
    hit2 = lane == i2
    onehot = jnp.where(jnp.logical_or(hit1, hit2), 1.0, 0.0)
    before = jnp.dot(tri_ref[...], onehot.astype(BF16), preferred_element_type=F32) + carry_ref[...]
    r1 = jnp.sum(jnp.where(hit1, before, 0.0), axis=-1, keepdims=True)
    r2 = jnp.sum(jnp.where(hit2, before, 0.0), axis=-1, keepdims=True)
    carry_ref[...] = carry_ref[...] + jnp.sum(onehot, axis=0, keepdims=True)
    cnt_ref[...] = carry_ref[...]
    meta = jnp.where(lane == 0, i1.astype(F32), 0.0)
    meta = jnp.where(lane == 1, i2.astype(F32), meta)
    meta = jnp.where(lane == 2, r1, meta)
    meta = jnp.where(lane == 3, r2, meta)
    meta = jnp.where(lane == 4, w1, meta)
    meta = jnp.where(lane == 5, w2, meta)
    meta_ref[...] = meta


def _router(x, w_router):
    n, d = x.shape
    tm = min(ROUTER_TM, n)
    w_hi = w_router.astype(BF16)
    w_lo = (w_router.astype(F32) - w_hi.astype(F32)).astype(BF16)
    wr = jnp.zeros((d, LANES), BF16).at[:, :N_EXPERTS].set(w_hi).at[:, N_EXPERTS:2 * N_EXPERTS].set(w_lo)
    tri = jnp.asarray(np.tril(np.ones((tm, tm), np.float32), -1), BF16)
    return pl.pallas_call(
        _router_kernel,
        grid=(n // tm,),
        in_specs=[pl.BlockSpec((tm, d), lambda i: (i, 0)),
                  pl.BlockSpec((d, LANES), lambda i: (0, 0)),
                  pl.BlockSpec((tm, tm), lambda i: (0, 0))],
        out_specs=[pl.BlockSpec((tm, LANES), lambda i: (i, 0)),
                   pl.BlockSpec((1, LANES), lambda i: (0, 0))],
        out_shape=[jax.ShapeDtypeStruct((n, LANES), F32), jax.ShapeDtypeStruct((1, LANES), F32)],
        scratch_shapes=[pltpu.VMEM((1, LANES), F32)],
        compiler_params=_cparams(("arbitrary",)),
        name="router",
    )(x, wr, tri)


MOVE_TM = 512
ISSUE_UNROLL = 8


def _combine_kernel(d1_ref, d2_ref, ys_hbm, x_ref, meta_ref, g_ref, b_ref, o_ref, buf_ref, sem, *, tm):
    base = pl.program_id(0) * tm

    def issue(t, carry):
        row = base + t
        _row_copy(ys_hbm, buf_ref.at[0], sem, d1_ref[row], t).start()
        _row_copy(ys_hbm, buf_ref.at[1], sem, d2_ref[row], t).start()
        return carry

    lax.fori_loop(0, tm, issue, 0, unroll=ISSUE_UNROLL)

    def drain(t, carry):
        _row_copy(ys_hbm, buf_ref.at[0], sem, 0, 0).wait()
        _row_copy(ys_hbm, buf_ref.at[1], sem, 0, 0).wait()
        return carry

    lax.fori_loop(0, tm, drain, 0, unroll=ISSUE_UNROLL)
    meta = meta_ref[...]
    lane = lax.broadcasted_iota(jnp.int32, meta.shape, 1)
    w1 = jnp.sum(jnp.where(lane == 4, meta, 0.0), axis=-1, keepdims=True)
    w2 = jnp.sum(jnp.where(lane == 5, meta, 0.0), axis=-1, keepdims=True)
    f = w1 * buf_ref[0] + w2 * buf_ref[1]
    o_ref[...] = _layer_norm(ALPHA * x_ref[...] + f, g_ref[...], b_ref[...])


def _combine(ys, x, meta, d1, d2, g, b):
    n, d = x.shape
    tm = min(MOVE_TM, n)
    grid_spec = pltpu.PrefetchScalarGridSpec(
        num_scalar_prefetch=2,
        grid=(n // tm,),
        in_specs=[pl.BlockSpec(memory_space=pl.ANY),
                  pl.BlockSpec((tm, d), lambda i, a, c: (i, 0)),
                  pl.BlockSpec((tm, LANES), lambda i, a, c: (i, 0)),
                  pl.BlockSpec((1, d), lambda i, a, c: (0, 0)),
                  pl.BlockSpec((1, d), lambda i, a, c: (0, 0))],
        out_specs=pl.BlockSpec((tm, d), lambda i, a, c: (i, 0)),
        scratch_shapes=[pltpu.VMEM((2, tm, d), F32), pltpu.SemaphoreType.DMA(())],
    )
    return pl.pallas_call(
        functools.partial(_combine_kernel, tm=tm),
        grid_spec=grid_spec,
        out_shape=jax.ShapeDtypeStruct((n, d), F32),
        compiler_params=_cparams(("arbitrary",)),
        name="moe_combine",
    )(d1, d2, ys, x, meta, g, b)


def _moe(x, w_router, w_gate, w_up, w_down, g, b):
    n, d = x.shape
    tm = min(FFN_TM, n)
    meta, cnt = _router(x, w_router)
    i1 = meta[:, 0].astype(jnp.int32)
    i2 = meta[:, 1].astype(jnp.int32)
    counts = cnt[0, :N_EXPERTS].astype(jnp.int32)
    tiles = (counts + tm - 1) // tm
    tile_end = jnp.cumsum(tiles)
    group_start = (tile_end - tiles) * tm
    experts = jnp.arange(N_EXPERTS, dtype=jnp.int32)[None, :]
    start_of = lambda idx: jnp.sum(jnp.where(idx[:, None] == experts, group_start[None, :], 0), axis=1)
    d1 = start_of(i1) + meta[:, 2].astype(jnp.int32)
    d2 = start_of(i2) + meta[:, 3].astype(jnp.int32)
    max_tiles = (2 * n) // tm + N_EXPERTS
    tile_ids = jnp.arange(max_tiles, dtype=jnp.int32)[:, None]
    eid = jnp.minimum(jnp.sum((tile_ids >= tile_end[None, :]).astype(jnp.int32), axis=1), N_EXPERTS - 1)
    nact = tile_end[-1:].astype(jnp.int32)
    token = jnp.arange(n, dtype=jnp.int32)
    src = jnp.zeros((max_tiles * tm,), jnp.int32).at[jnp.concatenate([d1, d2])].set(
        jnp.concatenate([token, token]), unique_indices=True)
    ys = _ffn_grouped(x, src, eid.astype(jnp.int32), nact, w_gate, w_up, w_down)
    return _combine(ys, x, meta, d1, d2, g, b)


def _lambda_init(layer_idx):
    return 0.8 - 0.6 * math.exp(-0.3 * layer_idx)


def _static_tables(s):
    c = C_CHUNK
    ang = jnp.repeat(1.0 / (10000.0 ** jnp.linspace(0.0, 1.0, C_QK_DIM // 2, dtype=F32)), 2)
    ang = jnp.arange(s, dtype=F32)[:, None] * ang[None, :]
    sign = jnp.where(jnp.arange(C_QK_DIM) % 2 == 0, -1.0, 1.0).astype(F32)
    sin_t = jnp.tile(jnp.sin(ang) * sign[None, :], (1, 2))
    cos_t = jnp.tile(jnp.cos(ang), (1, 2))
    log_g = jnp.log(1.0 - jnp.exp2(-5.0 - jnp.arange(C_HEADS, dtype=F32)))
    pos = jnp.arange(c)
    rel = (pos[:, None] - pos[None, :]).astype(F32)
    decay = jnp.where((rel >= 0)[None], jnp.exp(jnp.maximum(rel, 0.0)[None] * log_g[:, None, None]), 0.0)
    zeta = jnp.exp((c - 1 - pos).astype(F32)[:, None] * log_g[None, :])
    xi = jnp.exp((pos + 1).astype(F32)[:, None] * log_g[None, :])
    per_pair = lambda t: jnp.repeat(t.T.reshape(C_HEADS // 2, 2, c), C_QK_DIM, axis=1).transpose(0, 2, 1)
    gch = jnp.broadcast_to(jnp.exp(c * log_g)[:, None, None], (C_HEADS, 1, LANES))
    dist_a = np.arange(A_SUB)[None, :] - np.arange(A_BLOCK)[:, None]
    types = [jnp.where(dist_a + off >= 0, _rel_bucket(jnp.asarray(dist_a + off)), REL_BUCKETS)
             for off in (0, A_SUB, A_BLOCK)]
    types.append(jnp.full(dist_a.shape, REL_BUCKETS - 1, jnp.int32))
    bkt_a = jnp.concatenate([jnp.tile(b, (1, 2)) for b in types], axis=0).astype(jnp.int32)
    w = B_WINDOW
    dist = np.arange(w)[:, None] + w - np.arange(2 * w)[None, :]
    band = (dist >= 0) & (dist < w)
    has_prev = np.stack([np.broadcast_to(np.arange(2 * w)[None, :] >= w, band.shape), np.ones_like(band)])
    bkt_b = jnp.where(band[None] & has_prev, _rel_bucket(jnp.asarray(dist))[None], REL_BUCKETS)
    bkt_b = bkt_b.transpose(0, 2, 1).reshape(2 * 2 * w, w).astype(jnp.int32)
    return sin_t, cos_t, decay, per_pair(zeta), per_pair(xi), gch, bkt_a, bkt_b


def kernel(x, w_in, rel_bias, a_lambda, a_subln_g, b_sinks, w_out, ln_mix_g, ln_mix_b, ln_ffn_g, ln_ffn_b,
           dense_w_gate, dense_w_up, dense_w_down, moe_router, moe_w_gate, moe_w_up, moe_w_down):
    bsz, s, d = x.shape
    n = bsz * s
    sin_t, cos_t, decay, zeta_t, xi_t, gch, bkt_a, bkt_b = _static_tables(s)
    tab_t = rel_bias.astype(F32).T
    bias_a = _bias_lookup(tab_t[:A_HEADS] * LOG2E, bkt_a).reshape(A_HEADS, A_BIAS_TYPES, A_BLOCK, 2 * A_SUB)
    group = B_HEADS // B_KV_HEADS
    bias_b = _bias_lookup(tab_t[A_HEADS:] * LOG2E, bkt_b)
    bias_b = bias_b.reshape(B_KV_HEADS, group, 2, 2 * B_WINDOW, B_WINDOW).transpose(0, 2, 3, 1, 4)
    bias_b = bias_b.reshape(B_KV_HEADS, 2, 2 * B_WINDOW, group * B_WINDOW)
    xf = x.reshape(n, d).astype(F32)
    w_in_bf16, w_out_bf16 = w_in[0].astype(BF16), w_out[0].astype(BF16)
    for l in range(DEPTH):
        w_in_l = jnp.concatenate([w_in_bf16[:, a:b] for a, b in _PERM_RUNS]
                                 + [jnp.zeros((d, PROJ_PAD - PROJ_WIDTH), BF16)], axis=1)
        proj = _in_proj(xf, w_in_l).reshape(bsz, s, PROJ_PAD)
        ya = _attn_a(proj, bias_a, a_lambda[l].astype(F32), a_subln_g[l].astype(F32).reshape(1, A_V_DIM),
                     _lambda_init(l))
        sink_rows = jnp.repeat(b_sinks[l].astype(F32).reshape(B_KV_HEADS, 1, group), B_WINDOW, axis=2)
        yb, yc = _swa_retention(proj, bias_b, sink_rows, sin_t, cos_t, decay, zeta_t, xi_t, gch)
        g_mix = ln_mix_g[l].astype(F32).reshape(1, d)
        b_mix = ln_mix_b[l].astype(F32).reshape(1, d)
        j = l // 2
        dense = (dense_w_gate[j], dense_w_up[j], dense_w_down[j]) if l % 2 == 0 else ()
        ahead = (w_in[l + 1], w_out[l + 1]) if l + 1 < DEPTH else ()
        xf, cast = _out_proj_ln(ya.reshape(n, A_WIDTH), yb.reshape(n, B_WIDTH), yc.reshape(n, C_WIDTH),
                                w_out_bf16, xf, g_mix, b_mix, dense + ahead)
        dense_bf16 = cast[:len(dense)]
        if ahead:
            w_in_bf16, w_out_bf16 = cast[len(dense):]
        g_ffn = ln_ffn_g[l].astype(F32).reshape(1, d)
        b_ffn = ln_ffn_b[l].astype(F32).reshape(1, d)
        if l % 2 == 0:
            experts = (moe_w_gate[j], moe_w_up[j], moe_w_down[j]) if l + 1 < DEPTH else ()
            xf, moe_bf16 = _ffn(xf, *dense_bf16, g_ffn, b_ffn, experts)
        else:
            assert l > 0, "expert weights are cast by the preceding dense layer"
            xf = _moe(xf, moe_router[j], *moe_bf16, g_ffn, b_ffn)
    return xf.reshape(bsz, s, d).astype(x.dtype)
```

```python
import functools
import math

import jax
import jax.numpy as jnp
import numpy as np
from jax import lax
from jax.experimental import pallas as pl
from jax.experimental.pallas import tpu as pltpu

F32 = jnp.float32
BF16 = jnp.bfloat16

D_MODEL = 2048
DEPTH = 2
A_HEADS = 6
A_QK_DIM = 64
A_V_DIM = 128
B_HEADS = 12
B_KV_HEADS = 3
B_HEAD_DIM = 64
B_WINDOW = 128
C_HEADS = 4
C_QK_DIM = 64
C_V_DIM = 128
C_CHUNK = 128
A_WIDTH = A_HEADS * A_V_DIM
B_WIDTH = B_HEADS * B_HEAD_DIM
C_WIDTH = C_HEADS * C_V_DIM
REL_BUCKETS = 32
REL_MAX_DIST = 128
D_FF = 5632
N_EXPERTS = 8
ALPHA = (2.0 * DEPTH) ** 0.25
LN_EPS = 1e-5
NEG = -1e30

LANES = 128
VMEM_LIMIT = 56 * 1024 * 1024

_REF_SIZES = [768, 768, 768, 768, 192, 192, 256, 256, 512, 512]
_REF_OFF = [int(v) for v in np.concatenate([[0], np.cumsum(_REF_SIZES)[:-1]])]
PROJ_WIDTH = int(sum(_REF_SIZES))
OFF_AQ, OFF_AK, OFF_AV, OFF_BQ, OFF_CV, OFF_CG, OFF_CQ, OFF_CK, OFF_BKV = (
    0, 768, 1536, 2304, 3072, 3584, 4096, 4352, 4608)


def _proj_perm():
    aq, ak, av, bq, bk, bv, cq, ck, cv, cg = [np.arange(o, o + s) for o, s in zip(_REF_OFF, _REF_SIZES)]
    bkv = np.concatenate([np.concatenate([bk[g * 64:(g + 1) * 64], bv[g * 64:(g + 1) * 64]])
                          for g in range(B_KV_HEADS)])
    perm = np.concatenate([aq, ak, av, bq, cv, cg, cq, ck, bkv])
    assert perm.shape[0] == PROJ_WIDTH
    return perm


def _perm_runs():
    perm = _proj_perm()
    cuts = np.flatnonzero(np.diff(perm) != 1) + 1
    return [(int(r[0]), int(r[-1]) + 1) for r in np.split(perm, cuts)]


_PERM_RUNS = _perm_runs()


def _cparams(sem):
    return pltpu.CompilerParams(dimension_semantics=sem, vmem_limit_bytes=VMEM_LIMIT)


def _layer_norm(z, g, b):
    mu = jnp.mean(z, axis=-1, keepdims=True)
    zc = z - mu
    var = jnp.mean(zc * zc, axis=-1, keepdims=True)
    return zc * lax.rsqrt(var + LN_EPS) * g + b


def _silu(x):
    return x / (1.0 + jnp.exp(-x))


def _dot_nt(a, b):
    return lax.dot_general(a, b, (((1,), (1,)), ((), ())), preferred_element_type=F32)


def _dot_tn(a, b):
    return lax.dot_general(a, b, (((0,), (0,)), ((), ())), preferred_element_type=F32)


def _bias_kernel(tab_ref, bkt_ref, o_ref):
    h = pl.program_id(0)
    bkt = bkt_ref[...]
    acc = jnp.full(bkt.shape, NEG, F32)
    for b in range(REL_BUCKETS):
        acc = jnp.where(bkt == b, tab_ref[h, b], acc)
    o_ref[0] = acc


def _bias_lookup(tab_t, bkt):
    nh = tab_t.shape[0]
    r, c = bkt.shape
    return pl.pallas_call(
        _bias_kernel,
        grid=(nh,),
        in_specs=[pl.BlockSpec(memory_space=pltpu.SMEM),
                  pl.BlockSpec((r, c), lambda h: (0, 0))],
        out_specs=pl.BlockSpec((1, r, c), lambda h: (h, 0, 0)),
        out_shape=jax.ShapeDtypeStruct((nh, r, c), F32),
        compiler_params=_cparams(("arbitrary",)),
        name="bias_lookup",
    )(tab_t, bkt)


def _rel_bucket(dist):
    max_exact = REL_BUCKETS // 2
    d = jnp.maximum(dist, 0)
    ratio = jnp.maximum(d, 1).astype(F32) / max_exact
    large = max_exact + (jnp.log(ratio) / math.log(REL_MAX_DIST / max_exact)
                         * (REL_BUCKETS - max_exact)).astype(jnp.int32)
    large = jnp.minimum(large, REL_BUCKETS - 1)
    return jnp.where(d < max_exact, d, large)


MXU_COLS = 256
IN_PROJ_COL_STEPS = 4
PROJ_PAD = -(-PROJ_WIDTH // (IN_PROJ_COL_STEPS * MXU_COLS)) * (IN_PROJ_COL_STEPS * MXU_COLS)


def _in_proj_kernel(x_ref, w_ref, o_ref, xb_ref):
    @pl.when(pl.program_id(1) == 0)
    def _():
        xb_ref[...] = x_ref[...].astype(BF16)

    o_ref[...] = jnp.dot(xb_ref[...], w_ref[...], preferred_element_type=F32).astype(BF16)


def _in_proj(x, w):
    n, d = x.shape
    p = w.shape[1]
    tm = min(1024, n)
    tn = p // IN_PROJ_COL_STEPS
    return pl.pallas_call(
        _in_proj_kernel,
        grid=(n // tm, p // tn),
        in_specs=[pl.BlockSpec((tm, d), lambda i, j: (i, 0)),
                  pl.BlockSpec((d, tn), lambda i, j: (0, j))],
        out_specs=pl.BlockSpec((tm, tn), lambda i, j: (i, j)),
        out_shape=jax.ShapeDtypeStruct((n, p), BF16),
        scratch_shapes=[pltpu.VMEM((tm, d), BF16)],
        compiler_params=_cparams(("parallel", "arbitrary")),
        name="in_proj",
    )(x, w)


A_BLOCK = 256
A_SUB = 128
A_QROWS = 512
A_BIAS_TYPES = 4
LOG2E = math.log2(math.e)


def _attn_a_kernel(q_ref, k_ref, v_ref, bias_ref, lam_ref, g_ref, o_ref, qt_ref, s_ref, p_ref, acc_ref, *,
                   lam_init):
    t = A_BLOCK
    nch = A_QROWS // A_SUB
    per_key_block = t // A_SUB
    i = pl.program_id(2)
    lane = lax.broadcasted_iota(jnp.int32, (A_SUB, LANES), 1)
    for c in range(nch):
        q = q_ref[0, c * A_SUB:(c + 1) * A_SUB, :].astype(F32) * (A_QK_DIM ** -0.5 * LOG2E)
        qq = jnp.concatenate([jnp.where(lane < A_QK_DIM, q, 0.0), jnp.where(lane >= A_QK_DIM, q, 0.0)], axis=0)
        qt_ref[c] = qq.T.astype(BF16)
    first_block = i * (nch // per_key_block)
    n_blocks = first_block + nch // per_key_block

    all_chains = tuple(range(nch))
    late_chains = all_chains[per_key_block:]

    def scores_into(slot, j, chains):
        kj = k_ref[0, pl.ds(pl.multiple_of(j * t, t), t), :]
        for c in chains:
            back = first_block + c // per_key_block - j
            near = (0, 2) if c % per_key_block == 0 else (1, 3)
            kind = jnp.where(back == 0, near[0], jnp.where(back == 1, near[1], 3))
            s_ref[slot, c] = jnp.dot(kj, qt_ref[c], preferred_element_type=F32) + bias_ref[0, kind]

    def values_from(slot, j, alphas, chains):
        vj = v_ref[0, pl.ds(pl.multiple_of(j * t, t), t), :]
        for c in chains:
            acc_ref[c] = alphas[c] * acc_ref[c] + _dot_tn(vj, p_ref[slot, c])

    def sub_step(j, cur, carry, next_chains, chains):
        if next_chains:
            scores_into(1 - cur, j + 1, next_chains)
        new = list(carry)
        for c in chains:
            m_old, l_old, _ = carry[c]
            s = s_ref[cur, c]
            m_new = jnp.maximum(m_old, jnp.max(s, axis=0, keepdims=True))
            p = jnp.exp2(s - m_new)
            new[c] = (m_new, jnp.exp2(m_old - m_new) * l_old + jnp.sum(p, axis=0, keepdims=True),
                      jnp.exp2(m_old - m_new))
            p_ref[cur, c] = p.astype(BF16)
        values_from(1 - cur, jnp.maximum(j - 1, 0), [a for _, _, a in carry], all_chains)
        return tuple(new)

    def pair(jj, carry):
        carry = sub_step(2 * jj, 0, carry, all_chains, all_chains)
        return sub_step(2 * jj + 1, 1, carry, all_chains, all_chains)

    scores_into(0, 0, all_chains)
    p_ref[1] = jnp.zeros(p_ref.shape[1:], BF16)
    acc_ref[...] = jnp.zeros_like(acc_ref)
    init = tuple((jnp.full((1, 2 * A_SUB), NEG, F32), jnp.zeros((1, 2 * A_SUB), F32),
                  jnp.ones((1, 2 * A_SUB), F32)) for _ in range(nch))
    fin = lax.fori_loop(0, n_blocks // 2 - 1, pair, init)
    fin = sub_step(n_blocks - 2, 0, fin, late_chains, all_chains)
    fin = sub_step(n_blocks - 1, 1, fin, (), late_chains)
    values_from(1, n_blocks - 1, [a for _, _, a in fin], late_chains)
    lp = lam_ref[...]
    lam = (jnp.exp(jnp.sum(lp[0:1] * lp[1:2], axis=-1, keepdims=True))
           - jnp.exp(jnp.sum(lp[2:3] * lp[3:4], axis=-1, keepdims=True)) + lam_init)
    for c, (_, l_fin, _) in enumerate(fin):
        o_all = acc_ref[c] / l_fin
        o = (o_all[:, :A_SUB] - lam * o_all[:, A_SUB:]).T
        o = o * lax.rsqrt(jnp.mean(o * o, axis=-1, keepdims=True) + LN_EPS) * g_ref[...]
        o_ref[0, c * A_SUB:(c + 1) * A_SUB, :] = (o * (1.0 - lam_init)).astype(BF16)


def _attn_a(proj, bias_a, lam_params, subln_g, lam_init):
    bsz, s, _ = proj.shape
    t = A_QROWS
    nch = A_QROWS // A_SUB
    kb, vb = OFF_AK // LANES, OFF_AV // LANES
    return pl.pallas_call(
        functools.partial(_attn_a_kernel, lam_init=lam_init),
        grid=(bsz, A_HEADS, s // t),
        in_specs=[pl.BlockSpec((1, t, LANES), lambda b, h, i: (b, i, h)),
                  pl.BlockSpec((1, s, LANES), lambda b, h, i: (b, 0, kb + h)),
                  pl.BlockSpec((1, s, LANES), lambda b, h, i: (b, 0, vb + h)),
                  pl.BlockSpec((1, A_BIAS_TYPES, A_BLOCK, 2 * A_SUB), lambda b, h, i: (h, 0, 0, 0)),
                  pl.BlockSpec((4, A_QK_DIM), lambda b, h, i: (0, 0)),
                  pl.BlockSpec((1, A_V_DIM), lambda b, h, i: (0, 0))],
        out_specs=pl.BlockSpec((1, t, LANES), lambda b, h, i: (b, i, h)),
        out_shape=jax.ShapeDtypeStruct((bsz, s, A_WIDTH), BF16),
        scratch_shapes=[pltpu.VMEM((nch, LANES, 2 * A_SUB), BF16),
                        pltpu.VMEM((2, nch, A_BLOCK, 2 * A_SUB), F32),
                        pltpu.VMEM((2, nch, A_BLOCK, 2 * A_SUB), BF16),
                        pltpu.VMEM((nch, A_V_DIM, 2 * A_SUB), F32)],
        compiler_params=_cparams(("parallel", "parallel", "arbitrary")),
        name="attn_a",
    )(proj, proj, proj, bias_a, lam_params, subln_g)


def _swa_kernel(q_ref, kvp_ref, kvc_ref, bias_ref, sink_ref, o_ref):
    w = B_WINDOW
    d = B_HEAD_DIM
    group = B_HEADS // B_KV_HEADS
    table = jnp.minimum(pl.program_id(1), 1)
    row = lax.broadcasted_iota(jnp.int32, (LANES, w), 0)
    zeros = jnp.zeros((LANES - d, w), F32)
    kvs, q_ts = [], []
    for g in range(B_KV_HEADS):
        kvs.append(jnp.concatenate([kvp_ref[0, :, g * LANES:(g + 1) * LANES],
                                    kvc_ref[0, :, g * LANES:(g + 1) * LANES]], axis=0))
        cols = []
        for pair in range(group // 2):
            blk = g * (group // 2) + pair
            t = (q_ref[0, :, blk * LANES:(blk + 1) * LANES].astype(F32) * (d ** -0.5 * LOG2E)).T
            cols += [jnp.where(row < d, t, 0.0), jnp.concatenate([t[d:], zeros], axis=0)]
        q_ts.append(jnp.concatenate(cols, axis=1).astype(BF16))
    scores = [jnp.dot(kv, q_t, preferred_element_type=F32) + bias_ref[g, table]
              for g, (kv, q_t) in enumerate(zip(kvs, q_ts))]
    soft = []
    for g, s in enumerate(scores):
        sink = sink_ref[g] * LOG2E
        m = jnp.maximum(jnp.max(s, axis=0, keepdims=True), sink)
        e = jnp.exp2(s - m)
        soft.append((e.astype(BF16), 1.0 / (jnp.sum(e, axis=0, keepdims=True) + jnp.exp2(sink - m))))
    outs = [_dot_tn(kv, e) * inv for kv, (e, inv) in zip(kvs, soft)]
    for g, o_t in enumerate(outs):
        for pair in range(group // 2):
            blk = g * (group // 2) + pair
            both = jnp.concatenate([o_t[d:, (2 * pair) * w:(2 * pair + 1) * w],
                                    o_t[d:, (2 * pair + 1) * w:(2 * pair + 2) * w]], axis=0)
            o_ref[0, :, blk * LANES:(blk + 1) * LANES] = both.T.astype(BF16)


def _ret_kernel(q_ref, k_ref, v_ref, g_ref, sin_ref, cos_ref, decay_ref, zeta_ref, xi_ref, gch_ref, o_ref,
                state_ref):
    c = C_CHUNK

    @pl.when(pl.program_id(1) == 0)
    def _():
        state_ref[...] = jnp.zeros_like(state_ref)

    lane = lax.broadcasted_iota(jnp.int32, (c, LANES), 1)
    even = (lane & 1) == 0
    sn = sin_ref[...]
    cs = cos_ref[...]

    def rope(x):
        swapped = jnp.where(even, pltpu.roll(x, LANES - 1, 1), pltpu.roll(x, 1, 1))
        return x * cs + swapped * sn

    qk = []
    for pair in range(C_HEADS // 2):
        q = rope(q_ref[0, :, pair * LANES:(pair + 1) * LANES].astype(F32))
        k = rope(k_ref[0, :, pair * LANES:(pair + 1) * LANES].astype(F32) * (C_QK_DIM ** -0.5))
        qk.append((q, q * xi_ref[pair], k.astype(BF16), (k * zeta_ref[pair]).astype(BF16)))
    masked = []
    for h in range(C_HEADS):
        q, qx, _, _ = qk[h // 2]
        in_head = (lane < C_QK_DIM) if h % 2 == 0 else (lane >= C_QK_DIM)
        masked.append((jnp.where(in_head, q, 0.0).astype(BF16), jnp.where(in_head, qx, 0.0).astype(BF16)))
    inner = [(_dot_nt(masked[h][0], qk[h // 2][2]) * decay_ref[h]).astype(BF16) for h in range(C_HEADS)]
    outs = []
    for h in range(C_HEADS):
        vh = v_ref[0, :, h * LANES:(h + 1) * LANES]
        st = state_ref[h]
        outs.append(jnp.dot(inner[h], vh, preferred_element_type=F32)
                    + jnp.dot(masked[h][1], st.astype(BF16), preferred_element_type=F32))
        state_ref[h] = st * gch_ref[h] + _dot_tn(qk[h // 2][3], vh)
    for h, o in enumerate(outs):
        mu = jnp.mean(o, axis=-1, keepdims=True)
        oc = o - mu
        o = oc * lax.rsqrt(jnp.mean(oc * oc, axis=-1, keepdims=True) + LN_EPS)
        gate = g_ref[0, :, h * LANES:(h + 1) * LANES].astype(F32)
        o_ref[0, :, h * LANES:(h + 1) * LANES] = (_silu(gate) * o).astype(BF16)


N_SWA_INPUTS = 5


def _swa_ret_kernel(*refs):
    n_in = N_SWA_INPUTS + 10
    yb_ref, yc_ref, state_ref = refs[n_in:]
    _swa_kernel(*refs[:N_SWA_INPUTS], yb_ref)
    _ret_kernel(*refs[N_SWA_INPUTS:n_in], yc_ref, state_ref)


def _swa_retention(proj, bias_b, sink_rows, sin_t, cos_t, decay, zeta_t, xi_t, gch):
    assert B_WINDOW == C_CHUNK
    bsz, s, _ = proj.shape
    c = C_CHUNK
    qw = C_HEADS * C_QK_DIM
    kvw = 2 * B_KV_HEADS * B_HEAD_DIM
    qb, kvb = OFF_BQ // B_WIDTH, OFF_BKV // kvw
    cols = (B_HEADS // B_KV_HEADS) * c
    return pl.pallas_call(
        _swa_ret_kernel,
        grid=(bsz, s // c),
        in_specs=[pl.BlockSpec((1, c, B_WIDTH), lambda b, n: (b, n, qb)),
                  pl.BlockSpec((1, c, kvw), lambda b, n: (b, jnp.maximum(n - 1, 0), kvb)),
                  pl.BlockSpec((1, c, kvw), lambda b, n: (b, n, kvb)),
                  pl.BlockSpec((B_KV_HEADS, 2, 2 * c, cols), lambda b, n: (0, 0, 0, 0)),
                  pl.BlockSpec((B_KV_HEADS, 1, cols), lambda b, n: (0, 0, 0)),
                  pl.BlockSpec((1, c, qw), lambda b, n: (b, n, OFF_CQ // qw)),
                  pl.BlockSpec((1, c, qw), lambda b, n: (b, n, OFF_CK // qw)),
                  pl.BlockSpec((1, c, C_WIDTH), lambda b, n: (b, n, OFF_CV // C_WIDTH)),
                  pl.BlockSpec((1, c, C_WIDTH), lambda b, n: (b, n, OFF_CG // C_WIDTH)),
                  pl.BlockSpec((c, LANES), lambda b, n: (n, 0)),
                  pl.BlockSpec((c, LANES), lambda b, n: (n, 0)),
                  pl.BlockSpec((C_HEADS, c, c), lambda b, n: (0, 0, 0)),
                  pl.BlockSpec((C_HEADS // 2, c, LANES), lambda b, n: (0, 0, 0)),
                  pl.BlockSpec((C_HEADS // 2, c, LANES), lambda b, n: (0, 0, 0)),
                  pl.BlockSpec((C_HEADS, 1, LANES), lambda b, n: (0, 0, 0))],
        out_specs=[pl.BlockSpec((1, c, B_WIDTH), lambda b, n: (b, n, 0)),
                   pl.BlockSpec((1, c, C_WIDTH), lambda b, n: (b, n, 0))],
        out_shape=[jax.ShapeDtypeStruct((bsz, s, B_WIDTH), BF16), jax.ShapeDtypeStruct((bsz, s, C_WIDTH), BF16)],
        scratch_shapes=[pltpu.VMEM((C_HEADS, LANES, C_V_DIM), F32)],
        compiler_params=_cparams(("parallel", "arbitrary")),
        name="swa_retention",
    )(proj, proj, proj, bias_b, sink_rows, proj, proj, proj, proj, sin_t, cos_t, decay, zeta_t, xi_t, gch)


def _cast_riders(riders_in, riders_out):
    for src, dst in zip(riders_in, riders_out):
        dst[...] = src[...].astype(BF16)


def _row_split_riders(riders, steps):
    views = [r.reshape(r.size // r.shape[-1], r.shape[-1]) for r in riders]
    for v in views:
        assert v.shape[0] % (steps * 16) == 0, v.shape
    return views, [(v.shape[0] // steps, v.shape[1]) for v in views]


def _out_proj_kernel(ya_ref, yb_ref, yc_ref, wa_ref, wb_ref, wc_ref, x_ref, g_ref, b_ref, *rest):
    n_riders = (len(rest) - 1) // 2
    o_ref = rest[n_riders]
    half = o_ref.shape[0] // 2
    spans = [slice(0, half), slice(half, 2 * half)]
    mixes = [jnp.dot(ya_ref[r, :], wa_ref[...], preferred_element_type=F32)
             + jnp.dot(yb_ref[r, :], wb_ref[...], preferred_element_type=F32)
             + jnp.dot(yc_ref[r, :], wc_ref[...], preferred_element_type=F32) for r in spans]
    for r, mix in zip(spans, mixes):
        o_ref[r, :] = _layer_norm(ALPHA * x_ref[r, :] + mix, g_ref[...], b_ref[...])
    _cast_riders(rest[:n_riders], rest[n_riders + 1:])


def _out_proj_ln(ya, yb, yc, w, x, g, b, riders=()):
    n, d = x.shape
    tm = min(512, n)
    row = lambda i: (i, 0)
    fixed = lambda i: (0, 0)
    once = pl.Buffered(1)
    views, blocks = _row_split_riders(riders, n // tm)
    rider_specs = [pl.BlockSpec(blk, row) for blk in blocks]
    outs = pl.pallas_call(
        _out_proj_kernel,
        grid=(n // tm,),
        in_specs=[pl.BlockSpec((tm, A_WIDTH), row),
                  pl.BlockSpec((tm, B_WIDTH), row),
                  pl.BlockSpec((tm, C_WIDTH), row),
                  pl.BlockSpec((A_WIDTH, d), lambda i: (0, 0), pipeline_mode=once),
                  pl.BlockSpec((B_WIDTH, d), lambda i: (1, 0), pipeline_mode=once),
                  pl.BlockSpec((C_WIDTH, d), lambda i: ((A_WIDTH + B_WIDTH) // C_WIDTH, 0), pipeline_mode=once),
                  pl.BlockSpec((tm, d), row),
                  pl.BlockSpec((1, d), fixed),
                  pl.BlockSpec((1, d), fixed)] + rider_specs,
        out_specs=[pl.BlockSpec((tm, d), row)] + rider_specs,
        out_shape=[jax.ShapeDtypeStruct((n, d), F32)] + [jax.ShapeDtypeStruct(v.shape, BF16) for v in views],
        compiler_params=_cparams(("arbitrary",)),
        name="out_proj_ln",
    )(ya, yb, yc, w, w, w, x, g, b, *views)
    return outs[0], [o.reshape(r.shape) for o, r in zip(outs[1:], riders)]


FFN_TM = 512
FFN_TF = 512


def _swiglu_step(xb, wg, wu, wd):
    hg = jnp.dot(xb, wg, preferred_element_type=F32)
    hu = jnp.dot(xb, wu, preferred_element_type=F32)
    return jnp.dot((_silu(hg) * hu).astype(BF16), wd, preferred_element_type=F32)


def _ffn_kernel(x_ref, wg_ref, wu_ref, wd_ref, g_ref, b_ref, *rest):
    n_riders = (len(rest) - 2) // 2
    riders_in, o_ref = rest[:n_riders], rest[n_riders]
    riders_out, xb_ref = rest[n_riders + 1:-1], rest[-1]
    f = pl.program_id(1)

    @pl.when(f == 0)
    def _():
        xb = x_ref[...].astype(BF16)
        xb_ref[...] = xb
        o_ref[...] = _swiglu_step(xb, wg_ref[...], wu_ref[...], wd_ref[...])
        _cast_riders(riders_in, riders_out)

    @pl.when(f > 0)
    def _():
        o_ref[...] += _swiglu_step(xb_ref[...], wg_ref[...], wu_ref[...], wd_ref[...])
        _cast_riders(riders_in, riders_out)

    @pl.when(f == pl.num_programs(1) - 1)
    def _():
        o_ref[...] = _layer_norm(ALPHA * x_ref[...] + o_ref[...], g_ref[...], b_ref[...])


def _ffn(x, w_gate, w_up, w_down, g, b, riders=()):
    n, d = x.shape
    ff = w_gate.shape[1]
    tm = min(FFN_TM, n)
    tf = FFN_TF
    ni, nf = n // tm, ff // tf
    flat, rider_specs = [], []
    for r in riders:
        cols = r.shape[-1]
        rows = r.size // cols
        flat.append(r.reshape(rows, cols))
        if rows % (ni * nf * 16) == 0:
            rider_specs.append(pl.BlockSpec((rows // (ni * nf), cols), lambda i, f: (i * nf + f, 0)))
        else:
            assert rows % (ni * 16) == 0 and cols % (nf * LANES) == 0
            rider_specs.append(pl.BlockSpec((rows // ni, cols // nf), lambda i, f: (i, f)))
    outs = pl.pallas_call(
        _ffn_kernel,
        grid=(n // tm, ff // tf),
        in_specs=[pl.BlockSpec((tm, d), lambda i, f: (i, 0)),
                  pl.BlockSpec((d, tf), lambda i, f: (0, f)),
                  pl.BlockSpec((d, tf), lambda i, f: (0, f)),
                  pl.BlockSpec((tf, d), lambda i, f: (f, 0)),
                  pl.BlockSpec((1, d), lambda i, f: (0, 0)),
                  pl.BlockSpec((1, d), lambda i, f: (0, 0))] + rider_specs,
        out_specs=[pl.BlockSpec((tm, d), lambda i, f: (i, 0))] + rider_specs,
        out_shape=[jax.ShapeDtypeStruct((n, d), F32)] + [jax.ShapeDtypeStruct(r.shape, BF16) for r in flat],
        scratch_shapes=[pltpu.VMEM((tm, d), BF16)],
        compiler_params=_cparams(("arbitrary", "arbitrary")),
        name="ffn_ln",
    )(x, w_gate, w_up, w_down, g, b, *flat)
    return outs[0], [o.reshape(r.shape) for o, r in zip(outs[1:], riders)]


def _row_copy(src, dst, sem, s, t):
    return pltpu.make_async_copy(src.at[pl.ds(s, 1)], dst.at[pl.ds(t, 1)], sem)


def _ffn_grouped_kernel(eid_ref, nact_ref, src_ref, x_hbm, wg_ref, wu_ref, wd_ref, o_ref, rows_ref, xb_ref, sem,
                        *, tm, per_step):
    i = pl.program_id(0)
    f = pl.program_id(1)
    nact = nact_ref[0]
    active = i < nact
    slot = lax.rem(i, 2)
    buf_rows = rows_ref.shape[1]
    last_slot_row = src_ref.shape[0] - 1

    def fetch(tile, into, first_row, count):
        for r in range(count):
            row = first_row + r
            token = src_ref[jnp.minimum(tile * tm + row, last_slot_row)]
            _row_copy(x_hbm, rows_ref.at[into], sem.at[into], token, row).start()

    def wait_buffer(which):
        pltpu.make_async_copy(x_hbm.at[pl.ds(0, buf_rows)], rows_ref.at[which], sem.at[which]).wait()

    @pl.when(jnp.logical_and(i == 0, f == 0))
    def _():
        lax.fori_loop(0, buf_rows // per_step, lambda s, c: (fetch(0, 0, s * per_step, per_step), c)[1], 0)

    @pl.when(jnp.logical_and(f == 0, i <= nact))
    def _():
        wait_buffer(slot)

    @pl.when(jnp.logical_and(f == 0, jnp.logical_not(active)))
    def _():
        o_ref[...] = jnp.zeros_like(o_ref)

    @pl.when(jnp.logical_and(f == 0, active))
    def _():
        fetch(i + 1, 1 - slot, 0, per_step)
        xb = rows_ref[slot, :tm, :].astype(BF16)
        xb_ref[...] = xb
        o_ref[...] = _swiglu_step(xb, wg_ref[0], wu_ref[0], wd_ref[0])

    @pl.when(jnp.logical_and(f > 0, active))
    def _():
        fetch(i + 1, 1 - slot, f * per_step, per_step)
        o_ref[...] += _swiglu_step(xb_ref[...], wg_ref[0], wu_ref[0], wd_ref[0])


def _ffn_grouped(x, src, eid, nact, w_gate, w_up, w_down):
    n, d = x.shape
    slots = src.shape[0]
    ff = w_gate.shape[2]
    tm = min(FFN_TM, n)
    tf = FFN_TF
    nf = ff // tf
    per_step = -(-tm // nf)
    per_step += (-per_step) % 8
    assert per_step * nf <= n

    def fidx(i, f, nact_ref):
        return jnp.where(i < nact_ref[0], f, nf - 1)

    grid_spec = pltpu.PrefetchScalarGridSpec(
        num_scalar_prefetch=3,
        grid=(slots // tm, nf),
        in_specs=[pl.BlockSpec(memory_space=pl.ANY),
                  pl.BlockSpec((1, d, tf), lambda i, f, e, a, s: (e[i], 0, fidx(i, f, a))),
                  pl.BlockSpec((1, d, tf), lambda i, f, e, a, s: (e[i], 0, fidx(i, f, a))),
                  pl.BlockSpec((1, tf, d), lambda i, f, e, a, s: (e[i], fidx(i, f, a), 0))],
        out_specs=pl.BlockSpec((tm, d), lambda i, f, e, a, s: (i, 0)),
        scratch_shapes=[pltpu.VMEM((2, per_step * nf, d), F32), pltpu.VMEM((tm, d), BF16),
                        pltpu.SemaphoreType.DMA((2,))],
    )
    return pl.pallas_call(
        functools.partial(_ffn_grouped_kernel, tm=tm, per_step=per_step),
        grid_spec=grid_spec,
        out_shape=jax.ShapeDtypeStruct((slots, d), F32),
        compiler_params=_cparams(("arbitrary", "arbitrary")),
        name="ffn_grouped",
    )(eid, nact, src, x, w_gate, w_up, w_down)


ROUTER_TM = 512


def _router_kernel(x_ref, wr_ref, tri_ref, meta_ref, cnt_ref, carry_ref):
    @pl.when(pl.program_id(0) == 0)
    def _():
        carry_ref[...] = jnp.zeros_like(carry_ref)

    x = x_ref[...]
    x_hi = x.astype(BF16)
    x_lo = (x - x_hi.astype(F32)).astype(BF16)
    parts = (jnp.dot(x_hi, wr_ref[...], preferred_element_type=F32)
             + jnp.dot(x_lo, wr_ref[...], preferred_element_type=F32))
    logits = parts + pltpu.roll(parts, LANES - N_EXPERTS, 1)
    lane = lax.broadcasted_iota(jnp.int32, logits.shape, 1)
    logits = jnp.where(lane < N_EXPERTS, logits, -jnp.inf)
    m1 = jnp.max(logits, axis=-1, keepdims=True)
    i1 = jnp.min(jnp.where(logits == m1, lane, LANES), axis=-1, keepdims=True)
    rest = jnp.where(lane == i1, -jnp.inf, logits)
    m2 = jnp.max(rest, axis=-1, keepdims=True)
    i2 = jnp.min(jnp.where(rest == m2, lane, LANES), axis=-1, keepdims=True)
    e2 = jnp.exp(m2 - m1)
    w1 = 1.0 / (1.0 + e2)
    w2 = e2 * w1
    hit1 = lane == i1
    hit2 = lane == i2
    onehot = jnp.where(jnp.logical_or(hit1, hit2), 1.0, 0.0)
    before = jnp.dot(tri_ref[...], onehot.astype(BF16), preferred_element_type=F32) + carry_ref[...]
    r1 = jnp.sum(jnp.where(hit1, before, 0.0), axis=-1, keepdims=True)
    r2 = jnp.sum(jnp.where(hit2, before, 0.0), axis=-1, keepdims=True)
    carry_ref[...] = carry_ref[...] + jnp.sum(onehot, axis=0, keepdims=True)
    cnt_ref[...] = carry_ref[...]
    meta = jnp.where(lane == 0, i1.astype(F32), 0.0)
    meta = jnp.where(lane == 1, i2.astype(F32), meta)
    meta = jnp.where(lane == 2, r1, meta)
    meta = jnp.where(lane == 3, r2, meta)
    meta = jnp.where(lane == 4, w1, meta)
    meta = jnp.where(lane == 5, w2, meta)
    meta_ref[...] = meta


def _router(x, w_router):
    n, d = x.shape
    tm = min(ROUTER_TM, n)
    w_hi = w_router.astype(BF16)
    w_lo = (w_router.astype(F32) - w_hi.astype(F32)).astype(BF16)
    wr = jnp.zeros((d, LANES), BF16).at[:, :N_EXPERTS].set(w_hi).at[:, N_EXPERTS:2 * N_EXPERTS].set(w_lo)
    tri = jnp.asarray(np.tril(np.ones((tm, tm), np.float32), -1), BF16)
    return pl.pallas_call(
        _router_kernel,
        grid=(n // tm,),
        in_specs=[pl.BlockSpec((tm, d), lambda i: (i, 0)),
                  pl.BlockSpec((d, LANES), lambda i: (0, 0)),
                  pl.BlockSpec((tm, tm), lambda i: (0, 0))],
        out_specs=[pl.BlockSpec((tm, LANES), lambda i: (i, 0)),
                   pl.BlockSpec((1, LANES), lambda i: (0, 0))],
        out_shape=[jax.ShapeDtypeStruct((n, LANES), F32), jax.ShapeDtypeStruct((1, LANES), F32)],
        scratch_shapes=[pltpu.VMEM((1, LANES), F32)],
        compiler_params=_cparams(("arbitrary",)),
        name="router",
    )(x, wr, tri)


MOVE_TM = 256
ISSUE_UNROLL = 8


def _combine_kernel(d1_ref, d2_ref, ys_hbm, x_ref, meta_ref, g_ref, b_ref, o_ref, buf_ref, sem, *, tm):
    i = pl.program_id(0)
    last = pl.num_programs(0) - 1
    slot = lax.rem(i, 2)

    def start_row(tile, into, t):
        row = tile * tm + t
        _row_copy(ys_hbm, buf_ref.at[into, 0], sem.at[into], d1_ref[row], t).start()
        _row_copy(ys_hbm, buf_ref.at[into, 1], sem.at[into], d2_ref[row], t).start()

    def wait_slot(which):
        for k in range(2):
            pltpu.make_async_copy(ys_hbm.at[pl.ds(0, tm)], buf_ref.at[which, k], sem.at[which]).wait()

    @pl.when(i == 0)
    def _():
        lax.fori_loop(0, tm, lambda t, c: (start_row(0, 0, t), c)[1], 0, unroll=ISSUE_UNROLL)

    wait_slot(slot)
    nxt = jnp.minimum(i + 1, last)
    for t in range(tm):
        start_row(nxt, 1 - slot, t)
    meta = meta_ref[...]
    lane = lax.broadcasted_iota(jnp.int32, meta.shape, 1)
    w1 = jnp.sum(jnp.where(lane == 4, meta, 0.0), axis=-1, keepdims=True)
    w2 = jnp.sum(jnp.where(lane == 5, meta, 0.0), axis=-1, keepdims=True)
    f = w1 * buf_ref[slot, 0] + w2 * buf_ref[slot, 1]
    o_ref[...] = _layer_norm(ALPHA * x_ref[...] + f, g_ref[...], b_ref[...])

    @pl.when(i == last)
    def _():
        wait_slot(1 - slot)


def _combine(ys, x, meta, d1, d2, g, b):
    n, d = x.shape
    tm = min(MOVE_TM, n)
    grid_spec = pltpu.PrefetchScalarGridSpec(
        num_scalar_prefetch=2,
        grid=(n // tm,),
        in_specs=[pl.BlockSpec(memory_space=pl.ANY),
                  pl.BlockSpec((tm, d), lambda i, a, c: (i, 0)),
                  pl.BlockSpec((tm, LANES), lambda i, a, c: (i, 0)),
                  pl.BlockSpec((1, d), lambda i, a, c: (0, 0)),
                  pl.BlockSpec((1, d), lambda i, a, c: (0, 0))],
        out_specs=pl.BlockSpec((tm, d), lambda i, a, c: (i, 0)),
        scratch_shapes=[pltpu.VMEM((2, 2, tm, d), F32), pltpu.SemaphoreType.DMA((2,))],
    )
    return pl.pallas_call(
        functools.partial(_combine_kernel, tm=tm),
        grid_spec=grid_spec,
        out_shape=jax.ShapeDtypeStruct((n, d), F32),
        compiler_params=_cparams(("arbitrary",)),
        name="moe_combine",
    )(d1, d2, ys, x, meta, g, b)


def _moe(x, w_router, w_gate, w_up, w_down, g, b):
    n, d = x.shape
    tm = min(FFN_TM, n)
    meta, cnt = _router(x, w_router)
    i1 = meta[:, 0].astype(jnp.int32)
    i2 = meta[:, 1].astype(jnp.int32)
    counts = cnt[0, :N_EXPERTS].astype(jnp.int32)
    tiles = (counts + tm - 1) // tm
    tile_end = jnp.cumsum(tiles)
    group_start = (tile_end - tiles) * tm
    experts = jnp.arange(N_EXPERTS, dtype=jnp.int32)[None, :]
    start_of = lambda idx: jnp.sum(jnp.where(idx[:, None] == experts, group_start[None, :], 0), axis=1)
    d1 = start_of(i1) + meta[:, 2].astype(jnp.int32)
    d2 = start_of(i2) + meta[:, 3].astype(jnp.int32)
    max_tiles = (2 * n) // tm + N_EXPERTS
    tile_ids = jnp.arange(max_tiles, dtype=jnp.int32)[:, None]
    eid = jnp.minimum(jnp.sum((tile_ids >= tile_end[None, :]).astype(jnp.int32), axis=1), N_EXPERTS - 1)
    nact = tile_end[-1:].astype(jnp.int32)
    token = jnp.arange(n, dtype=jnp.int32)
    src = jnp.zeros((max_tiles * tm,), jnp.int32).at[jnp.concatenate([d1, d2])].set(
        jnp.concatenate([token, token]), unique_indices=True)
    ys = _ffn_grouped(x, src, eid.astype(jnp.int32), nact, w_gate, w_up, w_down)
    return _combine(ys, x, meta, d1, d2, g, b)


def _lambda_init(layer_idx):
    return 0.8 - 0.6 * math.exp(-0.3 * layer_idx)


def _static_tables(s):
    c = C_CHUNK
    ang = jnp.repeat(1.0 / (10000.0 ** jnp.linspace(0.0, 1.0, C_QK_DIM // 2, dtype=F32)), 2)
    ang = jnp.arange(s, dtype=F32)[:, None] * ang[None, :]
    sign = jnp.where(jnp.arange(C_QK_DIM) % 2 == 0, -1.0, 1.0).astype(F32)
    sin_t = jnp.tile(jnp.sin(ang) * sign[None, :], (1, 2))
    cos_t = jnp.tile(jnp.cos(ang), (1, 2))
    log_g = jnp.log(1.0 - jnp.exp2(-5.0 - jnp.arange(C_HEADS, dtype=F32)))
    pos = jnp.arange(c)
    rel = (pos[:, None] - pos[None, :]).astype(F32)
    decay = jnp.where((rel >= 0)[None], jnp.exp(jnp.maximum(rel, 0.0)[None] * log_g[:, None, None]), 0.0)
    zeta = jnp.exp((c - 1 - pos).astype(F32)[:, None] * log_g[None, :])
    xi = jnp.exp((pos + 1).astype(F32)[:, None] * log_g[None, :])
    per_pair = lambda t: jnp.repeat(t.T.reshape(C_HEADS // 2, 2, c), C_QK_DIM, axis=1).transpose(0, 2, 1)
    gch = jnp.broadcast_to(jnp.exp(c * log_g)[:, None, None], (C_HEADS, 1, LANES))
    dist_a = np.arange(A_SUB)[None, :] - np.arange(A_BLOCK)[:, None]
    types = [jnp.where(dist_a + off >= 0, _rel_bucket(jnp.asarray(dist_a + off)), REL_BUCKETS)
             for off in (0, A_SUB, A_BLOCK)]
    types.append(jnp.full(dist_a.shape, REL_BUCKETS - 1, jnp.int32))
    bkt_a = jnp.concatenate([jnp.tile(b, (1, 2)) for b in types], axis=0).astype(jnp.int32)
    w = B_WINDOW
    dist = np.arange(w)[:, None] + w - np.arange(2 * w)[None, :]
    band = (dist >= 0) & (dist < w)
    has_prev = np.stack([np.broadcast_to(np.arange(2 * w)[None, :] >= w, band.shape), np.ones_like(band)])
    bkt_b = jnp.where(band[None] & has_prev, _rel_bucket(jnp.asarray(dist))[None], REL_BUCKETS)
    bkt_b = bkt_b.transpose(0, 2, 1).reshape(2 * 2 * w, w).astype(jnp.int32)
    return sin_t, cos_t, decay, per_pair(zeta), per_pair(xi), gch, bkt_a, bkt_b


def kernel(x, w_in, rel_bias, a_lambda, a_subln_g, b_sinks, w_out, ln_mix_g, ln_mix_b, ln_ffn_g, ln_ffn_b,
           dense_w_gate, dense_w_up, dense_w_down, moe_router, moe_w_gate, moe_w_up, moe_w_down):
    bsz, s, d = x.shape
    n = bsz * s
    sin_t, cos_t, decay, zeta_t, xi_t, gch, bkt_a, bkt_b = _static_tables(s)
    tab_t = rel_bias.astype(F32).T
    bias_a = _bias_lookup(tab_t[:A_HEADS] * LOG2E, bkt_a).reshape(A_HEADS, A_BIAS_TYPES, A_BLOCK, 2 * A_SUB)
    group = B_HEADS // B_KV_HEADS
    bias_b = _bias_lookup(tab_t[A_HEADS:] * LOG2E, bkt_b)
    bias_b = bias_b.reshape(B_KV_HEADS, group, 2, 2 * B_WINDOW, B_WINDOW).transpose(0, 2, 3, 1, 4)
    bias_b = bias_b.reshape(B_KV_HEADS, 2, 2 * B_WINDOW, group * B_WINDOW)
    xf = x.reshape(n, d).astype(F32)
    w_in_bf16, w_out_bf16 = w_in[0].astype(BF16), w_out[0].astype(BF16)
    for l in range(DEPTH):
        w_in_l = jnp.concatenate([w_in_bf16[:, a:b] for a, b in _PERM_RUNS]
                                 + [jnp.zeros((d, PROJ_PAD - PROJ_WIDTH), BF16)], axis=1)
        proj = _in_proj(xf, w_in_l).reshape(bsz, s, PROJ_PAD)
        ya = _attn_a(proj, bias_a, a_lambda[l].astype(F32), a_subln_g[l].astype(F32).reshape(1, A_V_DIM),
                     _lambda_init(l))
        sink_rows = jnp.repeat(b_sinks[l].astype(F32).reshape(B_KV_HEADS, 1, group), B_WINDOW, axis=2)
        yb, yc = _swa_retention(proj, bias_b, sink_rows, sin_t, cos_t, decay, zeta_t, xi_t, gch)
        g_mix = ln_mix_g[l].astype(F32).reshape(1, d)
        b_mix = ln_mix_b[l].astype(F32).reshape(1, d)
        j = l // 2
        dense = (dense_w_gate[j], dense_w_up[j], dense_w_down[j]) if l % 2 == 0 else ()
        ahead = (w_in[l + 1], w_out[l + 1]) if l + 1 < DEPTH else ()
        xf, cast = _out_proj_ln(ya.reshape(n, A_WIDTH), yb.reshape(n, B_WIDTH), yc.reshape(n, C_WIDTH),
                                w_out_bf16, xf, g_mix, b_mix, dense + ahead)
        dense_bf16 = cast[:len(dense)]
        if ahead:
            w_in_bf16, w_out_bf16 = cast[len(dense):]
        g_ffn = ln_ffn_g[l].astype(F32).reshape(1, d)
        b_ffn = ln_ffn_b[l].astype(F32).reshape(1, d)
        if l % 2 == 0:
            experts = (moe_w_gate[j], moe_w_up[j], moe_w_down[j]) if l + 1 < DEPTH else ()
            xf, moe_bf16 = _ffn(xf, *dense_bf16, g_ffn, b_ffn, experts)
        else:
            assert l > 0, "expert weights are cast by the preceding dense layer"
            xf = _moe(xf, moe_router[j], *moe_bf16, g_ffn, b_ffn)
    return xf.reshape(bsz, s, d).astype(x.dtype)
```

```python
import functools
import math

import jax
import jax.numpy as jnp
import numpy as np
from jax import lax
from jax.experimental import pallas as pl
from jax.experimental.pallas import tpu as pltpu

F32 = jnp.float32
BF16 = jnp.bfloat16

D_MODEL = 2048
DEPTH = 2
A_HEADS = 6
A_QK_DIM = 64
A_V_DIM = 128
B_HEADS = 12
B_KV_HEADS = 3
B_HEAD_DIM = 64
B_WINDOW = 128
C_HEADS = 4
C_QK_DIM = 64
C_V_DIM = 128
C_CHUNK = 128
A_WIDTH = A_HEADS * A_V_DIM
B_WIDTH = B_HEADS * B_HEAD_DIM
C_WIDTH = C_HEADS * C_V_DIM
REL_BUCKETS = 32
REL_MAX_DIST = 128
D_FF = 5632
N_EXPERTS = 8
ALPHA = (2.0 * DEPTH) ** 0.25
LN_EPS = 1e-5
NEG = -1e30

LANES = 128
VMEM_LIMIT = 56 * 1024 * 1024

_REF_SIZES = [768, 768, 768, 768, 192, 192, 256, 256, 512, 512]
_REF_OFF = [int(v) for v in np.concatenate([[0], np.cumsum(_REF_SIZES)[:-1]])]
PROJ_WIDTH = int(sum(_REF_SIZES))
OFF_AQ, OFF_AK, OFF_AV, OFF_BQ, OFF_CV, OFF_CG, OFF_CQ, OFF_CK, OFF_BKV = (
    0, 768, 1536, 2304, 3072, 3584, 4096, 4352, 4608)


def _proj_perm():
    aq, ak, av, bq, bk, bv, cq, ck, cv, cg = [np.arange(o, o + s) for o, s in zip(_REF_OFF, _REF_SIZES)]
    bkv = np.concatenate([np.concatenate([bk[g * 64:(g + 1) * 64], bv[g * 64:(g + 1) * 64]])
                          for g in range(B_KV_HEADS)])
    perm = np.concatenate([aq, ak, av, bq, cv, cg, cq, ck, bkv])
    assert perm.shape[0] == PROJ_WIDTH
    return perm


def _perm_runs():
    perm = _proj_perm()
    cuts = np.flatnonzero(np.diff(perm) != 1) + 1
    return [(int(r[0]), int(r[-1]) + 1) for r in np.split(perm, cuts)]


_PERM_RUNS = _perm_runs()


def _cparams(sem):
    return pltpu.CompilerParams(dimension_semantics=sem, vmem_limit_bytes=VMEM_LIMIT)


def _layer_norm(z, g, b):
    mu = jnp.mean(z, axis=-1, keepdims=True)
    zc = z - mu
    var = jnp.mean(zc * zc, axis=-1, keepdims=True)
    return zc * lax.rsqrt(var + LN_EPS) * g + b


def _silu(x):
    return x / (1.0 + jnp.exp(-x))


def _dot_nt(a, b):
    return lax.dot_general(a, b, (((1,), (1,)), ((), ())), preferred_element_type=F32)


def _dot_tn(a, b):
    return lax.dot_general(a, b, (((0,), (0,)), ((), ())), preferred_element_type=F32)


def _bias_kernel(tab_ref, bkt_ref, o_ref):
    h = pl.program_id(0)
    bkt = bkt_ref[...]
    acc = jnp.full(bkt.shape, NEG, F32)
    for b in range(REL_BUCKETS):
        acc = jnp.where(bkt == b, tab_ref[h, b], acc)
    o_ref[0] = acc


def _bias_lookup(tab_t, bkt):
    nh = tab_t.shape[0]
    r, c = bkt.shape
    return pl.pallas_call(
        _bias_kernel,
        grid=(nh,),
        in_specs=[pl.BlockSpec(memory_space=pltpu.SMEM),
                  pl.BlockSpec((r, c), lambda h: (0, 0))],
        out_specs=pl.BlockSpec((1, r, c), lambda h: (h, 0, 0)),
        out_shape=jax.ShapeDtypeStruct((nh, r, c), F32),
        compiler_params=_cparams(("arbitrary",)),
        name="bias_lookup",
    )(tab_t, bkt)


def _rel_bucket(dist):
    max_exact = REL_BUCKETS // 2
    d = jnp.maximum(dist, 0)
    ratio = jnp.maximum(d, 1).astype(F32) / max_exact
    large = max_exact + (jnp.log(ratio) / math.log(REL_MAX_DIST / max_exact)
                         * (REL_BUCKETS - max_exact)).astype(jnp.int32)
    large = jnp.minimum(large, REL_BUCKETS - 1)
    return jnp.where(d < max_exact, d, large)


MXU_COLS = 256
IN_PROJ_COL_STEPS = 4
PROJ_PAD = -(-PROJ_WIDTH // (IN_PROJ_COL_STEPS * MXU_COLS)) * (IN_PROJ_COL_STEPS * MXU_COLS)


def _in_proj_kernel(x_ref, w_ref, o_ref, xb_ref):
    @pl.when(pl.program_id(1) == 0)
    def _():
        xb_ref[...] = x_ref[...].astype(BF16)

    o_ref[...] = jnp.dot(xb_ref[...], w_ref[...], preferred_element_type=F32).astype(BF16)


def _in_proj(x, w):
    n, d = x.shape
    p = w.shape[1]
    tm = min(1024, n)
    tn = p // IN_PROJ_COL_STEPS
    return pl.pallas_call(
        _in_proj_kernel,
        grid=(n // tm, p // tn),
        in_specs=[pl.BlockSpec((tm, d), lambda i, j: (i, 0)),
                  pl.BlockSpec((d, tn), lambda i, j: (0, j))],
        out_specs=pl.BlockSpec((tm, tn), lambda i, j: (i, j)),
        out_shape=jax.ShapeDtypeStruct((n, p), BF16),
        scratch_shapes=[pltpu.VMEM((tm, d), BF16)],
        compiler_params=_cparams(("parallel", "arbitrary")),
        name="in_proj",
    )(x, w)


A_BLOCK = 256
A_SUB = 128
A_QROWS = 512
A_PASSES = 4
A_BIAS_TYPES = 4
LOG2E = math.log2(math.e)


def _attn_a_kernel(q_ref, k_ref, v_ref, bias_ref, lam_ref, g_ref, o_ref, qt_ref, s_ref, p_ref, acc_ref, *,
                   lam_init):
    passes = q_ref.shape[1] // A_QROWS
    lp = lam_ref[...]
    lam = (jnp.exp(jnp.sum(lp[0:1] * lp[1:2], axis=-1, keepdims=True))
           - jnp.exp(jnp.sum(lp[2:3] * lp[3:4], axis=-1, keepdims=True)) + lam_init)
    for r in range(passes):
        _attn_a_pass(pl.program_id(2) * passes + r, q_ref.at[0, r * A_QROWS:(r + 1) * A_QROWS], k_ref, v_ref,
                     bias_ref, lam, g_ref, o_ref.at[0, r * A_QROWS:(r + 1) * A_QROWS],
                     qt_ref.at[r], s_ref.at[r], p_ref.at[r], acc_ref.at[r], lam_init)


def _attn_a_pass(i, q_ref, k_ref, v_ref, bias_ref, lam, g_ref, o_ref, qt_ref, s_ref, p_ref, acc_ref, lam_init):
    t = A_BLOCK
    nch = A_QROWS // A_SUB
    per_key_block = t // A_SUB
    lane = lax.broadcasted_iota(jnp.int32, (A_SUB, LANES), 1)
    for c in range(nch):
        q = q_ref[c * A_SUB:(c + 1) * A_SUB, :].astype(F32) * (A_QK_DIM ** -0.5 * LOG2E)
        qq = jnp.concatenate([jnp.where(lane < A_QK_DIM, q, 0.0), jnp.where(lane >= A_QK_DIM, q, 0.0)], axis=0)
        qt_ref[c] = qq.T.astype(BF16)
    first_block = i * (nch // per_key_block)
    n_blocks = first_block + nch // per_key_block

    all_chains = tuple(range(nch))
    late_chains = all_chains[per_key_block:]

    def scores_into(slot, j, chains):
        kj = k_ref[0, pl.ds(pl.multiple_of(j * t, t), t), :]
        for c in chains:
            back = first_block + c // per_key_block - j
            near = (0, 2) if c % per_key_block == 0 else (1, 3)
            kind = jnp.where(back == 0, near[0], jnp.where(back == 1, near[1], 3))
            s_ref[slot, c] = jnp.dot(kj, qt_ref[c], preferred_element_type=F32) + bias_ref[0, kind]

    def values_from(slot, j, alphas, chains):
        vj = v_ref[0, pl.ds(pl.multiple_of(j * t, t), t), :]
        for c in chains:
            acc_ref[c] = alphas[c] * acc_ref[c] + _dot_tn(vj, p_ref[slot, c])

    def sub_step(j, cur, carry, next_chains, chains):
        if next_chains:
            scores_into(1 - cur, j + 1, next_chains)
        new = list(carry)
        for c in chains:
            m_old, l_old, _ = carry[c]
            s = s_ref[cur, c]
            m_new = jnp.maximum(m_old, jnp.max(s, axis=0, keepdims=True))
            p = jnp.exp2(s - m_new)
            new[c] = (m_new, jnp.exp2(m_old - m_new) * l_old + jnp.sum(p, axis=0, keepdims=True),
                      jnp.exp2(m_old - m_new))
            p_ref[cur, c] = p.astype(BF16)
        values_from(1 - cur, jnp.maximum(j - 1, 0), [a for _, _, a in carry], all_chains)
        return tuple(new)

    def pair(jj, carry):
        carry = sub_step(2 * jj, 0, carry, all_chains, all_chains)
        return sub_step(2 * jj + 1, 1, carry, all_chains, all_chains)

    scores_into(0, 0, all_chains)
    p_ref[1] = jnp.zeros(p_ref.shape[1:], BF16)
    acc_ref[...] = jnp.zeros_like(acc_ref)
    init = tuple((jnp.full((1, 2 * A_SUB), NEG, F32), jnp.zeros((1, 2 * A_SUB), F32),
                  jnp.ones((1, 2 * A_SUB), F32)) for _ in range(nch))
    fin = lax.fori_loop(0, n_blocks // 2 - 1, pair, init)
    fin = sub_step(n_blocks - 2, 0, fin, late_chains, all_chains)
    fin = sub_step(n_blocks - 1, 1, fin, (), late_chains)
    values_from(1, n_blocks - 1, [a for _, _, a in fin], late_chains)
    for c, (_, l_fin, _) in enumerate(fin):
        o_all = acc_ref[c] / l_fin
        o = (o_all[:, :A_SUB] - lam * o_all[:, A_SUB:]).T
        o = o * lax.rsqrt(jnp.mean(o * o, axis=-1, keepdims=True) + LN_EPS) * g_ref[...]
        o_ref[c * A_SUB:(c + 1) * A_SUB, :] = (o * (1.0 - lam_init)).astype(BF16)


def _attn_a(proj, bias_a, lam_params, subln_g, lam_init):
    bsz, s, _ = proj.shape
    passes = min(A_PASSES, s // A_QROWS)
    t = A_QROWS * passes
    nch = A_QROWS // A_SUB
    kb, vb = OFF_AK // LANES, OFF_AV // LANES
    return pl.pallas_call(
        functools.partial(_attn_a_kernel, lam_init=lam_init),
        grid=(bsz, A_HEADS, s // t),
        in_specs=[pl.BlockSpec((1, t, LANES), lambda b, h, i: (b, i, h)),
                  pl.BlockSpec((1, s, LANES), lambda b, h, i: (b, 0, kb + h)),
                  pl.BlockSpec((1, s, LANES), lambda b, h, i: (b, 0, vb + h)),
                  pl.BlockSpec((1, A_BIAS_TYPES, A_BLOCK, 2 * A_SUB), lambda b, h, i: (h, 0, 0, 0)),
                  pl.BlockSpec((4, A_QK_DIM), lambda b, h, i: (0, 0)),
                  pl.BlockSpec((1, A_V_DIM), lambda b, h, i: (0, 0))],
        out_specs=pl.BlockSpec((1, t, LANES), lambda b, h, i: (b, i, h)),
        out_shape=jax.ShapeDtypeStruct((bsz, s, A_WIDTH), BF16),
        scratch_shapes=[pltpu.VMEM((passes, nch, LANES, 2 * A_SUB), BF16),
                        pltpu.VMEM((passes, 2, nch, A_BLOCK, 2 * A_SUB), F32),
                        pltpu.VMEM((passes, 2, nch, A_BLOCK, 2 * A_SUB), BF16),
                        pltpu.VMEM((passes, nch, A_V_DIM, 2 * A_SUB), F32)],
        compiler_params=_cparams(("parallel", "parallel", "arbitrary")),
        name="attn_a",
    )(proj, proj, proj, bias_a, lam_params, subln_g)


def _swa_kernel(q_ref, kvp_ref, kvc_ref, bias_ref, sink_ref, o_ref):
    w = B_WINDOW
    d = B_HEAD_DIM
    group = B_HEADS // B_KV_HEADS
    table = jnp.minimum(pl.program_id(1), 1)
    row = lax.broadcasted_iota(jnp.int32, (LANES, w), 0)
    zeros = jnp.zeros((LANES - d, w), F32)
    kvs, q_ts = [], []
    for g in range(B_KV_HEADS):
        kvs.append(jnp.concatenate([kvp_ref[0, :, g * LANES:(g + 1) * LANES],
                                    kvc_ref[0, :, g * LANES:(g + 1) * LANES]], axis=0))
        cols = []
        for pair in range(group // 2):
            blk = g * (group // 2) + pair
            t = (q_ref[0, :, blk * LANES:(blk + 1) * LANES].astype(F32) * (d ** -0.5 * LOG2E)).T
            cols += [jnp.where(row < d, t, 0.0), jnp.concatenate([t[d:], zeros], axis=0)]
        q_ts.append(jnp.concatenate(cols, axis=1).astype(BF16))
    scores = [jnp.dot(kv, q_t, preferred_element_type=F32) + bias_ref[g, table]
              for g, (kv, q_t) in enumerate(zip(kvs, q_ts))]
    soft = []
    for g, s in enumerate(scores):
        sink = sink_ref[g] * LOG2E
        m = jnp.maximum(jnp.max(s, axis=0, keepdims=True), sink)
        e = jnp.exp2(s - m)
        soft.append((e.astype(BF16), 1.0 / (jnp.sum(e, axis=0, keepdims=True) + jnp.exp2(sink - m))))
    outs = [_dot_tn(kv, e) * inv for kv, (e, inv) in zip(kvs, soft)]
    for g, o_t in enumerate(outs):
        for pair in range(group // 2):
            blk = g * (group // 2) + pair
            both = jnp.concatenate([o_t[d:, (2 * pair) * w:(2 * pair + 1) * w],
                                    o_t[d:, (2 * pair + 1) * w:(2 * pair + 2) * w]], axis=0)
            o_ref[0, :, blk * LANES:(blk + 1) * LANES] = both.T.astype(BF16)


def _ret_kernel(q_ref, k_ref, v_ref, g_ref, sin_ref, cos_ref, decay_ref, zeta_ref, xi_ref, gch_ref, o_ref,
                state_ref):
    c = C_CHUNK

    @pl.when(pl.program_id(1) == 0)
    def _():
        state_ref[...] = jnp.zeros_like(state_ref)

    lane = lax.broadcasted_iota(jnp.int32, (c, LANES), 1)
    even = (lane & 1) == 0
    sn = sin_ref[...]
    cs = cos_ref[...]

    def rope(x):
        swapped = jnp.where(even, pltpu.roll(x, LANES - 1, 1), pltpu.roll(x, 1, 1))
        return x * cs + swapped * sn

    qk = []
    for pair in range(C_HEADS // 2):
        q = rope(q_ref[0, :, pair * LANES:(pair + 1) * LANES].astype(F32))
        k = rope(k_ref[0, :, pair * LANES:(pair + 1) * LANES].astype(F32) * (C_QK_DIM ** -0.5))
        qk.append((q, q * xi_ref[pair], k.astype(BF16), (k * zeta_ref[pair]).astype(BF16)))
    masked = []
    for h in range(C_HEADS):
        q, qx, _, _ = qk[h // 2]
        in_head = (lane < C_QK_DIM) if h % 2 == 0 else (lane >= C_QK_DIM)
        masked.append((jnp.where(in_head, q, 0.0).astype(BF16), jnp.where(in_head, qx, 0.0).astype(BF16)))
    inner = [(_dot_nt(masked[h][0], qk[h // 2][2]) * decay_ref[h]).astype(BF16) for h in range(C_HEADS)]
    outs = []
    for h in range(C_HEADS):
        vh = v_ref[0, :, h * LANES:(h + 1) * LANES]
        st = state_ref[h]
        outs.append(jnp.dot(inner[h], vh, preferred_element_type=F32)
                    + jnp.dot(masked[h][1], st.astype(BF16), preferred_element_type=F32))
        state_ref[h] = st * gch_ref[h] + _dot_tn(qk[h // 2][3], vh)
    for h, o in enumerate(outs):
        mu = jnp.mean(o, axis=-1, keepdims=True)
        oc = o - mu
        o = oc * lax.rsqrt(jnp.mean(oc * oc, axis=-1, keepdims=True) + LN_EPS)
        gate = g_ref[0, :, h * LANES:(h + 1) * LANES].astype(F32)
        o_ref[0, :, h * LANES:(h + 1) * LANES] = (_silu(gate) * o).astype(BF16)


N_SWA_INPUTS = 5


def _swa_ret_kernel(*refs):
    n_in = N_SWA_INPUTS + 10
    yb_ref, yc_ref, state_ref = refs[n_in:]
    _swa_kernel(*refs[:N_SWA_INPUTS], yb_ref)
    _ret_kernel(*refs[N_SWA_INPUTS:n_in], yc_ref, state_ref)


def _swa_retention(proj, bias_b, sink_rows, sin_t, cos_t, decay, zeta_t, xi_t, gch):
    assert B_WINDOW == C_CHUNK
    bsz, s, _ = proj.shape
    c = C_CHUNK
    qw = C_HEADS * C_QK_DIM
    kvw = 2 * B_KV_HEADS * B_HEAD_DIM
    qb, kvb = OFF_BQ // B_WIDTH, OFF_BKV // kvw
    cols = (B_HEADS // B_KV_HEADS) * c
    return pl.pallas_call(
        _swa_ret_kernel,
        grid=(bsz, s // c),
        in_specs=[pl.BlockSpec((1, c, B_WIDTH), lambda b, n: (b, n, qb)),
                  pl.BlockSpec((1, c, kvw), lambda b, n: (b, jnp.maximum(n - 1, 0), kvb)),
                  pl.BlockSpec((1, c, kvw), lambda b, n: (b, n, kvb)),
                  pl.BlockSpec((B_KV_HEADS, 2, 2 * c, cols), lambda b, n: (0, 0, 0, 0)),
                  pl.BlockSpec((B_KV_HEADS, 1, cols), lambda b, n: (0, 0, 0)),
                  pl.BlockSpec((1, c, qw), lambda b, n: (b, n, OFF_CQ // qw)),
                  pl.BlockSpec((1, c, qw), lambda b, n: (b, n, OFF_CK // qw)),
                  pl.BlockSpec((1, c, C_WIDTH), lambda b, n: (b, n, OFF_CV // C_WIDTH)),
                  pl.BlockSpec((1, c, C_WIDTH), lambda b, n: (b, n, OFF_CG // C_WIDTH)),
                  pl.BlockSpec((c, LANES), lambda b, n: (n, 0)),
                  pl.BlockSpec((c, LANES), lambda b, n: (n, 0)),
                  pl.BlockSpec((C_HEADS, c, c), lambda b, n: (0, 0, 0)),
                  pl.BlockSpec((C_HEADS // 2, c, LANES), lambda b, n: (0, 0, 0)),
                  pl.BlockSpec((C_HEADS // 2, c, LANES), lambda b, n: (0, 0, 0)),
                  pl.BlockSpec((C_HEADS, 1, LANES), lambda b, n: (0, 0, 0))],
        out_specs=[pl.BlockSpec((1, c, B_WIDTH), lambda b, n: (b, n, 0)),
                   pl.BlockSpec((1, c, C_WIDTH), lambda b, n: (b, n, 0))],
        out_shape=[jax.ShapeDtypeStruct((bsz, s, B_WIDTH), BF16), jax.ShapeDtypeStruct((bsz, s, C_WIDTH), BF16)],
        scratch_shapes=[pltpu.VMEM((C_HEADS, LANES, C_V_DIM), F32)],
        compiler_params=_cparams(("parallel", "arbitrary")),
        name="swa_retention",
    )(proj, proj, proj, bias_b, sink_rows, proj, proj, proj, proj, sin_t, cos_t, decay, zeta_t, xi_t, gch)


def _cast_riders(riders_in, riders_out):
    for src, dst in zip(riders_in, riders_out):
        dst[...] = src[...].astype(BF16)


def _row_split_riders(riders, steps):
    views = [r.reshape(r.size // r.shape[-1], r.shape[-1]) for r in riders]
    for v in views:
        assert v.shape[0] % (steps * 16) == 0, v.shape
    return views, [(v.shape[0] // steps, v.shape[1]) for v in views]


def _out_proj_kernel(ya_ref, yb_ref, yc_ref, wa_ref, wb_ref, wc_ref, x_ref, g_ref, b_ref, *rest):
    n_riders = (len(rest) - 1) // 2
    o_ref = rest[n_riders]
    half = o_ref.shape[0] // 2
    spans = [slice(0, half), slice(half, 2 * half)]
    mixes = [jnp.dot(ya_ref[r, :], wa_ref[...], preferred_element_type=F32)
             + jnp.dot(yb_ref[r, :], wb_ref[...], preferred_element_type=F32)
             + jnp.dot(yc_ref[r, :], wc_ref[...], preferred_element_type=F32) for r in spans]
    for r, mix in zip(spans, mixes):
        o_ref[r, :] = _layer_norm(ALPHA * x_ref[r, :] + mix, g_ref[...], b_ref[...])
    _cast_riders(rest[:n_riders], rest[n_riders + 1:])


def _out_proj_ln(ya, yb, yc, w, x, g, b, riders=()):
    n, d = x.shape
    tm = min(512, n)
    row = lambda i: (i, 0)
    fixed = lambda i: (0, 0)
    once = pl.Buffered(1)
    views, blocks = _row_split_riders(riders, n // tm)
    rider_specs = [pl.BlockSpec(blk, row) for blk in blocks]
    outs = pl.pallas_call(
        _out_proj_kernel,
        grid=(n // tm,),
        in_specs=[pl.BlockSpec((tm, A_WIDTH), row),
                  pl.BlockSpec((tm, B_WIDTH), row),
                  pl.BlockSpec((tm, C_WIDTH), row),
                  pl.BlockSpec((A_WIDTH, d), lambda i: (0, 0), pipeline_mode=once),
                  pl.BlockSpec((B_WIDTH, d), lambda i: (1, 0), pipeline_mode=once),
                  pl.BlockSpec((C_WIDTH, d), lambda i: ((A_WIDTH + B_WIDTH) // C_WIDTH, 0), pipeline_mode=once),
                  pl.BlockSpec((tm, d), row),
                  pl.BlockSpec((1, d), fixed),
                  pl.BlockSpec((1, d), fixed)] + rider_specs,
        out_specs=[pl.BlockSpec((tm, d), row)] + rider_specs,
        out_shape=[jax.ShapeDtypeStruct((n, d), F32)] + [jax.ShapeDtypeStruct(v.shape, BF16) for v in views],
        compiler_params=_cparams(("arbitrary",)),
        name="out_proj_ln",
    )(ya, yb, yc, w, w, w, x, g, b, *views)
    return outs[0], [o.reshape(r.shape) for o, r in zip(outs[1:], riders)]


FFN_TM = 512
FFN_TF = 512


def _swiglu_step(xb, wg, wu, wd):
    hg = jnp.dot(xb, wg, preferred_element_type=F32)
    hu = jnp.dot(xb, wu, preferred_element_type=F32)
    return jnp.dot((_silu(hg) * hu).astype(BF16), wd, preferred_element_type=F32)


def _ffn_kernel(x_ref, wg_ref, wu_ref, wd_ref, g_ref, b_ref, *rest):
    n_riders = (len(rest) - 2) // 2
    riders_in, o_ref = rest[:n_riders], rest[n_riders]
    riders_out, xb_ref = rest[n_riders + 1:-1], rest[-1]
    f = pl.program_id(1)

    @pl.when(f == 0)
    def _():
        xb = x_ref[...].astype(BF16)
        xb_ref[...] = xb
        o_ref[...] = _swiglu_step(xb, wg_ref[...], wu_ref[...], wd_ref[...])
        _cast_riders(riders_in, riders_out)

    @pl.when(f > 0)
    def _():
        o_ref[...] += _swiglu_step(xb_ref[...], wg_ref[...], wu_ref[...], wd_ref[...])
        _cast_riders(riders_in, riders_out)

    @pl.when(f == pl.num_programs(1) - 1)
    def _():
        o_ref[...] = _layer_norm(ALPHA * x_ref[...] + o_ref[...], g_ref[...], b_ref[...])


def _ffn(x, w_gate, w_up, w_down, g, b, riders=()):
    n, d = x.shape
    ff = w_gate.shape[1]
    tm = min(FFN_TM, n)
    tf = FFN_TF
    ni, nf = n // tm, ff // tf
    flat, rider_specs = [], []
    for r in riders:
        cols = r.shape[-1]
        rows = r.size // cols
        flat.append(r.reshape(rows, cols))
        if rows % (ni * nf * 16) == 0:
            rider_specs.append(pl.BlockSpec((rows // (ni * nf), cols), lambda i, f: (i * nf + f, 0)))
        else:
            assert rows % (ni * 16) == 0 and cols % (nf * LANES) == 0
            rider_specs.append(pl.BlockSpec((rows // ni, cols // nf), lambda i, f: (i, f)))
    outs = pl.pallas_call(
        _ffn_kernel,
        grid=(n // tm, ff // tf),
        in_specs=[pl.BlockSpec((tm, d), lambda i, f: (i, 0)),
                  pl.BlockSpec((d, tf), lambda i, f: (0, f)),
                  pl.BlockSpec((d, tf), lambda i, f: (0, f)),
                  pl.BlockSpec((tf, d), lambda i, f: (f, 0)),
                  pl.BlockSpec((1, d), lambda i, f: (0, 0)),
                  pl.BlockSpec((1, d), lambda i, f: (0, 0))] + rider_specs,
        out_specs=[pl.BlockSpec((tm, d), lambda i, f: (i, 0))] + rider_specs,
        out_shape=[jax.ShapeDtypeStruct((n, d), F32)] + [jax.ShapeDtypeStruct(r.shape, BF16) for r in flat],
        scratch_shapes=[pltpu.VMEM((tm, d), BF16)],
        compiler_params=_cparams(("arbitrary", "arbitrary")),
        name="ffn_ln",
    )(x, w_gate, w_up, w_down, g, b, *flat)
    return outs[0], [o.reshape(r.shape) for o, r in zip(outs[1:], riders)]


def _row_copy(src, dst, sem, s, t):
    return pltpu.make_async_copy(src.at[pl.ds(s, 1)], dst.at[pl.ds(t, 1)], sem)


def _ffn_grouped_kernel(eid_ref, nact_ref, src_ref, x_hbm, wg_ref, wu_ref, wd_ref, o_ref, rows_ref, xb_ref, sem,
                        *, tm, per_step):
    i = pl.program_id(0)
    f = pl.program_id(1)
    nact = nact_ref[0]
    active = i < nact
    slot = lax.rem(i, 2)
    buf_rows = rows_ref.shape[1]
    last_slot_row = src_ref.shape[0] - 1

    def fetch(tile, into, first_row, count):
        for r in range(count):
            row = first_row + r
            token = src_ref[jnp.minimum(tile * tm + row, last_slot_row)]
            _row_copy(x_hbm, rows_ref.at[into], sem.at[into], token, row).start()

    def wait_buffer(which):
        pltpu.make_async_copy(x_hbm.at[pl.ds(0, buf_rows)], rows_ref.at[which], sem.at[which]).wait()

    @pl.when(jnp.logical_and(i == 0, f == 0))
    def _():
        lax.fori_loop(0, buf_rows // per_step, lambda s, c: (fetch(0, 0, s * per_step, per_step), c)[1], 0)

    @pl.when(jnp.logical_and(f == 0, i <= nact))
    def _():
        wait_buffer(slot)

    @pl.when(jnp.logical_and(f == 0, jnp.logical_not(active)))
    def _():
        o_ref[...] = jnp.zeros_like(o_ref)

    @pl.when(jnp.logical_and(f == 0, active))
    def _():
        fetch(i + 1, 1 - slot, 0, per_step)
        xb = rows_ref[slot, :tm, :].astype(BF16)
        xb_ref[...] = xb
        o_ref[...] = _swiglu_step(xb, wg_ref[0], wu_ref[0], wd_ref[0])

    @pl.when(jnp.logical_and(f > 0, active))
    def _():
        fetch(i + 1, 1 - slot, f * per_step, per_step)
        o_ref[...] += _swiglu_step(xb_ref[...], wg_ref[0], wu_ref[0], wd_ref[0])


def _ffn_grouped(x, src, eid, nact, w_gate, w_up, w_down):
    n, d = x.shape
    slots = src.shape[0]
    ff = w_gate.shape[2]
    tm = min(FFN_TM, n)
    tf = FFN_TF
    nf = ff // tf
    per_step = -(-tm // nf)
    per_step += (-per_step) % 8
    assert per_step * nf <= n

    def fidx(i, f, nact_ref):
        return jnp.where(i < nact_ref[0], f, nf - 1)

    grid_spec = pltpu.PrefetchScalarGridSpec(
        num_scalar_prefetch=3,
        grid=(slots // tm, nf),
        in_specs=[pl.BlockSpec(memory_space=pl.ANY),
                  pl.BlockSpec((1, d, tf), lambda i, f, e, a, s: (e[i], 0, fidx(i, f, a))),
                  pl.BlockSpec((1, d, tf), lambda i, f, e, a, s: (e[i], 0, fidx(i, f, a))),
                  pl.BlockSpec((1, tf, d), lambda i, f, e, a, s: (e[i], fidx(i, f, a), 0))],
        out_specs=pl.BlockSpec((tm, d), lambda i, f, e, a, s: (i, 0)),
        scratch_shapes=[pltpu.VMEM((2, per_step * nf, d), F32), pltpu.VMEM((tm, d), BF16),
                        pltpu.SemaphoreType.DMA((2,))],
    )
    return pl.pallas_call(
        functools.partial(_ffn_grouped_kernel, tm=tm, per_step=per_step),
        grid_spec=grid_spec,
        out_shape=jax.ShapeDtypeStruct((slots, d), F32),
        compiler_params=_cparams(("arbitrary", "arbitrary")),
        name="ffn_grouped",
    )(eid, nact, src, x, w_gate, w_up, w_down)


ROUTER_TM = 512


def _router_kernel(x_ref, wr_ref, tri_ref, meta_ref, cnt_ref, carry_ref):
    @pl.when(pl.program_id(0) == 0)
    def _():
        carry_ref[...] = jnp.zeros_like(carry_ref)

    x = x_ref[...]
    x_hi = x.astype(BF16)
    x_lo = (x - x_hi.astype(F32)).astype(BF16)
    parts = (jnp.dot(x_hi, wr_ref[...], preferred_element_type=F32)
             + jnp.dot(x_lo, wr_ref[...], preferred_element_type=F32))
    logits = parts + pltpu.roll(parts, LANES - N_EXPERTS, 1)
    lane = lax.broadcasted_iota(jnp.int32, logits.shape, 1)
    logits = jnp.where(lane < N_EXPERTS, logits, -jnp.inf)
    m1 = jnp.max(logits, axis=-1, keepdims=True)
    i1 = jnp.min(jnp.where(logits == m1, lane, LANES), axis=-1, keepdims=True)
    rest = jnp.where(lane == i1, -jnp.inf, logits)
    m2 = jnp.max(rest, axis=-1, keepdims=True)
    i2 = jnp.min(jnp.where(rest == m2, lane, LANES), axis=-1, keepdims=True)
    e2 = jnp.exp(m2 - m1)
    w1 = 1.0 / (1.0 + e2)
    w2 = e2 * w1
    hit1 = lane == i1
    hit2 = lane == i2
    onehot = jnp.where(jnp.logical_or(hit1, hit2), 1.0, 0.0)
    before = jnp.dot(tri_ref[...], onehot.astype(BF16), preferred_element_type=F32) + carry_ref[...]
    r1 = jnp.sum(jnp.where(hit1, before, 0.0), axis=-1, keepdims=True)
    r2 = jnp.sum(jnp.where(hit2, before, 0.0), axis=-1, keepdims=True)
    carry_ref[...] = carry_ref[...] + jnp.sum(onehot, axis=0, keepdims=True)
    cnt_ref[...] = carry_ref[...]
    meta = jnp.where(lane == 0, i1.astype(F32), 0.0)
    meta = jnp.where(lane == 1, i2.astype(F32), meta)
    meta = jnp.where(lane == 2, r1, meta)
    meta = jnp.where(lane == 3, r2, meta)
    meta = jnp.where(lane == 4, w1, meta)
    meta = jnp.where(lane == 5, w2, meta)
    meta_ref[...] = meta


def _router(x, w_router):
    n, d = x.shape
    tm = min(ROUTER_TM, n)
    w_hi = w_router.astype(BF16)
    w_lo = (w_router.astype(F32) - w_hi.astype(F32)).astype(BF16)
    wr = jnp.zeros((d, LANES), BF16).at[:, :N_EXPERTS].set(w_hi).at[:, N_EXPERTS:2 * N_EXPERTS].set(w_lo)
    tri = jnp.asarray(np.tril(np.ones((tm, tm), np.float32), -1), BF16)
    return pl.pallas_call(
        _router_kernel,
        grid=(n // tm,),
        in_specs=[pl.BlockSpec((tm, d), lambda i: (i, 0)),
                  pl.BlockSpec((d, LANES), lambda i: (0, 0)),
                  pl.BlockSpec((tm, tm), lambda i: (0, 0))],
        out_specs=[pl.BlockSpec((tm, LANES), lambda i: (i, 0)),
                   pl.BlockSpec((1, LANES), lambda i: (0, 0))],
        out_shape=[jax.ShapeDtypeStruct((n, LANES), F32), jax.ShapeDtypeStruct((1, LANES), F32)],
        scratch_shapes=[pltpu.VMEM((1, LANES), F32)],
        compiler_params=_cparams(("arbitrary",)),
        name="router",
    )(x, wr, tri)


MOVE_TM = 256
ISSUE_UNROLL = 8


def _combine_kernel(d1_ref, d2_ref, ys_hbm, x_ref, meta_ref, g_ref, b_ref, o_ref, buf_ref, sem, *, tm):
    i = pl.program_id(0)
    last = pl.num_programs(0) - 1
    slot = lax.rem(i, 2)

    def start_row(tile, into, t):
        row = tile * tm + t
        _row_copy(ys_hbm, buf_ref.at[into, 0], sem.at[into], d1_ref[row], t).start()
        _row_copy(ys_hbm, buf_ref.at[into, 1], sem.at[into], d2_ref[row], t).start()

    def wait_slot(which):
        for k in range(2):
            pltpu.make_async_copy(ys_hbm.at[pl.ds(0, tm)], buf_ref.at[which, k], sem.at[which]).wait()

    @pl.when(i == 0)
    def _():
        lax.fori_loop(0, tm, lambda t, c: (start_row(0, 0, t), c)[1], 0, unroll=ISSUE_UNROLL)

    wait_slot(slot)
    nxt = jnp.minimum(i + 1, last)
    for t in range(tm):
        start_row(nxt, 1 - slot, t)
    meta = meta_ref[...]
    lane = lax.broadcasted_iota(jnp.int32, meta.shape, 1)
    w1 = jnp.sum(jnp.where(lane == 4, meta, 0.0), axis=-1, keepdims=True)
    w2 = jnp.sum(jnp.where(lane == 5, meta, 0.0), axis=-1, keepdims=True)
    f = w1 * buf_ref[slot, 0] + w2 * buf_ref[slot, 1]
    o_ref[...] = _layer_norm(ALPHA * x_ref[...] + f, g_ref[...], b_ref[...])

    @pl.when(i == last)
    def _():
        wait_slot(1 - slot)


def _combine(ys, x, meta, d1, d2, g, b):
    n, d = x.shape
    tm = min(MOVE_TM, n)
    grid_spec = pltpu.PrefetchScalarGridSpec(
        num_scalar_prefetch=2,
        grid=(n // tm,),
        in_specs=[pl.BlockSpec(memory_space=pl.ANY),
                  pl.BlockSpec((tm, d), lambda i, a, c: (i, 0)),
                  pl.BlockSpec((tm, LANES), lambda i, a, c: (i, 0)),
                  pl.BlockSpec((1, d), lambda i, a, c: (0, 0)),
                  pl.BlockSpec((1, d), lambda i, a, c: (0, 0))],
        out_specs=pl.BlockSpec((tm, d), lambda i, a, c: (i, 0)),
        scratch_shapes=[pltpu.VMEM((2, 2, tm, d), F32), pltpu.SemaphoreType.DMA((2,))],
    )
    return pl.pallas_call(
        functools.partial(_combine_kernel, tm=tm),
        grid_spec=grid_spec,
        out_shape=jax.ShapeDtypeStruct((n, d), F32),
        compiler_params=_cparams(("arbitrary",)),
        name="moe_combine",
    )(d1, d2, ys, x, meta, g, b)


def _moe(x, w_router, w_gate, w_up, w_down, g, b):
    n, d = x.shape
    tm = min(FFN_TM, n)
    meta, cnt = _router(x, w_router)
    i1 = meta[:, 0].astype(jnp.int32)
    i2 = meta[:, 1].astype(jnp.int32)
    counts = cnt[0, :N_EXPERTS].astype(jnp.int32)
    tiles = (counts + tm - 1) // tm
    tile_end = jnp.cumsum(tiles)
    group_start = (tile_end - tiles) * tm
    experts = jnp.arange(N_EXPERTS, dtype=jnp.int32)[None, :]
    start_of = lambda idx: jnp.sum(jnp.where(idx[:, None] == experts, group_start[None, :], 0), axis=1)
    d1 = start_of(i1) + meta[:, 2].astype(jnp.int32)
    d2 = start_of(i2) + meta[:, 3].astype(jnp.int32)
    max_tiles = (2 * n) // tm + N_EXPERTS
    tile_ids = jnp.arange(max_tiles, dtype=jnp.int32)[:, None]
    eid = jnp.minimum(jnp.sum((tile_ids >= tile_end[None, :]).astype(jnp.int32), axis=1), N_EXPERTS - 1)
    nact = tile_end[-1:].astype(jnp.int32)
    token = jnp.arange(n, dtype=jnp.int32)
    src = jnp.zeros((max_tiles * tm,), jnp.int32).at[jnp.concatenate([d1, d2])].set(
        jnp.concatenate([token, token]), unique_indices=True)
    ys = _ffn_grouped(x, src, eid.astype(jnp.int32), nact, w_gate, w_up, w_down)
    return _combine(ys, x, meta, d1, d2, g, b)


def _lambda_init(layer_idx):
    return 0.8 - 0.6 * math.exp(-0.3 * layer_idx)


def _static_tables(s):
    c = C_CHUNK
    ang = jnp.repeat(1.0 / (10000.0 ** jnp.linspace(0.0, 1.0, C_QK_DIM // 2, dtype=F32)), 2)
    ang = jnp.arange(s, dtype=F32)[:, None] * ang[None, :]
    sign = jnp.where(jnp.arange(C_QK_DIM) % 2 == 0, -1.0, 1.0).astype(F32)
    sin_t = jnp.tile(jnp.sin(ang) * sign[None, :], (1, 2))
    cos_t = jnp.tile(jnp.cos(ang), (1, 2))
    log_g = jnp.log(1.0 - jnp.exp2(-5.0 - jnp.arange(C_HEADS, dtype=F32)))
    pos = jnp.arange(c)
    rel = (pos[:, None] - pos[None, :]).astype(F32)
    decay = jnp.where((rel >= 0)[None], jnp.exp(jnp.maximum(rel, 0.0)[None] * log_g[:, None, None]), 0.0)
    zeta = jnp.exp((c - 1 - pos).astype(F32)[:, None] * log_g[None, :])
    xi = jnp.exp((pos + 1).astype(F32)[:, None] * log_g[None, :])
    per_pair = lambda t: jnp.repeat(t.T.reshape(C_HEADS // 2, 2, c), C_QK_DIM, axis=1).transpose(0, 2, 1)
    gch = jnp.broadcast_to(jnp.exp(c * log_g)[:, None, None], (C_HEADS, 1, LANES))
    dist_a = np.arange(A_SUB)[None, :] - np.arange(A_BLOCK)[:, None]
    types = [jnp.where(dist_a + off >= 0, _rel_bucket(jnp.asarray(dist_a + off)), REL_BUCKETS)
             for off in (0, A_SUB, A_BLOCK)]
    types.append(jnp.full(dist_a.shape, REL_BUCKETS - 1, jnp.int32))
    bkt_a = jnp.concatenate([jnp.tile(b, (1, 2)) for b in types], axis=0).astype(jnp.int32)
    w = B_WINDOW
    dist = np.arange(w)[:, None] + w - np.arange(2 * w)[None, :]
    band = (dist >= 0) & (dist < w)
    has_prev = np.stack([np.broadcast_to(np.arange(2 * w)[None, :] >= w, band.shape), np.ones_like(band)])
    bkt_b = jnp.where(band[None] & has_prev, _rel_bucket(jnp.asarray(dist))[None], REL_BUCKETS)
    bkt_b = bkt_b.transpose(0, 2, 1).reshape(2 * 2 * w, w).astype(jnp.int32)
    return sin_t, cos_t, decay, per_pair(zeta), per_pair(xi), gch, bkt_a, bkt_b


def kernel(x, w_in, rel_bias, a_lambda, a_subln_g, b_sinks, w_out, ln_mix_g, ln_mix_b, ln_ffn_g, ln_ffn_b,
           dense_w_gate, dense_w_up, dense_w_down, moe_router, moe_w_gate, moe_w_up, moe_w_down):
    bsz, s, d = x.shape
    n = bsz * s
    sin_t, cos_t, decay, zeta_t, xi_t, gch, bkt_a, bkt_b = _static_tables(s)
    tab_t = rel_bias.astype(F32).T
    bias_a = _bias_lookup(tab_t[:A_HEADS] * LOG2E, bkt_a).reshape(A_HEADS, A_BIAS_TYPES, A_BLOCK, 2 * A_SUB)
    group = B_HEADS // B_KV_HEADS
    bias_b = _bias_lookup(tab_t[A_HEADS:] * LOG2E, bkt_b)
    bias_b = bias_b.reshape(B_KV_HEADS, group, 2, 2 * B_WINDOW, B_WINDOW).transpose(0, 2, 3, 1, 4)
    bias_b = bias_b.reshape(B_KV_HEADS, 2, 2 * B_WINDOW, group * B_WINDOW)
    xf = x.reshape(n, d).astype(F32)
    w_in_bf16, w_out_bf16 = w_in[0].astype(BF16), w_out[0].astype(BF16)
    for l in range(DEPTH):
        w_in_l = jnp.concatenate([w_in_bf16[:, a:b] for a, b in _PERM_RUNS]
                                 + [jnp.zeros((d, PROJ_PAD - PROJ_WIDTH), BF16)], axis=1)
        proj = _in_proj(xf, w_in_l).reshape(bsz, s, PROJ_PAD)
        ya = _attn_a(proj, bias_a, a_lambda[l].astype(F32), a_subln_g[l].astype(F32).reshape(1, A_V_DIM),
                     _lambda_init(l))
        sink_rows = jnp.repeat(b_sinks[l].astype(F32).reshape(B_KV_HEADS, 1, group), B_WINDOW, axis=2)
        yb, yc = _swa_retention(proj, bias_b, sink_rows, sin_t, cos_t, decay, zeta_t, xi_t, gch)
        g_mix = ln_mix_g[l].astype(F32).reshape(1, d)
        b_mix = ln_mix_b[l].astype(F32).reshape(1, d)
        j = l // 2
        dense = (dense_w_gate[j], dense_w_up[j], dense_w_down[j]) if l % 2 == 0 else ()
        ahead = (w_in[l + 1], w_out[l + 1]) if l + 1 < DEPTH else ()
        xf, cast = _out_proj_ln(ya.reshape(n, A_WIDTH), yb.reshape(n, B_WIDTH), yc.reshape(n, C_WIDTH),
                                w_out_bf16, xf, g_mix, b_mix, dense + ahead)
        dense_bf16 = cast[:len(dense)]
        if ahead:
            w_in_bf16, w_out_bf16 = cast[len(dense):]
        g_ffn = ln_ffn_g[l].astype(F32).reshape(1, d)
        b_ffn = ln_ffn_b[l].astype(F32).reshape(1, d)
        if l % 2 == 0:
            experts = (moe_w_gate[j], moe_w_up[j], moe_w_down[j]) if l + 1 < DEPTH else ()
            xf, moe_bf16 = _ffn(xf, *dense_bf16, g_ffn, b_ffn, experts)
        else:
            assert l > 0, "expert weights are cast by the preceding dense layer"
            xf = _moe(xf, moe_router[j], *moe_bf16, g_ffn, b_ffn)
    return xf.reshape(bsz, s, d).astype(x.dtype)
```

```python
import functools
import math

import jax
import jax.numpy as jnp
import numpy as np
from jax import lax
from jax.experimental import pallas as pl
from jax.experimental.pallas import tpu as pltpu

F32 = jnp.float32
BF16 = jnp.bfloat16

D_MODEL = 2048
DEPTH = 2
A_HEADS = 6
A_QK_DIM = 64
A_V_DIM = 128
B_HEADS = 12
B_KV_HEADS = 3
B_HEAD_DIM = 64
B_WINDOW = 128
C_HEADS = 4
C_QK_DIM = 64
C_V_DIM = 128
C_CHUNK = 128
A_WIDTH = A_HEADS * A_V_DIM
B_WIDTH = B_HEADS * B_HEAD_DIM
C_WIDTH = C_HEADS * C_V_DIM
REL_BUCKETS = 32
REL_MAX_DIST = 128
D_FF = 5632
N_EXPERTS = 8
ALPHA = (2.0 * DEPTH) ** 0.25
LN_EPS = 1e-5
NEG = -1e30

LANES = 128
VMEM_LIMIT = 56 * 1024 * 1024

_REF_SIZES = [768, 768, 768, 768, 192, 192, 256, 256, 512, 512]
_REF_OFF = [int(v) for v in np.concatenate([[0], np.cumsum(_REF_SIZES)[:-1]])]
PROJ_WIDTH = int(sum(_REF_SIZES))
OFF_AQ, OFF_AK, OFF_AV, OFF_BQ, OFF_CV, OFF_CG, OFF_CQ, OFF_CK, OFF_BKV = (
    0, 768, 1536, 2304, 3072, 3584, 4096, 4352, 4608)


def _proj_perm():
    aq, ak, av, bq, bk, bv, cq, ck, cv, cg = [np.arange(o, o + s) for o, s in zip(_REF_OFF, _REF_SIZES)]
    bkv = np.concatenate([np.concatenate([bk[g * 64:(g + 1) * 64], bv[g * 64:(g + 1) * 64]])
                          for g in range(B_KV_HEADS)])
    perm = np.concatenate([aq, ak, av, bq, cv, cg, cq, ck, bkv])
    assert perm.shape[0] == PROJ_WIDTH
    return perm


def _perm_runs():
    perm = _proj_perm()
    cuts = np.flatnonzero(np.diff(perm) != 1) + 1
    return [(int(r[0]), int(r[-1]) + 1) for r in np.split(perm, cuts)]


_PERM_RUNS = _perm_runs()


def _cparams(sem):
    return pltpu.CompilerParams(dimension_semantics=sem, vmem_limit_bytes=VMEM_LIMIT)


def _layer_norm(z, g, b):
    mu = jnp.mean(z, axis=-1, keepdims=True)
    zc = z - mu
    var = jnp.mean(zc * zc, axis=-1, keepdims=True)
    return zc * lax.rsqrt(var + LN_EPS) * g + b


def _silu(x):
    return x / (1.0 + jnp.exp(-x))


def _dot_nt(a, b):
    return lax.dot_general(a, b, (((1,), (1,)), ((), ())), preferred_element_type=F32)


def _dot_tn(a, b):
    return lax.dot_general(a, b, (((0,), (0,)), ((), ())), preferred_element_type=F32)


def _bias_kernel(tab_ref, bkt_ref, o_ref):
    h = pl.program_id(0)
    bkt = bkt_ref[...]
    acc = jnp.full(bkt.shape, NEG, F32)
    for b in range(REL_BUCKETS):
        acc = jnp.where(bkt == b, tab_ref[h, b], acc)
    o_ref[0] = acc


def _bias_lookup(tab_t, bkt):
    nh = tab_t.shape[0]
    r, c = bkt.shape
    return pl.pallas_call(
        _bias_kernel,
        grid=(nh,),
        in_specs=[pl.BlockSpec(memory_space=pltpu.SMEM),
                  pl.BlockSpec((r, c), lambda h: (0, 0))],
        out_specs=pl.BlockSpec((1, r, c), lambda h: (h, 0, 0)),
        out_shape=jax.ShapeDtypeStruct((nh, r, c), F32),
        compiler_params=_cparams(("arbitrary",)),
        name="bias_lookup",
    )(tab_t, bkt)


def _rel_bucket(dist):
    max_exact = REL_BUCKETS // 2
    d = jnp.maximum(dist, 0)
    ratio = jnp.maximum(d, 1).astype(F32) / max_exact
    large = max_exact + (jnp.log(ratio) / math.log(REL_MAX_DIST / max_exact)
                         * (REL_BUCKETS - max_exact)).astype(jnp.int32)
    large = jnp.minimum(large, REL_BUCKETS - 1)
    return jnp.where(d < max_exact, d, large)


MXU_COLS = 256
IN_PROJ_COL_STEPS = 4
PROJ_PAD = -(-PROJ_WIDTH // (IN_PROJ_COL_STEPS * MXU_COLS)) * (IN_PROJ_COL_STEPS * MXU_COLS)


def _in_proj_kernel(x_ref, w_ref, o_ref, xb_ref):
    @pl.when(pl.program_id(1) == 0)
    def _():
        xb_ref[...] = x_ref[...].astype(BF16)

    o_ref[...] = jnp.dot(xb_ref[...], w_ref[...], preferred_element_type=F32).astype(BF16)


def _in_proj(x, w):
    n, d = x.shape
    p = w.shape[1]
    tm = min(1024, n)
    tn = p // IN_PROJ_COL_STEPS
    return pl.pallas_call(
        _in_proj_kernel,
        grid=(n // tm, p // tn),
        in_specs=[pl.BlockSpec((tm, d), lambda i, j: (i, 0)),
                  pl.BlockSpec((d, tn), lambda i, j: (0, j))],
        out_specs=pl.BlockSpec((tm, tn), lambda i, j: (i, j)),
        out_shape=jax.ShapeDtypeStruct((n, p), BF16),
        scratch_shapes=[pltpu.VMEM((tm, d), BF16)],
        compiler_params=_cparams(("parallel", "arbitrary")),
        name="in_proj",
    )(x, w)


A_BLOCK = 256
A_SUB = 128
A_QROWS = 512
A_PASSES = 4
A_BIAS_TYPES = 4
LOG2E = math.log2(math.e)


def _attn_a_kernel(q_ref, k_ref, v_ref, bias_ref, lam_ref, g_ref, o_ref, qt_ref, s_ref, p_ref, acc_ref, *,
                   lam_init):
    passes = q_ref.shape[1] // A_QROWS
    lp = lam_ref[...]
    lam = (jnp.exp(jnp.sum(lp[0:1] * lp[1:2], axis=-1, keepdims=True))
           - jnp.exp(jnp.sum(lp[2:3] * lp[3:4], axis=-1, keepdims=True)) + lam_init)
    for r in range(passes):
        _attn_a_pass(pl.program_id(2) * passes + r, q_ref.at[0, r * A_QROWS:(r + 1) * A_QROWS], k_ref, v_ref,
                     bias_ref, lam, g_ref, o_ref.at[0, r * A_QROWS:(r + 1) * A_QROWS],
                     qt_ref.at[r], s_ref.at[r], p_ref.at[r], acc_ref.at[r], lam_init)


def _attn_a_pass(i, q_ref, k_ref, v_ref, bias_ref, lam, g_ref, o_ref, qt_ref, s_ref, p_ref, acc_ref, lam_init):
    t = A_BLOCK
    nch = A_QROWS // A_SUB
    per_key_block = t // A_SUB
    lane = lax.broadcasted_iota(jnp.int32, (A_SUB, LANES), 1)
    for c in range(nch):
        q = q_ref[c * A_SUB:(c + 1) * A_SUB, :].astype(F32) * (A_QK_DIM ** -0.5 * LOG2E)
        qq = jnp.concatenate([jnp.where(lane < A_QK_DIM, q, 0.0), jnp.where(lane >= A_QK_DIM, q, 0.0)], axis=0)
        qt_ref[c] = qq.T.astype(BF16)
    first_block = i * (nch // per_key_block)
    n_blocks = first_block + nch // per_key_block

    all_chains = tuple(range(nch))
    late_chains = all_chains[per_key_block:]

    def scores_into(slot, j, chains):
        kj = k_ref[0, pl.ds(pl.multiple_of(j * t, t), t), :]
        for c in chains:
            back = first_block + c // per_key_block - j
            near = (0, 2) if c % per_key_block == 0 else (1, 3)
            kind = jnp.where(back == 0, near[0], jnp.where(back == 1, near[1], 3))
            s_ref[slot, c] = jnp.dot(kj, qt_ref[c], preferred_element_type=F32) + bias_ref[0, kind]

    def values_from(slot, j, alphas, chains):
        vj = v_ref[0, pl.ds(pl.multiple_of(j * t, t), t), :]
        for c in chains:
            acc_ref[c] = alphas[c] * acc_ref[c] + _dot_tn(vj, p_ref[slot, c])

    def sub_step(j, cur, carry, next_chains, chains):
        if next_chains:
            scores_into(1 - cur, j + 1, next_chains)
        new = list(carry)
        for c in chains:
            m_old, l_old, _ = carry[c]
            s = s_ref[cur, c]
            m_new = jnp.maximum(m_old, jnp.max(s, axis=0, keepdims=True))
            p = jnp.exp2(s - m_new)
            new[c] = (m_new, jnp.exp2(m_old - m_new) * l_old + jnp.sum(p, axis=0, keepdims=True),
                      jnp.exp2(m_old - m_new))
            p_ref[cur, c] = p.astype(BF16)
        values_from(1 - cur, jnp.maximum(j - 1, 0), [a for _, _, a in carry], all_chains)
        return tuple(new)

    def pair(jj, carry):
        carry = sub_step(2 * jj, 0, carry, all_chains, all_chains)
        return sub_step(2 * jj + 1, 1, carry, all_chains, all_chains)

    scores_into(0, 0, all_chains)
    p_ref[1] = jnp.zeros(p_ref.shape[1:], BF16)
    acc_ref[...] = jnp.zeros_like(acc_ref)
    init = tuple((jnp.full((1, 2 * A_SUB), NEG, F32), jnp.zeros((1, 2 * A_SUB), F32),
                  jnp.ones((1, 2 * A_SUB), F32)) for _ in range(nch))
    fin = lax.fori_loop(0, n_blocks // 2 - 1, pair, init)
    fin = sub_step(n_blocks - 2, 0, fin, late_chains, all_chains)
    fin = sub_step(n_blocks - 1, 1, fin, (), late_chains)
    values_from(1, n_blocks - 1, [a for _, _, a in fin], late_chains)
    for c, (_, l_fin, _) in enumerate(fin):
        o_all = acc_ref[c] / l_fin
        o = (o_all[:, :A_SUB] - lam * o_all[:, A_SUB:]).T
        o = o * lax.rsqrt(jnp.mean(o * o, axis=-1, keepdims=True) + LN_EPS) * g_ref[...]
        o_ref[c * A_SUB:(c + 1) * A_SUB, :] = (o * (1.0 - lam_init)).astype(BF16)


def _attn_a(proj, bias_a, lam_params, subln_g, lam_init):
    bsz, s, _ = proj.shape
    passes = min(A_PASSES, s // A_QROWS)
    t = A_QROWS * passes
    nch = A_QROWS // A_SUB
    kb, vb = OFF_AK // LANES, OFF_AV // LANES
    return pl.pallas_call(
        functools.partial(_attn_a_kernel, lam_init=lam_init),
        grid=(bsz, A_HEADS, s // t),
        in_specs=[pl.BlockSpec((1, t, LANES), lambda b, h, i: (b, i, h)),
                  pl.BlockSpec((1, s, LANES), lambda b, h, i: (b, 0, kb + h)),
                  pl.BlockSpec((1, s, LANES), lambda b, h, i: (b, 0, vb + h)),
                  pl.BlockSpec((1, A_BIAS_TYPES, A_BLOCK, 2 * A_SUB), lambda b, h, i: (h, 0, 0, 0)),
                  pl.BlockSpec((4, A_QK_DIM), lambda b, h, i: (0, 0)),
                  pl.BlockSpec((1, A_V_DIM), lambda b, h, i: (0, 0))],
        out_specs=pl.BlockSpec((1, t, LANES), lambda b, h, i: (b, i, h)),
        out_shape=jax.ShapeDtypeStruct((bsz, s, A_WIDTH), BF16),
        scratch_shapes=[pltpu.VMEM((passes, nch, LANES, 2 * A_SUB), BF16),
                        pltpu.VMEM((passes, 2, nch, A_BLOCK, 2 * A_SUB), F32),
                        pltpu.VMEM((passes, 2, nch, A_BLOCK, 2 * A_SUB), BF16),
                        pltpu.VMEM((passes, nch, A_V_DIM, 2 * A_SUB), F32)],
        compiler_params=_cparams(("parallel", "parallel", "arbitrary")),
        name="attn_a",
    )(proj, proj, proj, bias_a, lam_params, subln_g)


def _swa_block(q_ref, kvp_ref, kvc_ref, bias_ref, sink_ref, o_ref, table):
    w = B_WINDOW
    d = B_HEAD_DIM
    group = B_HEADS // B_KV_HEADS
    row = lax.broadcasted_iota(jnp.int32, (LANES, w), 0)
    zeros = jnp.zeros((LANES - d, w), F32)
    kvs, q_ts = [], []
    for g in range(B_KV_HEADS):
        kvs.append(jnp.concatenate([kvp_ref[:, g * LANES:(g + 1) * LANES],
                                    kvc_ref[:, g * LANES:(g + 1) * LANES]], axis=0))
        cols = []
        for pair in range(group // 2):
            blk = g * (group // 2) + pair
            t = (q_ref[:, blk * LANES:(blk + 1) * LANES].astype(F32) * (d ** -0.5 * LOG2E)).T
            cols += [jnp.where(row < d, t, 0.0), jnp.concatenate([t[d:], zeros], axis=0)]
        q_ts.append(jnp.concatenate(cols, axis=1).astype(BF16))
    scores = [jnp.dot(kv, q_t, preferred_element_type=F32) + bias_ref[g, table]
              for g, (kv, q_t) in enumerate(zip(kvs, q_ts))]
    soft = []
    for g, s in enumerate(scores):
        sink = sink_ref[g] * LOG2E
        m = jnp.maximum(jnp.max(s, axis=0, keepdims=True), sink)
        e = jnp.exp2(s - m)
        soft.append((e.astype(BF16), 1.0 / (jnp.sum(e, axis=0, keepdims=True) + jnp.exp2(sink - m))))
    outs = [_dot_tn(kv, e) * inv for kv, (e, inv) in zip(kvs, soft)]
    for g, o_t in enumerate(outs):
        for pair in range(group // 2):
            blk = g * (group // 2) + pair
            both = jnp.concatenate([o_t[d:, (2 * pair) * w:(2 * pair + 1) * w],
                                    o_t[d:, (2 * pair + 1) * w:(2 * pair + 2) * w]], axis=0)
            o_ref[:, blk * LANES:(blk + 1) * LANES] = both.T.astype(BF16)


def _ret_block(q_ref, k_ref, v_ref, g_ref, sin_ref, cos_ref, decay_ref, zeta_ref, xi_ref, gch_ref, o_ref,
               state_ref, first):
    c = C_CHUNK

    @pl.when(first)
    def _():
        state_ref[...] = jnp.zeros_like(state_ref)

    lane = lax.broadcasted_iota(jnp.int32, (c, LANES), 1)
    even = (lane & 1) == 0
    sn = sin_ref[...]
    cs = cos_ref[...]

    def rope(x):
        swapped = jnp.where(even, pltpu.roll(x, LANES - 1, 1), pltpu.roll(x, 1, 1))
        return x * cs + swapped * sn

    qk = []
    for pair in range(C_HEADS // 2):
        q = rope(q_ref[:, pair * LANES:(pair + 1) * LANES].astype(F32))
        k = rope(k_ref[:, pair * LANES:(pair + 1) * LANES].astype(F32) * (C_QK_DIM ** -0.5))
        qk.append((q, q * xi_ref[pair], k.astype(BF16), (k * zeta_ref[pair]).astype(BF16)))
    masked = []
    for h in range(C_HEADS):
        q, qx, _, _ = qk[h // 2]
        in_head = (lane < C_QK_DIM) if h % 2 == 0 else (lane >= C_QK_DIM)
        masked.append((jnp.where(in_head, q, 0.0).astype(BF16), jnp.where(in_head, qx, 0.0).astype(BF16)))
    inner = [(_dot_nt(masked[h][0], qk[h // 2][2]) * decay_ref[h]).astype(BF16) for h in range(C_HEADS)]
    outs = []
    for h in range(C_HEADS):
        vh = v_ref[:, h * LANES:(h + 1) * LANES]
        st = state_ref[h]
        outs.append(jnp.dot(inner[h], vh, preferred_element_type=F32)
                    + jnp.dot(masked[h][1], st.astype(BF16), preferred_element_type=F32))
        state_ref[h] = st * gch_ref[h] + _dot_tn(qk[h // 2][3], vh)
    for h, o in enumerate(outs):
        mu = jnp.mean(o, axis=-1, keepdims=True)
        oc = o - mu
        o = oc * lax.rsqrt(jnp.mean(oc * oc, axis=-1, keepdims=True) + LN_EPS)
        gate = g_ref[:, h * LANES:(h + 1) * LANES].astype(F32)
        o_ref[:, h * LANES:(h + 1) * LANES] = (_silu(gate) * o).astype(BF16)


BC_BLOCKS = 4


def _swa_ret_kernel(q_ref, kvp_ref, kvc_ref, bias_ref, sink_ref, cq_ref, ck_ref, cv_ref, cg_ref, sin_ref, cos_ref,
                    decay_ref, zeta_ref, xi_ref, gch_ref, yb_ref, yc_ref, state_ref):
    c = C_CHUNK
    blocks = q_ref.shape[1] // c
    first_block = pl.program_id(1) * blocks
    for r in range(blocks):
        rows = slice(r * c, (r + 1) * c)
        prev = kvp_ref.at[0] if r == 0 else kvc_ref.at[0, (r - 1) * c:r * c]
        _swa_block(q_ref.at[0, rows], prev, kvc_ref.at[0, rows], bias_ref, sink_ref, yb_ref.at[0, rows],
                   jnp.minimum(first_block + r, 1))
        _ret_block(cq_ref.at[0, rows], ck_ref.at[0, rows], cv_ref.at[0, rows], cg_ref.at[0, rows],
                   sin_ref.at[rows], cos_ref.at[rows], decay_ref, zeta_ref, xi_ref, gch_ref, yc_ref.at[0, rows],
                   state_ref, first_block + r == 0)


def _swa_retention(proj, bias_b, sink_rows, sin_t, cos_t, decay, zeta_t, xi_t, gch):
    assert B_WINDOW == C_CHUNK
    bsz, s, _ = proj.shape
    c = C_CHUNK
    blocks = min(BC_BLOCKS, s // c)
    t = blocks * c
    qw = C_HEADS * C_QK_DIM
    kvw = 2 * B_KV_HEADS * B_HEAD_DIM
    qb, kvb = OFF_BQ // B_WIDTH, OFF_BKV // kvw
    cols = (B_HEADS // B_KV_HEADS) * c
    return pl.pallas_call(
        _swa_ret_kernel,
        grid=(bsz, s // t),
        in_specs=[pl.BlockSpec((1, t, B_WIDTH), lambda b, n: (b, n, qb)),
                  pl.BlockSpec((1, c, kvw), lambda b, n: (b, jnp.maximum(n * blocks - 1, 0), kvb)),
                  pl.BlockSpec((1, t, kvw), lambda b, n: (b, n, kvb)),
                  pl.BlockSpec((B_KV_HEADS, 2, 2 * c, cols), lambda b, n: (0, 0, 0, 0)),
                  pl.BlockSpec((B_KV_HEADS, 1, cols), lambda b, n: (0, 0, 0)),
                  pl.BlockSpec((1, t, qw), lambda b, n: (b, n, OFF_CQ // qw)),
                  pl.BlockSpec((1, t, qw), lambda b, n: (b, n, OFF_CK // qw)),
                  pl.BlockSpec((1, t, C_WIDTH), lambda b, n: (b, n, OFF_CV // C_WIDTH)),
                  pl.BlockSpec((1, t, C_WIDTH), lambda b, n: (b, n, OFF_CG // C_WIDTH)),
                  pl.BlockSpec((t, LANES), lambda b, n: (n, 0)),
                  pl.BlockSpec((t, LANES), lambda b, n: (n, 0)),
                  pl.BlockSpec((C_HEADS, c, c), lambda b, n: (0, 0, 0)),
                  pl.BlockSpec((C_HEADS // 2, c, LANES), lambda b, n: (0, 0, 0)),
                  pl.BlockSpec((C_HEADS // 2, c, LANES), lambda b, n: (0, 0, 0)),
                  pl.BlockSpec((C_HEADS, 1, LANES), lambda b, n: (0, 0, 0))],
        out_specs=[pl.BlockSpec((1, t, B_WIDTH), lambda b, n: (b, n, 0)),
                   pl.BlockSpec((1, t, C_WIDTH), lambda b, n: (b, n, 0))],
        out_shape=[jax.ShapeDtypeStruct((bsz, s, B_WIDTH), BF16), jax.ShapeDtypeStruct((bsz, s, C_WIDTH), BF16)],
        scratch_shapes=[pltpu.VMEM((C_HEADS, LANES, C_V_DIM), F32)],
        compiler_params=_cparams(("parallel", "arbitrary")),
        name="swa_retention",
    )(proj, proj, proj, bias_b, sink_rows, proj, proj, proj, proj, sin_t, cos_t, decay, zeta_t, xi_t, gch)


def _cast_riders(riders_in, riders_out):
    for src, dst in zip(riders_in, riders_out):
        dst[...] = src[...].astype(BF16)


def _row_split_riders(riders, steps):
    views = [r.reshape(r.size // r.shape[-1], r.shape[-1]) for r in riders]
    for v in views:
        assert v.shape[0] % (steps * 16) == 0, v.shape
    return views, [(v.shape[0] // steps, v.shape[1]) for v in views]


def _out_proj_kernel(ya_ref, yb_ref, yc_ref, wa_ref, wb_ref, wc_ref, x_ref, g_ref, b_ref, *rest):
    n_riders = (len(rest) - 1) // 2
    o_ref = rest[n_riders]
    half = o_ref.shape[0] // 2
    spans = [slice(0, half), slice(half, 2 * half)]
    mixes = [jnp.dot(ya_ref[r, :], wa_ref[...], preferred_element_type=F32)
             + jnp.dot(yb_ref[r, :], wb_ref[...], preferred_element_type=F32)
             + jnp.dot(yc_ref[r, :], wc_ref[...], preferred_element_type=F32) for r in spans]
    for r, mix in zip(spans, mixes):
        o_ref[r, :] = _layer_norm(ALPHA * x_ref[r, :] + mix, g_ref[...], b_ref[...])
    _cast_riders(rest[:n_riders], rest[n_riders + 1:])


def _out_proj_ln(ya, yb, yc, w, x, g, b, riders=()):
    n, d = x.shape
    tm = min(512, n)
    row = lambda i: (i, 0)
    fixed = lambda i: (0, 0)
    once = pl.Buffered(1)
    views, blocks = _row_split_riders(riders, n // tm)
    rider_specs = [pl.BlockSpec(blk, row) for blk in blocks]
    outs = pl.pallas_call(
        _out_proj_kernel,
        grid=(n // tm,),
        in_specs=[pl.BlockSpec((tm, A_WIDTH), row),
                  pl.BlockSpec((tm, B_WIDTH), row),
                  pl.BlockSpec((tm, C_WIDTH), row),
                  pl.BlockSpec((A_WIDTH, d), lambda i: (0, 0), pipeline_mode=once),
                  pl.BlockSpec((B_WIDTH, d), lambda i: (1, 0), pipeline_mode=once),
                  pl.BlockSpec((C_WIDTH, d), lambda i: ((A_WIDTH + B_WIDTH) // C_WIDTH, 0), pipeline_mode=once),
                  pl.BlockSpec((tm, d), row),
                  pl.BlockSpec((1, d), fixed),
                  pl.BlockSpec((1, d), fixed)] + rider_specs,
        out_specs=[pl.BlockSpec((tm, d), row)] + rider_specs,
        out_shape=[jax.ShapeDtypeStruct((n, d), F32)] + [jax.ShapeDtypeStruct(v.shape, BF16) for v in views],
        compiler_params=_cparams(("arbitrary",)),
        name="out_proj_ln",
    )(ya, yb, yc, w, w, w, x, g, b, *views)
    return outs[0], [o.reshape(r.shape) for o, r in zip(outs[1:], riders)]


FFN_TM = 512
FFN_TF = 512


def _swiglu_step(xb, wg, wu, wd):
    hg = jnp.dot(xb, wg, preferred_element_type=F32)
    hu = jnp.dot(xb, wu, preferred_element_type=F32)
    return jnp.dot((_silu(hg) * hu).astype(BF16), wd, preferred_element_type=F32)


def _ffn_kernel(x_ref, wg_ref, wu_ref, wd_ref, g_ref, b_ref, *rest):
    n_riders = (len(rest) - 2) // 2
    riders_in, o_ref = rest[:n_riders], rest[n_riders]
    riders_out, xb_ref = rest[n_riders + 1:-1], rest[-1]
    f = pl.program_id(1)

    @pl.when(f == 0)
    def _():
        xb = x_ref[...].astype(BF16)
        xb_ref[...] = xb
        o_ref[...] = _swiglu_step(xb, wg_ref[...], wu_ref[...], wd_ref[...])
        _cast_riders(riders_in, riders_out)

    @pl.when(f > 0)
    def _():
        o_ref[...] += _swiglu_step(xb_ref[...], wg_ref[...], wu_ref[...], wd_ref[...])
        _cast_riders(riders_in, riders_out)

    @pl.when(f == pl.num_programs(1) - 1)
    def _():
        o_ref[...] = _layer_norm(ALPHA * x_ref[...] + o_ref[...], g_ref[...], b_ref[...])


def _ffn(x, w_gate, w_up, w_down, g, b, riders=()):
    n, d = x.shape
    ff = w_gate.shape[1]
    tm = min(FFN_TM, n)
    tf = FFN_TF
    ni, nf = n // tm, ff // tf
    flat, rider_specs = [], []
    for r in riders:
        cols = r.shape[-1]
        rows = r.size // cols
        flat.append(r.reshape(rows, cols))
        if rows % (ni * nf * 16) == 0:
            rider_specs.append(pl.BlockSpec((rows // (ni * nf), cols), lambda i, f: (i * nf + f, 0)))
        else:
            assert rows % (ni * 16) == 0 and cols % (nf * LANES) == 0
            rider_specs.append(pl.BlockSpec((rows // ni, cols // nf), lambda i, f: (i, f)))
    outs = pl.pallas_call(
        _ffn_kernel,
        grid=(n // tm, ff // tf),
        in_specs=[pl.BlockSpec((tm, d), lambda i, f: (i, 0)),
                  pl.BlockSpec((d, tf), lambda i, f: (0, f)),
                  pl.BlockSpec((d, tf), lambda i, f: (0, f)),
                  pl.BlockSpec((tf, d), lambda i, f: (f, 0)),
                  pl.BlockSpec((1, d), lambda i, f: (0, 0)),
                  pl.BlockSpec((1, d), lambda i, f: (0, 0))] + rider_specs,
        out_specs=[pl.BlockSpec((tm, d), lambda i, f: (i, 0))] + rider_specs,
        out_shape=[jax.ShapeDtypeStruct((n, d), F32)] + [jax.ShapeDtypeStruct(r.shape, BF16) for r in flat],
        scratch_shapes=[pltpu.VMEM((tm, d), BF16)],
        compiler_params=_cparams(("arbitrary", "arbitrary")),
        name="ffn_ln",
    )(x, w_gate, w_up, w_down, g, b, *flat)
    return outs[0], [o.reshape(r.shape) for o, r in zip(outs[1:], riders)]


def _row_copy(src, dst, sem, s, t):
    return pltpu.make_async_copy(src.at[pl.ds(s, 1)], dst.at[pl.ds(t, 1)], sem)


def _ffn_grouped_kernel(eid_ref, nact_ref, src_ref, x_hbm, wg_ref, wu_ref, wd_ref, o_ref, rows_ref, xb_ref, sem,
                        *, tm, per_step):
    i = pl.program_id(0)
    f = pl.program_id(1)
    nact = nact_ref[0]
    active = i < nact
    slot = lax.rem(i, 2)
    buf_rows = rows_ref.shape[1]
    last_slot_row = src_ref.shape[0] - 1

    def fetch(tile, into, first_row, count):
        for r in range(count):
            row = first_row + r
            token = src_ref[jnp.minimum(tile * tm + row, last_slot_row)]
            _row_copy(x_hbm, rows_ref.at[into], sem.at[into], token, row).start()

    def wait_buffer(which):
        pltpu.make_async_copy(x_hbm.at[pl.ds(0, buf_rows)], rows_ref.at[which], sem.at[which]).wait()

    @pl.when(jnp.logical_and(i == 0, f == 0))
    def _():
        lax.fori_loop(0, buf_rows // per_step, lambda s, c: (fetch(0, 0, s * per_step, per_step), c)[1], 0)

    @pl.when(jnp.logical_and(f == 0, i <= nact))
    def _():
        wait_buffer(slot)

    @pl.when(jnp.logical_and(f == 0, jnp.logical_not(active)))
    def _():
        o_ref[...] = jnp.zeros_like(o_ref)

    @pl.when(jnp.logical_and(f == 0, active))
    def _():
        fetch(i + 1, 1 - slot, 0, per_step)
        xb = rows_ref[slot, :tm, :].astype(BF16)
        xb_ref[...] = xb
        o_ref[...] = _swiglu_step(xb, wg_ref[0], wu_ref[0], wd_ref[0])

    @pl.when(jnp.logical_and(f > 0, active))
    def _():
        fetch(i + 1, 1 - slot, f * per_step, per_step)
        o_ref[...] += _swiglu_step(xb_ref[...], wg_ref[0], wu_ref[0], wd_ref[0])


def _ffn_grouped(x, src, eid, nact, w_gate, w_up, w_down):
    n, d = x.shape
    slots = src.shape[0]
    ff = w_gate.shape[2]
    tm = min(FFN_TM, n)
    tf = FFN_TF
    nf = ff // tf
    per_step = -(-tm // nf)
    per_step += (-per_step) % 8
    assert per_step * nf <= n

    def fidx(i, f, nact_ref):
        return jnp.where(i < nact_ref[0], f, nf - 1)

    grid_spec = pltpu.PrefetchScalarGridSpec(
        num_scalar_prefetch=3,
        grid=(slots // tm, nf),
        in_specs=[pl.BlockSpec(memory_space=pl.ANY),
                  pl.BlockSpec((1, d, tf), lambda i, f, e, a, s: (e[i], 0, fidx(i, f, a))),
                  pl.BlockSpec((1, d, tf), lambda i, f, e, a, s: (e[i], 0, fidx(i, f, a))),
                  pl.BlockSpec((1, tf, d), lambda i, f, e, a, s: (e[i], fidx(i, f, a), 0))],
        out_specs=pl.BlockSpec((tm, d), lambda i, f, e, a, s: (i, 0)),
        scratch_shapes=[pltpu.VMEM((2, per_step * nf, d), F32), pltpu.VMEM((tm, d), BF16),
                        pltpu.SemaphoreType.DMA((2,))],
    )
    return pl.pallas_call(
        functools.partial(_ffn_grouped_kernel, tm=tm, per_step=per_step),
        grid_spec=grid_spec,
        out_shape=jax.ShapeDtypeStruct((slots, d), F32),
        compiler_params=_cparams(("arbitrary", "arbitrary")),
        name="ffn_grouped",
    )(eid, nact, src, x, w_gate, w_up, w_down)


ROUTER_TM = 512


def _router_kernel(x_ref, wr_ref, tri_ref, meta_ref, cnt_ref, carry_ref):
    @pl.when(pl.program_id(0) == 0)
    def _():
        carry_ref[...] = jnp.zeros_like(carry_ref)

    x = x_ref[...]
    x_hi = x.astype(BF16)
    x_lo = (x - x_hi.astype(F32)).astype(BF16)
    parts = (jnp.dot(x_hi, wr_ref[...], preferred_element_type=F32)
             + jnp.dot(x_lo, wr_ref[...], preferred_element_type=F32))
    logits = parts + pltpu.roll(parts, LANES - N_EXPERTS, 1)
    lane = lax.broadcasted_iota(jnp.int32, logits.shape, 1)
    logits = jnp.where(lane < N_EXPERTS, logits, -jnp.inf)
    m1 = jnp.max(logits, axis=-1, keepdims=True)
    i1 = jnp.min(jnp.where(logits == m1, lane, LANES), axis=-1, keepdims=True)
    rest = jnp.where(lane == i1, -jnp.inf, logits)
    m2 = jnp.max(rest, axis=-1, keepdims=True)
    i2 = jnp.min(jnp.where(rest == m2, lane, LANES), axis=-1, keepdims=True)
    e2 = jnp.exp(m2 - m1)
    w1 = 1.0 / (1.0 + e2)
    w2 = e2 * w1
    hit1 = lane == i1
    hit2 = lane == i2
    onehot = jnp.where(jnp.logical_or(hit1, hit2), 1.0, 0.0)
    before = jnp.dot(tri_ref[...], onehot.astype(BF16), preferred_element_type=F32) + carry_ref[...]
    r1 = jnp.sum(jnp.where(hit1, before, 0.0), axis=-1, keepdims=True)
    r2 = jnp.sum(jnp.where(hit2, before, 0.0), axis=-1, keepdims=True)
    carry_ref[...] = carry_ref[...] + jnp.sum(onehot, axis=0, keepdims=True)
    cnt_ref[...] = carry_ref[...]
    meta = jnp.where(lane == 0, i1.astype(F32), 0.0)
    meta = jnp.where(lane == 1, i2.astype(F32), meta)
    meta = jnp.where(lane == 2, r1, meta)
    meta = jnp.where(lane == 3, r2, meta)
    meta = jnp.where(lane == 4, w1, meta)
    meta = jnp.where(lane == 5, w2, meta)
    meta_ref[...] = meta


def _router(x, w_router):
    n, d = x.shape
    tm = min(ROUTER_TM, n)
    w_hi = w_router.astype(BF16)
    w_lo = (w_router.astype(F32) - w_hi.astype(F32)).astype(BF16)
    wr = jnp.zeros((d, LANES), BF16).at[:, :N_EXPERTS].set(w_hi).at[:, N_EXPERTS:2 * N_EXPERTS].set(w_lo)
    tri = jnp.asarray(np.tril(np.ones((tm, tm), np.float32), -1), BF16)
    return pl.pallas_call(
        _router_kernel,
        grid=(n // tm,),
        in_specs=[pl.BlockSpec((tm, d), lambda i: (i, 0)),
                  pl.BlockSpec((d, LANES), lambda i: (0, 0)),
                  pl.BlockSpec((tm, tm), lambda i: (0, 0))],
        out_specs=[pl.BlockSpec((tm, LANES), lambda i: (i, 0)),
                   pl.BlockSpec((1, LANES), lambda i: (0, 0))],
        out_shape=[jax.ShapeDtypeStruct((n, LANES), F32), jax.ShapeDtypeStruct((1, LANES), F32)],
        scratch_shapes=[pltpu.VMEM((1, LANES), F32)],
        compiler_params=_cparams(("arbitrary",)),
        name="router",
    )(x, wr, tri)


MOVE_TM = 256
ISSUE_UNROLL = 8


def _combine_kernel(d1_ref, d2_ref, ys_hbm, x_ref, meta_ref, g_ref, b_ref, o_ref, buf_ref, sem, *, tm):
    i = pl.program_id(0)
    last = pl.num_programs(0) - 1
    slot = lax.rem(i, 2)

    def start_row(tile, into, t):
        row = tile * tm + t
        _row_copy(ys_hbm, buf_ref.at[into, 0], sem.at[into], d1_ref[row], t).start()
        _row_copy(ys_hbm, buf_ref.at[into, 1], sem.at[into], d2_ref[row], t).start()

    def wait_slot(which):
        for k in range(2):
            pltpu.make_async_copy(ys_hbm.at[pl.ds(0, tm)], buf_ref.at[which, k], sem.at[which]).wait()

    @pl.when(i == 0)
    def _():
        lax.fori_loop(0, tm, lambda t, c: (start_row(0, 0, t), c)[1], 0, unroll=ISSUE_UNROLL)

    wait_slot(slot)
    nxt = jnp.minimum(i + 1, last)
    for t in range(tm):
        start_row(nxt, 1 - slot, t)
    meta = meta_ref[...]
    lane = lax.broadcasted_iota(jnp.int32, meta.shape, 1)
    w1 = jnp.sum(jnp.where(lane == 4, meta, 0.0), axis=-1, keepdims=True)
    w2 = jnp.sum(jnp.where(lane == 5, meta, 0.0), axis=-1, keepdims=True)
    f = w1 * buf_ref[slot, 0] + w2 * buf_ref[slot, 1]
    o_ref[...] = _layer_norm(ALPHA * x_ref[...] + f, g_ref[...], b_ref[...])

    @pl.when(i == last)
    def _():
        wait_slot(1 - slot)


def _combine(ys, x, meta, d1, d2, g, b):
    n, d = x.shape
    tm = min(MOVE_TM, n)
    grid_spec = pltpu.PrefetchScalarGridSpec(
        num_scalar_prefetch=2,
        grid=(n // tm,),
        in_specs=[pl.BlockSpec(memory_space=pl.ANY),
                  pl.BlockSpec((tm, d), lambda i, a, c: (i, 0)),
                  pl.BlockSpec((tm, LANES), lambda i, a, c: (i, 0)),
                  pl.BlockSpec((1, d), lambda i, a, c: (0, 0)),
                  pl.BlockSpec((1, d), lambda i, a, c: (0, 0))],
        out_specs=pl.BlockSpec((tm, d), lambda i, a, c: (i, 0)),
        scratch_shapes=[pltpu.VMEM((2, 2, tm, d), F32), pltpu.SemaphoreType.DMA((2,))],
    )
    return pl.pallas_call(
        functools.partial(_combine_kernel, tm=tm),
        grid_spec=grid_spec,
        out_shape=jax.ShapeDtypeStruct((n, d), F32),
        compiler_params=_cparams(("arbitrary",)),
        name="moe_combine",
    )(d1, d2, ys, x, meta, g, b)


def _moe(x, w_router, w_gate, w_up, w_down, g, b):
    n, d = x.shape
    tm = min(FFN_TM, n)
    meta, cnt = _router(x, w_router)
    i1 = meta[:, 0].astype(jnp.int32)
    i2 = meta[:, 1].astype(jnp.int32)
    counts = cnt[0, :N_EXPERTS].astype(jnp.int32)
    tiles = (counts + tm - 1) // tm
    tile_end = jnp.cumsum(tiles)
    group_start = (tile_end - tiles) * tm
    experts = jnp.arange(N_EXPERTS, dtype=jnp.int32)[None, :]
    start_of = lambda idx: jnp.sum(jnp.where(idx[:, None] == experts, group_start[None, :], 0), axis=1)
    d1 = start_of(i1) + meta[:, 2].astype(jnp.int32)
    d2 = start_of(i2) + meta[:, 3].astype(jnp.int32)
    max_tiles = (2 * n) // tm + N_EXPERTS
    tile_ids = jnp.arange(max_tiles, dtype=jnp.int32)[:, None]
    eid = jnp.minimum(jnp.sum((tile_ids >= tile_end[None, :]).astype(jnp.int32), axis=1), N_EXPERTS - 1)
    nact = tile_end[-1:].astype(jnp.int32)
    token = jnp.arange(n, dtype=jnp.int32)
    src = jnp.zeros((max_tiles * tm,), jnp.int32).at[jnp.concatenate([d1, d2])].set(
        jnp.concatenate([token, token]), unique_indices=True)
    ys = _ffn_grouped(x, src, eid.astype(jnp.int32), nact, w_gate, w_up, w_down)
    return _combine(ys, x, meta, d1, d2, g, b)


def _lambda_init(layer_idx):
    return 0.8 - 0.6 * math.exp(-0.3 * layer_idx)


def _static_tables(s):
    c = C_CHUNK
    ang = jnp.repeat(1.0 / (10000.0 ** jnp.linspace(0.0, 1.0, C_QK_DIM // 2, dtype=F32)), 2)
    ang = jnp.arange(s, dtype=F32)[:, None] * ang[None, :]
    sign = jnp.where(jnp.arange(C_QK_DIM) % 2 == 0, -1.0, 1.0).astype(F32)
    sin_t = jnp.tile(jnp.sin(ang) * sign[None, :], (1, 2))
    cos_t = jnp.tile(jnp.cos(ang), (1, 2))
    log_g = jnp.log(1.0 - jnp.exp2(-5.0 - jnp.arange(C_HEADS, dtype=F32)))
    pos = jnp.arange(c)
    rel = (pos[:, None] - pos[None, :]).astype(F32)
    decay = jnp.where((rel >= 0)[None], jnp.exp(jnp.maximum(rel, 0.0)[None] * log_g[:, None, None]), 0.0)
    zeta = jnp.exp((c - 1 - pos).astype(F32)[:, None] * log_g[None, :])
    xi = jnp.exp((pos + 1).astype(F32)[:, None] * log_g[None, :])
    per_pair = lambda t: jnp.repeat(t.T.reshape(C_HEADS // 2, 2, c), C_QK_DIM, axis=1).transpose(0, 2, 1)
    gch = jnp.broadcast_to(jnp.exp(c * log_g)[:, None, None], (C_HEADS, 1, LANES))
    dist_a = np.arange(A_SUB)[None, :] - np.arange(A_BLOCK)[:, None]
    types = [jnp.where(dist_a + off >= 0, _rel_bucket(jnp.asarray(dist_a + off)), REL_BUCKETS)
             for off in (0, A_SUB, A_BLOCK)]
    types.append(jnp.full(dist_a.shape, REL_BUCKETS - 1, jnp.int32))
    bkt_a = jnp.concatenate([jnp.tile(b, (1, 2)) for b in types], axis=0).astype(jnp.int32)
    w = B_WINDOW
    dist = np.arange(w)[:, None] + w - np.arange(2 * w)[None, :]
    band = (dist >= 0) & (dist < w)
    has_prev = np.stack([np.broadcast_to(np.arange(2 * w)[None, :] >= w, band.shape), np.ones_like(band)])
    bkt_b = jnp.where(band[None] & has_prev, _rel_bucket(jnp.asarray(dist))[None], REL_BUCKETS)
    bkt_b = bkt_b.transpose(0, 2, 1).reshape(2 * 2 * w, w).astype(jnp.int32)
    return sin_t, cos_t, decay, per_pair(zeta), per_pair(xi), gch, bkt_a, bkt_b


def kernel(x, w_in, rel_bias, a_lambda, a_subln_g, b_sinks, w_out, ln_mix_g, ln_mix_b, ln_ffn_g, ln_ffn_b,
           dense_w_gate, dense_w_up, dense_w_down, moe_router, moe_w_gate, moe_w_up, moe_w_down):
    bsz, s, d = x.shape
    n = bsz * s
    sin_t, cos_t, decay, zeta_t, xi_t, gch, bkt_a, bkt_b = _static_tables(s)
    tab_t = rel_bias.astype(F32).T
    bias_a = _bias_lookup(tab_t[:A_HEADS] * LOG2E, bkt_a).reshape(A_HEADS, A_BIAS_TYPES, A_BLOCK, 2 * A_SUB)
    group = B_HEADS // B_KV_HEADS
    bias_b = _bias_lookup(tab_t[A_HEADS:] * LOG2E, bkt_b)
    bias_b = bias_b.reshape(B_KV_HEADS, group, 2, 2 * B_WINDOW, B_WINDOW).transpose(0, 2, 3, 1, 4)
    bias_b = bias_b.reshape(B_KV_HEADS, 2, 2 * B_WINDOW, group * B_WINDOW)
    xf = x.reshape(n, d).astype(F32)
    w_in_bf16, w_out_bf16 = w_in[0].astype(BF16), w_out[0].astype(BF16)
    for l in range(DEPTH):
        w_in_l = jnp.concatenate([w_in_bf16[:, a:b] for a, b in _PERM_RUNS]
                                 + [jnp.zeros((d, PROJ_PAD - PROJ_WIDTH), BF16)], axis=1)
        proj = _in_proj(xf, w_in_l).reshape(bsz, s, PROJ_PAD)
        ya = _attn_a(proj, bias_a, a_lambda[l].astype(F32), a_subln_g[l].astype(F32).reshape(1, A_V_DIM),
                     _lambda_init(l))
        sink_rows = jnp.repeat(b_sinks[l].astype(F32).reshape(B_KV_HEADS, 1, group), B_WINDOW, axis=2)
        yb, yc = _swa_retention(proj, bias_b, sink_rows, sin_t, cos_t, decay, zeta_t, xi_t, gch)
        g_mix = ln_mix_g[l].astype(F32).reshape(1, d)
        b_mix = ln_mix_b[l].astype(F32).reshape(1, d)
        j = l // 2
        dense = (dense_w_gate[j], dense_w_up[j], dense_w_down[j]) if l % 2 == 0 else ()
        ahead = (w_in[l + 1], w_out[l + 1]) if l + 1 < DEPTH else ()
        xf, cast = _out_proj_ln(ya.reshape(n, A_WIDTH), yb.reshape(n, B_WIDTH), yc.reshape(n, C_WIDTH),
                                w_out_bf16, xf, g_mix, b_mix, dense + ahead)
        dense_bf16 = cast[:len(dense)]
        if ahead:
            w_in_bf16, w_out_bf16 = cast[len(dense):]
        g_ffn = ln_ffn_g[l].astype(F32).reshape(1, d)
        b_ffn = ln_ffn_b[l].astype(F32).reshape(1, d)
        if l % 2 == 0:
            experts = (moe_w_gate[j], moe_w_up[j], moe_w_down[j]) if l + 1 < DEPTH else ()
            xf, moe_bf16 = _ffn(xf, *dense_bf16, g_ffn, b_ffn, experts)
        else:
            assert l > 0, "expert weights are cast by the preceding dense layer"
            xf = _moe(xf, moe_router[j], *moe_bf16, g_ffn, b_ffn)
    return xf.reshape(bsz, s, d).astype(x.dtype)
```

```python
import functools
import math

import jax
import jax.numpy as jnp
import numpy as np
from jax import lax
from jax.experimental import pallas as pl
from jax.experimental.pallas import tpu as pltpu

F32 = jnp.float32
BF16 = jnp.bfloat16

D_MODEL = 2048
DEPTH = 2
A_HEADS = 6
A_QK_DIM = 64
A_V_DIM = 128
B_HEADS = 12
B_KV_HEADS = 3
B_HEAD_DIM = 64
B_WINDOW = 128
C_HEADS = 4
C_QK_DIM = 64
C_V_DIM = 128
C_CHUNK = 128
A_WIDTH = A_HEADS * A_V_DIM
B_WIDTH = B_HEADS * B_HEAD_DIM
C_WIDTH = C_HEADS * C_V_DIM
REL_BUCKETS = 32
REL_MAX_DIST = 128
D_FF = 5632
N_EXPERTS = 8
ALPHA = (2.0 * DEPTH) ** 0.25
LN_EPS = 1e-5
NEG = -1e30

LANES = 128
VMEM_LIMIT = 56 * 1024 * 1024

_REF_SIZES = [768, 768, 768, 768, 192, 192, 256, 256, 512, 512]
_REF_OFF = [int(v) for v in np.concatenate([[0], np.cumsum(_REF_SIZES)[:-1]])]
PROJ_WIDTH = int(sum(_REF_SIZES))
OFF_AQ, OFF_AK, OFF_AV, OFF_BQ, OFF_CV, OFF_CG, OFF_CQ, OFF_CK, OFF_BKV = (
    0, 768, 1536, 2304, 3072, 3584, 4096, 4352, 4608)


def _proj_perm():
    aq, ak, av, bq, bk, bv, cq, ck, cv, cg = [np.arange(o, o + s) for o, s in zip(_REF_OFF, _REF_SIZES)]
    bkv = np.concatenate([np.concatenate([bk[g * 64:(g + 1) * 64], bv[g * 64:(g + 1) * 64]])
                          for g in range(B_KV_HEADS)])
    perm = np.concatenate([aq, ak, av, bq, cv, cg, cq, ck, bkv])
    assert perm.shape[0] == PROJ_WIDTH
    return perm


def _perm_runs():
    perm = _proj_perm()
    cuts = np.flatnonzero(np.diff(perm) != 1) + 1
    return [(int(r[0]), int(r[-1]) + 1) for r in np.split(perm, cuts)]


_PERM_RUNS = _perm_runs()


def _cparams(sem):
    return pltpu.CompilerParams(dimension_semantics=sem, vmem_limit_bytes=VMEM_LIMIT)


def _layer_norm(z, g, b):
    mu = jnp.mean(z, axis=-1, keepdims=True)
    zc = z - mu
    var = jnp.mean(zc * zc, axis=-1, keepdims=True)
    return zc * lax.rsqrt(var + LN_EPS) * g + b


def _silu(x):
    return x / (1.0 + jnp.exp(-x))


def _dot_nt(a, b):
    return lax.dot_general(a, b, (((1,), (1,)), ((), ())), preferred_element_type=F32)


def _dot_tn(a, b):
    return lax.dot_general(a, b, (((0,), (0,)), ((), ())), preferred_element_type=F32)


def _bias_kernel(tab_ref, bkt_ref, o_ref):
    h = pl.program_id(0)
    bkt = bkt_ref[...]
    acc = jnp.full(bkt.shape, NEG, F32)
    for b in range(REL_BUCKETS):
        acc = jnp.where(bkt == b, tab_ref[h, b], acc)
    o_ref[0] = acc


def _bias_lookup(tab_t, bkt):
    nh = tab_t.shape[0]
    r, c = bkt.shape
    return pl.pallas_call(
        _bias_kernel,
        grid=(nh,),
        in_specs=[pl.BlockSpec(memory_space=pltpu.SMEM),
                  pl.BlockSpec((r, c), lambda h: (0, 0))],
        out_specs=pl.BlockSpec((1, r, c), lambda h: (h, 0, 0)),
        out_shape=jax.ShapeDtypeStruct((nh, r, c), F32),
        compiler_params=_cparams(("arbitrary",)),
        name="bias_lookup",
    )(tab_t, bkt)


def _rel_bucket(dist):
    max_exact = REL_BUCKETS // 2
    d = jnp.maximum(dist, 0)
    ratio = jnp.maximum(d, 1).astype(F32) / max_exact
    large = max_exact + (jnp.log(ratio) / math.log(REL_MAX_DIST / max_exact)
                         * (REL_BUCKETS - max_exact)).astype(jnp.int32)
    large = jnp.minimum(large, REL_BUCKETS - 1)
    return jnp.where(d < max_exact, d, large)


MXU_COLS = 256
IN_PROJ_COL_STEPS = 4
PROJ_PAD = -(-PROJ_WIDTH // (IN_PROJ_COL_STEPS * MXU_COLS)) * (IN_PROJ_COL_STEPS * MXU_COLS)


def _in_proj_kernel(x_ref, w_ref, o_ref, xb_ref):
    @pl.when(pl.program_id(1) == 0)
    def _():
        xb_ref[...] = x_ref[...].astype(BF16)

    o_ref[...] = jnp.dot(xb_ref[...], w_ref[...], preferred_element_type=F32).astype(BF16)


def _in_proj(x, w):
    n, d = x.shape
    p = w.shape[1]
    tm = min(1024, n)
    tn = p // IN_PROJ_COL_STEPS
    return pl.pallas_call(
        _in_proj_kernel,
        grid=(n // tm, p // tn),
        in_specs=[pl.BlockSpec((tm, d), lambda i, j: (i, 0)),
                  pl.BlockSpec((d, tn), lambda i, j: (0, j))],
        out_specs=pl.BlockSpec((tm, tn), lambda i, j: (i, j)),
        out_shape=jax.ShapeDtypeStruct((n, p), BF16),
        scratch_shapes=[pltpu.VMEM((tm, d), BF16)],
        compiler_params=_cparams(("parallel", "arbitrary")),
        name="in_proj",
    )(x, w)


A_BLOCK = 256
A_SUB = 128
A_BIAS_TYPES = 4
LOG2E = math.log2(math.e)


def _attn_a_kernel(q_ref, k_ref, v_ref, bias_ref, lam_ref, g_ref, o_ref, qt_ref, s_ref, p_ref, acc_ref, *,
                   lam_init):
    t = A_BLOCK
    per_key_block = t // A_SUB
    n_chains = q_ref.shape[1] // A_SUB
    n_blocks = k_ref.shape[1] // t
    diag = [g // per_key_block for g in range(n_chains)]
    lp = lam_ref[...]
    lam = (jnp.exp(jnp.sum(lp[0:1] * lp[1:2], axis=-1, keepdims=True))
           - jnp.exp(jnp.sum(lp[2:3] * lp[3:4], axis=-1, keepdims=True)) + lam_init)
    lane = lax.broadcasted_iota(jnp.int32, (A_SUB, LANES), 1)
    for g in range(n_chains):
        q = q_ref[0, g * A_SUB:(g + 1) * A_SUB, :].astype(F32) * (A_QK_DIM ** -0.5 * LOG2E)
        qq = jnp.concatenate([jnp.where(lane < A_QK_DIM, q, 0.0), jnp.where(lane >= A_QK_DIM, q, 0.0)], axis=0)
        qt_ref[g] = qq.T.astype(BF16)

    def needing(j):
        return [g for g in range(n_chains) if diag[g] >= j]

    def scores_into(j):
        kj = k_ref[0, j * t:(j + 1) * t, :]
        for g in needing(j):
            back = diag[g] - j
            near = (0, 2) if g % per_key_block == 0 else (1, 3)
            kind = near[0] if back == 0 else (near[1] if back == 1 else 3)
            s_ref[j % 2, g] = jnp.dot(kj, qt_ref[g], preferred_element_type=F32) + bias_ref[0, kind]

    def finish(g, l_fin):
        o_all = acc_ref[g] / l_fin
        o = (o_all[:, :A_SUB] - lam * o_all[:, A_SUB:]).T
        o = o * lax.rsqrt(jnp.mean(o * o, axis=-1, keepdims=True) + LN_EPS) * g_ref[...]
        o_ref[0, g * A_SUB:(g + 1) * A_SUB, :] = (o * (1.0 - lam_init)).astype(BF16)

    stats = [(jnp.full((1, 2 * A_SUB), NEG, F32), jnp.zeros((1, 2 * A_SUB), F32), None) for _ in range(n_chains)]
    scores_into(0)
    for j in range(n_blocks + 1):
        if j + 1 < n_blocks:
            scores_into(j + 1)
        before = list(stats)
        if j < n_blocks:
            for g in needing(j):
                m_old, l_old, _ = before[g]
                s = s_ref[j % 2, g]
                m_new = jnp.maximum(m_old, jnp.max(s, axis=0, keepdims=True))
                p = jnp.exp2(s - m_new)
                alpha = jnp.exp2(m_old - m_new)
                stats[g] = (m_new, alpha * l_old + jnp.sum(p, axis=0, keepdims=True), alpha)
                p_ref[j % 2, g] = p.astype(BF16)
        if j >= 1:
            vj = v_ref[0, (j - 1) * t:j * t, :]
            for g in needing(j - 1):
                pv = _dot_tn(vj, p_ref[(j - 1) % 2, g])
                acc_ref[g] = pv if j == 1 else before[g][2] * acc_ref[g] + pv
                if diag[g] == j - 1:
                    finish(g, before[g][1])


def _attn_a(proj, bias_a, lam_params, subln_g, lam_init):
    bsz, s, _ = proj.shape
    assert s % A_BLOCK == 0
    nch = s // A_SUB
    kb, vb = OFF_AK // LANES, OFF_AV // LANES
    return pl.pallas_call(
        functools.partial(_attn_a_kernel, lam_init=lam_init),
        grid=(bsz, A_HEADS),
        in_specs=[pl.BlockSpec((1, s, LANES), lambda b, h: (b, 0, h)),
                  pl.BlockSpec((1, s, LANES), lambda b, h: (b, 0, kb + h)),
                  pl.BlockSpec((1, s, LANES), lambda b, h: (b, 0, vb + h)),
                  pl.BlockSpec((1, A_BIAS_TYPES, A_BLOCK, 2 * A_SUB), lambda b, h: (h, 0, 0, 0)),
                  pl.BlockSpec((4, A_QK_DIM), lambda b, h: (0, 0)),
                  pl.BlockSpec((1, A_V_DIM), lambda b, h: (0, 0))],
        out_specs=pl.BlockSpec((1, s, LANES), lambda b, h: (b, 0, h)),
        out_shape=jax.ShapeDtypeStruct((bsz, s, A_WIDTH), BF16),
        scratch_shapes=[pltpu.VMEM((nch, LANES, 2 * A_SUB), BF16),
                        pltpu.VMEM((2, nch, A_BLOCK, 2 * A_SUB), F32),
                        pltpu.VMEM((2, nch, A_BLOCK, 2 * A_SUB), BF16),
                        pltpu.VMEM((nch, A_V_DIM, 2 * A_SUB), F32)],
        compiler_params=_cparams(("parallel", "parallel")),
        name="attn_a",
    )(proj, proj, proj, bias_a, lam_params, subln_g)


def _swa_block(q_ref, kvp_ref, kvc_ref, bias_ref, sink_ref, o_ref, table):
    w = B_WINDOW
    d = B_HEAD_DIM
    group = B_HEADS // B_KV_HEADS
    row = lax.broadcasted_iota(jnp.int32, (LANES, w), 0)
    zeros = jnp.zeros((LANES - d, w), F32)
    kvs, q_ts = [], []
    for g in range(B_KV_HEADS):
        kvs.append(jnp.concatenate([kvp_ref[:, g * LANES:(g + 1) * LANES],
                                    kvc_ref[:, g * LANES:(g + 1) * LANES]], axis=0))
        cols = []
        for pair in range(group // 2):
            blk = g * (group // 2) + pair
            t = (q_ref[:, blk * LANES:(blk + 1) * LANES].astype(F32) * (d ** -0.5 * LOG2E)).T
            cols += [jnp.where(row < d, t, 0.0), jnp.concatenate([t[d:], zeros], axis=0)]
        q_ts.append(jnp.concatenate(cols, axis=1).astype(BF16))
    scores = [jnp.dot(kv, q_t, preferred_element_type=F32) + bias_ref[g, table]
              for g, (kv, q_t) in enumerate(zip(kvs, q_ts))]
    soft = []
    for g, s in enumerate(scores):
        sink = sink_ref[g] * LOG2E
        m = jnp.maximum(jnp.max(s, axis=0, keepdims=True), sink)
        e = jnp.exp2(s - m)
        soft.append((e.astype(BF16), 1.0 / (jnp.sum(e, axis=0, keepdims=True) + jnp.exp2(sink - m))))
    outs = [_dot_tn(kv, e) * inv for kv, (e, inv) in zip(kvs, soft)]
    for g, o_t in enumerate(outs):
        for pair in range(group // 2):
            blk = g * (group // 2) + pair
            both = jnp.concatenate([o_t[d:, (2 * pair) * w:(2 * pair + 1) * w],
                                    o_t[d:, (2 * pair + 1) * w:(2 * pair + 2) * w]], axis=0)
            o_ref[:, blk * LANES:(blk + 1) * LANES] = both.T.astype(BF16)


def _ret_block(q_ref, k_ref, v_ref, g_ref, sin_ref, cos_ref, decay_ref, zeta_ref, xi_ref, gch_ref, o_ref,
               state_ref, first):
    c = C_CHUNK

    @pl.when(first)
    def _():
        state_ref[...] = jnp.zeros_like(state_ref)

    lane = lax.broadcasted_iota(jnp.int32, (c, LANES), 1)
    even = (lane & 1) == 0
    sn = sin_ref[...]
    cs = cos_ref[...]

    def rope(x):
        swapped = jnp.where(even, pltpu.roll(x, LANES - 1, 1), pltpu.roll(x, 1, 1))
        return x * cs + swapped * sn

    qk = []
    for pair in range(C_HEADS // 2):
        q = rope(q_ref[:, pair * LANES:(pair + 1) * LANES].astype(F32))
        k = rope(k_ref[:, pair * LANES:(pair + 1) * LANES].astype(F32) * (C_QK_DIM ** -0.5))
        qk.append((q, q * xi_ref[pair], k.astype(BF16), (k * zeta_ref[pair]).astype(BF16)))
    masked = []
    for h in range(C_HEADS):
        q, qx, _, _ = qk[h // 2]
        in_head = (lane < C_QK_DIM) if h % 2 == 0 else (lane >= C_QK_DIM)
        masked.append((jnp.where(in_head, q, 0.0).astype(BF16), jnp.where(in_head, qx, 0.0).astype(BF16)))
    inner = [(_dot_nt(masked[h][0], qk[h // 2][2]) * decay_ref[h]).astype(BF16) for h in range(C_HEADS)]
    outs = []
    for h in range(C_HEADS):
        vh = v_ref[:, h * LANES:(h + 1) * LANES]
        st = state_ref[h]
        outs.append(jnp.dot(inner[h], vh, preferred_element_type=F32)
                    + jnp.dot(masked[h][1], st.astype(BF16), preferred_element_type=F32))
        state_ref[h] = st * gch_ref[h] + _dot_tn(qk[h // 2][3], vh)
    for h, o in enumerate(outs):
        mu = jnp.mean(o, axis=-1, keepdims=True)
        oc = o - mu
        o = oc * lax.rsqrt(jnp.mean(oc * oc, axis=-1, keepdims=True) + LN_EPS)
        gate = g_ref[:, h * LANES:(h + 1) * LANES].astype(F32)
        o_ref[:, h * LANES:(h + 1) * LANES] = (_silu(gate) * o).astype(BF16)


BC_BLOCKS = 4


def _swa_ret_kernel(q_ref, kvp_ref, kvc_ref, bias_ref, sink_ref, cq_ref, ck_ref, cv_ref, cg_ref, sin_ref, cos_ref,
                    decay_ref, zeta_ref, xi_ref, gch_ref, yb_ref, yc_ref, state_ref):
    c = C_CHUNK
    blocks = q_ref.shape[1] // c
    first_block = pl.program_id(1) * blocks
    for r in range(blocks):
        rows = slice(r * c, (r + 1) * c)
        prev = kvp_ref.at[0] if r == 0 else kvc_ref.at[0, (r - 1) * c:r * c]
        _swa_block(q_ref.at[0, rows], prev, kvc_ref.at[0, rows], bias_ref, sink_ref, yb_ref.at[0, rows],
                   jnp.minimum(first_block + r, 1))
        _ret_block(cq_ref.at[0, rows], ck_ref.at[0, rows], cv_ref.at[0, rows], cg_ref.at[0, rows],
                   sin_ref.at[rows], cos_ref.at[rows], decay_ref, zeta_ref, xi_ref, gch_ref, yc_ref.at[0, rows],
                   state_ref, first_block + r == 0)


def _swa_retention(proj, bias_b, sink_rows, sin_t, cos_t, decay, zeta_t, xi_t, gch):
    assert B_WINDOW == C_CHUNK
    bsz, s, _ = proj.shape
    c = C_CHUNK
    blocks = min(BC_BLOCKS, s // c)
    t = blocks * c
    qw = C_HEADS * C_QK_DIM
    kvw = 2 * B_KV_HEADS * B_HEAD_DIM
    qb, kvb = OFF_BQ // B_WIDTH, OFF_BKV // kvw
    cols = (B_HEADS // B_KV_HEADS) * c
    return pl.pallas_call(
        _swa_ret_kernel,
        grid=(bsz, s // t),
        in_specs=[pl.BlockSpec((1, t, B_WIDTH), lambda b, n: (b, n, qb)),
                  pl.BlockSpec((1, c, kvw), lambda b, n: (b, jnp.maximum(n * blocks - 1, 0), kvb)),
                  pl.BlockSpec((1, t, kvw), lambda b, n: (b, n, kvb)),
                  pl.BlockSpec((B_KV_HEADS, 2, 2 * c, cols), lambda b, n: (0, 0, 0, 0)),
                  pl.BlockSpec((B_KV_HEADS, 1, cols), lambda b, n: (0, 0, 0)),
                  pl.BlockSpec((1, t, qw), lambda b, n: (b, n, OFF_CQ // qw)),
                  pl.BlockSpec((1, t, qw), lambda b, n: (b, n, OFF_CK // qw)),
                  pl.BlockSpec((1, t, C_WIDTH), lambda b, n: (b, n, OFF_CV // C_WIDTH)),
                  pl.BlockSpec((1, t, C_WIDTH), lambda b, n: (b, n, OFF_CG // C_WIDTH)),
                  pl.BlockSpec((t, LANES), lambda b, n: (n, 0)),
                  pl.BlockSpec((t, LANES), lambda b, n: (n, 0)),
                  pl.BlockSpec((C_HEADS, c, c), lambda b, n: (0, 0, 0)),
                  pl.BlockSpec((C_HEADS // 2, c, LANES), lambda b, n: (0, 0, 0)),
                  pl.BlockSpec((C_HEADS // 2, c, LANES), lambda b, n: (0, 0, 0)),
                  pl.BlockSpec((C_HEADS, 1, LANES), lambda b, n: (0, 0, 0))],
        out_specs=[pl.BlockSpec((1, t, B_WIDTH), lambda b, n: (b, n, 0)),
                   pl.BlockSpec((1, t, C_WIDTH), lambda b, n: (b, n, 0))],
        out_shape=[jax.ShapeDtypeStruct((bsz, s, B_WIDTH), BF16), jax.ShapeDtypeStruct((bsz, s, C_WIDTH), BF16)],
        scratch_shapes=[pltpu.VMEM((C_HEADS, LANES, C_V_DIM), F32)],
        compiler_params=_cparams(("parallel", "arbitrary")),
        name="swa_retention",
    )(proj, proj, proj, bias_b, sink_rows, proj, proj, proj, proj, sin_t, cos_t, decay, zeta_t, xi_t, gch)


def _cast_riders(riders_in, riders_out):
    for src, dst in zip(riders_in, riders_out):
        dst[...] = src[...].astype(BF16)


def _row_split_riders(riders, steps):
    views = [r.reshape(r.size // r.shape[-1], r.shape[-1]) for r in riders]
    for v in views:
        assert v.shape[0] % (steps * 16) == 0, v.shape
    return views, [(v.shape[0] // steps, v.shape[1]) for v in views]


def _out_proj_kernel(ya_ref, yb_ref, yc_ref, wa_ref, wb_ref, wc_ref, x_ref, g_ref, b_ref, *rest):
    n_riders = (len(rest) - 1) // 2
    o_ref = rest[n_riders]
    half = o_ref.shape[0] // 2
    spans = [slice(0, half), slice(half, 2 * half)]
    mixes = [jnp.dot(ya_ref[r, :], wa_ref[...], preferred_element_type=F32)
             + jnp.dot(yb_ref[r, :], wb_ref[...], preferred_element_type=F32)
             + jnp.dot(yc_ref[r, :], wc_ref[...], preferred_element_type=F32) for r in spans]
    for r, mix in zip(spans, mixes):
        o_ref[r, :] = _layer_norm(ALPHA * x_ref[r, :] + mix, g_ref[...], b_ref[...])
    _cast_riders(rest[:n_riders], rest[n_riders + 1:])


def _out_proj_ln(ya, yb, yc, w, x, g, b, riders=()):
    n, d = x.shape
    tm = min(512, n)
    row = lambda i: (i, 0)
    fixed = lambda i: (0, 0)
    once = pl.Buffered(1)
    views, blocks = _row_split_riders(riders, n // tm)
    rider_specs = [pl.BlockSpec(blk, row) for blk in blocks]
    outs = pl.pallas_call(
        _out_proj_kernel,
        grid=(n // tm,),
        in_specs=[pl.BlockSpec((tm, A_WIDTH), row),
                  pl.BlockSpec((tm, B_WIDTH), row),
                  pl.BlockSpec((tm, C_WIDTH), row),
                  pl.BlockSpec((A_WIDTH, d), lambda i: (0, 0), pipeline_mode=once),
                  pl.BlockSpec((B_WIDTH, d), lambda i: (1, 0), pipeline_mode=once),
                  pl.BlockSpec((C_WIDTH, d), lambda i: ((A_WIDTH + B_WIDTH) // C_WIDTH, 0), pipeline_mode=once),
                  pl.BlockSpec((tm, d), row),
                  pl.BlockSpec((1, d), fixed),
                  pl.BlockSpec((1, d), fixed)] + rider_specs,
        out_specs=[pl.BlockSpec((tm, d), row)] + rider_specs,
        out_shape=[jax.ShapeDtypeStruct((n, d), F32)] + [jax.ShapeDtypeStruct(v.shape, BF16) for v in views],
        compiler_params=_cparams(("arbitrary",)),
        name="out_proj_ln",
    )(ya, yb, yc, w, w, w, x, g, b, *views)
    return outs[0], [o.reshape(r.shape) for o, r in zip(outs[1:], riders)]


FFN_TM = 512
FFN_TF = 512


def _swiglu_step(xb, wg, wu, wd):
    hg = jnp.dot(xb, wg, preferred_element_type=F32)
    hu = jnp.dot(xb, wu, preferred_element_type=F32)
    return jnp.dot((_silu(hg) * hu).astype(BF16), wd, preferred_element_type=F32)


def _ffn_kernel(x_ref, wg_ref, wu_ref, wd_ref, g_ref, b_ref, *rest):
    n_riders = (len(rest) - 2) // 2
    riders_in, o_ref = rest[:n_riders], rest[n_riders]
    riders_out, xb_ref = rest[n_riders + 1:-1], rest[-1]
    f = pl.program_id(1)

    @pl.when(f == 0)
    def _():
        xb = x_ref[...].astype(BF16)
        xb_ref[...] = xb
        o_ref[...] = _swiglu_step(xb, wg_ref[...], wu_ref[...], wd_ref[...])
        _cast_riders(riders_in, riders_out)

    @pl.when(f > 0)
    def _():
        o_ref[...] += _swiglu_step(xb_ref[...], wg_ref[...], wu_ref[...], wd_ref[...])
        _cast_riders(riders_in, riders_out)

    @pl.when(f == pl.num_programs(1) - 1)
    def _():
        o_ref[...] = _layer_norm(ALPHA * x_ref[...] + o_ref[...], g_ref[...], b_ref[...])


def _ffn(x, w_gate, w_up, w_down, g, b, riders=()):
    n, d = x.shape
    ff = w_gate.shape[1]
    tm = min(FFN_TM, n)
    tf = FFN_TF
    ni, nf = n // tm, ff // tf
    flat, rider_specs = [], []
    for r in riders:
        cols = r.shape[-1]
        rows = r.size // cols
        flat.append(r.reshape(rows, cols))
        if rows % (ni * nf * 16) == 0:
            rider_specs.append(pl.BlockSpec((rows // (ni * nf), cols), lambda i, f: (i * nf + f, 0)))
        else:
            assert rows % (ni * 16) == 0 and cols % (nf * LANES) == 0
            rider_specs.append(pl.BlockSpec((rows // ni, cols // nf), lambda i, f: (i, f)))
    outs = pl.pallas_call(
        _ffn_kernel,
        grid=(n // tm, ff // tf),
        in_specs=[pl.BlockSpec((tm, d), lambda i, f: (i, 0)),
                  pl.BlockSpec((d, tf), lambda i, f: (0, f)),
                  pl.BlockSpec((d, tf), lambda i, f: (0, f)),
                  pl.BlockSpec((tf, d), lambda i, f: (f, 0)),
                  pl.BlockSpec((1, d), lambda i, f: (0, 0)),
                  pl.BlockSpec((1, d), lambda i, f: (0, 0))] + rider_specs,
        out_specs=[pl.BlockSpec((tm, d), lambda i, f: (i, 0))] + rider_specs,
        out_shape=[jax.ShapeDtypeStruct((n, d), F32)] + [jax.ShapeDtypeStruct(r.shape, BF16) for r in flat],
        scratch_shapes=[pltpu.VMEM((tm, d), BF16)],
        compiler_params=_cparams(("arbitrary", "arbitrary")),
        name="ffn_ln",
    )(x, w_gate, w_up, w_down, g, b, *flat)
    return outs[0], [o.reshape(r.shape) for o, r in zip(outs[1:], riders)]


def _row_copy(src, dst, sem, s, t):
    return pltpu.make_async_copy(src.at[pl.ds(s, 1)], dst.at[pl.ds(t, 1)], sem)


def _ffn_grouped_kernel(eid_ref, nact_ref, src_ref, x_hbm, wg_ref, wu_ref, wd_ref, o_ref, rows_ref, xb_ref, sem,
                        *, tm, per_step):
    i = pl.program_id(0)
    f = pl.program_id(1)
    nact = nact_ref[0]
    active = i < nact
    slot = lax.rem(i, 2)
    buf_rows = rows_ref.shape[1]
    last_slot_row = src_ref.shape[0] - 1

    def fetch(tile, into, first_row, count):
        for r in range(count):
            row = first_row + r
            token = src_ref[jnp.minimum(tile * tm + row, last_slot_row)]
            _row_copy(x_hbm, rows_ref.at[into], sem.at[into], token, row).start()

    def wait_buffer(which):
        pltpu.make_async_copy(x_hbm.at[pl.ds(0, buf_rows)], rows_ref.at[which], sem.at[which]).wait()

    @pl.when(jnp.logical_and(i == 0, f == 0))
    def _():
        lax.fori_loop(0, buf_rows // per_step, lambda s, c: (fetch(0, 0, s * per_step, per_step), c)[1], 0)

    @pl.when(jnp.logical_and(f == 0, i <= nact))
    def _():
        wait_buffer(slot)

    @pl.when(jnp.logical_and(f == 0, jnp.logical_not(active)))
    def _():
        o_ref[...] = jnp.zeros_like(o_ref)

    @pl.when(jnp.logical_and(f == 0, active))
    def _():
        fetch(i + 1, 1 - slot, 0, per_step)
        xb = rows_ref[slot, :tm, :].astype(BF16)
        xb_ref[...] = xb
        o_ref[...] = _swiglu_step(xb, wg_ref[0], wu_ref[0], wd_ref[0])

    @pl.when(jnp.logical_and(f > 0, active))
    def _():
        fetch(i + 1, 1 - slot, f * per_step, per_step)
        o_ref[...] += _swiglu_step(xb_ref[...], wg_ref[0], wu_ref[0], wd_ref[0])


def _ffn_grouped(x, src, eid, nact, w_gate, w_up, w_down):
    n, d = x.shape
    slots = src.shape[0]
    ff = w_gate.shape[2]
    tm = min(FFN_TM, n)
    tf = FFN_TF
    nf = ff // tf
    per_step = -(-tm // nf)
    per_step += (-per_step) % 8
    assert per_step * nf <= n

    def fidx(i, f, nact_ref):
        return jnp.where(i < nact_ref[0], f, nf - 1)

    grid_spec = pltpu.PrefetchScalarGridSpec(
        num_scalar_prefetch=3,
        grid=(slots // tm, nf),
        in_specs=[pl.BlockSpec(memory_space=pl.ANY),
                  pl.BlockSpec((1, d, tf), lambda i, f, e, a, s: (e[i], 0, fidx(i, f, a))),
                  pl.BlockSpec((1, d, tf), lambda i, f, e, a, s: (e[i], 0, fidx(i, f, a))),
                  pl.BlockSpec((1, tf, d), lambda i, f, e, a, s: (e[i], fidx(i, f, a), 0))],
        out_specs=pl.BlockSpec((tm, d), lambda i, f, e, a, s: (i, 0)),
        scratch_shapes=[pltpu.VMEM((2, per_step * nf, d), F32), pltpu.VMEM((tm, d), BF16),
                        pltpu.SemaphoreType.DMA((2,))],
    )
    return pl.pallas_call(
        functools.partial(_ffn_grouped_kernel, tm=tm, per_step=per_step),
        grid_spec=grid_spec,
        out_shape=jax.ShapeDtypeStruct((slots, d), F32),
        compiler_params=_cparams(("arbitrary", "arbitrary")),
        name="ffn_grouped",
    )(eid, nact, src, x, w_gate, w_up, w_down)


ROUTER_TM = 512


def _router_kernel(x_ref, wr_ref, tri_ref, meta_ref, cnt_ref, carry_ref):
    @pl.when(pl.program_id(0) == 0)
    def _():
        carry_ref[...] = jnp.zeros_like(carry_ref)

    x = x_ref[...]
    x_hi = x.astype(BF16)
    x_lo = (x - x_hi.astype(F32)).astype(BF16)
    parts = (jnp.dot(x_hi, wr_ref[...], preferred_element_type=F32)
             + jnp.dot(x_lo, wr_ref[...], preferred_element_type=F32))
    logits = parts + pltpu.roll(parts, LANES - N_EXPERTS, 1)
    lane = lax.broadcasted_iota(jnp.int32, logits.shape, 1)
    logits = jnp.where(lane < N_EXPERTS, logits, -jnp.inf)
    m1 = jnp.max(logits, axis=-1, keepdims=True)
    i1 = jnp.min(jnp.where(logits == m1, lane, LANES), axis=-1, keepdims=True)
    rest = jnp.where(lane == i1, -jnp.inf, logits)
    m2 = jnp.max(rest, axis=-1, keepdims=True)
    i2 = jnp.min(jnp.where(rest == m2, lane, LANES), axis=-1, keepdims=True)
    e2 = jnp.exp(m2 - m1)
    w1 = 1.0 / (1.0 + e2)
    w2 = e2 * w1
    hit1 = lane == i1
    hit2 = lane == i2
    onehot = jnp.where(jnp.logical_or(hit1, hit2), 1.0, 0.0)
    before = jnp.dot(tri_ref[...], onehot.astype(BF16), preferred_element_type=F32) + carry_ref[...]
    r1 = jnp.sum(jnp.where(hit1, before, 0.0), axis=-1, keepdims=True)
    r2 = jnp.sum(jnp.where(hit2, before, 0.0), axis=-1, keepdims=True)
    carry_ref[...] = carry_ref[...] + jnp.sum(onehot, axis=0, keepdims=True)
    cnt_ref[...] = carry_ref[...]
    meta = jnp.where(lane == 0, i1.astype(F32), 0.0)
    meta = jnp.where(lane == 1, i2.astype(F32), meta)
    meta = jnp.where(lane == 2, r1, meta)
    meta = jnp.where(lane == 3, r2, meta)
    meta = jnp.where(lane == 4, w1, meta)
    meta = jnp.where(lane == 5, w2, meta)
    meta_ref[...] = meta


def _router(x, w_router):
    n, d = x.shape
    tm = min(ROUTER_TM, n)
    w_hi = w_router.astype(BF16)
    w_lo = (w_router.astype(F32) - w_hi.astype(F32)).astype(BF16)
    wr = jnp.zeros((d, LANES), BF16).at[:, :N_EXPERTS].set(w_hi).at[:, N_EXPERTS:2 * N_EXPERTS].set(w_lo)
    tri = jnp.asarray(np.tril(np.ones((tm, tm), np.float32), -1), BF16)
    return pl.pallas_call(
        _router_kernel,
        grid=(n // tm,),
        in_specs=[pl.BlockSpec((tm, d), lambda i: (i, 0)),
                  pl.BlockSpec((d, LANES), lambda i: (0, 0)),
                  pl.BlockSpec((tm, tm), lambda i: (0, 0))],
        out_specs=[pl.BlockSpec((tm, LANES), lambda i: (i, 0)),
                   pl.BlockSpec((1, LANES), lambda i: (0, 0))],
        out_shape=[jax.ShapeDtypeStruct((n, LANES), F32), jax.ShapeDtypeStruct((1, LANES), F32)],
        scratch_shapes=[pltpu.VMEM((1, LANES), F32)],
        compiler_params=_cparams(("arbitrary",)),
        name="router",
    )(x, wr, tri)


MOVE_TM = 256
ISSUE_UNROLL = 8


def _combine_kernel(d1_ref, d2_ref, ys_hbm, x_ref, meta_ref, g_ref, b_ref, o_ref, buf_ref, sem, *, tm):
    i = pl.program_id(0)
    last = pl.num_programs(0) - 1
    slot = lax.rem(i, 2)

    def start_row(tile, into, t):
        row = tile * tm + t
        _row_copy(ys_hbm, buf_ref.at[into, 0], sem.at[into], d1_ref[row], t).start()
        _row_copy(ys_hbm, buf_ref.at[into, 1], sem.at[into], d2_ref[row], t).start()

    def wait_slot(which):
        for k in range(2):
            pltpu.make_async_copy(ys_hbm.at[pl.ds(0, tm)], buf_ref.at[which, k], sem.at[which]).wait()

    @pl.when(i == 0)
    def _():
        lax.fori_loop(0, tm, lambda t, c: (start_row(0, 0, t), c)[1], 0, unroll=ISSUE_UNROLL)

    wait_slot(slot)
    nxt = jnp.minimum(i + 1, last)
    for t in range(tm):
        start_row(nxt, 1 - slot, t)
    meta = meta_ref[...]
    lane = lax.broadcasted_iota(jnp.int32, meta.shape, 1)
    w1 = jnp.sum(jnp.where(lane == 4, meta, 0.0), axis=-1, keepdims=True)
    w2 = jnp.sum(jnp.where(lane == 5, meta, 0.0), axis=-1, keepdims=True)
    f = w1 * buf_ref[slot, 0] + w2 * buf_ref[slot, 1]
    o_ref[...] = _layer_norm(ALPHA * x_ref[...] + f, g_ref[...], b_ref[...])

    @pl.when(i == last)
    def _():
        wait_slot(1 - slot)


def _combine(ys, x, meta, d1, d2, g, b):
    n, d = x.shape
    tm = min(MOVE_TM, n)
    grid_spec = pltpu.PrefetchScalarGridSpec(
        num_scalar_prefetch=2,
        grid=(n // tm,),
        in_specs=[pl.BlockSpec(memory_space=pl.ANY),
                  pl.BlockSpec((tm, d), lambda i, a, c: (i, 0)),
                  pl.BlockSpec((tm, LANES), lambda i, a, c: (i, 0)),
                  pl.BlockSpec((1, d), lambda i, a, c: (0, 0)),
                  pl.BlockSpec((1, d), lambda i, a, c: (0, 0))],
        out_specs=pl.BlockSpec((tm, d), lambda i, a, c: (i, 0)),
        scratch_shapes=[pltpu.VMEM((2, 2, tm, d), F32), pltpu.SemaphoreType.DMA((2,))],
    )
    return pl.pallas_call(
        functools.partial(_combine_kernel, tm=tm),
        grid_spec=grid_spec,
        out_shape=jax.ShapeDtypeStruct((n, d), F32),
        compiler_params=_cparams(("arbitrary",)),
        name="moe_combine",
    )(d1, d2, ys, x, meta, g, b)


def _moe(x, w_router, w_gate, w_up, w_down, g, b):
    n, d = x.shape
    tm = min(FFN_TM, n)
    meta, cnt = _router(x, w_router)
    i1 = meta[:, 0].astype(jnp.int32)
    i2 = meta[:, 1].astype(jnp.int32)
    counts = cnt[0, :N_EXPERTS].astype(jnp.int32)
    tiles = (counts + tm - 1) // tm
    tile_end = jnp.cumsum(tiles)
    group_start = (tile_end - tiles) * tm
    experts = jnp.arange(N_EXPERTS, dtype=jnp.int32)[None, :]
    start_of = lambda idx: jnp.sum(jnp.where(idx[:, None] == experts, group_start[None, :], 0), axis=1)
    d1 = start_of(i1) + meta[:, 2].astype(jnp.int32)
    d2 = start_of(i2) + meta[:, 3].astype(jnp.int32)
    max_tiles = (2 * n) // tm + N_EXPERTS
    tile_ids = jnp.arange(max_tiles, dtype=jnp.int32)[:, None]
    eid = jnp.minimum(jnp.sum((tile_ids >= tile_end[None, :]).astype(jnp.int32), axis=1), N_EXPERTS - 1)
    nact = tile_end[-1:].astype(jnp.int32)
    token = jnp.arange(n, dtype=jnp.int32)
    src = jnp.zeros((max_tiles * tm,), jnp.int32).at[jnp.concatenate([d1, d2])].set(
        jnp.concatenate([token, token]), unique_indices=True)
    ys = _ffn_grouped(x, src, eid.astype(jnp.int32), nact, w_gate, w_up, w_down)
    return _combine(ys, x, meta, d1, d2, g, b)


def _lambda_init(layer_idx):
    return 0.8 - 0.6 * math.exp(-0.3 * layer_idx)


def _static_tables(s):
    c = C_CHUNK
    ang = jnp.repeat(1.0 / (10000.0 ** jnp.linspace(0.0, 1.0, C_QK_DIM // 2, dtype=F32)), 2)
    ang = jnp.arange(s, dtype=F32)[:, None] * ang[None, :]
    sign = jnp.where(jnp.arange(C_QK_DIM) % 2 == 0, -1.0, 1.0).astype(F32)
    sin_t = jnp.tile(jnp.sin(ang) * sign[None, :], (1, 2))
    cos_t = jnp.tile(jnp.cos(ang), (1, 2))
    log_g = jnp.log(1.0 - jnp.exp2(-5.0 - jnp.arange(C_HEADS, dtype=F32)))
    pos = jnp.arange(c)
    rel = (pos[:, None] - pos[None, :]).astype(F32)
    decay = jnp.where((rel >= 0)[None], jnp.exp(jnp.maximum(rel, 0.0)[None] * log_g[:, None, None]), 0.0)
    zeta = jnp.exp((c - 1 - pos).astype(F32)[:, None] * log_g[None, :])
    xi = jnp.exp((pos + 1).astype(F32)[:, None] * log_g[None, :])
    per_pair = lambda t: jnp.repeat(t.T.reshape(C_HEADS // 2, 2, c), C_QK_DIM, axis=1).transpose(0, 2, 1)
    gch = jnp.broadcast_to(jnp.exp(c * log_g)[:, None, None], (C_HEADS, 1, LANES))
    dist_a = np.arange(A_SUB)[None, :] - np.arange(A_BLOCK)[:, None]
    types = [jnp.where(dist_a + off >= 0, _rel_bucket(jnp.asarray(dist_a + off)), REL_BUCKETS)
             for off in (0, A_SUB, A_BLOCK)]
    types.append(jnp.full(dist_a.shape, REL_BUCKETS - 1, jnp.int32))
    bkt_a = jnp.concatenate([jnp.tile(b, (1, 2)) for b in types], axis=0).astype(jnp.int32)
    w = B_WINDOW
    dist = np.arange(w)[:, None] + w - np.arange(2 * w)[None, :]
    band = (dist >= 0) & (dist < w)
    has_prev = np.stack([np.broadcast_to(np.arange(2 * w)[None, :] >= w, band.shape), np.ones_like(band)])
    bkt_b = jnp.where(band[None] & has_prev, _rel_bucket(jnp.asarray(dist))[None], REL_BUCKETS)
    bkt_b = bkt_b.transpose(0, 2, 1).reshape(2 * 2 * w, w).astype(jnp.int32)
    return sin_t, cos_t, decay, per_pair(zeta), per_pair(xi), gch, bkt_a, bkt_b


def kernel(x, w_in, rel_bias, a_lambda, a_subln_g, b_sinks, w_out, ln_mix_g, ln_mix_b, ln_ffn_g, ln_ffn_b,
           dense_w_gate, dense_w_up, dense_w_down, moe_router, moe_w_gate, moe_w_up, moe_w_down):
    bsz, s, d = x.shape
    n = bsz * s
    sin_t, cos_t, decay, zeta_t, xi_t, gch, bkt_a, bkt_b = _static_tables(s)
    tab_t = rel_bias.astype(F32).T
    bias_a = _bias_lookup(tab_t[:A_HEADS] * LOG2E, bkt_a).reshape(A_HEADS, A_BIAS_TYPES, A_BLOCK, 2 * A_SUB)
    group = B_HEADS // B_KV_HEADS
    bias_b = _bias_lookup(tab_t[A_HEADS:] * LOG2E, bkt_b)
    bias_b = bias_b.reshape(B_KV_HEADS, group, 2, 2 * B_WINDOW, B_WINDOW).transpose(0, 2, 3, 1, 4)
    bias_b = bias_b.reshape(B_KV_HEADS, 2, 2 * B_WINDOW, group * B_WINDOW)
    xf = x.reshape(n, d).astype(F32)
    w_in_bf16, w_out_bf16 = w_in[0].astype(BF16), w_out[0].astype(BF16)
    for l in range(DEPTH):
        w_in_l = jnp.concatenate([w_in_bf16[:, a:b] for a, b in _PERM_RUNS]
                                 + [jnp.zeros((d, PROJ_PAD - PROJ_WIDTH), BF16)], axis=1)
        proj = _in_proj(xf, w_in_l).reshape(bsz, s, PROJ_PAD)
        ya = _attn_a(proj, bias_a, a_lambda[l].astype(F32), a_subln_g[l].astype(F32).reshape(1, A_V_DIM),
                     _lambda_init(l))
        sink_rows = jnp.repeat(b_sinks[l].astype(F32).reshape(B_KV_HEADS, 1, group), B_WINDOW, axis=2)
        yb, yc = _swa_retention(proj, bias_b, sink_rows, sin_t, cos_t, decay, zeta_t, xi_t, gch)
        g_mix = ln_mix_g[l].astype(F32).reshape(1, d)
        b_mix = ln_mix_b[l].astype(F32).reshape(1, d)
        j = l // 2
        dense = (dense_w_gate[j], dense_w_up[j], dense_w_down[j]) if l % 2 == 0 else ()
        ahead = (w_in[l + 1], w_out[l + 1]) if l + 1 < DEPTH else ()
        xf, cast = _out_proj_ln(ya.reshape(n, A_WIDTH), yb.reshape(n, B_WIDTH), yc.reshape(n, C_WIDTH),
                                w_out_bf16, xf, g_mix, b_mix, dense + ahead)
        dense_bf16 = cast[:len(dense)]
        if ahead:
            w_in_bf16, w_out_bf16 = cast[len(dense):]
        g_ffn = ln_ffn_g[l].astype(F32).reshape(1, d)
        b_ffn = ln_ffn_b[l].astype(F32).reshape(1, d)
        if l % 2 == 0:
            experts = (moe_w_gate[j], moe_w_up[j], moe_w_down[j]) if l + 1 < DEPTH else ()
            xf, moe_bf16 = _ffn(xf, *dense_bf16, g_ffn, b_ffn, experts)
        else:
            assert l > 0, "expert weights are cast by the preceding dense layer"
            xf = _moe(xf, moe_router[j], *moe_bf16, g_ffn, b_ffn)
    return xf.reshape(bsz, s, d).astype(x.dtype)
```

```python
import functools
import math

import jax
import jax.numpy as jnp
import numpy as np
from jax import lax
from jax.experimental import pallas as pl
from jax.experimental.pallas import tpu as pltpu

F32 = jnp.float32
BF16 = jnp.bfloat16

D_MODEL = 2048
DEPTH = 2
A_HEADS = 6
A_QK_DIM = 64
A_V_DIM = 128
B_HEADS = 12
B_KV_HEADS = 3
B_HEAD_DIM = 64
B_WINDOW = 128
C_HEADS = 4
C_QK_DIM = 64
C_V_DIM = 128
C_CHUNK = 128
A_WIDTH = A_HEADS * A_V_DIM
B_WIDTH = B_HEADS * B_HEAD_DIM
C_WIDTH = C_HEADS * C_V_DIM
REL_BUCKETS = 32
REL_MAX_DIST = 128
D_FF = 5632
N_EXPERTS = 8
ALPHA = (2.0 * DEPTH) ** 0.25
LN_EPS = 1e-5
NEG = -1e30

LANES = 128
VMEM_LIMIT = 56 * 1024 * 1024

_REF_SIZES = [768, 768, 768, 768, 192, 192, 256, 256, 512, 512]
_REF_OFF = [int(v) for v in np.concatenate([[0], np.cumsum(_REF_SIZES)[:-1]])]
PROJ_WIDTH = int(sum(_REF_SIZES))
OFF_AQ, OFF_AK, OFF_AV, OFF_BQ, OFF_CV, OFF_CG, OFF_CQ, OFF_CK, OFF_BKV = (
    0, 768, 1536, 2304, 3072, 3584, 4096, 4352, 4608)


def _proj_perm():
    aq, ak, av, bq, bk, bv, cq, ck, cv, cg = [np.arange(o, o + s) for o, s in zip(_REF_OFF, _REF_SIZES)]
    perm = np.concatenate([aq, ak, av, bq, cv, cg, cq, ck, bk, bv])
    assert perm.shape[0] == PROJ_WIDTH
    return perm


def _perm_runs():
    perm = _proj_perm()
    cuts = np.flatnonzero(np.diff(perm) != 1) + 1
    return [(int(r[0]), int(r[-1]) + 1) for r in np.split(perm, cuts)]


_PERM_RUNS = _perm_runs()


def _cparams(sem):
    return pltpu.CompilerParams(dimension_semantics=sem, vmem_limit_bytes=VMEM_LIMIT)


def _layer_norm(z, g, b):
    mu = jnp.mean(z, axis=-1, keepdims=True)
    zc = z - mu
    var = jnp.mean(zc * zc, axis=-1, keepdims=True)
    return zc * lax.rsqrt(var + LN_EPS) * g + b


def _silu(x):
    return x / (1.0 + jnp.exp(-x))


def _dot_nt(a, b):
    return lax.dot_general(a, b, (((1,), (1,)), ((), ())), preferred_element_type=F32)


def _dot_tn(a, b):
    return lax.dot_general(a, b, (((0,), (0,)), ((), ())), preferred_element_type=F32)


def _bias_kernel(tab_ref, bkt_ref, o_ref):
    h = pl.program_id(0)
    bkt = bkt_ref[...]
    acc = jnp.full(bkt.shape, NEG, F32)
    for b in range(REL_BUCKETS):
        acc = jnp.where(bkt == b, tab_ref[h, b], acc)
    o_ref[0] = acc


def _bias_lookup(tab_t, bkt):
    nh = tab_t.shape[0]
    r, c = bkt.shape
    return pl.pallas_call(
        _bias_kernel,
        grid=(nh,),
        in_specs=[pl.BlockSpec(memory_space=pltpu.SMEM),
                  pl.BlockSpec((r, c), lambda h: (0, 0))],
        out_specs=pl.BlockSpec((1, r, c), lambda h: (h, 0, 0)),
        out_shape=jax.ShapeDtypeStruct((nh, r, c), F32),
        compiler_params=_cparams(("arbitrary",)),
        name="bias_lookup",
    )(tab_t, bkt)


def _rel_bucket(dist):
    max_exact = REL_BUCKETS // 2
    d = jnp.maximum(dist, 0)
    ratio = jnp.maximum(d, 1).astype(F32) / max_exact
    large = max_exact + (jnp.log(ratio) / math.log(REL_MAX_DIST / max_exact)
                         * (REL_BUCKETS - max_exact)).astype(jnp.int32)
    large = jnp.minimum(large, REL_BUCKETS - 1)
    return jnp.where(d < max_exact, d, large)


MXU_COLS = 256
IN_PROJ_COL_STEPS = 4
PROJ_PAD = -(-PROJ_WIDTH // (IN_PROJ_COL_STEPS * MXU_COLS)) * (IN_PROJ_COL_STEPS * MXU_COLS)


def _in_proj_kernel(x_ref, w_ref, o_ref, xb_ref):
    @pl.when(pl.program_id(1) == 0)
    def _():
        xb_ref[...] = x_ref[...].astype(BF16)

    o_ref[...] = jnp.dot(xb_ref[...], w_ref[...], preferred_element_type=F32).astype(BF16)


def _in_proj(x, w):
    n, d = x.shape
    p = w.shape[1]
    tm = min(1024, n)
    tn = p // IN_PROJ_COL_STEPS
    return pl.pallas_call(
        _in_proj_kernel,
        grid=(n // tm, p // tn),
        in_specs=[pl.BlockSpec((tm, d), lambda i, j: (i, 0)),
                  pl.BlockSpec((d, tn), lambda i, j: (0, j))],
        out_specs=pl.BlockSpec((tm, tn), lambda i, j: (i, j)),
        out_shape=jax.ShapeDtypeStruct((n, p), BF16),
        scratch_shapes=[pltpu.VMEM((tm, d), BF16)],
        compiler_params=_cparams(("parallel", "arbitrary")),
        name="in_proj",
    )(x, w)


A_BLOCK = 256
A_SUB = 128
A_BIAS_TYPES = 4
LOG2E = math.log2(math.e)


def _attn_a_kernel(q_ref, k_ref, v_ref, bias_ref, lam_ref, g_ref, o_ref, qt_ref, s_ref, p_ref, acc_ref, *,
                   lam_init):
    t = A_BLOCK
    per_key_block = t // A_SUB
    n_chains = q_ref.shape[1] // A_SUB
    n_blocks = k_ref.shape[1] // t
    diag = [g // per_key_block for g in range(n_chains)]
    lp = lam_ref[...]
    lam = (jnp.exp(jnp.sum(lp[0:1] * lp[1:2], axis=-1, keepdims=True))
           - jnp.exp(jnp.sum(lp[2:3] * lp[3:4], axis=-1, keepdims=True)) + lam_init)
    lane = lax.broadcasted_iota(jnp.int32, (A_SUB, LANES), 1)
    for g in range(n_chains):
        q = q_ref[0, g * A_SUB:(g + 1) * A_SUB, :].astype(F32) * (A_QK_DIM ** -0.5 * LOG2E)
        qq = jnp.concatenate([jnp.where(lane < A_QK_DIM, q, 0.0), jnp.where(lane >= A_QK_DIM, q, 0.0)], axis=0)
        qt_ref[g] = qq.T.astype(BF16)

    def needing(j):
        return [g for g in range(n_chains) if diag[g] >= j]

    def scores_into(j):
        kj = k_ref[0, j * t:(j + 1) * t, :]
        for g in needing(j):
            back = diag[g] - j
            near = (0, 2) if g % per_key_block == 0 else (1, 3)
            kind = near[0] if back == 0 else (near[1] if back == 1 else 3)
            s_ref[j % 2, g] = jnp.dot(kj, qt_ref[g], preferred_element_type=F32) + bias_ref[0, kind]

    def finish(g, l_fin):
        o_all = acc_ref[g] / l_fin
        o = (o_all[:, :A_SUB] - lam * o_all[:, A_SUB:]).T
        o = o * lax.rsqrt(jnp.mean(o * o, axis=-1, keepdims=True) + LN_EPS) * g_ref[...]
        o_ref[0, g * A_SUB:(g + 1) * A_SUB, :] = (o * (1.0 - lam_init)).astype(BF16)

    stats = [(jnp.full((1, 2 * A_SUB), NEG, F32), jnp.zeros((1, 2 * A_SUB), F32), None) for _ in range(n_chains)]
    scores_into(0)
    for j in range(n_blocks + 1):
        if j + 1 < n_blocks:
            scores_into(j + 1)
        before = list(stats)
        if j < n_blocks:
            for g in needing(j):
                m_old, l_old, _ = before[g]
                s = s_ref[j % 2, g]
                m_new = jnp.maximum(m_old, jnp.max(s, axis=0, keepdims=True))
                p = jnp.exp2(s - m_new)
                alpha = jnp.exp2(m_old - m_new)
                stats[g] = (m_new, alpha * l_old + jnp.sum(p, axis=0, keepdims=True), alpha)
                p_ref[j % 2, g] = p.astype(BF16)
        if j >= 1:
            vj = v_ref[0, (j - 1) * t:j * t, :]
            for g in needing(j - 1):
                pv = _dot_tn(vj, p_ref[(j - 1) % 2, g])
                acc_ref[g] = pv if j == 1 else before[g][2] * acc_ref[g] + pv
                if diag[g] == j - 1:
                    finish(g, before[g][1])


def _attn_a(proj, bias_a, lam_params, subln_g, lam_init):
    bsz, s, _ = proj.shape
    assert s % A_BLOCK == 0
    nch = s // A_SUB
    kb, vb = OFF_AK // LANES, OFF_AV // LANES
    return pl.pallas_call(
        functools.partial(_attn_a_kernel, lam_init=lam_init),
        grid=(bsz, A_HEADS),
        in_specs=[pl.BlockSpec((1, s, LANES), lambda b, h: (b, 0, h)),
                  pl.BlockSpec((1, s, LANES), lambda b, h: (b, 0, kb + h)),
                  pl.BlockSpec((1, s, LANES), lambda b, h: (b, 0, vb + h)),
                  pl.BlockSpec((1, A_BIAS_TYPES, A_BLOCK, 2 * A_SUB), lambda b, h: (h, 0, 0, 0)),
                  pl.BlockSpec((4, A_QK_DIM), lambda b, h: (0, 0)),
                  pl.BlockSpec((1, A_V_DIM), lambda b, h: (0, 0))],
        out_specs=pl.BlockSpec((1, s, LANES), lambda b, h: (b, 0, h)),
        out_shape=jax.ShapeDtypeStruct((bsz, s, A_WIDTH), BF16),
        scratch_shapes=[pltpu.VMEM((nch, LANES, 2 * A_SUB), BF16),
                        pltpu.VMEM((2, nch, A_BLOCK, 2 * A_SUB), F32),
                        pltpu.VMEM((2, nch, A_BLOCK, 2 * A_SUB), BF16),
                        pltpu.VMEM((nch, A_V_DIM, 2 * A_SUB), F32)],
        compiler_params=_cparams(("parallel", "parallel")),
        name="attn_a",
    )(proj, proj, proj, bias_a, lam_params, subln_g)


def _swa_block(q_ref, kvp_ref, kvc_ref, bias_ref, sink_ref, o_ref, table):
    w = B_WINDOW
    d = B_HEAD_DIM
    group = B_HEADS // B_KV_HEADS
    row = lax.broadcasted_iota(jnp.int32, (LANES, w), 0)
    lane = lax.broadcasted_iota(jnp.int32, (2 * w, LANES), 1)
    zeros = jnp.zeros((LANES - d, w), F32)

    def window(col):
        return jnp.concatenate([kvp_ref[:, col * LANES:(col + 1) * LANES],
                                kvc_ref[:, col * LANES:(col + 1) * LANES]], axis=0)

    kvs, q_ts, swaps = [], [], []
    for g in range(B_KV_HEADS):
        k_off, v_off = g * d, (B_KV_HEADS + g) * d
        swapped = k_off % LANES != 0
        assert (v_off % LANES != 0) != swapped
        k_col, v_col = window(k_off // LANES), window(v_off // LANES)
        kvs.append(jnp.where(lane < d, v_col if swapped else k_col, k_col if swapped else v_col))
        swaps.append(swapped)
        cols = []
        for pair in range(group // 2):
            blk = g * (group // 2) + pair
            t = (q_ref[:, blk * LANES:(blk + 1) * LANES].astype(F32) * (d ** -0.5 * LOG2E)).T
            if swapped:
                cols += [jnp.concatenate([zeros, t[:d]], axis=0), jnp.where(row >= d, t, 0.0)]
            else:
                cols += [jnp.where(row < d, t, 0.0), jnp.concatenate([t[d:], zeros], axis=0)]
        q_ts.append(jnp.concatenate(cols, axis=1).astype(BF16))
    scores = [jnp.dot(kv, q_t, preferred_element_type=F32) + bias_ref[g, table]
              for g, (kv, q_t) in enumerate(zip(kvs, q_ts))]
    soft = []
    for g, s in enumerate(scores):
        sink = sink_ref[g] * LOG2E
        m = jnp.maximum(jnp.max(s, axis=0, keepdims=True), sink)
        e = jnp.exp2(s - m)
        soft.append((e.astype(BF16), 1.0 / (jnp.sum(e, axis=0, keepdims=True) + jnp.exp2(sink - m))))
    outs = [_dot_tn(kv, e) * inv for kv, (e, inv) in zip(kvs, soft)]
    for g, o_t in enumerate(outs):
        v_rows = slice(0, d) if swaps[g] else slice(d, 2 * d)
        for pair in range(group // 2):
            blk = g * (group // 2) + pair
            both = jnp.concatenate([o_t[v_rows, (2 * pair) * w:(2 * pair + 1) * w],
                                    o_t[v_rows, (2 * pair + 1) * w:(2 * pair + 2) * w]], axis=0)
            o_ref[:, blk * LANES:(blk + 1) * LANES] = both.T.astype(BF16)


def _ret_block(q_ref, k_ref, v_ref, g_ref, sin_ref, cos_ref, decay_ref, zeta_ref, xi_ref, gch_ref, o_ref,
               state_ref, first):
    c = C_CHUNK

    @pl.when(first)
    def _():
        state_ref[...] = jnp.zeros_like(state_ref)

    lane = lax.broadcasted_iota(jnp.int32, (c, LANES), 1)
    even = (lane & 1) == 0
    sn = sin_ref[...]
    cs = cos_ref[...]

    def rope(x):
        swapped = jnp.where(even, pltpu.roll(x, LANES - 1, 1), pltpu.roll(x, 1, 1))
        return x * cs + swapped * sn

    qk = []
    for pair in range(C_HEADS // 2):
        q = rope(q_ref[:, pair * LANES:(pair + 1) * LANES].astype(F32))
        k = rope(k_ref[:, pair * LANES:(pair + 1) * LANES].astype(F32) * (C_QK_DIM ** -0.5))
        qk.append((q, q * xi_ref[pair], k.astype(BF16), (k * zeta_ref[pair]).astype(BF16)))
    masked = []
    for h in range(C_HEADS):
        q, qx, _, _ = qk[h // 2]
        in_head = (lane < C_QK_DIM) if h % 2 == 0 else (lane >= C_QK_DIM)
        masked.append((jnp.where(in_head, q, 0.0).astype(BF16), jnp.where(in_head, qx, 0.0).astype(BF16)))
    inner = [(_dot_nt(masked[h][0], qk[h // 2][2]) * decay_ref[h]).astype(BF16) for h in range(C_HEADS)]
    outs = []
    for h in range(C_HEADS):
        vh = v_ref[:, h * LANES:(h + 1) * LANES]
        st = state_ref[h]
        outs.append(jnp.dot(inner[h], vh, preferred_element_type=F32)
                    + jnp.dot(masked[h][1], st.astype(BF16), preferred_element_type=F32))
        state_ref[h] = st * gch_ref[h] + _dot_tn(qk[h // 2][3], vh)
    for h, o in enumerate(outs):
        mu = jnp.mean(o, axis=-1, keepdims=True)
        oc = o - mu
        o = oc * lax.rsqrt(jnp.mean(oc * oc, axis=-1, keepdims=True) + LN_EPS)
        gate = g_ref[:, h * LANES:(h + 1) * LANES].astype(F32)
        o_ref[:, h * LANES:(h + 1) * LANES] = (_silu(gate) * o).astype(BF16)


BC_BLOCKS = 8


def _swa_ret_kernel(q_ref, kvp_ref, kvc_ref, bias_ref, sink_ref, cq_ref, ck_ref, cv_ref, cg_ref, sin_ref, cos_ref,
                    decay_ref, zeta_ref, xi_ref, gch_ref, yb_ref, yc_ref, state_ref):
    c = C_CHUNK
    blocks = q_ref.shape[1] // c
    first_block = pl.program_id(1) * blocks
    for r in range(blocks):
        rows = slice(r * c, (r + 1) * c)
        prev = kvp_ref.at[0] if r == 0 else kvc_ref.at[0, (r - 1) * c:r * c]
        _swa_block(q_ref.at[0, rows], prev, kvc_ref.at[0, rows], bias_ref, sink_ref, yb_ref.at[0, rows],
                   jnp.minimum(first_block + r, 1))
        _ret_block(cq_ref.at[0, rows], ck_ref.at[0, rows], cv_ref.at[0, rows], cg_ref.at[0, rows],
                   sin_ref.at[rows], cos_ref.at[rows], decay_ref, zeta_ref, xi_ref, gch_ref, yc_ref.at[0, rows],
                   state_ref, first_block + r == 0)


def _swa_retention(proj, bias_b, sink_rows, sin_t, cos_t, decay, zeta_t, xi_t, gch):
    assert B_WINDOW == C_CHUNK
    bsz, s, _ = proj.shape
    c = C_CHUNK
    blocks = min(BC_BLOCKS, s // c)
    t = blocks * c
    qw = C_HEADS * C_QK_DIM
    kvw = 2 * B_KV_HEADS * B_HEAD_DIM
    qb, kvb = OFF_BQ // B_WIDTH, OFF_BKV // kvw
    cols = (B_HEADS // B_KV_HEADS) * c
    return pl.pallas_call(
        _swa_ret_kernel,
        grid=(bsz, s // t),
        in_specs=[pl.BlockSpec((1, t, B_WIDTH), lambda b, n: (b, n, qb)),
                  pl.BlockSpec((1, c, kvw), lambda b, n: (b, jnp.maximum(n * blocks - 1, 0), kvb)),
                  pl.BlockSpec((1, t, kvw), lambda b, n: (b, n, kvb)),
                  pl.BlockSpec((B_KV_HEADS, 2, 2 * c, cols), lambda b, n: (0, 0, 0, 0)),
                  pl.BlockSpec((B_KV_HEADS, 1, cols), lambda b, n: (0, 0, 0)),
                  pl.BlockSpec((1, t, qw), lambda b, n: (b, n, OFF_CQ // qw)),
                  pl.BlockSpec((1, t, qw), lambda b, n: (b, n, OFF_CK // qw)),
                  pl.BlockSpec((1, t, C_WIDTH), lambda b, n: (b, n, OFF_CV // C_WIDTH)),
                  pl.BlockSpec((1, t, C_WIDTH), lambda b, n: (b, n, OFF_CG // C_WIDTH)),
                  pl.BlockSpec((t, LANES), lambda b, n: (n, 0)),
                  pl.BlockSpec((t, LANES), lambda b, n: (n, 0)),
                  pl.BlockSpec((C_HEADS, c, c), lambda b, n: (0, 0, 0)),
                  pl.BlockSpec((C_HEADS // 2, c, LANES), lambda b, n: (0, 0, 0)),
                  pl.BlockSpec((C_HEADS // 2, c, LANES), lambda b, n: (0, 0, 0)),
                  pl.BlockSpec((C_HEADS, 1, LANES), lambda b, n: (0, 0, 0))],
        out_specs=[pl.BlockSpec((1, t, B_WIDTH), lambda b, n: (b, n, 0)),
                   pl.BlockSpec((1, t, C_WIDTH), lambda b, n: (b, n, 0))],
        out_shape=[jax.ShapeDtypeStruct((bsz, s, B_WIDTH), BF16), jax.ShapeDtypeStruct((bsz, s, C_WIDTH), BF16)],
        scratch_shapes=[pltpu.VMEM((C_HEADS, LANES, C_V_DIM), F32)],
        compiler_params=_cparams(("parallel", "arbitrary")),
        name="swa_retention",
    )(proj, proj, proj, bias_b, sink_rows, proj, proj, proj, proj, sin_t, cos_t, decay, zeta_t, xi_t, gch)


def _cast_riders(riders_in, riders_out):
    for src, dst in zip(riders_in, riders_out):
        dst[...] = src[...].astype(BF16)


def _row_split_riders(riders, steps):
    views = [r.reshape(r.size // r.shape[-1], r.shape[-1]) for r in riders]
    for v in views:
        assert v.shape[0] % (steps * 16) == 0, v.shape
    return views, [(v.shape[0] // steps, v.shape[1]) for v in views]


def _out_proj_kernel(ya_ref, yb_ref, yc_ref, wa_ref, wb_ref, wc_ref, x_ref, g_ref, b_ref, *rest):
    n_riders = (len(rest) - 1) // 2
    o_ref = rest[n_riders]
    half = o_ref.shape[0] // 2
    spans = [slice(0, half), slice(half, 2 * half)]
    mixes = [jnp.dot(ya_ref[r, :], wa_ref[...], preferred_element_type=F32)
             + jnp.dot(yb_ref[r, :], wb_ref[...], preferred_element_type=F32)
             + jnp.dot(yc_ref[r, :], wc_ref[...], preferred_element_type=F32) for r in spans]
    for r, mix in zip(spans, mixes):
        o_ref[r, :] = _layer_norm(ALPHA * x_ref[r, :] + mix, g_ref[...], b_ref[...])
    _cast_riders(rest[:n_riders], rest[n_riders + 1:])


def _out_proj_ln(ya, yb, yc, w, x, g, b, riders=()):
    n, d = x.shape
    tm = min(512, n)
    row = lambda i: (i, 0)
    fixed = lambda i: (0, 0)
    once = pl.Buffered(1)
    views, blocks = _row_split_riders(riders, n // tm)
    rider_specs = [pl.BlockSpec(blk, row) for blk in blocks]
    outs = pl.pallas_call(
        _out_proj_kernel,
        grid=(n // tm,),
        in_specs=[pl.BlockSpec((tm, A_WIDTH), row),
                  pl.BlockSpec((tm, B_WIDTH), row),
                  pl.BlockSpec((tm, C_WIDTH), row),
                  pl.BlockSpec((A_WIDTH, d), lambda i: (0, 0), pipeline_mode=once),
                  pl.BlockSpec((B_WIDTH, d), lambda i: (1, 0), pipeline_mode=once),
                  pl.BlockSpec((C_WIDTH, d), lambda i: ((A_WIDTH + B_WIDTH) // C_WIDTH, 0), pipeline_mode=once),
                  pl.BlockSpec((tm, d), row),
                  pl.BlockSpec((1, d), fixed),
                  pl.BlockSpec((1, d), fixed)] + rider_specs,
        out_specs=[pl.BlockSpec((tm, d), row)] + rider_specs,
        out_shape=[jax.ShapeDtypeStruct((n, d), F32)] + [jax.ShapeDtypeStruct(v.shape, BF16) for v in views],
        compiler_params=_cparams(("arbitrary",)),
        name="out_proj_ln",
    )(ya, yb, yc, w, w, w, x, g, b, *views)
    return outs[0], [o.reshape(r.shape) for o, r in zip(outs[1:], riders)]


FFN_TM = 512
FFN_TF = 512


def _swiglu_step(xb, wg, wu, wd):
    hg = jnp.dot(xb, wg, preferred_element_type=F32)
    hu = jnp.dot(xb, wu, preferred_element_type=F32)
    return jnp.dot((_silu(hg) * hu).astype(BF16), wd, preferred_element_type=F32)


def _ffn_kernel(x_ref, wg_ref, wu_ref, wd_ref, g_ref, b_ref, *rest):
    n_riders = (len(rest) - 2) // 2
    riders_in, o_ref = rest[:n_riders], rest[n_riders]
    riders_out, xb_ref = rest[n_riders + 1:-1], rest[-1]
    f = pl.program_id(1)

    @pl.when(f == 0)
    def _():
        xb = x_ref[...].astype(BF16)
        xb_ref[...] = xb
        o_ref[...] = _swiglu_step(xb, wg_ref[...], wu_ref[...], wd_ref[...])
        _cast_riders(riders_in, riders_out)

    @pl.when(f > 0)
    def _():
        o_ref[...] += _swiglu_step(xb_ref[...], wg_ref[...], wu_ref[...], wd_ref[...])
        _cast_riders(riders_in, riders_out)

    @pl.when(f == pl.num_programs(1) - 1)
    def _():
        o_ref[...] = _layer_norm(ALPHA * x_ref[...] + o_ref[...], g_ref[...], b_ref[...])


def _ffn(x, w_gate, w_up, w_down, g, b, riders=()):
    n, d = x.shape
    ff = w_gate.shape[1]
    tm = min(FFN_TM, n)
    tf = FFN_TF
    ni, nf = n // tm, ff // tf
    flat, rider_specs = [], []
    for r in riders:
        cols = r.shape[-1]
        rows = r.size // cols
        flat.append(r.reshape(rows, cols))
        if rows % (ni * nf * 16) == 0:
            rider_specs.append(pl.BlockSpec((rows // (ni * nf), cols), lambda i, f: (i * nf + f, 0)))
        else:
            assert rows % (ni * 16) == 0 and cols % (nf * LANES) == 0
            rider_specs.append(pl.BlockSpec((rows // ni, cols // nf), lambda i, f: (i, f)))
    outs = pl.pallas_call(
        _ffn_kernel,
        grid=(n // tm, ff // tf),
        in_specs=[pl.BlockSpec((tm, d), lambda i, f: (i, 0)),
                  pl.BlockSpec((d, tf), lambda i, f: (0, f)),
                  pl.BlockSpec((d, tf), lambda i, f: (0, f)),
                  pl.BlockSpec((tf, d), lambda i, f: (f, 0)),
                  pl.BlockSpec((1, d), lambda i, f: (0, 0)),
                  pl.BlockSpec((1, d), lambda i, f: (0, 0))] + rider_specs,
        out_specs=[pl.BlockSpec((tm, d), lambda i, f: (i, 0))] + rider_specs,
        out_shape=[jax.ShapeDtypeStruct((n, d), F32)] + [jax.ShapeDtypeStruct(r.shape, BF16) for r in flat],
        scratch_shapes=[pltpu.VMEM((tm, d), BF16)],
        compiler_params=_cparams(("arbitrary", "arbitrary")),
        name="ffn_ln",
    )(x, w_gate, w_up, w_down, g, b, *flat)
    return outs[0], [o.reshape(r.shape) for o, r in zip(outs[1:], riders)]


def _row_copy(src, dst, sem, s, t):
    return pltpu.make_async_copy(src.at[pl.ds(s, 1)], dst.at[pl.ds(t, 1)], sem)


def _ffn_grouped_kernel(eid_ref, nact_ref, src_ref, x_hbm, wg_ref, wu_ref, wd_ref, o_ref, rows_ref, xb_ref, sem,
                        *, tm, per_step):
    i = pl.program_id(0)
    f = pl.program_id(1)
    nact = nact_ref[0]
    active = i < nact
    slot = lax.rem(i, 2)
    buf_rows = rows_ref.shape[1]
    last_slot_row = src_ref.shape[0] - 1

    def fetch(tile, into, first_row, count):
        for r in range(count):
            row = first_row + r
            token = src_ref[jnp.minimum(tile * tm + row, last_slot_row)]
            _row_copy(x_hbm, rows_ref.at[into], sem.at[into], token, row).start()

    def wait_buffer(which):
        pltpu.make_async_copy(x_hbm.at[pl.ds(0, buf_rows)], rows_ref.at[which], sem.at[which]).wait()

    @pl.when(jnp.logical_and(i == 0, f == 0))
    def _():
        lax.fori_loop(0, buf_rows // per_step, lambda s, c: (fetch(0, 0, s * per_step, per_step), c)[1], 0)

    @pl.when(jnp.logical_and(f == 0, i <= nact))
    def _():
        wait_buffer(slot)

    @pl.when(jnp.logical_and(f == 0, jnp.logical_not(active)))
    def _():
        o_ref[...] = jnp.zeros_like(o_ref)

    @pl.when(jnp.logical_and(f == 0, active))
    def _():
        fetch(i + 1, 1 - slot, 0, per_step)
        xb = rows_ref[slot, :tm, :].astype(BF16)
        xb_ref[...] = xb
        o_ref[...] = _swiglu_step(xb, wg_ref[0], wu_ref[0], wd_ref[0])

    @pl.when(jnp.logical_and(f > 0, active))
    def _():
        fetch(i + 1, 1 - slot, f * per_step, per_step)
        o_ref[...] += _swiglu_step(xb_ref[...], wg_ref[0], wu_ref[0], wd_ref[0])


def _ffn_grouped(x, src, eid, nact, w_gate, w_up, w_down):
    n, d = x.shape
    slots = src.shape[0]
    ff = w_gate.shape[2]
    tm = min(FFN_TM, n)
    tf = FFN_TF
    nf = ff // tf
    per_step = -(-tm // nf)
    per_step += (-per_step) % 8
    assert per_step * nf <= n

    def fidx(i, f, nact_ref):
        return jnp.where(i < nact_ref[0], f, nf - 1)

    grid_spec = pltpu.PrefetchScalarGridSpec(
        num_scalar_prefetch=3,
        grid=(slots // tm, nf),
        in_specs=[pl.BlockSpec(memory_space=pl.ANY),
                  pl.BlockSpec((1, d, tf), lambda i, f, e, a, s: (e[i], 0, fidx(i, f, a))),
                  pl.BlockSpec((1, d, tf), lambda i, f, e, a, s: (e[i], 0, fidx(i, f, a))),
                  pl.BlockSpec((1, tf, d), lambda i, f, e, a, s: (e[i], fidx(i, f, a), 0))],
        out_specs=pl.BlockSpec((tm, d), lambda i, f, e, a, s: (i, 0)),
        scratch_shapes=[pltpu.VMEM((2, per_step * nf, d), F32), pltpu.VMEM((tm, d), BF16),
                        pltpu.SemaphoreType.DMA((2,))],
    )
    return pl.pallas_call(
        functools.partial(_ffn_grouped_kernel, tm=tm, per_step=per_step),
        grid_spec=grid_spec,
        out_shape=jax.ShapeDtypeStruct((slots, d), F32),
        compiler_params=_cparams(("arbitrary", "arbitrary")),
        name="ffn_grouped",
    )(eid, nact, src, x, w_gate, w_up, w_down)


ROUTER_TM = 512


def _router_kernel(x_ref, wr_ref, tri_ref, meta_ref, cnt_ref, carry_ref):
    @pl.when(pl.program_id(0) == 0)
    def _():
        carry_ref[...] = jnp.zeros_like(carry_ref)

    x = x_ref[...]
    x_hi = x.astype(BF16)
    x_lo = (x - x_hi.astype(F32)).astype(BF16)
    parts = (jnp.dot(x_hi, wr_ref[...], preferred_element_type=F32)
             + jnp.dot(x_lo, wr_ref[...], preferred_element_type=F32))
    logits = parts + pltpu.roll(parts, LANES - N_EXPERTS, 1)
    lane = lax.broadcasted_iota(jnp.int32, logits.shape, 1)
    logits = jnp.where(lane < N_EXPERTS, logits, -jnp.inf)
    m1 = jnp.max(logits, axis=-1, keepdims=True)
    i1 = jnp.min(jnp.where(logits == m1, lane, LANES), axis=-1, keepdims=True)
    rest = jnp.where(lane == i1, -jnp.inf, logits)
    m2 = jnp.max(rest, axis=-1, keepdims=True)
    i2 = jnp.min(jnp.where(rest == m2, lane, LANES), axis=-1, keepdims=True)
    e2 = jnp.exp(m2 - m1)
    w1 = 1.0 / (1.0 + e2)
    w2 = e2 * w1
    hit1 = lane == i1
    hit2 = lane == i2
    onehot = jnp.where(jnp.logical_or(hit1, hit2), 1.0, 0.0)
    before = jnp.dot(tri_ref[...], onehot.astype(BF16), preferred_element_type=F32) + carry_ref[...]
    r1 = jnp.sum(jnp.where(hit1, before, 0.0), axis=-1, keepdims=True)
    r2 = jnp.sum(jnp.where(hit2, before, 0.0), axis=-1, keepdims=True)
    carry_ref[...] = carry_ref[...] + jnp.sum(onehot, axis=0, keepdims=True)
    cnt_ref[...] = carry_ref[...]
    meta = jnp.where(lane == 0, i1.astype(F32), 0.0)
    meta = jnp.where(lane == 1, i2.astype(F32), meta)
    meta = jnp.where(lane == 2, r1, meta)
    meta = jnp.where(lane == 3, r2, meta)
    meta = jnp.where(lane == 4, w1, meta)
    meta = jnp.where(lane == 5, w2, meta)
    meta_ref[...] = meta


def _router(x, w_router):
    n, d = x.shape
    tm = min(ROUTER_TM, n)
    w_hi = w_router.astype(BF16)
    w_lo = (w_router.astype(F32) - w_hi.astype(F32)).astype(BF16)
    wr = jnp.zeros((d, LANES), BF16).at[:, :N_EXPERTS].set(w_hi).at[:, N_EXPERTS:2 * N_EXPERTS].set(w_lo)
    tri = jnp.asarray(np.tril(np.ones((tm, tm), np.float32), -1), BF16)
    return pl.pallas_call(
        _router_kernel,
        grid=(n // tm,),
        in_specs=[pl.BlockSpec((tm, d), lambda i: (i, 0)),
                  pl.BlockSpec((d, LANES), lambda i: (0, 0)),
                  pl.BlockSpec((tm, tm), lambda i: (0, 0))],
        out_specs=[pl.BlockSpec((tm, LANES), lambda i: (i, 0)),
                   pl.BlockSpec((1, LANES), lambda i: (0, 0))],
        out_shape=[jax.ShapeDtypeStruct((n, LANES), F32), jax.ShapeDtypeStruct((1, LANES), F32)],
        scratch_shapes=[pltpu.VMEM((1, LANES), F32)],
        compiler_params=_cparams(("arbitrary",)),
        name="router",
    )(x, wr, tri)


MOVE_TM = 256
ISSUE_UNROLL = 8


def _combine_kernel(d1_ref, d2_ref, ys_hbm, x_ref, meta_ref, g_ref, b_ref, o_ref, buf_ref, sem, *, tm):
    i = pl.program_id(0)
    last = pl.num_programs(0) - 1
    slot = lax.rem(i, 2)

    def start_row(tile, into, t):
        row = tile * tm + t
        _row_copy(ys_hbm, buf_ref.at[into, 0], sem.at[into], d1_ref[row], t).start()
        _row_copy(ys_hbm, buf_ref.at[into, 1], sem.at[into], d2_ref[row], t).start()

    def wait_slot(which):
        for k in range(2):
            pltpu.make_async_copy(ys_hbm.at[pl.ds(0, tm)], buf_ref.at[which, k], sem.at[which]).wait()

    @pl.when(i == 0)
    def _():
        lax.fori_loop(0, tm, lambda t, c: (start_row(0, 0, t), c)[1], 0, unroll=ISSUE_UNROLL)

    wait_slot(slot)
    nxt = jnp.minimum(i + 1, last)
    for t in range(tm):
        start_row(nxt, 1 - slot, t)
    meta = meta_ref[...]
    lane = lax.broadcasted_iota(jnp.int32, meta.shape, 1)
    w1 = jnp.sum(jnp.where(lane == 4, meta, 0.0), axis=-1, keepdims=True)
    w2 = jnp.sum(jnp.where(lane == 5, meta, 0.0), axis=-1, keepdims=True)
    f = w1 * buf_ref[slot, 0] + w2 * buf_ref[slot, 1]
    o_ref[...] = _layer_norm(ALPHA * x_ref[...] + f, g_ref[...], b_ref[...])

    @pl.when(i == last)
    def _():
        wait_slot(1 - slot)


def _combine(ys, x, meta, d1, d2, g, b):
    n, d = x.shape
    tm = min(MOVE_TM, n)
    grid_spec = pltpu.PrefetchScalarGridSpec(
        num_scalar_prefetch=2,
        grid=(n // tm,),
        in_specs=[pl.BlockSpec(memory_space=pl.ANY),
                  pl.BlockSpec((tm, d), lambda i, a, c: (i, 0)),
                  pl.BlockSpec((tm, LANES), lambda i, a, c: (i, 0)),
                  pl.BlockSpec((1, d), lambda i, a, c: (0, 0)),
                  pl.BlockSpec((1, d), lambda i, a, c: (0, 0))],
        out_specs=pl.BlockSpec((tm, d), lambda i, a, c: (i, 0)),
        scratch_shapes=[pltpu.VMEM((2, 2, tm, d), F32), pltpu.SemaphoreType.DMA((2,))],
    )
    return pl.pallas_call(
        functools.partial(_combine_kernel, tm=tm),
        grid_spec=grid_spec,
        out_shape=jax.ShapeDtypeStruct((n, d), F32),
        compiler_params=_cparams(("arbitrary",)),
        name="moe_combine",
    )(d1, d2, ys, x, meta, g, b)


def _moe(x, w_router, w_gate, w_up, w_down, g, b):
    n, d = x.shape
    tm = min(FFN_TM, n)
    meta, cnt = _router(x, w_router)
    i1 = meta[:, 0].astype(jnp.int32)
    i2 = meta[:, 1].astype(jnp.int32)
    counts = cnt[0, :N_EXPERTS].astype(jnp.int32)
    tiles = (counts + tm - 1) // tm
    tile_end = jnp.cumsum(tiles)
    group_start = (tile_end - tiles) * tm
    experts = jnp.arange(N_EXPERTS, dtype=jnp.int32)[None, :]
    start_of = lambda idx: jnp.sum(jnp.where(idx[:, None] == experts, group_start[None, :], 0), axis=1)
    d1 = start_of(i1) + meta[:, 2].astype(jnp.int32)
    d2 = start_of(i2) + meta[:, 3].astype(jnp.int32)
    max_tiles = (2 * n) // tm + N_EXPERTS
    tile_ids = jnp.arange(max_tiles, dtype=jnp.int32)[:, None]
    eid = jnp.minimum(jnp.sum((tile_ids >= tile_end[None, :]).astype(jnp.int32), axis=1), N_EXPERTS - 1)
    nact = tile_end[-1:].astype(jnp.int32)
    token = jnp.arange(n, dtype=jnp.int32)
    src = jnp.zeros((max_tiles * tm,), jnp.int32).at[jnp.concatenate([d1, d2])].set(
        jnp.concatenate([token, token]), unique_indices=True)
    ys = _ffn_grouped(x, src, eid.astype(jnp.int32), nact, w_gate, w_up, w_down)
    return _combine(ys, x, meta, d1, d2, g, b)


def _lambda_init(layer_idx):
    return 0.8 - 0.6 * math.exp(-0.3 * layer_idx)


def _static_tables(s):
    c = C_CHUNK
    ang = jnp.repeat(1.0 / (10000.0 ** jnp.linspace(0.0, 1.0, C_QK_DIM // 2, dtype=F32)), 2)
    ang = jnp.arange(s, dtype=F32)[:, None] * ang[None, :]
    sign = jnp.where(jnp.arange(C_QK_DIM) % 2 == 0, -1.0, 1.0).astype(F32)
    sin_t = jnp.tile(jnp.sin(ang) * sign[None, :], (1, 2))
    cos_t = jnp.tile(jnp.cos(ang), (1, 2))
    log_g = jnp.log(1.0 - jnp.exp2(-5.0 - jnp.arange(C_HEADS, dtype=F32)))
    pos = jnp.arange(c)
    rel = (pos[:, None] - pos[None, :]).astype(F32)
    decay = jnp.where((rel >= 0)[None], jnp.exp(jnp.maximum(rel, 0.0)[None] * log_g[:, None, None]), 0.0)
    zeta = jnp.exp((c - 1 - pos).astype(F32)[:, None] * log_g[None, :])
    xi = jnp.exp((pos + 1).astype(F32)[:, None] * log_g[None, :])
    per_pair = lambda t: jnp.repeat(t.T.reshape(C_HEADS // 2, 2, c), C_QK_DIM, axis=1).transpose(0, 2, 1)
    gch = jnp.broadcast_to(jnp.exp(c * log_g)[:, None, None], (C_HEADS, 1, LANES))
    dist_a = np.arange(A_SUB)[None, :] - np.arange(A_BLOCK)[:, None]
    types = [jnp.where(dist_a + off >= 0, _rel_bucket(jnp.asarray(dist_a + off)), REL_BUCKETS)
             for off in (0, A_SUB, A_BLOCK)]
    types.append(jnp.full(dist_a.shape, REL_BUCKETS - 1, jnp.int32))
    bkt_a = jnp.concatenate([jnp.tile(b, (1, 2)) for b in types], axis=0).astype(jnp.int32)
    w = B_WINDOW
    dist = np.arange(w)[:, None] + w - np.arange(2 * w)[None, :]
    band = (dist >= 0) & (dist < w)
    has_prev = np.stack([np.broadcast_to(np.arange(2 * w)[None, :] >= w, band.shape), np.ones_like(band)])
    bkt_b = jnp.where(band[None] & has_prev, _rel_bucket(jnp.asarray(dist))[None], REL_BUCKETS)
    bkt_b = bkt_b.transpose(0, 2, 1).reshape(2 * 2 * w, w).astype(jnp.int32)
    return sin_t, cos_t, decay, per_pair(zeta), per_pair(xi), gch, bkt_a, bkt_b


def kernel(x, w_in, rel_bias, a_lambda, a_subln_g, b_sinks, w_out, ln_mix_g, ln_mix_b, ln_ffn_g, ln_ffn_b,
           dense_w_gate, dense_w_up, dense_w_down, moe_router, moe_w_gate, moe_w_up, moe_w_down):
    bsz, s, d = x.shape
    n = bsz * s
    sin_t, cos_t, decay, zeta_t, xi_t, gch, bkt_a, bkt_b = _static_tables(s)
    tab_t = rel_bias.astype(F32).T
    bias_a = _bias_lookup(tab_t[:A_HEADS] * LOG2E, bkt_a).reshape(A_HEADS, A_BIAS_TYPES, A_BLOCK, 2 * A_SUB)
    group = B_HEADS // B_KV_HEADS
    bias_b = _bias_lookup(tab_t[A_HEADS:] * LOG2E, bkt_b)
    bias_b = bias_b.reshape(B_KV_HEADS, group, 2, 2 * B_WINDOW, B_WINDOW).transpose(0, 2, 3, 1, 4)
    bias_b = bias_b.reshape(B_KV_HEADS, 2, 2 * B_WINDOW, group * B_WINDOW)
    xf = x.reshape(n, d).astype(F32)
    w_in_bf16, w_out_bf16 = w_in[0].astype(BF16), w_out[0].astype(BF16)
    for l in range(DEPTH):
        w_in_l = jnp.concatenate([w_in_bf16[:, a:b] for a, b in _PERM_RUNS]
                                 + [jnp.zeros((d, PROJ_PAD - PROJ_WIDTH), BF16)], axis=1)
        proj = _in_proj(xf, w_in_l).reshape(bsz, s, PROJ_PAD)
        ya = _attn_a(proj, bias_a, a_lambda[l].astype(F32), a_subln_g[l].astype(F32).reshape(1, A_V_DIM),
                     _lambda_init(l))
        sink_rows = jnp.repeat(b_sinks[l].astype(F32).reshape(B_KV_HEADS, 1, group), B_WINDOW, axis=2)
        yb, yc = _swa_retention(proj, bias_b, sink_rows, sin_t, cos_t, decay, zeta_t, xi_t, gch)
        g_mix = ln_mix_g[l].astype(F32).reshape(1, d)
        b_mix = ln_mix_b[l].astype(F32).reshape(1, d)
        j = l // 2
        dense = (dense_w_gate[j], dense_w_up[j], dense_w_down[j]) if l % 2 == 0 else ()
        ahead = (w_in[l + 1], w_out[l + 1]) if l + 1 < DEPTH else ()
        xf, cast = _out_proj_ln(ya.reshape(n, A_WIDTH), yb.reshape(n, B_WIDTH), yc.reshape(n, C_WIDTH),
                                w_out_bf16, xf, g_mix, b_mix, dense + ahead)
        dense_bf16 = cast[:len(dense)]
        if ahead:
            w_in_bf16, w_out_bf16 = cast[len(dense):]
        g_ffn = ln_ffn_g[l].astype(F32).reshape(1, d)
        b_ffn = ln_ffn_b[l].astype(F32).reshape(1, d)
        if l % 2 == 0:
            experts = (moe_w_gate[j], moe_w_up[j], moe_w_down[j]) if l + 1 < DEPTH else ()
            xf, moe_bf16 = _ffn(xf, *dense_bf16, g_ffn, b_ffn, experts)
        else:
            assert l > 0, "expert weights are cast by the preceding dense layer"
            xf = _moe(xf, moe_router[j], *moe_bf16, g_ffn, b_ffn)
    return xf.reshape(bsz, s, d).astype(x.dtype)
```

```python
import functools
import math

import jax
import jax.numpy as jnp
import numpy as np
from jax import lax
from jax.experimental import pallas as pl
from jax.experimental.pallas import tpu as pltpu

F32 = jnp.float32
BF16 = jnp.bfloat16

D_MODEL = 2048
DEPTH = 2
A_HEADS = 6
A_QK_DIM = 64
A_V_DIM = 128
B_HEADS = 12
B_KV_HEADS = 3
B_HEAD_DIM = 64
B_WINDOW = 128
C_HEADS = 4
C_QK_DIM = 64
C_V_DIM = 128
C_CHUNK = 128
A_WIDTH = A_HEADS * A_V_DIM
B_WIDTH = B_HEADS * B_HEAD_DIM
C_WIDTH = C_HEADS * C_V_DIM
REL_BUCKETS = 32
REL_MAX_DIST = 128
D_FF = 5632
N_EXPERTS = 8
ALPHA = (2.0 * DEPTH) ** 0.25
LN_EPS = 1e-5
NEG = -1e30

LANES = 128
VMEM_LIMIT = 56 * 1024 * 1024

_REF_SIZES = [768, 768, 768, 768, 192, 192, 256, 256, 512, 512]
_REF_OFF = [int(v) for v in np.concatenate([[0], np.cumsum(_REF_SIZES)[:-1]])]
PROJ_WIDTH = int(sum(_REF_SIZES))
OFF_AQ, OFF_AK, OFF_AV, OFF_BQ, OFF_CV, OFF_CG, OFF_CQ, OFF_CK, OFF_BKV = (
    0, 768, 1536, 2304, 3072, 3584, 4096, 4352, 4608)


def _proj_perm():
    aq, ak, av, bq, bk, bv, cq, ck, cv, cg = [np.arange(o, o + s) for o, s in zip(_REF_OFF, _REF_SIZES)]
    perm = np.concatenate([aq, ak, av, bq, cv, cg, cq, ck, bk, bv])
    assert perm.shape[0] == PROJ_WIDTH
    return perm


def _perm_runs():
    perm = _proj_perm()
    cuts = np.flatnonzero(np.diff(perm) != 1) + 1
    return [(int(r[0]), int(r[-1]) + 1) for r in np.split(perm, cuts)]


_PERM_RUNS = _perm_runs()


def _cparams(sem):
    return pltpu.CompilerParams(dimension_semantics=sem, vmem_limit_bytes=VMEM_LIMIT)


def _layer_norm(z, g, b):
    mu = jnp.mean(z, axis=-1, keepdims=True)
    zc = z - mu
    var = jnp.mean(zc * zc, axis=-1, keepdims=True)
    return zc * lax.rsqrt(var + LN_EPS) * g + b


def _silu(x):
    return x / (1.0 + jnp.exp(-x))


def _dot_nt(a, b):
    return lax.dot_general(a, b, (((1,), (1,)), ((), ())), preferred_element_type=F32)


def _dot_tn(a, b):
    return lax.dot_general(a, b, (((0,), (0,)), ((), ())), preferred_element_type=F32)


def _bias_kernel(tab_ref, bkt_ref, o_ref):
    h = pl.program_id(0)
    bkt = bkt_ref[...]
    acc = jnp.full(bkt.shape, NEG, F32)
    for b in range(REL_BUCKETS):
        acc = jnp.where(bkt == b, tab_ref[h, b], acc)
    o_ref[0] = acc


def _bias_lookup(tab_t, bkt):
    nh = tab_t.shape[0]
    r, c = bkt.shape
    return pl.pallas_call(
        _bias_kernel,
        grid=(nh,),
        in_specs=[pl.BlockSpec(memory_space=pltpu.SMEM),
                  pl.BlockSpec((r, c), lambda h: (0, 0))],
        out_specs=pl.BlockSpec((1, r, c), lambda h: (h, 0, 0)),
        out_shape=jax.ShapeDtypeStruct((nh, r, c), F32),
        compiler_params=_cparams(("arbitrary",)),
        name="bias_lookup",
    )(tab_t, bkt)


def _rel_bucket(dist):
    max_exact = REL_BUCKETS // 2
    d = jnp.maximum(dist, 0)
    ratio = jnp.maximum(d, 1).astype(F32) / max_exact
    large = max_exact + (jnp.log(ratio) / math.log(REL_MAX_DIST / max_exact)
                         * (REL_BUCKETS - max_exact)).astype(jnp.int32)
    large = jnp.minimum(large, REL_BUCKETS - 1)
    return jnp.where(d < max_exact, d, large)


MXU_COLS = 256
IN_PROJ_COL_STEPS = 4
PROJ_PAD = -(-PROJ_WIDTH // (IN_PROJ_COL_STEPS * MXU_COLS)) * (IN_PROJ_COL_STEPS * MXU_COLS)


def _in_proj_kernel(x_ref, w_ref, o_ref, xb_ref):
    @pl.when(pl.program_id(1) == 0)
    def _():
        xb = x_ref[...].astype(BF16)
        xb_ref[...] = xb
        o_ref[...] = jnp.dot(xb, w_ref[...], preferred_element_type=F32).astype(BF16)

    @pl.when(pl.program_id(1) > 0)
    def _():
        o_ref[...] = jnp.dot(xb_ref[...], w_ref[...], preferred_element_type=F32).astype(BF16)


def _in_proj(x, w):
    n, d = x.shape
    p = w.shape[1]
    tm = min(1024, n)
    tn = p // IN_PROJ_COL_STEPS
    return pl.pallas_call(
        _in_proj_kernel,
        grid=(n // tm, p // tn),
        in_specs=[pl.BlockSpec((tm, d), lambda i, j: (i, 0)),
                  pl.BlockSpec((d, tn), lambda i, j: (0, j))],
        out_specs=pl.BlockSpec((tm, tn), lambda i, j: (i, j)),
        out_shape=jax.ShapeDtypeStruct((n, p), BF16),
        scratch_shapes=[pltpu.VMEM((tm, d), BF16)],
        compiler_params=_cparams(("parallel", "arbitrary")),
        name="in_proj",
    )(x, w)


A_BLOCK = 256
A_SUB = 128
A_BIAS_TYPES = 4
LOG2E = math.log2(math.e)


def _attn_a_kernel(q_ref, k_ref, v_ref, bias_ref, lam_ref, g_ref, o_ref, qt_ref, s_ref, p_ref, acc_ref, *,
                   lam_init):
    t = A_BLOCK
    per_key_block = t // A_SUB
    n_chains = q_ref.shape[1] // A_SUB
    n_blocks = k_ref.shape[1] // t
    diag = [g // per_key_block for g in range(n_chains)]
    lp = lam_ref[...]
    lam = (jnp.exp(jnp.sum(lp[0:1] * lp[1:2], axis=-1, keepdims=True))
           - jnp.exp(jnp.sum(lp[2:3] * lp[3:4], axis=-1, keepdims=True)) + lam_init)
    lane = lax.broadcasted_iota(jnp.int32, (A_SUB, LANES), 1)
    for g in range(n_chains):
        q = q_ref[0, g * A_SUB:(g + 1) * A_SUB, :].astype(F32) * (A_QK_DIM ** -0.5 * LOG2E)
        qq = jnp.concatenate([jnp.where(lane < A_QK_DIM, q, 0.0), jnp.where(lane >= A_QK_DIM, q, 0.0)], axis=0)
        qt_ref[g] = qq.T.astype(BF16)

    def needing(j):
        return [g for g in range(n_chains) if diag[g] >= j]

    def scores_into(j):
        kj = k_ref[0, j * t:(j + 1) * t, :]
        for g in needing(j):
            back = diag[g] - j
            near = (0, 2) if g % per_key_block == 0 else (1, 3)
            kind = near[0] if back == 0 else (near[1] if back == 1 else 3)
            s_ref[j % 2, g] = jnp.dot(kj, qt_ref[g], preferred_element_type=F32) + bias_ref[0, kind]

    def finish(g, l_fin):
        o_all = acc_ref[g] / l_fin
        o = (o_all[:, :A_SUB] - lam * o_all[:, A_SUB:]).T
        o = o * lax.rsqrt(jnp.mean(o * o, axis=-1, keepdims=True) + LN_EPS) * g_ref[...]
        o_ref[0, g * A_SUB:(g + 1) * A_SUB, :] = (o * (1.0 - lam_init)).astype(BF16)

    stats = [(jnp.full((1, 2 * A_SUB), NEG, F32), jnp.zeros((1, 2 * A_SUB), F32), None) for _ in range(n_chains)]
    scores_into(0)
    for j in range(n_blocks + 1):
        if j + 1 < n_blocks:
            scores_into(j + 1)
        before = list(stats)
        if j < n_blocks:
            for g in needing(j):
                m_old, l_old, _ = before[g]
                s = s_ref[j % 2, g]
                m_new = jnp.maximum(m_old, jnp.max(s, axis=0, keepdims=True))
                p = jnp.exp2(s - m_new)
                alpha = jnp.exp2(m_old - m_new)
                stats[g] = (m_new, alpha * l_old + jnp.sum(p, axis=0, keepdims=True), alpha)
                p_ref[j % 2, g] = p.astype(BF16)
        if j >= 1:
            vj = v_ref[0, (j - 1) * t:j * t, :]
            for g in needing(j - 1):
                pv = _dot_tn(vj, p_ref[(j - 1) % 2, g])
                acc_ref[g] = pv if j == 1 else before[g][2] * acc_ref[g] + pv
                if diag[g] == j - 1:
                    finish(g, before[g][1])


def _attn_a(proj, bias_a, lam_params, subln_g, lam_init):
    bsz, s, _ = proj.shape
    assert s % A_BLOCK == 0
    nch = s // A_SUB
    kb, vb = OFF_AK // LANES, OFF_AV // LANES
    return pl.pallas_call(
        functools.partial(_attn_a_kernel, lam_init=lam_init),
        grid=(bsz, A_HEADS),
        in_specs=[pl.BlockSpec((1, s, LANES), lambda b, h: (b, 0, h)),
                  pl.BlockSpec((1, s, LANES), lambda b, h: (b, 0, kb + h)),
                  pl.BlockSpec((1, s, LANES), lambda b, h: (b, 0, vb + h)),
                  pl.BlockSpec((1, A_BIAS_TYPES, A_BLOCK, 2 * A_SUB), lambda b, h: (h, 0, 0, 0)),
                  pl.BlockSpec((4, A_QK_DIM), lambda b, h: (0, 0)),
                  pl.BlockSpec((1, A_V_DIM), lambda b, h: (0, 0))],
        out_specs=pl.BlockSpec((1, s, LANES), lambda b, h: (b, 0, h)),
        out_shape=jax.ShapeDtypeStruct((bsz, s, A_WIDTH), BF16),
        scratch_shapes=[pltpu.VMEM((nch, LANES, 2 * A_SUB), BF16),
                        pltpu.VMEM((2, nch, A_BLOCK, 2 * A_SUB), F32),
                        pltpu.VMEM((2, nch, A_BLOCK, 2 * A_SUB), BF16),
                        pltpu.VMEM((nch, A_V_DIM, 2 * A_SUB), F32)],
        compiler_params=_cparams(("parallel", "parallel")),
        name="attn_a",
    )(proj, proj, proj, bias_a, lam_params, subln_g)


def _swa_block(q_ref, kvp_ref, kvc_ref, bias_ref, sink_ref, o_ref, table):
    w = B_WINDOW
    d = B_HEAD_DIM
    group = B_HEADS // B_KV_HEADS
    row = lax.broadcasted_iota(jnp.int32, (LANES, w), 0)
    lane = lax.broadcasted_iota(jnp.int32, (2 * w, LANES), 1)
    zeros = jnp.zeros((LANES - d, w), F32)

    def window(col):
        return jnp.concatenate([kvp_ref[:, col * LANES:(col + 1) * LANES],
                                kvc_ref[:, col * LANES:(col + 1) * LANES]], axis=0)

    kvs, q_ts, swaps = [], [], []
    for g in range(B_KV_HEADS):
        k_off, v_off = g * d, (B_KV_HEADS + g) * d
        swapped = k_off % LANES != 0
        assert (v_off % LANES != 0) != swapped
        k_col, v_col = window(k_off // LANES), window(v_off // LANES)
        kvs.append(jnp.where(lane < d, v_col if swapped else k_col, k_col if swapped else v_col))
        swaps.append(swapped)
        cols = []
        for pair in range(group // 2):
            blk = g * (group // 2) + pair
            t = (q_ref[:, blk * LANES:(blk + 1) * LANES].astype(F32) * (d ** -0.5 * LOG2E)).T
            if swapped:
                cols += [jnp.concatenate([zeros, t[:d]], axis=0), jnp.where(row >= d, t, 0.0)]
            else:
                cols += [jnp.where(row < d, t, 0.0), jnp.concatenate([t[d:], zeros], axis=0)]
        q_ts.append(jnp.concatenate(cols, axis=1).astype(BF16))
    scores = [jnp.dot(kv, q_t, preferred_element_type=F32) + bias_ref[g, table]
              for g, (kv, q_t) in enumerate(zip(kvs, q_ts))]
    soft = []
    for g, s in enumerate(scores):
        sink = sink_ref[g] * LOG2E
        m = jnp.maximum(jnp.max(s, axis=0, keepdims=True), sink)
        e = jnp.exp2(s - m)
        soft.append((e.astype(BF16), 1.0 / (jnp.sum(e, axis=0, keepdims=True) + jnp.exp2(sink - m))))
    outs = [_dot_tn(kv, e) * inv for kv, (e, inv) in zip(kvs, soft)]
    for g, o_t in enumerate(outs):
        v_rows = slice(0, d) if swaps[g] else slice(d, 2 * d)
        for pair in range(group // 2):
            blk = g * (group // 2) + pair
            both = jnp.concatenate([o_t[v_rows, (2 * pair) * w:(2 * pair + 1) * w],
                                    o_t[v_rows, (2 * pair + 1) * w:(2 * pair + 2) * w]], axis=0)
            o_ref[:, blk * LANES:(blk + 1) * LANES] = both.T.astype(BF16)


def _ret_block(q_ref, k_ref, v_ref, g_ref, sin_ref, cos_ref, decay_ref, zeta_ref, xi_ref, gch_ref, o_ref,
               state_ref, first):
    c = C_CHUNK

    @pl.when(first)
    def _():
        state_ref[...] = jnp.zeros_like(state_ref)

    lane = lax.broadcasted_iota(jnp.int32, (c, LANES), 1)
    even = (lane & 1) == 0
    sn = sin_ref[...]
    cs = cos_ref[...]

    def rope(x):
        swapped = jnp.where(even, pltpu.roll(x, LANES - 1, 1), pltpu.roll(x, 1, 1))
        return x * cs + swapped * sn

    qk = []
    for pair in range(C_HEADS // 2):
        q = rope(q_ref[:, pair * LANES:(pair + 1) * LANES].astype(F32))
        k = rope(k_ref[:, pair * LANES:(pair + 1) * LANES].astype(F32) * (C_QK_DIM ** -0.5))
        qk.append((q, q * xi_ref[pair], k.astype(BF16), (k * zeta_ref[pair]).astype(BF16)))
    masked = []
    for h in range(C_HEADS):
        q, qx, _, _ = qk[h // 2]
        in_head = (lane < C_QK_DIM) if h % 2 == 0 else (lane >= C_QK_DIM)
        masked.append((jnp.where(in_head, q, 0.0).astype(BF16), jnp.where(in_head, qx, 0.0).astype(BF16)))
    inner = [(_dot_nt(masked[h][0], qk[h // 2][2]) * decay_ref[h]).astype(BF16) for h in range(C_HEADS)]
    outs = []
    for h in range(C_HEADS):
        vh = v_ref[:, h * LANES:(h + 1) * LANES]
        st = state_ref[h]
        outs.append(jnp.dot(inner[h], vh, preferred_element_type=F32)
                    + jnp.dot(masked[h][1], st.astype(BF16), preferred_element_type=F32))
        state_ref[h] = st * gch_ref[h] + _dot_tn(qk[h // 2][3], vh)
    for h, o in enumerate(outs):
        mu = jnp.mean(o, axis=-1, keepdims=True)
        oc = o - mu
        o = oc * lax.rsqrt(jnp.mean(oc * oc, axis=-1, keepdims=True) + LN_EPS)
        gate = g_ref[:, h * LANES:(h + 1) * LANES].astype(F32)
        o_ref[:, h * LANES:(h + 1) * LANES] = (_silu(gate) * o).astype(BF16)


BC_BLOCKS = 8


def _swa_ret_kernel(q_ref, kvp_ref, kvc_ref, bias_ref, sink_ref, cq_ref, ck_ref, cv_ref, cg_ref, sin_ref, cos_ref,
                    decay_ref, zeta_ref, xi_ref, gch_ref, yb_ref, yc_ref, state_ref):
    c = C_CHUNK
    blocks = q_ref.shape[1] // c
    first_block = pl.program_id(1) * blocks
    for r in range(blocks):
        rows = slice(r * c, (r + 1) * c)
        prev = kvp_ref.at[0] if r == 0 else kvc_ref.at[0, (r - 1) * c:r * c]
        _swa_block(q_ref.at[0, rows], prev, kvc_ref.at[0, rows], bias_ref, sink_ref, yb_ref.at[0, rows],
                   jnp.minimum(first_block + r, 1))
        _ret_block(cq_ref.at[0, rows], ck_ref.at[0, rows], cv_ref.at[0, rows], cg_ref.at[0, rows],
                   sin_ref.at[rows], cos_ref.at[rows], decay_ref, zeta_ref, xi_ref, gch_ref, yc_ref.at[0, rows],
                   state_ref, first_block + r == 0)


def _swa_retention(proj, bias_b, sink_rows, sin_t, cos_t, decay, zeta_t, xi_t, gch):
    assert B_WINDOW == C_CHUNK
    bsz, s, _ = proj.shape
    c = C_CHUNK
    blocks = min(BC_BLOCKS, s // c)
    t = blocks * c
    qw = C_HEADS * C_QK_DIM
    kvw = 2 * B_KV_HEADS * B_HEAD_DIM
    qb, kvb = OFF_BQ // B_WIDTH, OFF_BKV // kvw
    cols = (B_HEADS // B_KV_HEADS) * c
    return pl.pallas_call(
        _swa_ret_kernel,
        grid=(bsz, s // t),
        in_specs=[pl.BlockSpec((1, t, B_WIDTH), lambda b, n: (b, n, qb)),
                  pl.BlockSpec((1, c, kvw), lambda b, n: (b, jnp.maximum(n * blocks - 1, 0), kvb)),
                  pl.BlockSpec((1, t, kvw), lambda b, n: (b, n, kvb)),
                  pl.BlockSpec((B_KV_HEADS, 2, 2 * c, cols), lambda b, n: (0, 0, 0, 0)),
                  pl.BlockSpec((B_KV_HEADS, 1, cols), lambda b, n: (0, 0, 0)),
                  pl.BlockSpec((1, t, qw), lambda b, n: (b, n, OFF_CQ // qw)),
                  pl.BlockSpec((1, t, qw), lambda b, n: (b, n, OFF_CK // qw)),
                  pl.BlockSpec((1, t, C_WIDTH), lambda b, n: (b, n, OFF_CV // C_WIDTH)),
                  pl.BlockSpec((1, t, C_WIDTH), lambda b, n: (b, n, OFF_CG // C_WIDTH)),
                  pl.BlockSpec((t, LANES), lambda b, n: (n, 0)),
                  pl.BlockSpec((t, LANES), lambda b, n: (n, 0)),
                  pl.BlockSpec((C_HEADS, c, c), lambda b, n: (0, 0, 0)),
                  pl.BlockSpec((C_HEADS // 2, c, LANES), lambda b, n: (0, 0, 0)),
                  pl.BlockSpec((C_HEADS // 2, c, LANES), lambda b, n: (0, 0, 0)),
                  pl.BlockSpec((C_HEADS, 1, LANES), lambda b, n: (0, 0, 0))],
        out_specs=[pl.BlockSpec((1, t, B_WIDTH), lambda b, n: (b, n, 0)),
                   pl.BlockSpec((1, t, C_WIDTH), lambda b, n: (b, n, 0))],
        out_shape=[jax.ShapeDtypeStruct((bsz, s, B_WIDTH), BF16), jax.ShapeDtypeStruct((bsz, s, C_WIDTH), BF16)],
        scratch_shapes=[pltpu.VMEM((C_HEADS, LANES, C_V_DIM), F32)],
        compiler_params=_cparams(("parallel", "arbitrary")),
        name="swa_retention",
    )(proj, proj, proj, bias_b, sink_rows, proj, proj, proj, proj, sin_t, cos_t, decay, zeta_t, xi_t, gch)


def _cast_riders(riders_in, riders_out):
    for src, dst in zip(riders_in, riders_out):
        dst[...] = src[...].astype(BF16)


def _row_split_riders(riders, steps):
    views = [r.reshape(r.size // r.shape[-1], r.shape[-1]) for r in riders]
    for v in views:
        assert v.shape[0] % (steps * 16) == 0, v.shape
    return views, [(v.shape[0] // steps, v.shape[1]) for v in views]


def _out_proj_kernel(ya_ref, yb_ref, yc_ref, wa_ref, wb_ref, wc_ref, x_ref, g_ref, b_ref, *rest):
    n_riders = (len(rest) - 1) // 2
    o_ref = rest[n_riders]
    half = o_ref.shape[0] // 2
    spans = [slice(0, half), slice(half, 2 * half)]
    mixes = [jnp.dot(ya_ref[r, :], wa_ref[...], preferred_element_type=F32)
             + jnp.dot(yb_ref[r, :], wb_ref[...], preferred_element_type=F32)
             + jnp.dot(yc_ref[r, :], wc_ref[...], preferred_element_type=F32) for r in spans]
    for r, mix in zip(spans, mixes):
        o_ref[r, :] = _layer_norm(ALPHA * x_ref[r, :] + mix, g_ref[...], b_ref[...])
    _cast_riders(rest[:n_riders], rest[n_riders + 1:])


def _out_proj_ln(ya, yb, yc, w, x, g, b, riders=()):
    n, d = x.shape
    tm = min(512, n)
    row = lambda i: (i, 0)
    fixed = lambda i: (0, 0)
    once = pl.Buffered(1)
    views, blocks = _row_split_riders(riders, n // tm)
    rider_specs = [pl.BlockSpec(blk, row) for blk in blocks]
    outs = pl.pallas_call(
        _out_proj_kernel,
        grid=(n // tm,),
        in_specs=[pl.BlockSpec((tm, A_WIDTH), row),
                  pl.BlockSpec((tm, B_WIDTH), row),
                  pl.BlockSpec((tm, C_WIDTH), row),
                  pl.BlockSpec((A_WIDTH, d), lambda i: (0, 0), pipeline_mode=once),
                  pl.BlockSpec((B_WIDTH, d), lambda i: (1, 0), pipeline_mode=once),
                  pl.BlockSpec((C_WIDTH, d), lambda i: ((A_WIDTH + B_WIDTH) // C_WIDTH, 0), pipeline_mode=once),
                  pl.BlockSpec((tm, d), row),
                  pl.BlockSpec((1, d), fixed),
                  pl.BlockSpec((1, d), fixed)] + rider_specs,
        out_specs=[pl.BlockSpec((tm, d), row)] + rider_specs,
        out_shape=[jax.ShapeDtypeStruct((n, d), F32)] + [jax.ShapeDtypeStruct(v.shape, BF16) for v in views],
        compiler_params=_cparams(("arbitrary",)),
        name="out_proj_ln",
    )(ya, yb, yc, w, w, w, x, g, b, *views)
    return outs[0], [o.reshape(r.shape) for o, r in zip(outs[1:], riders)]


FFN_TM = 512
FFN_TF = 512


def _swiglu_step(xb, wg, wu, wd):
    hg = jnp.dot(xb, wg, preferred_element_type=F32)
    hu = jnp.dot(xb, wu, preferred_element_type=F32)
    return jnp.dot((_silu(hg) * hu).astype(BF16), wd, preferred_element_type=F32)


def _ffn_kernel(x_ref, wg_ref, wu_ref, wd_ref, g_ref, b_ref, *rest):
    n_riders = (len(rest) - 2) // 2
    riders_in, o_ref = rest[:n_riders], rest[n_riders]
    riders_out, xb_ref = rest[n_riders + 1:-1], rest[-1]
    f = pl.program_id(1)

    @pl.when(f == 0)
    def _():
        xb = x_ref[...].astype(BF16)
        xb_ref[...] = xb
        o_ref[...] = _swiglu_step(xb, wg_ref[...], wu_ref[...], wd_ref[...])
        _cast_riders(riders_in, riders_out)

    @pl.when(f > 0)
    def _():
        o_ref[...] += _swiglu_step(xb_ref[...], wg_ref[...], wu_ref[...], wd_ref[...])
        _cast_riders(riders_in, riders_out)

    @pl.when(f == pl.num_programs(1) - 1)
    def _():
        o_ref[...] = _layer_norm(ALPHA * x_ref[...] + o_ref[...], g_ref[...], b_ref[...])


def _ffn(x, w_gate, w_up, w_down, g, b, riders=()):
    n, d = x.shape
    ff = w_gate.shape[1]
    tm = min(FFN_TM, n)
    tf = FFN_TF
    ni, nf = n // tm, ff // tf
    flat, rider_specs = [], []
    for r in riders:
        cols = r.shape[-1]
        rows = r.size // cols
        flat.append(r.reshape(rows, cols))
        if rows % (ni * nf * 16) == 0:
            rider_specs.append(pl.BlockSpec((rows // (ni * nf), cols), lambda i, f: (i * nf + f, 0)))
        else:
            assert rows % (ni * 16) == 0 and cols % (nf * LANES) == 0
            rider_specs.append(pl.BlockSpec((rows // ni, cols // nf), lambda i, f: (i, f)))
    outs = pl.pallas_call(
        _ffn_kernel,
        grid=(n // tm, ff // tf),
        in_specs=[pl.BlockSpec((tm, d), lambda i, f: (i, 0)),
                  pl.BlockSpec((d, tf), lambda i, f: (0, f)),
                  pl.BlockSpec((d, tf), lambda i, f: (0, f)),
                  pl.BlockSpec((tf, d), lambda i, f: (f, 0)),
                  pl.BlockSpec((1, d), lambda i, f: (0, 0)),
                  pl.BlockSpec((1, d), lambda i, f: (0, 0))] + rider_specs,
        out_specs=[pl.BlockSpec((tm, d), lambda i, f: (i, 0))] + rider_specs,
        out_shape=[jax.ShapeDtypeStruct((n, d), F32)] + [jax.ShapeDtypeStruct(r.shape, BF16) for r in flat],
        scratch_shapes=[pltpu.VMEM((tm, d), BF16)],
        compiler_params=_cparams(("arbitrary", "arbitrary")),
        name="ffn_ln",
    )(x, w_gate, w_up, w_down, g, b, *flat)
    return outs[0], [o.reshape(r.shape) for o, r in zip(outs[1:], riders)]


def _row_copy(src, dst, sem, s, t):
    return pltpu.make_async_copy(src.at[pl.ds(s, 1)], dst.at[pl.ds(t, 1)], sem)


def _ffn_grouped_kernel(eid_ref, nact_ref, src_ref, x_hbm, wg_ref, wu_ref, wd_ref, o_ref, rows_ref, xb_ref, sem,
                        *, tm, per_step):
    i = pl.program_id(0)
    f = pl.program_id(1)
    nact = nact_ref[0]
    active = i < nact
    slot = lax.rem(i, 2)
    buf_rows = rows_ref.shape[1]
    last_slot_row = src_ref.shape[0] - 1

    def fetch(tile, into, first_row, count):
        for r in range(count):
            row = first_row + r
            token = src_ref[jnp.minimum(tile * tm + row, last_slot_row)]
            _row_copy(x_hbm, rows_ref.at[into], sem.at[into], token, row).start()

    def wait_buffer(which):
        pltpu.make_async_copy(x_hbm.at[pl.ds(0, buf_rows)], rows_ref.at[which], sem.at[which]).wait()

    @pl.when(jnp.logical_and(i == 0, f == 0))
    def _():
        lax.fori_loop(0, buf_rows // per_step, lambda s, c: (fetch(0, 0, s * per_step, per_step), c)[1], 0)

    @pl.when(jnp.logical_and(f == 0, i <= nact))
    def _():
        wait_buffer(slot)

    @pl.when(jnp.logical_and(f == 0, jnp.logical_not(active)))
    def _():
        o_ref[...] = jnp.zeros_like(o_ref)

    @pl.when(jnp.logical_and(f == 0, active))
    def _():
        fetch(i + 1, 1 - slot, 0, per_step)
        xb = rows_ref[slot, :tm, :].astype(BF16)
        xb_ref[...] = xb
        o_ref[...] = _swiglu_step(xb, wg_ref[0], wu_ref[0], wd_ref[0])

    @pl.when(jnp.logical_and(f > 0, active))
    def _():
        fetch(i + 1, 1 - slot, f * per_step, per_step)
        o_ref[...] += _swiglu_step(xb_ref[...], wg_ref[0], wu_ref[0], wd_ref[0])


def _ffn_grouped(x, src, eid, nact, w_gate, w_up, w_down):
    n, d = x.shape
    slots = src.shape[0]
    ff = w_gate.shape[2]
    tm = min(FFN_TM, n)
    tf = FFN_TF
    nf = ff // tf
    per_step = -(-tm // nf)
    per_step += (-per_step) % 8
    assert per_step * nf <= n

    def fidx(i, f, nact_ref):
        return jnp.where(i < nact_ref[0], f, nf - 1)

    grid_spec = pltpu.PrefetchScalarGridSpec(
        num_scalar_prefetch=3,
        grid=(slots // tm, nf),
        in_specs=[pl.BlockSpec(memory_space=pl.ANY),
                  pl.BlockSpec((1, d, tf), lambda i, f, e, a, s: (e[i], 0, fidx(i, f, a))),
                  pl.BlockSpec((1, d, tf), lambda i, f, e, a, s: (e[i], 0, fidx(i, f, a))),
                  pl.BlockSpec((1, tf, d), lambda i, f, e, a, s: (e[i], fidx(i, f, a), 0))],
        out_specs=pl.BlockSpec((tm, d), lambda i, f, e, a, s: (i, 0)),
        scratch_shapes=[pltpu.VMEM((2, per_step * nf, d), F32), pltpu.VMEM((tm, d), BF16),
                        pltpu.SemaphoreType.DMA((2,))],
    )
    return pl.pallas_call(
        functools.partial(_ffn_grouped_kernel, tm=tm, per_step=per_step),
        grid_spec=grid_spec,
        out_shape=jax.ShapeDtypeStruct((slots, d), F32),
        compiler_params=_cparams(("arbitrary", "arbitrary")),
        name="ffn_grouped",
    )(eid, nact, src, x, w_gate, w_up, w_down)


ROUTER_TM = 512


def _router_kernel(x_ref, wr_ref, tri_ref, meta_ref, cnt_ref, carry_ref):
    @pl.when(pl.program_id(0) == 0)
    def _():
        carry_ref[...] = jnp.zeros_like(carry_ref)

    x = x_ref[...]
    x_hi = x.astype(BF16)
    x_lo = (x - x_hi.astype(F32)).astype(BF16)
    parts = (jnp.dot(x_hi, wr_ref[...], preferred_element_type=F32)
             + jnp.dot(x_lo, wr_ref[...], preferred_element_type=F32))
    logits = parts + pltpu.roll(parts, LANES - N_EXPERTS, 1)
    lane = lax.broadcasted_iota(jnp.int32, logits.shape, 1)
    logits = jnp.where(lane < N_EXPERTS, logits, -jnp.inf)
    m1 = jnp.max(logits, axis=-1, keepdims=True)
    i1 = jnp.min(jnp.where(logits == m1, lane, LANES), axis=-1, keepdims=True)
    rest = jnp.where(lane == i1, -jnp.inf, logits)
    m2 = jnp.max(rest, axis=-1, keepdims=True)
    i2 = jnp.min(jnp.where(rest == m2, lane, LANES), axis=-1, keepdims=True)
    e2 = jnp.exp(m2 - m1)
    w1 = 1.0 / (1.0 + e2)
    w2 = e2 * w1
    hit1 = lane == i1
    hit2 = lane == i2
    onehot = jnp.where(jnp.logical_or(hit1, hit2), 1.0, 0.0)
    before = jnp.dot(tri_ref[...], onehot.astype(BF16), preferred_element_type=F32) + carry_ref[...]
    r1 = jnp.sum(jnp.where(hit1, before, 0.0), axis=-1, keepdims=True)
    r2 = jnp.sum(jnp.where(hit2, before, 0.0), axis=-1, keepdims=True)
    carry_ref[...] = carry_ref[...] + jnp.sum(onehot, axis=0, keepdims=True)
    cnt_ref[...] = carry_ref[...]
    meta = jnp.where(lane == 0, i1.astype(F32), 0.0)
    meta = jnp.where(lane == 1, i2.astype(F32), meta)
    meta = jnp.where(lane == 2, r1, meta)
    meta = jnp.where(lane == 3, r2, meta)
    meta = jnp.where(lane == 4, w1, meta)
    meta = jnp.where(lane == 5, w2, meta)
    meta_ref[...] = meta


def _router(x, w_router):
    n, d = x.shape
    tm = min(ROUTER_TM, n)
    w_hi = w_router.astype(BF16)
    w_lo = (w_router.astype(F32) - w_hi.astype(F32)).astype(BF16)
    wr = jnp.zeros((d, LANES), BF16).at[:, :N_EXPERTS].set(w_hi).at[:, N_EXPERTS:2 * N_EXPERTS].set(w_lo)
    tri = jnp.asarray(np.tril(np.ones((tm, tm), np.float32), -1), BF16)
    return pl.pallas_call(
        _router_kernel,
        grid=(n // tm,),
        in_specs=[pl.BlockSpec((tm, d), lambda i: (i, 0)),
                  pl.BlockSpec((d, LANES), lambda i: (0, 0)),
                  pl.BlockSpec((tm, tm), lambda i: (0, 0))],
        out_specs=[pl.BlockSpec((tm, LANES), lambda i: (i, 0)),
                   pl.BlockSpec((1, LANES), lambda i: (0, 0))],
        out_shape=[jax.ShapeDtypeStruct((n, LANES), F32), jax.ShapeDtypeStruct((1, LANES), F32)],
        scratch_shapes=[pltpu.VMEM((1, LANES), F32)],
        compiler_params=_cparams(("arbitrary",)),
        name="router",
    )(x, wr, tri)


MOVE_TM = 256
ISSUE_UNROLL = 8


def _combine_kernel(d1_ref, d2_ref, ys_hbm, x_ref, meta_ref, g_ref, b_ref, o_ref, buf_ref, sem, *, tm):
    i = pl.program_id(0)
    last = pl.num_programs(0) - 1
    slot = lax.rem(i, 2)

    def start_row(tile, into, t):
        row = tile * tm + t
        _row_copy(ys_hbm, buf_ref.at[into, 0], sem.at[into], d1_ref[row], t).start()
        _row_copy(ys_hbm, buf_ref.at[into, 1], sem.at[into], d2_ref[row], t).start()

    def wait_slot(which):
        for k in range(2):
            pltpu.make_async_copy(ys_hbm.at[pl.ds(0, tm)], buf_ref.at[which, k], sem.at[which]).wait()

    @pl.when(i == 0)
    def _():
        lax.fori_loop(0, tm, lambda t, c: (start_row(0, 0, t), c)[1], 0, unroll=ISSUE_UNROLL)

    wait_slot(slot)
    nxt = jnp.minimum(i + 1, last)
    for t in range(tm):
        start_row(nxt, 1 - slot, t)
    meta = meta_ref[...]
    lane = lax.broadcasted_iota(jnp.int32, meta.shape, 1)
    w1 = jnp.sum(jnp.where(lane == 4, meta, 0.0), axis=-1, keepdims=True)
    w2 = jnp.sum(jnp.where(lane == 5, meta, 0.0), axis=-1, keepdims=True)
    f = w1 * buf_ref[slot, 0] + w2 * buf_ref[slot, 1]
    o_ref[...] = _layer_norm(ALPHA * x_ref[...] + f, g_ref[...], b_ref[...])

    @pl.when(i == last)
    def _():
        wait_slot(1 - slot)


def _combine(ys, x, meta, d1, d2, g, b):
    n, d = x.shape
    tm = min(MOVE_TM, n)
    grid_spec = pltpu.PrefetchScalarGridSpec(
        num_scalar_prefetch=2,
        grid=(n // tm,),
        in_specs=[pl.BlockSpec(memory_space=pl.ANY),
                  pl.BlockSpec((tm, d), lambda i, a, c: (i, 0)),
                  pl.BlockSpec((tm, LANES), lambda i, a, c: (i, 0)),
                  pl.BlockSpec((1, d), lambda i, a, c: (0, 0)),
                  pl.BlockSpec((1, d), lambda i, a, c: (0, 0))],
        out_specs=pl.BlockSpec((tm, d), lambda i, a, c: (i, 0)),
        scratch_shapes=[pltpu.VMEM((2, 2, tm, d), F32), pltpu.SemaphoreType.DMA((2,))],
    )
    return pl.pallas_call(
        functools.partial(_combine_kernel, tm=tm),
        grid_spec=grid_spec,
        out_shape=jax.ShapeDtypeStruct((n, d), F32),
        compiler_params=_cparams(("arbitrary",)),
        name="moe_combine",
    )(d1, d2, ys, x, meta, g, b)


def _moe(x, w_router, w_gate, w_up, w_down, g, b):
    n, d = x.shape
    tm = min(FFN_TM, n)
    meta, cnt = _router(x, w_router)
    i1 = meta[:, 0].astype(jnp.int32)
    i2 = meta[:, 1].astype(jnp.int32)
    counts = cnt[0, :N_EXPERTS].astype(jnp.int32)
    tiles = (counts + tm - 1) // tm
    tile_end = jnp.cumsum(tiles)
    group_start = (tile_end - tiles) * tm
    experts = jnp.arange(N_EXPERTS, dtype=jnp.int32)[None, :]
    start_of = lambda idx: jnp.sum(jnp.where(idx[:, None] == experts, group_start[None, :], 0), axis=1)
    d1 = start_of(i1) + meta[:, 2].astype(jnp.int32)
    d2 = start_of(i2) + meta[:, 3].astype(jnp.int32)
    max_tiles = (2 * n) // tm + N_EXPERTS
    tile_ids = jnp.arange(max_tiles, dtype=jnp.int32)[:, None]
    eid = jnp.minimum(jnp.sum((tile_ids >= tile_end[None, :]).astype(jnp.int32), axis=1), N_EXPERTS - 1)
    nact = tile_end[-1:].astype(jnp.int32)
    token = jnp.arange(n, dtype=jnp.int32)
    src = jnp.zeros((max_tiles * tm,), jnp.int32).at[jnp.concatenate([d1, d2])].set(
        jnp.concatenate([token, token]), unique_indices=True)
    ys = _ffn_grouped(x, src, eid.astype(jnp.int32), nact, w_gate, w_up, w_down)
    return _combine(ys, x, meta, d1, d2, g, b)


def _lambda_init(layer_idx):
    return 0.8 - 0.6 * math.exp(-0.3 * layer_idx)


def _static_tables(s):
    c = C_CHUNK
    ang = jnp.repeat(1.0 / (10000.0 ** jnp.linspace(0.0, 1.0, C_QK_DIM // 2, dtype=F32)), 2)
    ang = jnp.arange(s, dtype=F32)[:, None] * ang[None, :]
    sign = jnp.where(jnp.arange(C_QK_DIM) % 2 == 0, -1.0, 1.0).astype(F32)
    sin_t = jnp.tile(jnp.sin(ang) * sign[None, :], (1, 2))
    cos_t = jnp.tile(jnp.cos(ang), (1, 2))
    log_g = jnp.log(1.0 - jnp.exp2(-5.0 - jnp.arange(C_HEADS, dtype=F32)))
    pos = jnp.arange(c)
    rel = (pos[:, None] - pos[None, :]).astype(F32)
    decay = jnp.where((rel >= 0)[None], jnp.exp(jnp.maximum(rel, 0.0)[None] * log_g[:, None, None]), 0.0)
    zeta = jnp.exp((c - 1 - pos).astype(F32)[:, None] * log_g[None, :])
    xi = jnp.exp((pos + 1).astype(F32)[:, None] * log_g[None, :])
    per_pair = lambda t: jnp.repeat(t.T.reshape(C_HEADS // 2, 2, c), C_QK_DIM, axis=1).transpose(0, 2, 1)
    gch = jnp.broadcast_to(jnp.exp(c * log_g)[:, None, None], (C_HEADS, 1, LANES))
    dist_a = np.arange(A_SUB)[None, :] - np.arange(A_BLOCK)[:, None]
    types = [jnp.where(dist_a + off >= 0, _rel_bucket(jnp.asarray(dist_a + off)), REL_BUCKETS)
             for off in (0, A_SUB, A_BLOCK)]
    types.append(jnp.full(dist_a.shape, REL_BUCKETS - 1, jnp.int32))
    bkt_a = jnp.concatenate([jnp.tile(b, (1, 2)) for b in types], axis=0).astype(jnp.int32)
    w = B_WINDOW
    dist = np.arange(w)[:, None] + w - np.arange(2 * w)[None, :]
    band = (dist >= 0) & (dist < w)
    has_prev = np.stack([np.broadcast_to(np.arange(2 * w)[None, :] >= w, band.shape), np.ones_like(band)])
    bkt_b = jnp.where(band[None] & has_prev, _rel_bucket(jnp.asarray(dist))[None], REL_BUCKETS)
    bkt_b = bkt_b.transpose(0, 2, 1).reshape(2 * 2 * w, w).astype(jnp.int32)
    return sin_t, cos_t, decay, per_pair(zeta), per_pair(xi), gch, bkt_a, bkt_b


def kernel(x, w_in, rel_bias, a_lambda, a_subln_g, b_sinks, w_out, ln_mix_g, ln_mix_b, ln_ffn_g, ln_ffn_b,
           dense_w_gate, dense_w_up, dense_w_down, moe_router, moe_w_gate, moe_w_up, moe_w_down):
    bsz, s, d = x.shape
    n = bsz * s
    sin_t, cos_t, decay, zeta_t, xi_t, gch, bkt_a, bkt_b = _static_tables(s)
    tab_t = rel_bias.astype(F32).T
    bias_a = _bias_lookup(tab_t[:A_HEADS] * LOG2E, bkt_a).reshape(A_HEADS, A_BIAS_TYPES, A_BLOCK, 2 * A_SUB)
    group = B_HEADS // B_KV_HEADS
    bias_b = _bias_lookup(tab_t[A_HEADS:] * LOG2E, bkt_b)
    bias_b = bias_b.reshape(B_KV_HEADS, group, 2, 2 * B_WINDOW, B_WINDOW).transpose(0, 2, 3, 1, 4)
    bias_b = bias_b.reshape(B_KV_HEADS, 2, 2 * B_WINDOW, group * B_WINDOW)
    xf = x.reshape(n, d).astype(F32)
    w_in_bf16, w_out_bf16 = w_in[0].astype(BF16), w_out[0].astype(BF16)
    for l in range(DEPTH):
        w_in_l = jnp.concatenate([w_in_bf16[:, a:b] for a, b in _PERM_RUNS]
                                 + [jnp.zeros((d, PROJ_PAD - PROJ_WIDTH), BF16)], axis=1)
        proj = _in_proj(xf, w_in_l).reshape(bsz, s, PROJ_PAD)
        ya = _attn_a(proj, bias_a, a_lambda[l].astype(F32), a_subln_g[l].astype(F32).reshape(1, A_V_DIM),
                     _lambda_init(l))
        sink_rows = jnp.repeat(b_sinks[l].astype(F32).reshape(B_KV_HEADS, 1, group), B_WINDOW, axis=2)
        yb, yc = _swa_retention(proj, bias_b, sink_rows, sin_t, cos_t, decay, zeta_t, xi_t, gch)
        g_mix = ln_mix_g[l].astype(F32).reshape(1, d)
        b_mix = ln_mix_b[l].astype(F32).reshape(1, d)
        j = l // 2
        dense = (dense_w_gate[j], dense_w_up[j], dense_w_down[j]) if l % 2 == 0 else ()
        ahead = (w_in[l + 1], w_out[l + 1]) if l + 1 < DEPTH else ()
        xf, cast = _out_proj_ln(ya.reshape(n, A_WIDTH), yb.reshape(n, B_WIDTH), yc.reshape(n, C_WIDTH),
                                w_out_bf16, xf, g_mix, b_mix, dense + ahead)
        dense_bf16 = cast[:len(dense)]
        if ahead:
            w_in_bf16, w_out_bf16 = cast[len(dense):]
        g_ffn = ln_ffn_g[l].astype(F32).reshape(1, d)
        b_ffn = ln_ffn_b[l].astype(F32).reshape(1, d)
        if l % 2 == 0:
            experts = (moe_w_gate[j], moe_w_up[j], moe_w_down[j]) if l + 1 < DEPTH else ()
            xf, moe_bf16 = _ffn(xf, *dense_bf16, g_ffn, b_ffn, experts)
        else:
            assert l > 0, "expert weights are cast by the preceding dense layer"
            xf = _moe(xf, moe_router[j], *moe_bf16, g_ffn, b_ffn)
    return xf.reshape(bsz, s, d).astype(x.dtype)
```

```python
import functools
import math

import jax
import jax.numpy as jnp
import numpy as np
from jax import lax
from jax.experimental import pallas as pl
from jax.experimental.pallas import tpu as pltpu

F32 = jnp.float32
BF16 = jnp.bfloat16

D_MODEL = 2048
DEPTH = 2
A_HEADS = 6
A_QK_DIM = 64
A_V_DIM = 128
B_HEADS = 12
B_KV_HEADS = 3
B_HEAD_DIM = 64
B_WINDOW = 128
C_HEADS = 4
C_QK_DIM = 64
C_V_DIM = 128
C_CHUNK = 128
A_WIDTH = A_HEADS * A_V_DIM
B_WIDTH = B_HEADS * B_HEAD_DIM
C_WIDTH = C_HEADS * C_V_DIM
REL_BUCKETS = 32
REL_MAX_DIST = 128
D_FF = 5632
N_EXPERTS = 8
ALPHA = (2.0 * DEPTH) ** 0.25
LN_EPS = 1e-5
NEG = -1e30

LANES = 128
VMEM_LIMIT = 56 * 1024 * 1024

_REF_SIZES = [768, 768, 768, 768, 192, 192, 256, 256, 512, 512]
_REF_OFF = [int(v) for v in np.concatenate([[0], np.cumsum(_REF_SIZES)[:-1]])]
PROJ_WIDTH = int(sum(_REF_SIZES))
OFF_AQ, OFF_AK, OFF_AV, OFF_BQ, OFF_CV, OFF_CG, OFF_CQ, OFF_CK, OFF_BKV = (
    0, 768, 1536, 2304, 3072, 3584, 4096, 4352, 4608)


def _proj_perm():
    aq, ak, av, bq, bk, bv, cq, ck, cv, cg = [np.arange(o, o + s) for o, s in zip(_REF_OFF, _REF_SIZES)]
    perm = np.concatenate([aq, ak, av, bq, cv, cg, cq, ck, bk, bv])
    assert perm.shape[0] == PROJ_WIDTH
    return perm


def _perm_runs():
    perm = _proj_perm()
    cuts = np.flatnonzero(np.diff(perm) != 1) + 1
    return [(int(r[0]), int(r[-1]) + 1) for r in np.split(perm, cuts)]


_PERM_RUNS = _perm_runs()


def _cparams(sem):
    return pltpu.CompilerParams(dimension_semantics=sem, vmem_limit_bytes=VMEM_LIMIT)


def _layer_norm(z, g, b):
    mu = jnp.mean(z, axis=-1, keepdims=True)
    zc = z - mu
    var = jnp.mean(zc * zc, axis=-1, keepdims=True)
    return zc * lax.rsqrt(var + LN_EPS) * g + b


def _silu(x):
    return x / (1.0 + jnp.exp(-x))


def _dot_nt(a, b):
    return lax.dot_general(a, b, (((1,), (1,)), ((), ())), preferred_element_type=F32)


def _dot_tn(a, b):
    return lax.dot_general(a, b, (((0,), (0,)), ((), ())), preferred_element_type=F32)


def _bias_kernel(tab_ref, bkt_ref, o_ref):
    h = pl.program_id(0)
    bkt = bkt_ref[...]
    acc = jnp.full(bkt.shape, NEG, F32)
    for b in range(REL_BUCKETS):
        acc = jnp.where(bkt == b, tab_ref[h, b], acc)
    o_ref[0] = acc


def _bias_lookup(tab_t, bkt):
    nh = tab_t.shape[0]
    r, c = bkt.shape
    return pl.pallas_call(
        _bias_kernel,
        grid=(nh,),
        in_specs=[pl.BlockSpec(memory_space=pltpu.SMEM),
                  pl.BlockSpec((r, c), lambda h: (0, 0))],
        out_specs=pl.BlockSpec((1, r, c), lambda h: (h, 0, 0)),
        out_shape=jax.ShapeDtypeStruct((nh, r, c), F32),
        compiler_params=_cparams(("arbitrary",)),
        name="bias_lookup",
    )(tab_t, bkt)


def _rel_bucket(dist):
    max_exact = REL_BUCKETS // 2
    d = jnp.maximum(dist, 0)
    ratio = jnp.maximum(d, 1).astype(F32) / max_exact
    large = max_exact + (jnp.log(ratio) / math.log(REL_MAX_DIST / max_exact)
                         * (REL_BUCKETS - max_exact)).astype(jnp.int32)
    large = jnp.minimum(large, REL_BUCKETS - 1)
    return jnp.where(d < max_exact, d, large)


MXU_COLS = 256
IN_PROJ_COL_STEPS = 4
PROJ_PAD = -(-PROJ_WIDTH // (IN_PROJ_COL_STEPS * MXU_COLS)) * (IN_PROJ_COL_STEPS * MXU_COLS)


def _in_proj_kernel(x_ref, w_ref, o_ref, xb_ref):
    @pl.when(pl.program_id(1) == 0)
    def _():
        xb = x_ref[...].astype(BF16)
        xb_ref[...] = xb
        o_ref[...] = jnp.dot(xb, w_ref[...], preferred_element_type=F32).astype(BF16)

    @pl.when(pl.program_id(1) > 0)
    def _():
        o_ref[...] = jnp.dot(xb_ref[...], w_ref[...], preferred_element_type=F32).astype(BF16)


def _in_proj(x, w):
    n, d = x.shape
    p = w.shape[1]
    tm = min(1024, n)
    tn = p // IN_PROJ_COL_STEPS
    return pl.pallas_call(
        _in_proj_kernel,
        grid=(n // tm, p // tn),
        in_specs=[pl.BlockSpec((tm, d), lambda i, j: (i, 0)),
                  pl.BlockSpec((d, tn), lambda i, j: (0, j))],
        out_specs=pl.BlockSpec((tm, tn), lambda i, j: (i, j)),
        out_shape=jax.ShapeDtypeStruct((n, p), BF16),
        scratch_shapes=[pltpu.VMEM((tm, d), BF16)],
        compiler_params=_cparams(("parallel", "arbitrary")),
        name="in_proj",
    )(x, w)


A_BLOCK = 256
A_SUB = 128
A_BIAS_TYPES = 4
LOG2E = math.log2(math.e)


def _attn_a_kernel(q_ref, k_ref, v_ref, bias_ref, lam_ref, g_ref, o_ref, qt_ref, s_ref, p_ref, acc_ref, *,
                   lam_init):
    t = A_BLOCK
    per_key_block = t // A_SUB
    n_chains = q_ref.shape[1] // A_SUB
    n_blocks = k_ref.shape[1] // t
    diag = [g // per_key_block for g in range(n_chains)]
    lp = lam_ref[...]
    lam = (jnp.exp(jnp.sum(lp[0:1] * lp[1:2], axis=-1, keepdims=True))
           - jnp.exp(jnp.sum(lp[2:3] * lp[3:4], axis=-1, keepdims=True)) + lam_init)
    lane = lax.broadcasted_iota(jnp.int32, (A_SUB, LANES), 1)
    for g in range(n_chains):
        q = q_ref[0, g * A_SUB:(g + 1) * A_SUB, :].astype(F32) * (A_QK_DIM ** -0.5 * LOG2E)
        qq = jnp.concatenate([jnp.where(lane < A_QK_DIM, q, 0.0), jnp.where(lane >= A_QK_DIM, q, 0.0)], axis=0)
        qt_ref[g] = qq.T.astype(BF16)

    def needing(j):
        return [g for g in range(n_chains) if diag[g] >= j]

    def scores_into(j):
        kj = k_ref[0, j * t:(j + 1) * t, :]
        for g in needing(j):
            back = diag[g] - j
            near = (0, 2) if g % per_key_block == 0 else (1, 3)
            kind = near[0] if back == 0 else (near[1] if back == 1 else 3)
            s_ref[j % 2, g] = jnp.dot(kj, qt_ref[g], preferred_element_type=F32) + bias_ref[0, kind]

    def finish(g, l_fin):
        o_all = acc_ref[g] / l_fin
        o = (o_all[:, :A_SUB] - lam * o_all[:, A_SUB:]).T
        o = o * lax.rsqrt(jnp.mean(o * o, axis=-1, keepdims=True) + LN_EPS) * g_ref[...]
        o_ref[0, g * A_SUB:(g + 1) * A_SUB, :] = (o * (1.0 - lam_init)).astype(BF16)

    stats = [(jnp.full((1, 2 * A_SUB), NEG, F32), jnp.zeros((1, 2 * A_SUB), F32), None) for _ in range(n_chains)]
    scores_into(0)
    for j in range(n_blocks + 1):
        if j + 1 < n_blocks:
            scores_into(j + 1)
        before = list(stats)
        if j < n_blocks:
            for g in needing(j):
                m_old, l_old, _ = before[g]
                s = s_ref[j % 2, g]
                m_new = jnp.maximum(m_old, jnp.max(s, axis=0, keepdims=True))
                p = jnp.exp2(s - m_new)
                alpha = jnp.exp2(m_old - m_new)
                stats[g] = (m_new, alpha * l_old + jnp.sum(p, axis=0, keepdims=True), alpha)
                p_ref[j % 2, g] = p.astype(BF16)
        if j >= 1:
            vj = v_ref[0, (j - 1) * t:j * t, :]
            for g in needing(j - 1):
                pv = _dot_tn(vj, p_ref[(j - 1) % 2, g])
                acc_ref[g] = pv if j == 1 else before[g][2] * acc_ref[g] + pv
                if diag[g] == j - 1:
                    finish(g, before[g][1])


def _attn_a(proj, bias_a, lam_params, subln_g, lam_init):
    bsz, s, _ = proj.shape
    assert s % A_BLOCK == 0
    nch = s // A_SUB
    kb, vb = OFF_AK // LANES, OFF_AV // LANES
    return pl.pallas_call(
        functools.partial(_attn_a_kernel, lam_init=lam_init),
        grid=(bsz, A_HEADS),
        in_specs=[pl.BlockSpec((1, s, LANES), lambda b, h: (b, 0, h)),
                  pl.BlockSpec((1, s, LANES), lambda b, h: (b, 0, kb + h)),
                  pl.BlockSpec((1, s, LANES), lambda b, h: (b, 0, vb + h)),
                  pl.BlockSpec((1, A_BIAS_TYPES, A_BLOCK, 2 * A_SUB), lambda b, h: (h, 0, 0, 0)),
                  pl.BlockSpec((4, A_QK_DIM), lambda b, h: (0, 0)),
                  pl.BlockSpec((1, A_V_DIM), lambda b, h: (0, 0))],
        out_specs=pl.BlockSpec((1, s, LANES), lambda b, h: (b, 0, h)),
        out_shape=jax.ShapeDtypeStruct((bsz, s, A_WIDTH), BF16),
        scratch_shapes=[pltpu.VMEM((nch, LANES, 2 * A_SUB), BF16),
                        pltpu.VMEM((2, nch, A_BLOCK, 2 * A_SUB), F32),
                        pltpu.VMEM((2, nch, A_BLOCK, 2 * A_SUB), BF16),
                        pltpu.VMEM((nch, A_V_DIM, 2 * A_SUB), F32)],
        compiler_params=_cparams(("parallel", "parallel")),
        name="attn_a",
    )(proj, proj, proj, bias_a, lam_params, subln_g)


def _swa_block(q_ref, kvp_ref, kvc_ref, bias_ref, sink_ref, o_ref, table):
    w = B_WINDOW
    d = B_HEAD_DIM
    group = B_HEADS // B_KV_HEADS
    row = lax.broadcasted_iota(jnp.int32, (LANES, w), 0)
    lane = lax.broadcasted_iota(jnp.int32, (2 * w, LANES), 1)
    zeros = jnp.zeros((LANES - d, w), F32)

    def window(col):
        return jnp.concatenate([kvp_ref[:, col * LANES:(col + 1) * LANES],
                                kvc_ref[:, col * LANES:(col + 1) * LANES]], axis=0)

    kvs, q_ts, swaps = [], [], []
    for g in range(B_KV_HEADS):
        k_off, v_off = g * d, (B_KV_HEADS + g) * d
        swapped = k_off % LANES != 0
        assert (v_off % LANES != 0) != swapped
        k_col, v_col = window(k_off // LANES), window(v_off // LANES)
        kvs.append(jnp.where(lane < d, v_col if swapped else k_col, k_col if swapped else v_col))
        swaps.append(swapped)
        cols = []
        for pair in range(group // 2):
            blk = g * (group // 2) + pair
            t = (q_ref[:, blk * LANES:(blk + 1) * LANES].astype(F32) * (d ** -0.5 * LOG2E)).T
            if swapped:
                cols += [jnp.concatenate([zeros, t[:d]], axis=0), jnp.where(row >= d, t, 0.0)]
            else:
                cols += [jnp.where(row < d, t, 0.0), jnp.concatenate([t[d:], zeros], axis=0)]
        q_ts.append(jnp.concatenate(cols, axis=1).astype(BF16))
    scores = [jnp.dot(kv, q_t, preferred_element_type=F32) + bias_ref[g, table]
              for g, (kv, q_t) in enumerate(zip(kvs, q_ts))]
    soft = []
    for g, s in enumerate(scores):
        sink = sink_ref[g] * LOG2E
        m = jnp.maximum(jnp.max(s, axis=0, keepdims=True), sink)
        e = jnp.exp2(s - m)
        soft.append((e.astype(BF16), 1.0 / (jnp.sum(e, axis=0, keepdims=True) + jnp.exp2(sink - m))))
    outs = [_dot_tn(kv, e) * inv for kv, (e, inv) in zip(kvs, soft)]
    for g, o_t in enumerate(outs):
        v_rows = slice(0, d) if swaps[g] else slice(d, 2 * d)
        for pair in range(group // 2):
            blk = g * (group // 2) + pair
            both = jnp.concatenate([o_t[v_rows, (2 * pair) * w:(2 * pair + 1) * w],
                                    o_t[v_rows, (2 * pair + 1) * w:(2 * pair + 2) * w]], axis=0)
            o_ref[:, blk * LANES:(blk + 1) * LANES] = both.T.astype(BF16)


def _ret_block(q_ref, k_ref, v_ref, g_ref, sin_ref, cos_ref, decay_ref, zeta_ref, xi_ref, gch_ref, o_ref,
               state_ref, first):
    c = C_CHUNK

    @pl.when(first)
    def _():
        state_ref[...] = jnp.zeros_like(state_ref)

    lane = lax.broadcasted_iota(jnp.int32, (c, LANES), 1)
    even = (lane & 1) == 0
    sn = sin_ref[...]
    cs = cos_ref[...]

    def rope(x):
        swapped = jnp.where(even, pltpu.roll(x, LANES - 1, 1), pltpu.roll(x, 1, 1))
        return x * cs + swapped * sn

    qk = []
    for pair in range(C_HEADS // 2):
        q = rope(q_ref[:, pair * LANES:(pair + 1) * LANES].astype(F32))
        k = rope(k_ref[:, pair * LANES:(pair + 1) * LANES].astype(F32) * (C_QK_DIM ** -0.5))
        qk.append((q, q * xi_ref[pair], k.astype(BF16), (k * zeta_ref[pair]).astype(BF16)))
    masked = []
    for h in range(C_HEADS):
        q, qx, _, _ = qk[h // 2]
        in_head = (lane < C_QK_DIM) if h % 2 == 0 else (lane >= C_QK_DIM)
        masked.append((jnp.where(in_head, q, 0.0).astype(BF16), jnp.where(in_head, qx, 0.0).astype(BF16)))
    inner = [(_dot_nt(masked[h][0], qk[h // 2][2]) * decay_ref[h]).astype(BF16) for h in range(C_HEADS)]
    outs = []
    for h in range(C_HEADS):
        vh = v_ref[:, h * LANES:(h + 1) * LANES]
        st = state_ref[h]
        outs.append(jnp.dot(inner[h], vh, preferred_element_type=F32)
                    + jnp.dot(masked[h][1], st.astype(BF16), preferred_element_type=F32))
        state_ref[h] = st * gch_ref[h] + _dot_tn(qk[h // 2][3], vh)
    for h, o in enumerate(outs):
        mu = jnp.mean(o, axis=-1, keepdims=True)
        oc = o - mu
        o = oc * lax.rsqrt(jnp.mean(oc * oc, axis=-1, keepdims=True) + LN_EPS)
        gate = g_ref[:, h * LANES:(h + 1) * LANES].astype(F32)
        o_ref[:, h * LANES:(h + 1) * LANES] = (_silu(gate) * o).astype(BF16)


BC_BLOCKS = 8


def _swa_ret_kernel(q_ref, kvp_ref, kvc_ref, bias_ref, sink_ref, cq_ref, ck_ref, cv_ref, cg_ref, sin_ref, cos_ref,
                    decay_ref, zeta_ref, xi_ref, gch_ref, yb_ref, yc_ref, state_ref):
    c = C_CHUNK
    blocks = q_ref.shape[1] // c
    first_block = pl.program_id(1) * blocks
    for r in range(blocks):
        rows = slice(r * c, (r + 1) * c)
        prev = kvp_ref.at[0] if r == 0 else kvc_ref.at[0, (r - 1) * c:r * c]
        _swa_block(q_ref.at[0, rows], prev, kvc_ref.at[0, rows], bias_ref, sink_ref, yb_ref.at[0, rows],
                   jnp.minimum(first_block + r, 1))
        _ret_block(cq_ref.at[0, rows], ck_ref.at[0, rows], cv_ref.at[0, rows], cg_ref.at[0, rows],
                   sin_ref.at[rows], cos_ref.at[rows], decay_ref, zeta_ref, xi_ref, gch_ref, yc_ref.at[0, rows],
                   state_ref, first_block + r == 0)


def _swa_retention(proj, bias_b, sink_rows, sin_t, cos_t, decay, zeta_t, xi_t, gch):
    assert B_WINDOW == C_CHUNK
    bsz, s, _ = proj.shape
    c = C_CHUNK
    blocks = min(BC_BLOCKS, s // c)
    t = blocks * c
    qw = C_HEADS * C_QK_DIM
    kvw = 2 * B_KV_HEADS * B_HEAD_DIM
    qb, kvb = OFF_BQ // B_WIDTH, OFF_BKV // kvw
    cols = (B_HEADS // B_KV_HEADS) * c
    return pl.pallas_call(
        _swa_ret_kernel,
        grid=(bsz, s // t),
        in_specs=[pl.BlockSpec((1, t, B_WIDTH), lambda b, n: (b, n, qb)),
                  pl.BlockSpec((1, c, kvw), lambda b, n: (b, jnp.maximum(n * blocks - 1, 0), kvb)),
                  pl.BlockSpec((1, t, kvw), lambda b, n: (b, n, kvb)),
                  pl.BlockSpec((B_KV_HEADS, 2, 2 * c, cols), lambda b, n: (0, 0, 0, 0)),
                  pl.BlockSpec((B_KV_HEADS, 1, cols), lambda b, n: (0, 0, 0)),
                  pl.BlockSpec((1, t, qw), lambda b, n: (b, n, OFF_CQ // qw)),
                  pl.BlockSpec((1, t, qw), lambda b, n: (b, n, OFF_CK // qw)),
                  pl.BlockSpec((1, t, C_WIDTH), lambda b, n: (b, n, OFF_CV // C_WIDTH)),
                  pl.BlockSpec((1, t, C_WIDTH), lambda b, n: (b, n, OFF_CG // C_WIDTH)),
                  pl.BlockSpec((t, LANES), lambda b, n: (n, 0)),
                  pl.BlockSpec((t, LANES), lambda b, n: (n, 0)),
                  pl.BlockSpec((C_HEADS, c, c), lambda b, n: (0, 0, 0)),
                  pl.BlockSpec((C_HEADS // 2, c, LANES), lambda b, n: (0, 0, 0)),
                  pl.BlockSpec((C_HEADS // 2, c, LANES), lambda b, n: (0, 0, 0)),
                  pl.BlockSpec((C_HEADS, 1, LANES), lambda b, n: (0, 0, 0))],
        out_specs=[pl.BlockSpec((1, t, B_WIDTH), lambda b, n: (b, n, 0)),
                   pl.BlockSpec((1, t, C_WIDTH), lambda b, n: (b, n, 0))],
        out_shape=[jax.ShapeDtypeStruct((bsz, s, B_WIDTH), BF16), jax.ShapeDtypeStruct((bsz, s, C_WIDTH), BF16)],
        scratch_shapes=[pltpu.VMEM((C_HEADS, LANES, C_V_DIM), F32)],
        compiler_params=_cparams(("parallel", "arbitrary")),
        name="swa_retention",
    )(proj, proj, proj, bias_b, sink_rows, proj, proj, proj, proj, sin_t, cos_t, decay, zeta_t, xi_t, gch)


def _cast_riders(riders_in, riders_out):
    for src, dst in zip(riders_in, riders_out):
        dst[...] = src[...].astype(BF16)


def _row_split_riders(riders, steps):
    views = [r.reshape(r.size // r.shape[-1], r.shape[-1]) for r in riders]
    for v in views:
        assert v.shape[0] % (steps * 16) == 0, v.shape
    return views, [(v.shape[0] // steps, v.shape[1]) for v in views]


def _out_proj_kernel(ya_ref, yb_ref, yc_ref, wa_ref, wb_ref, wc_ref, x_ref, g_ref, b_ref, *rest):
    n_riders = (len(rest) - 1) // 2
    o_ref = rest[n_riders]
    half = o_ref.shape[0] // 2
    spans = [slice(0, half), slice(half, 2 * half)]
    mixes = [jnp.dot(ya_ref[r, :], wa_ref[...], preferred_element_type=F32)
             + jnp.dot(yb_ref[r, :], wb_ref[...], preferred_element_type=F32)
             + jnp.dot(yc_ref[r, :], wc_ref[...], preferred_element_type=F32) for r in spans]
    for r, mix in zip(spans, mixes):
        o_ref[r, :] = _layer_norm(ALPHA * x_ref[r, :] + mix, g_ref[...], b_ref[...])
    _cast_riders(rest[:n_riders], rest[n_riders + 1:])


def _out_proj_ln(ya, yb, yc, w, x, g, b, riders=()):
    n, d = x.shape
    tm = min(512, n)
    row = lambda i: (i, 0)
    fixed = lambda i: (0, 0)
    once = pl.Buffered(1)
    views, blocks = _row_split_riders(riders, n // tm)
    rider_specs = [pl.BlockSpec(blk, row) for blk in blocks]
    outs = pl.pallas_call(
        _out_proj_kernel,
        grid=(n // tm,),
        in_specs=[pl.BlockSpec((tm, A_WIDTH), row),
                  pl.BlockSpec((tm, B_WIDTH), row),
                  pl.BlockSpec((tm, C_WIDTH), row),
                  pl.BlockSpec((A_WIDTH, d), lambda i: (0, 0), pipeline_mode=once),
                  pl.BlockSpec((B_WIDTH, d), lambda i: (1, 0), pipeline_mode=once),
                  pl.BlockSpec((C_WIDTH, d), lambda i: ((A_WIDTH + B_WIDTH) // C_WIDTH, 0), pipeline_mode=once),
                  pl.BlockSpec((tm, d), row),
                  pl.BlockSpec((1, d), fixed),
                  pl.BlockSpec((1, d), fixed)] + rider_specs,
        out_specs=[pl.BlockSpec((tm, d), row)] + rider_specs,
        out_shape=[jax.ShapeDtypeStruct((n, d), F32)] + [jax.ShapeDtypeStruct(v.shape, BF16) for v in views],
        compiler_params=_cparams(("arbitrary",)),
        name="out_proj_ln",
    )(ya, yb, yc, w, w, w, x, g, b, *views)
    return outs[0], [o.reshape(r.shape) for o, r in zip(outs[1:], riders)]


FFN_TM = 512
FFN_TF = 512


def _swiglu_step(xb, wg, wu, wd):
    hg = jnp.dot(xb, wg, preferred_element_type=F32)
    hu = jnp.dot(xb, wu, preferred_element_type=F32)
    return jnp.dot((_silu(hg) * hu).astype(BF16), wd, preferred_element_type=F32)


def _ffn_kernel(x_ref, wg_ref, wu_ref, wd_ref, g_ref, b_ref, *rest):
    n_riders = (len(rest) - 2) // 2
    riders_in, o_ref = rest[:n_riders], rest[n_riders]
    riders_out, xb_ref = rest[n_riders + 1:-1], rest[-1]
    f = pl.program_id(1)

    @pl.when(f == 0)
    def _():
        xb = x_ref[...].astype(BF16)
        xb_ref[...] = xb
        o_ref[...] = _swiglu_step(xb, wg_ref[...], wu_ref[...], wd_ref[...])
        _cast_riders(riders_in, riders_out)

    @pl.when(f > 0)
    def _():
        o_ref[...] += _swiglu_step(xb_ref[...], wg_ref[...], wu_ref[...], wd_ref[...])
        _cast_riders(riders_in, riders_out)

    @pl.when(f == pl.num_programs(1) - 1)
    def _():
        o_ref[...] = _layer_norm(ALPHA * x_ref[...] + o_ref[...], g_ref[...], b_ref[...])


def _ffn(x, w_gate, w_up, w_down, g, b, riders=()):
    n, d = x.shape
    ff = w_gate.shape[1]
    tm = min(FFN_TM, n)
    tf = FFN_TF
    ni, nf = n // tm, ff // tf
    flat, rider_specs = [], []
    for r in riders:
        cols = r.shape[-1]
        rows = r.size // cols
        flat.append(r.reshape(rows, cols))
        if rows % (ni * nf * 16) == 0:
            rider_specs.append(pl.BlockSpec((rows // (ni * nf), cols), lambda i, f: (i * nf + f, 0)))
        else:
            assert rows % (ni * 16) == 0 and cols % (nf * LANES) == 0
            rider_specs.append(pl.BlockSpec((rows // ni, cols // nf), lambda i, f: (i, f)))
    outs = pl.pallas_call(
        _ffn_kernel,
        grid=(n // tm, ff // tf),
        in_specs=[pl.BlockSpec((tm, d), lambda i, f: (i, 0)),
                  pl.BlockSpec((d, tf), lambda i, f: (0, f)),
                  pl.BlockSpec((d, tf), lambda i, f: (0, f)),
                  pl.BlockSpec((tf, d), lambda i, f: (f, 0)),
                  pl.BlockSpec((1, d), lambda i, f: (0, 0)),
                  pl.BlockSpec((1, d), lambda i, f: (0, 0))] + rider_specs,
        out_specs=[pl.BlockSpec((tm, d), lambda i, f: (i, 0))] + rider_specs,
        out_shape=[jax.ShapeDtypeStruct((n, d), F32)] + [jax.ShapeDtypeStruct(r.shape, BF16) for r in flat],
        scratch_shapes=[pltpu.VMEM((tm, d), BF16)],
        compiler_params=_cparams(("arbitrary", "arbitrary")),
        name="ffn_ln",
    )(x, w_gate, w_up, w_down, g, b, *flat)
    return outs[0], [o.reshape(r.shape) for o, r in zip(outs[1:], riders)]


def _row_copy(src, dst, sem, s, t):
    return pltpu.make_async_copy(src.at[pl.ds(s, 1)], dst.at[pl.ds(t, 1)], sem)


def _ffn_grouped_kernel(eid_ref, nact_ref, src_ref, x_hbm, wg_ref, wu_ref, wd_ref, o_ref, rows_ref, xb_ref, sem,
                        *, tm, per_step):
    i = pl.program_id(0)
    f = pl.program_id(1)
    nact = nact_ref[0]
    active = i < nact
    slot = lax.rem(i, 2)
    buf_rows = rows_ref.shape[1]
    last_slot_row = src_ref.shape[0] - 1

    def fetch(tile, into, first_row, count):
        for r in range(count):
            row = first_row + r
            token = src_ref[jnp.minimum(tile * tm + row, last_slot_row)]
            _row_copy(x_hbm, rows_ref.at[into], sem.at[into], token, row).start(priority=1)

    def wait_buffer(which):
        pltpu.make_async_copy(x_hbm.at[pl.ds(0, buf_rows)], rows_ref.at[which], sem.at[which]).wait()

    @pl.when(jnp.logical_and(i == 0, f == 0))
    def _():
        lax.fori_loop(0, buf_rows // per_step, lambda s, c: (fetch(0, 0, s * per_step, per_step), c)[1], 0)

    @pl.when(jnp.logical_and(f == 0, i <= nact))
    def _():
        wait_buffer(slot)

    @pl.when(jnp.logical_and(f == 0, jnp.logical_not(active)))
    def _():
        o_ref[...] = jnp.zeros_like(o_ref)

    @pl.when(jnp.logical_and(f == 0, active))
    def _():
        fetch(i + 1, 1 - slot, 0, per_step)
        xb = rows_ref[slot, :tm, :].astype(BF16)
        xb_ref[...] = xb
        o_ref[...] = _swiglu_step(xb, wg_ref[0], wu_ref[0], wd_ref[0])

    @pl.when(jnp.logical_and(f > 0, active))
    def _():
        fetch(i + 1, 1 - slot, f * per_step, per_step)
        o_ref[...] += _swiglu_step(xb_ref[...], wg_ref[0], wu_ref[0], wd_ref[0])


def _ffn_grouped(x, src, eid, nact, w_gate, w_up, w_down):
    n, d = x.shape
    slots = src.shape[0]
    ff = w_gate.shape[2]
    tm = min(FFN_TM, n)
    tf = FFN_TF
    nf = ff // tf
    per_step = -(-tm // nf)
    per_step += (-per_step) % 8
    assert per_step * nf <= n

    def fidx(i, f, nact_ref):
        return jnp.where(i < nact_ref[0], f, nf - 1)

    grid_spec = pltpu.PrefetchScalarGridSpec(
        num_scalar_prefetch=3,
        grid=(slots // tm, nf),
        in_specs=[pl.BlockSpec(memory_space=pl.ANY),
                  pl.BlockSpec((1, d, tf), lambda i, f, e, a, s: (e[i], 0, fidx(i, f, a))),
                  pl.BlockSpec((1, d, tf), lambda i, f, e, a, s: (e[i], 0, fidx(i, f, a))),
                  pl.BlockSpec((1, tf, d), lambda i, f, e, a, s: (e[i], fidx(i, f, a), 0))],
        out_specs=pl.BlockSpec((tm, d), lambda i, f, e, a, s: (i, 0)),
        scratch_shapes=[pltpu.VMEM((2, per_step * nf, d), F32), pltpu.VMEM((tm, d), BF16),
                        pltpu.SemaphoreType.DMA((2,))],
    )
    return pl.pallas_call(
        functools.partial(_ffn_grouped_kernel, tm=tm, per_step=per_step),
        grid_spec=grid_spec,
        out_shape=jax.ShapeDtypeStruct((slots, d), F32),
        compiler_params=_cparams(("arbitrary", "arbitrary")),
        name="ffn_grouped",
    )(eid, nact, src, x, w_gate, w_up, w_down)


ROUTER_TM = 512


def _router_kernel(x_ref, wr_ref, tri_ref, meta_ref, cnt_ref, carry_ref):
    @pl.when(pl.program_id(0) == 0)
    def _():
        carry_ref[...] = jnp.zeros_like(carry_ref)

    x = x_ref[...]
    x_hi = x.astype(BF16)
    x_lo = (x - x_hi.astype(F32)).astype(BF16)
    parts = (jnp.dot(x_hi, wr_ref[...], preferred_element_type=F32)
             + jnp.dot(x_lo, wr_ref[...], preferred_element_type=F32))
    logits = parts + pltpu.roll(parts, LANES - N_EXPERTS, 1)
    lane = lax.broadcasted_iota(jnp.int32, logits.shape, 1)
    logits = jnp.where(lane < N_EXPERTS, logits, -jnp.inf)
    m1 = jnp.max(logits, axis=-1, keepdims=True)
    i1 = jnp.min(jnp.where(logits == m1, lane, LANES), axis=-1, keepdims=True)
    rest = jnp.where(lane == i1, -jnp.inf, logits)
    m2 = jnp.max(rest, axis=-1, keepdims=True)
    i2 = jnp.min(jnp.where(rest == m2, lane, LANES), axis=-1, keepdims=True)
    e2 = jnp.exp(m2 - m1)
    w1 = 1.0 / (1.0 + e2)
    w2 = e2 * w1
    hit1 = lane == i1
    hit2 = lane == i2
    onehot = jnp.where(jnp.logical_or(hit1, hit2), 1.0, 0.0)
    before = jnp.dot(tri_ref[...], onehot.astype(BF16), preferred_element_type=F32) + carry_ref[...]
    r1 = jnp.sum(jnp.where(hit1, before, 0.0), axis=-1, keepdims=True)
    r2 = jnp.sum(jnp.where(hit2, before, 0.0), axis=-1, keepdims=True)
    carry_ref[...] = carry_ref[...] + jnp.sum(onehot, axis=0, keepdims=True)
    cnt_ref[...] = carry_ref[...]
    meta = jnp.where(lane == 0, i1.astype(F32), 0.0)
    meta = jnp.where(lane == 1, i2.astype(F32), meta)
    meta = jnp.where(lane == 2, r1, meta)
    meta = jnp.where(lane == 3, r2, meta)
    meta = jnp.where(lane == 4, w1, meta)
    meta = jnp.where(lane == 5, w2, meta)
    meta_ref[...] = meta


def _router(x, w_router):
    n, d = x.shape
    tm = min(ROUTER_TM, n)
    w_hi = w_router.astype(BF16)
    w_lo = (w_router.astype(F32) - w_hi.astype(F32)).astype(BF16)
    wr = jnp.zeros((d, LANES), BF16).at[:, :N_EXPERTS].set(w_hi).at[:, N_EXPERTS:2 * N_EXPERTS].set(w_lo)
    tri = jnp.asarray(np.tril(np.ones((tm, tm), np.float32), -1), BF16)
    return pl.pallas_call(
        _router_kernel,
        grid=(n // tm,),
        in_specs=[pl.BlockSpec((tm, d), lambda i: (i, 0)),
                  pl.BlockSpec((d, LANES), lambda i: (0, 0)),
                  pl.BlockSpec((tm, tm), lambda i: (0, 0))],
        out_specs=[pl.BlockSpec((tm, LANES), lambda i: (i, 0)),
                   pl.BlockSpec((1, LANES), lambda i: (0, 0))],
        out_shape=[jax.ShapeDtypeStruct((n, LANES), F32), jax.ShapeDtypeStruct((1, LANES), F32)],
        scratch_shapes=[pltpu.VMEM((1, LANES), F32)],
        compiler_params=_cparams(("arbitrary",)),
        name="router",
    )(x, wr, tri)


MOVE_TM = 256
ISSUE_UNROLL = 8


def _combine_kernel(d1_ref, d2_ref, ys_hbm, x_ref, meta_ref, g_ref, b_ref, o_ref, buf_ref, sem, *, tm):
    i = pl.program_id(0)
    last = pl.num_programs(0) - 1
    slot = lax.rem(i, 2)

    def start_row(tile, into, t):
        row = tile * tm + t
        _row_copy(ys_hbm, buf_ref.at[into, 0], sem.at[into], d1_ref[row], t).start(priority=0)
        _row_copy(ys_hbm, buf_ref.at[into, 1], sem.at[into], d2_ref[row], t).start(priority=1)

    def wait_slot(which):
        for k in range(2):
            pltpu.make_async_copy(ys_hbm.at[pl.ds(0, tm)], buf_ref.at[which, k], sem.at[which]).wait()

    @pl.when(i == 0)
    def _():
        lax.fori_loop(0, tm, lambda t, c: (start_row(0, 0, t), c)[1], 0, unroll=ISSUE_UNROLL)

    wait_slot(slot)
    nxt = jnp.minimum(i + 1, last)
    for t in range(tm):
        start_row(nxt, 1 - slot, t)
    meta = meta_ref[...]
    lane = lax.broadcasted_iota(jnp.int32, meta.shape, 1)
    w1 = jnp.sum(jnp.where(lane == 4, meta, 0.0), axis=-1, keepdims=True)
    w2 = jnp.sum(jnp.where(lane == 5, meta, 0.0), axis=-1, keepdims=True)
    f = w1 * buf_ref[slot, 0] + w2 * buf_ref[slot, 1]
    o_ref[...] = _layer_norm(ALPHA * x_ref[...] + f, g_ref[...], b_ref[...])

    @pl.when(i == last)
    def _():
        wait_slot(1 - slot)


def _combine(ys, x, meta, d1, d2, g, b):
    n, d = x.shape
    tm = min(MOVE_TM, n)
    grid_spec = pltpu.PrefetchScalarGridSpec(
        num_scalar_prefetch=2,
        grid=(n // tm,),
        in_specs=[pl.BlockSpec(memory_space=pl.ANY),
                  pl.BlockSpec((tm, d), lambda i, a, c: (i, 0)),
                  pl.BlockSpec((tm, LANES), lambda i, a, c: (i, 0)),
                  pl.BlockSpec((1, d), lambda i, a, c: (0, 0)),
                  pl.BlockSpec((1, d), lambda i, a, c: (0, 0))],
        out_specs=pl.BlockSpec((tm, d), lambda i, a, c: (i, 0)),
        scratch_shapes=[pltpu.VMEM((2, 2, tm, d), F32), pltpu.SemaphoreType.DMA((2,))],
    )
    return pl.pallas_call(
        functools.partial(_combine_kernel, tm=tm),
        grid_spec=grid_spec,
        out_shape=jax.ShapeDtypeStruct((n, d), F32),
        compiler_params=_cparams(("arbitrary",)),
        name="moe_combine",
    )(d1, d2, ys, x, meta, g, b)


def _moe(x, w_router, w_gate, w_up, w_down, g, b):
    n, d = x.shape
    tm = min(FFN_TM, n)
    meta, cnt = _router(x, w_router)
    i1 = meta[:, 0].astype(jnp.int32)
    i2 = meta[:, 1].astype(jnp.int32)
    counts = cnt[0, :N_EXPERTS].astype(jnp.int32)
    tiles = (counts + tm - 1) // tm
    tile_end = jnp.cumsum(tiles)
    group_start = (tile_end - tiles) * tm
    experts = jnp.arange(N_EXPERTS, dtype=jnp.int32)[None, :]
    start_of = lambda idx: jnp.sum(jnp.where(idx[:, None] == experts, group_start[None, :], 0), axis=1)
    d1 = start_of(i1) + meta[:, 2].astype(jnp.int32)
    d2 = start_of(i2) + meta[:, 3].astype(jnp.int32)
    max_tiles = (2 * n) // tm + N_EXPERTS
    tile_ids = jnp.arange(max_tiles, dtype=jnp.int32)[:, None]
    eid = jnp.minimum(jnp.sum((tile_ids >= tile_end[None, :]).astype(jnp.int32), axis=1), N_EXPERTS - 1)
    nact = tile_end[-1:].astype(jnp.int32)
    token = jnp.arange(n, dtype=jnp.int32)
    src = jnp.zeros((max_tiles * tm,), jnp.int32).at[jnp.concatenate([d1, d2])].set(
        jnp.concatenate([token, token]), unique_indices=True)
    ys = _ffn_grouped(x, src, eid.astype(jnp.int32), nact, w_gate, w_up, w_down)
    return _combine(ys, x, meta, d1, d2, g, b)


def _lambda_init(layer_idx):
    return 0.8 - 0.6 * math.exp(-0.3 * layer_idx)


def _static_tables(s):
    c = C_CHUNK
    ang = jnp.repeat(1.0 / (10000.0 ** jnp.linspace(0.0, 1.0, C_QK_DIM // 2, dtype=F32)), 2)
    ang = jnp.arange(s, dtype=F32)[:, None] * ang[None, :]
    sign = jnp.where(jnp.arange(C_QK_DIM) % 2 == 0, -1.0, 1.0).astype(F32)
    sin_t = jnp.tile(jnp.sin(ang) * sign[None, :], (1, 2))
    cos_t = jnp.tile(jnp.cos(ang), (1, 2))
    log_g = jnp.log(1.0 - jnp.exp2(-5.0 - jnp.arange(C_HEADS, dtype=F32)))
    pos = jnp.arange(c)
    rel = (pos[:, None] - pos[None, :]).astype(F32)
    decay = jnp.where((rel >= 0)[None], jnp.exp(jnp.maximum(rel, 0.0)[None] * log_g[:, None, None]), 0.0)
    zeta = jnp.exp((c - 1 - pos).astype(F32)[:, None] * log_g[None, :])
    xi = jnp.exp((pos + 1).astype(F32)[:, None] * log_g[None, :])
    per_pair = lambda t: jnp.repeat(t.T.reshape(C_HEADS // 2, 2, c), C_QK_DIM, axis=1).transpose(0, 2, 1)
    gch = jnp.broadcast_to(jnp.exp(c * log_g)[:, None, None], (C_HEADS, 1, LANES))
    dist_a = np.arange(A_SUB)[None, :] - np.arange(A_BLOCK)[:, None]
    types = [jnp.where(dist_a + off >= 0, _rel_bucket(jnp.asarray(dist_a + off)), REL_BUCKETS)
             for off in (0, A_SUB, A_BLOCK)]
    types.append(jnp.full(dist_a.shape, REL_BUCKETS - 1, jnp.int32))
    bkt_a = jnp.concatenate([jnp.tile(b, (1, 2)) for b in types], axis=0).astype(jnp.int32)
    w = B_WINDOW
    dist = np.arange(w)[:, None] + w - np.arange(2 * w)[None, :]
    band = (dist >= 0) & (dist < w)
    has_prev = np.stack([np.broadcast_to(np.arange(2 * w)[None, :] >= w, band.shape), np.ones_like(band)])
    bkt_b = jnp.where(band[None] & has_prev, _rel_bucket(jnp.asarray(dist))[None], REL_BUCKETS)
    bkt_b = bkt_b.transpose(0, 2, 1).reshape(2 * 2 * w, w).astype(jnp.int32)
    return sin_t, cos_t, decay, per_pair(zeta), per_pair(xi), gch, bkt_a, bkt_b


def kernel(x, w_in, rel_bias, a_lambda, a_subln_g, b_sinks, w_out, ln_mix_g, ln_mix_b, ln_ffn_g, ln_ffn_b,
           dense_w_gate, dense_w_up, dense_w_down, moe_router, moe_w_gate, moe_w_up, moe_w_down):
    bsz, s, d = x.shape
    n = bsz * s
    sin_t, cos_t, decay, zeta_t, xi_t, gch, bkt_a, bkt_b = _static_tables(s)
    tab_t = rel_bias.astype(F32).T
    bias_a = _bias_lookup(tab_t[:A_HEADS] * LOG2E, bkt_a).reshape(A_HEADS, A_BIAS_TYPES, A_BLOCK, 2 * A_SUB)
    group = B_HEADS // B_KV_HEADS
    bias_b = _bias_lookup(tab_t[A_HEADS:] * LOG2E, bkt_b)
    bias_b = bias_b.reshape(B_KV_HEADS, group, 2, 2 * B_WINDOW, B_WINDOW).transpose(0, 2, 3, 1, 4)
    bias_b = bias_b.reshape(B_KV_HEADS, 2, 2 * B_WINDOW, group * B_WINDOW)
    xf = x.reshape(n, d).astype(F32)
    w_in_bf16, w_out_bf16 = w_in[0].astype(BF16), w_out[0].astype(BF16)
    for l in range(DEPTH):
        w_in_l = jnp.concatenate([w_in_bf16[:, a:b] for a, b in _PERM_RUNS]
                                 + [jnp.zeros((d, PROJ_PAD - PROJ_WIDTH), BF16)], axis=1)
        proj = _in_proj(xf, w_in_l).reshape(bsz, s, PROJ_PAD)
        ya = _attn_a(proj, bias_a, a_lambda[l].astype(F32), a_subln_g[l].astype(F32).reshape(1, A_V_DIM),
                     _lambda_init(l))
        sink_rows = jnp.repeat(b_sinks[l].astype(F32).reshape(B_KV_HEADS, 1, group), B_WINDOW, axis=2)
        yb, yc = _swa_retention(proj, bias_b, sink_rows, sin_t, cos_t, decay, zeta_t, xi_t, gch)
        g_mix = ln_mix_g[l].astype(F32).reshape(1, d)
        b_mix = ln_mix_b[l].astype(F32).reshape(1, d)
        j = l // 2
        dense = (dense_w_gate[j], dense_w_up[j], dense_w_down[j]) if l % 2 == 0 else ()
        ahead = (w_in[l + 1], w_out[l + 1]) if l + 1 < DEPTH else ()
        xf, cast = _out_proj_ln(ya.reshape(n, A_WIDTH), yb.reshape(n, B_WIDTH), yc.reshape(n, C_WIDTH),
                                w_out_bf16, xf, g_mix, b_mix, dense + ahead)
        dense_bf16 = cast[:len(dense)]
        if ahead:
            w_in_bf16, w_out_bf16 = cast[len(dense):]
        g_ffn = ln_ffn_g[l].astype(F32).reshape(1, d)
        b_ffn = ln_ffn_b[l].astype(F32).reshape(1, d)
        if l % 2 == 0:
            experts = (moe_w_gate[j], moe_w_up[j], moe_w_down[j]) if l + 1 < DEPTH else ()
            xf, moe_bf16 = _ffn(xf, *dense_bf16, g_ffn, b_ffn, experts)
        else:
            assert l > 0, "expert weights are cast by the preceding dense layer"
            xf = _moe(xf, moe_router[j], *moe_bf16, g_ffn, b_ffn)
    return xf.reshape(bsz, s, d).astype(x.dtype)
```
